```python
import math
import jax, jax.numpy as jnp
from jax import lax
import numpy as np

D_MODEL = 1024
BATCH = 16
SEQ = 2048
DEPTH = 2

PLE_DIM = 256
BRANCH_WIDTH = D_MODEL // 2
HEAD_DIM = 128
RET_HEADS = BRANCH_WIDTH // HEAD_DIM
GDN_HEADS = BRANCH_WIDTH // HEAD_DIM
LRU_WIDTH = BRANCH_WIDTH
LRU_BLOCKS = 8
LRU_C = 8.0
CONV_WIDTH = 4
CHUNK = 64
N_BRANCHES = 3
ROPE_BASE = 10000.0
N_EXPERTS = 64
TOP_K = 8
N_GROUPS = 8
TOPK_GROUPS = 4
EXPERT_FF = 256
SHARED_FF = 256
ROUTED_SCALE = 2.5
MOE_BLOCK = 256
LN_EPS = 1e-5
DN_ALPHA = (2 * DEPTH) ** 0.25
DN_BETA = (8 * DEPTH) ** -0.25
IN_SPLITS = (BRANCH_WIDTH, BRANCH_WIDTH, BRANCH_WIDTH, BRANCH_WIDTH,
             3 * BRANCH_WIDTH, BRANCH_WIDTH, GDN_HEADS, GDN_HEADS,
             LRU_WIDTH, LRU_WIDTH,
             N_BRANCHES * D_MODEL)
D_IN = sum(IN_SPLITS)

kernel_name = 'hybrid_ret_gdn_rglru_moe_deepnorm'


def layer_norm(x, g, b):
    xf = x.astype(jnp.float32)
    mu = xf.mean(-1, keepdims=True)
    var = jnp.square(xf - mu).mean(-1, keepdims=True)
    return ((xf - mu) * lax.rsqrt(var + LN_EPS) * g + b).astype(x.dtype)


def causal_dwconv(x, w):
    c = x.shape[-1]
    return lax.conv_general_dilated(x, w[:, None, :].astype(x.dtype), window_strides=(1,),
                                    padding=[(w.shape[0] - 1, 0)],
                                    dimension_numbers=('NWC', 'WIO', 'NWC'), feature_group_count=c)


def to_chunks(t):
    b, s, h, d = t.shape
    return t.reshape(b, s // CHUNK, CHUNK, h, d).transpose(0, 3, 1, 2, 4)


def from_chunks(t):
    b, h, n, c, d = t.shape
    return t.transpose(0, 2, 3, 1, 4).reshape(b, n * c, h, d)


def rotary(x):
    s, d = x.shape[1], x.shape[-1]
    inv_freq = ROPE_BASE ** (-jnp.linspace(0.0, 1.0, d // 2, dtype=jnp.float32))
    ang = jnp.arange(s, dtype=jnp.float32)[:, None] * inv_freq[None, :]
    cos = jnp.cos(ang)[None, :, None, :]
    sin = jnp.sin(ang)[None, :, None, :]
    x1, x2 = x[..., : d // 2], x[..., d // 2:]
    return jnp.concatenate([x1 * cos - x2 * sin, x2 * cos + x1 * sin], axis=-1)


def l2norm(x):
    return x * lax.rsqrt(jnp.sum(x * x, axis=-1, keepdims=True) + 1e-6)


def retention(q, k, v):
    b, s, h, dk = q.shape
    log_gamma = jnp.log1p(-jnp.exp2(-5.0 - jnp.arange(h, dtype=jnp.float32)))
    q = rotary(q)
    k = rotary(k) * (dk ** -0.5)
    qc, kc, vc = to_chunks(q), to_chunks(k), to_chunks(v)
    pos = jnp.arange(CHUNK, dtype=jnp.float32)
    diff = pos[:, None] - pos[None, :]
    causal = diff >= 0
    intra_decay = jnp.where(causal, jnp.exp(log_gamma[:, None, None] * jnp.where(causal, diff, 0.0)), 0.0)
    scores = jnp.einsum('bhncd,bhnmd->bhncm', qc, kc) * intra_decay[None, :, None]
    o_intra = jnp.einsum('bhncm,bhnmv->bhncv', scores, vc)
    q_decay = jnp.exp(log_gamma[:, None] * (pos + 1.0))[None, :, :, None]
    k_decay = jnp.exp(log_gamma[:, None] * (CHUNK - 1.0 - pos))[None, :, :, None]
    chunk_decay = jnp.exp(log_gamma * CHUNK)[None, :, None, None]

    def step(state, xs):
        q_n, k_n, v_n = xs
        o_n = jnp.einsum('bhcd,bhdv->bhcv', q_n * q_decay, state)
        state = state * chunk_decay + jnp.einsum('bhcd,bhcv->bhdv', k_n * k_decay, v_n)
        return state, o_n

    state0 = jnp.zeros((b, h, dk, v.shape[-1]), jnp.float32)
    _, o_inter = lax.scan(step, state0, (jnp.moveaxis(qc, 2, 0), jnp.moveaxis(kc, 2, 0), jnp.moveaxis(vc, 2, 0)))
    return from_chunks(o_intra + jnp.moveaxis(o_inter, 0, 2))


def gated_delta_rule(q, k, v, beta, log_alpha):
    b, s, h, dk = q.shape
    q = q * (dk ** -0.5)
    qc, kc, vc = to_chunks(q), to_chunks(k), to_chunks(v)
    bc = to_chunks(beta[..., None])[..., 0]
    gc = jnp.cumsum(to_chunks(log_alpha[..., None])[..., 0], axis=-1)
    lower = jnp.tril(jnp.ones((CHUNK, CHUNK), bool))
    strict = jnp.tril(jnp.ones((CHUNK, CHUNK), bool), -1)
    gdiff = gc[..., :, None] - gc[..., None, :]
    decay = jnp.where(lower, jnp.exp(jnp.where(lower, gdiff, 0.0)), 0.0)
    kb = kc * bc[..., None]
    a_strict = jnp.where(strict, jnp.einsum('bhncd,bhnmd->bhncm', kb, kc) * decay, 0.0)
    t_mat = a_strict + jnp.eye(CHUNK, dtype=jnp.float32)
    u = lax.linalg.triangular_solve(t_mat, vc * bc[..., None], left_side=True, lower=True, unit_diagonal=True)
    w = lax.linalg.triangular_solve(t_mat, kb * jnp.exp(gc)[..., None], left_side=True, lower=True, unit_diagonal=True)
    qk = jnp.einsum('bhncd,bhnmd->bhncm', qc, kc) * decay
    q_dec = qc * jnp.exp(gc)[..., None]
    k_dec = kc * jnp.exp(gc[..., -1:] - gc)[..., None]
    g_last = jnp.exp(gc[..., -1])[..., None, None]

    def step(state, xs):
        u_n, w_n, qk_n, qd_n, kd_n, gl_n = xs
        v_new = u_n - jnp.einsum('bhcd,bhdv->bhcv', w_n, state)
        o_n = jnp.einsum('bhcd,bhdv->bhcv', qd_n, state) + jnp.einsum('bhcm,bhmv->bhcv', qk_n, v_new)
        state = state * gl_n + jnp.einsum('bhcd,bhcv->bhdv', kd_n, v_new)
        return state, o_n

    xs = (jnp.moveaxis(u, 2, 0), jnp.moveaxis(w, 2, 0), jnp.moveaxis(qk, 2, 0),
          jnp.moveaxis(q_dec, 2, 0), jnp.moveaxis(k_dec, 2, 0), jnp.moveaxis(g_last, 2, 0))
    state0 = jnp.zeros((b, h, dk, v.shape[-1]), jnp.float32)
    _, o = lax.scan(step, state0, xs)
    return from_chunks(jnp.moveaxis(o, 0, 2))


def rg_lru(x, w_r, b_r, w_i, b_i, lam):
    b, s, wdt = x.shape
    xb = x.reshape(b, s, LRU_BLOCKS, wdt // LRU_BLOCKS)
    r = jax.nn.sigmoid(jnp.einsum('bsgi,gij->bsgj', xb, w_r.astype(jnp.float32)).reshape(b, s, wdt) + b_r)
    i = jax.nn.sigmoid(jnp.einsum('bsgi,gij->bsgj', xb, w_i.astype(jnp.float32)).reshape(b, s, wdt) + b_i)
    log_a = -LRU_C * r * jax.nn.softplus(-lam.astype(jnp.float32))
    a = jnp.exp(log_a)
    gated_in = jnp.sqrt(-jnp.expm1(2.0 * log_a)) * (i * x)

    def combine(left, right):
        a1, h1 = left
        a2, h2 = right
        return a1 * a2, a2 * h1 + h2

    _, h = lax.associative_scan(combine, (a, gated_in), axis=1)
    return h


def hybrid_mixer(h, w_in, b_in, ret_norm_g, ret_norm_b, gdn_conv_w, gdn_a_log, gdn_dt_bias, gdn_norm_g,
                 lru_conv_w, lru_conv_b, lru_w_r, lru_b_r, lru_w_i, lru_b_i, lru_lambda, w_branch, w_out):
    b, s, d = h.shape
    f32 = jnp.float32
    split_at = [int(c) for c in np.cumsum(IN_SPLITS)[:-1]]
    (rq, rk, rv, rg, gqkv, gg, gb, ga, lx, lg, mg) = jnp.split(h @ w_in + b_in, split_at, axis=-1)

    o_ret = retention(rq.astype(f32).reshape(b, s, RET_HEADS, HEAD_DIM),
                      rk.astype(f32).reshape(b, s, RET_HEADS, HEAD_DIM),
                      rv.astype(f32).reshape(b, s, RET_HEADS, HEAD_DIM))
    mu = o_ret.mean(-1, keepdims=True)
    var = jnp.square(o_ret - mu).mean(-1, keepdims=True)
    o_ret = ((o_ret - mu) * lax.rsqrt(var + LN_EPS)).reshape(b, s, BRANCH_WIDTH) * ret_norm_g + ret_norm_b
    y_ret = jax.nn.silu(rg.astype(f32)) * o_ret

    qkv = jax.nn.silu(causal_dwconv(gqkv, gdn_conv_w)).astype(f32)
    gq, gk, gv = jnp.split(qkv, 3, axis=-1)
    q = l2norm(gq.reshape(b, s, GDN_HEADS, HEAD_DIM))
    k = l2norm(gk.reshape(b, s, GDN_HEADS, HEAD_DIM))
    v = gv.reshape(b, s, GDN_HEADS, HEAD_DIM)
    beta = jax.nn.sigmoid(gb.astype(f32))
    log_alpha = -jnp.exp(gdn_a_log.astype(f32)) * jax.nn.softplus(ga.astype(f32) + gdn_dt_bias)
    o_gdn = gated_delta_rule(q, k, v, beta, log_alpha)
    o_gdn = o_gdn * lax.rsqrt(jnp.mean(o_gdn * o_gdn, axis=-1, keepdims=True) + 1e-6) * gdn_norm_g
    y_gdn = (o_gdn * jax.nn.silu(gg.astype(f32).reshape(b, s, GDN_HEADS, HEAD_DIM))).reshape(b, s, BRANCH_WIDTH)

    xc = (causal_dwconv(lx, lru_conv_w) + lru_conv_b).astype(f32)
    y_lru = jax.nn.gelu(lg.astype(f32)) * rg_lru(xc, lru_w_r, lru_b_r, lru_w_i, lru_b_i, lru_lambda)

    branches = jnp.stack([y_ret, y_gdn, y_lru], axis=2).astype(h.dtype)
    branches = jnp.einsum('bsnw,nwd->bsnd', branches, w_branch)
    gates = jax.nn.sigmoid(mg.reshape(b, s, N_BRANCHES, d))
    return jnp.sum(gates * branches, axis=2) @ w_out


def swiglu(x, w_gu, w_down):
    gate, up = jnp.split(x @ w_gu, 2, axis=-1)
    return (jax.nn.silu(gate) * up) @ w_down


def routed_experts(xt, eidx, gw, w_gu, w_down):
    n, d = xt.shape
    e = w_gu.shape[0]
    k = eidx.shape[1]
    a = n * k
    flat_e = eidx.reshape(-1)
    flat_t = jnp.repeat(jnp.arange(n, dtype=jnp.int32), k)
    flat_w = gw.reshape(-1).astype(xt.dtype)
    order = jnp.argsort(flat_e)
    se = flat_e[order]
    counts = jnp.zeros((e,), jnp.int32).at[flat_e].add(1)
    starts = jnp.cumsum(counts) - counts
    padded = (counts + MOE_BLOCK - 1) // MOE_BLOCK * MOE_BLOCK
    pends = jnp.cumsum(padded)
    pstarts = pends - padded
    dest = pstarts[se] + jnp.arange(a, dtype=jnp.int32) - starts[se]
    n_blocks = -(-a // MOE_BLOCK) + e
    rows = n_blocks * MOE_BLOCK
    row_tok = jnp.full((rows,), n, jnp.int32).at[dest].set(flat_t[order])
    row_w = jnp.zeros((rows,), xt.dtype).at[dest].set(flat_w[order])
    blk_start = jnp.arange(n_blocks, dtype=jnp.int32) * MOE_BLOCK
    blk_e = jnp.minimum(jnp.searchsorted(pends, blk_start, side='right'), e - 1).astype(jnp.int32)
    x_pad = jnp.concatenate([xt, jnp.zeros((1, d), xt.dtype)], axis=0)

    def block(acc, inp):
        tok, wt, ex = inp
        y = swiglu(x_pad[tok], w_gu[ex], w_down[ex])
        return acc.at[tok].add((y * wt[:, None]).astype(acc.dtype)), None

    out, _ = lax.scan(block, jnp.zeros((n + 1, d), xt.dtype),
                      (row_tok.reshape(n_blocks, MOE_BLOCK), row_w.reshape(n_blocks, MOE_BLOCK), blk_e))
    return out[:n]


def moe(x, router_w, router_b, exp_w_gu, exp_w_down, sh_w_gu, sh_w_down):
    b, s, d = x.shape
    xt = x.reshape(b * s, d)
    n = xt.shape[0]
    scores = jax.nn.sigmoid((xt @ router_w).astype(jnp.float32))
    sel = scores + router_b.astype(jnp.float32)
    grp_score = lax.top_k(sel.reshape(n, N_GROUPS, N_EXPERTS // N_GROUPS), 2)[0].sum(-1)
    _, gidx = lax.top_k(grp_score, TOPK_GROUPS)
    gmask = jax.nn.one_hot(gidx, N_GROUPS, dtype=jnp.float32).sum(1) > 0
    emask = jnp.repeat(gmask, N_EXPERTS // N_GROUPS, axis=1)
    _, eidx = lax.top_k(jnp.where(emask, sel, -jnp.inf), TOP_K)
    gw = jnp.take_along_axis(scores, eidx, axis=1)
    gw = gw / jnp.sum(gw, axis=-1, keepdims=True) * ROUTED_SCALE
    routed = routed_experts(xt, eidx, gw, exp_w_gu, exp_w_down)
    shared = swiglu(xt, sh_w_gu, sh_w_down)
    return (routed + shared).reshape(b, s, d)


def setup_inputs(seed: int = 0) -> dict:
    key = jax.random.key(seed)
    ks = iter(jax.random.split(key, 40))
    L, D, W, H = DEPTH, D_MODEL, BRANCH_WIDTH, GDN_HEADS
    nrm = lambda shape, scale: jax.random.normal(next(ks), shape, jnp.float32) * scale
    gain = lambda shape: 1.0 + nrm(shape, 0.02)
    dt = jnp.exp(jax.random.uniform(next(ks), (L, H), jnp.float32, math.log(0.001), math.log(0.1)))
    a0 = jax.random.uniform(next(ks), (L, LRU_WIDTH), jnp.float32, 0.9, 0.999)
    a_base = a0 ** (1.0 / LRU_C)
    return {
        'x': nrm((BATCH, SEQ, D), 1.0),
        'p': nrm((DEPTH, BATCH, SEQ, PLE_DIM), 1.0),
        'ln_in_g': gain((D,)),
        'ln_in_b': nrm((D,), 0.02),
        'w_in': nrm((L, D, D_IN), D ** -0.5),
        'b_in': nrm((L, D_IN), 0.02),
        'ret_norm_g': gain((L, W)),
        'ret_norm_b': nrm((L, W), 0.02),
        'gdn_conv_w': nrm((L, CONV_WIDTH, 3 * W), CONV_WIDTH ** -0.5),
        'gdn_a_log': jnp.log(jax.random.uniform(next(ks), (L, H), jnp.float32, 1.0, 16.0)),
        'gdn_dt_bias': dt + jnp.log(-jnp.expm1(-dt)),
        'gdn_norm_g': gain((L, HEAD_DIM)),
        'lru_conv_w': nrm((L, CONV_WIDTH, LRU_WIDTH), CONV_WIDTH ** -0.5),
        'lru_conv_b': nrm((L, LRU_WIDTH), 0.02),
        'lru_w_r': nrm((L, LRU_BLOCKS, LRU_WIDTH // LRU_BLOCKS, LRU_WIDTH // LRU_BLOCKS), (LRU_WIDTH // LRU_BLOCKS) ** -0.5),
        'lru_b_r': nrm((L, LRU_WIDTH), 0.02),
        'lru_w_i': nrm((L, LRU_BLOCKS, LRU_WIDTH // LRU_BLOCKS, LRU_WIDTH // LRU_BLOCKS), (LRU_WIDTH // LRU_BLOCKS) ** -0.5),
        'lru_b_i': nrm((L, LRU_WIDTH), 0.02),
        'lru_lambda': jnp.log(a_base) - jnp.log1p(-a_base),
        'w_branch': nrm((L, N_BRANCHES, W, D), W ** -0.5),
        'w_out': nrm((L, D, D), D ** -0.5 * DN_BETA),
        'ln1_g': gain((L, D)),
        'ln1_b': nrm((L, D), 0.02),
        'router_w': nrm((L, D, N_EXPERTS), D ** -0.5),
        'router_b': nrm((L, N_EXPERTS), 0.01),
        'exp_w_gu': nrm((L, N_EXPERTS, D, 2 * EXPERT_FF), D ** -0.5),
        'exp_w_down': nrm((L, N_EXPERTS, EXPERT_FF, D), EXPERT_FF ** -0.5 * DN_BETA),
        'sh_w_gu': nrm((L, D, 2 * SHARED_FF), D ** -0.5),
        'sh_w_down': nrm((L, SHARED_FF, D), SHARED_FF ** -0.5 * DN_BETA),
        'ple_w_e': nrm((L, PLE_DIM, D), PLE_DIM ** -0.5 * DN_BETA),
        'ple_w_g': nrm((L, D, D), D ** -0.5),
        'ple_b_g': nrm((L, D), 0.02),
        'ln2_g': gain((L, D)),
        'ln2_b': nrm((L, D), 0.02),
    }


def reference(x, p, ln_in_g, ln_in_b, w_in, b_in, ret_norm_g, ret_norm_b, gdn_conv_w, gdn_a_log, gdn_dt_bias,
              gdn_norm_g, lru_conv_w, lru_conv_b, lru_w_r, lru_b_r, lru_w_i, lru_b_i, lru_lambda, w_branch, w_out,
              ln1_g, ln1_b, router_w, router_b, exp_w_gu, exp_w_down, sh_w_gu, sh_w_down, ple_w_e, ple_w_g,
              ple_b_g, ln2_g, ln2_b):
    h = layer_norm(x, ln_in_g, ln_in_b)
    for l in range(DEPTH):
        mix = hybrid_mixer(h, w_in[l], b_in[l], ret_norm_g[l], ret_norm_b[l], gdn_conv_w[l], gdn_a_log[l],
                           gdn_dt_bias[l], gdn_norm_g[l], lru_conv_w[l], lru_conv_b[l], lru_w_r[l], lru_b_r[l],
                           lru_w_i[l], lru_b_i[l], lru_lambda[l], w_branch[l], w_out[l])
        h = layer_norm(DN_ALPHA * h + mix, ln1_g[l], ln1_b[l])
        ffn = moe(h, router_w[l], router_b[l], exp_w_gu[l], exp_w_down[l], sh_w_gu[l], sh_w_down[l])
        ple = jax.nn.sigmoid(h @ ple_w_g[l] + ple_b_g[l]) * (p[l].astype(h.dtype) @ ple_w_e[l])
        h = layer_norm(DN_ALPHA * h + ffn + ple, ln2_g[l], ln2_b[l])
    return h
```

```python
import functools
import math

import numpy as np
import jax
import jax.numpy as jnp
from jax import lax
from jax.experimental import pallas as pl
from jax.experimental.pallas import tpu as pltpu

F32 = jnp.float32
BF16 = jnp.bfloat16

HEAD_DIM = 128
N_HEADS = 4
LRU_BLOCKS = 8
LRU_C = 8.0
CONV_WIDTH = 4
N_BRANCHES = 3
ROPE_BASE = 10000.0
N_GROUPS = 8
TOPK_GROUPS = 4
TOP_K = 8
ROUTED_SCALE = 2.5
LN_EPS = 1e-5
GDN_CHUNK = 64
SEQ_TILE = 256
ROW_TILE = 256
MOE_TILE = 1024
CARRY_ROWS = 8
VMEM_LIMIT_BYTES = 56 * 1024 * 1024
NEG_INF = float("-inf")


def _const_spec(shape):
    nd = len(shape)
    return pl.BlockSpec(shape, lambda *_: (0,) * nd, pipeline_mode=pl.Buffered(1))


def _params(*sem):
    return pltpu.CompilerParams(dimension_semantics=sem, vmem_limit_bytes=VMEM_LIMIT_BYTES)


def _layer_norm(x, g, b):
    mu = jnp.mean(x, axis=-1, keepdims=True)
    xc = x - mu
    var = jnp.mean(xc * xc, axis=-1, keepdims=True)
    return xc * lax.rsqrt(var + LN_EPS) * g + b


def _sigmoid(x):
    return 1.0 / (1.0 + jnp.exp(-x))


def _silu(x):
    return x * _sigmoid(x)


def _softplus(x):
    return jnp.maximum(x, 0.0) + jnp.log1p(jnp.exp(-jnp.abs(x)))


def _dot(a, b):
    return jnp.dot(a, b, preferred_element_type=F32)


def _dot_nt(a, b):
    return lax.dot_general(a, b, (((1,), (1,)), ((), ())), preferred_element_type=F32)


def _dot_tn(a, b):
    return lax.dot_general(a, b, (((0,), (0,)), ((), ())), preferred_element_type=F32)


def _dot_f32(a, b):
    return jnp.dot(a, b, preferred_element_type=F32, precision=lax.Precision.HIGHEST)


def _ln_kernel(x_ref, g_ref, b_ref, o_ref):
    o_ref[...] = _layer_norm(x_ref[...], g_ref[...], b_ref[...])


def _entry_norm(x2, g, b):
    n, d = x2.shape
    t = min(1024, n)
    return pl.pallas_call(
        _ln_kernel,
        grid=(n // t,),
        in_specs=[pl.BlockSpec((t, d), lambda i: (i, 0)), _const_spec((1, d)), _const_spec((1, d))],
        out_specs=pl.BlockSpec((t, d), lambda i: (i, 0)),
        out_shape=jax.ShapeDtypeStruct((n, d), F32),
        compiler_params=_params("parallel"),
        name="entry_norm",
    )(x2, g.reshape(1, d), b.reshape(1, d))


def _retention_kernel(h_ref, w_ref, b_ref, cos_ref, sin_ref, dmat_ref, qd_ref, kd_ref, ng_ref, nb_ref,
                      y_ref, state_ref, *, chunk_decay):
    @pl.when(pl.program_id(1) == 0)
    def _():
        state_ref[...] = jnp.zeros_like(state_ref)

    width = N_HEADS * HEAD_DIM
    hb = h_ref[0].astype(BF16)
    proj = _dot(hb, w_ref[...]) + b_ref[...]
    cos = cos_ref[...]
    sin = sin_ref[...]
    for hh in range(N_HEADS):
        lo = hh * HEAD_DIM
        q = proj[:, lo:lo + HEAD_DIM]
        k = proj[:, width + lo:width + lo + HEAD_DIM]
        v = proj[:, 2 * width + lo:2 * width + lo + HEAD_DIM]
        gate = proj[:, 3 * width + lo:3 * width + lo + HEAD_DIM]
        q = q * cos + pltpu.roll(q, HEAD_DIM // 2, axis=1) * sin
        k = (k * cos + pltpu.roll(k, HEAD_DIM // 2, axis=1) * sin) * (HEAD_DIM ** -0.5)
        state = state_ref[hh]
        vb = v.astype(BF16)
        scores = _dot_nt(q.astype(BF16), k.astype(BF16)) * dmat_ref[hh]
        o = _dot(scores.astype(BF16), vb) + _dot((q * qd_ref[hh]).astype(BF16), state.astype(BF16))
        state_ref[hh] = state * chunk_decay[hh] + _dot_tn((k * kd_ref[hh]).astype(BF16), vb)
        mu = jnp.mean(o, axis=-1, keepdims=True)
        oc = o - mu
        var = jnp.mean(oc * oc, axis=-1, keepdims=True)
        on = oc * lax.rsqrt(var + LN_EPS) * ng_ref[:, lo:lo + HEAD_DIM] + nb_ref[:, lo:lo + HEAD_DIM]
        y_ref[0, :, lo:lo + HEAD_DIM] = (_silu(gate) * on).astype(y_ref.dtype)


def _retention_tables(seq, tile):
    half = HEAD_DIM // 2
    inv_freq = ROPE_BASE ** (-np.linspace(0.0, 1.0, half))
    ang = np.arange(seq)[:, None] * inv_freq[None, :]
    cos = np.concatenate([np.cos(ang), np.cos(ang)], axis=1)
    sin = np.concatenate([-np.sin(ang), np.sin(ang)], axis=1)
    log_gamma = np.log1p(-np.exp2(-5.0 - np.arange(N_HEADS)))
    pos = np.arange(tile)
    diff = pos[:, None] - pos[None, :]
    dmat = np.where(diff >= 0, np.exp(log_gamma[:, None, None] * np.maximum(diff, 0)), 0.0)
    qd = np.exp(log_gamma[:, None] * (pos + 1.0))[:, :, None] * np.ones((1, 1, HEAD_DIM))
    kd = np.exp(log_gamma[:, None] * (tile - 1.0 - pos))[:, :, None] * np.ones((1, 1, HEAD_DIM))
    chunk_decay = tuple(float(c) for c in np.exp(log_gamma * tile))
    as32 = lambda a: jnp.asarray(a, F32)
    return as32(cos), as32(sin), as32(dmat), as32(qd), as32(kd), chunk_decay


def _retention_branch(h, w, b, norm_g, norm_b):
    bsz, seq, d = h.shape
    t = min(SEQ_TILE, seq)
    width = N_HEADS * HEAD_DIM
    cos, sin, dmat, qd, kd, chunk_decay = _retention_tables(seq, t)
    return pl.pallas_call(
        functools.partial(_retention_kernel, chunk_decay=chunk_decay),
        grid=(bsz, seq // t),
        in_specs=[
            pl.BlockSpec((1, t, d), lambda i, j: (i, j, 0)),
            _const_spec((d, 4 * width)),
            _const_spec((1, 4 * width)),
            pl.BlockSpec((t, HEAD_DIM), lambda i, j: (j, 0)),
            pl.BlockSpec((t, HEAD_DIM), lambda i, j: (j, 0)),
            _const_spec((N_HEADS, t, t)),
            _const_spec((N_HEADS, t, HEAD_DIM)),
            _const_spec((N_HEADS, t, HEAD_DIM)),
            _const_spec((1, width)),
            _const_spec((1, width)),
        ],
        out_specs=pl.BlockSpec((1, t, width), lambda i, j: (i, j, 0)),
        out_shape=jax.ShapeDtypeStruct((bsz, seq, width), BF16),
        scratch_shapes=[pltpu.VMEM((N_HEADS, HEAD_DIM, HEAD_DIM), F32)],
        compiler_params=_params("parallel", "arbitrary"),
        name="retention_branch",
    )(h, w, b.reshape(1, -1), cos, sin, dmat, qd, kd, norm_g.reshape(1, -1), norm_b.reshape(1, -1))


def _causal_conv(x, xs_ref, cw_ref):
    t = x.shape[0]
    xs_ref[CARRY_ROWS:, :] = x
    acc = None
    for j in range(CONV_WIDTH):
        start = CARRY_ROWS - (CONV_WIDTH - 1) + j
        term = xs_ref[start:start + t, :] * cw_ref[j:j + 1, :]
        acc = term if acc is None else acc + term
    xs_ref[0:CARRY_ROWS, :] = xs_ref[t:t + CARRY_ROWS, :]
    return acc


def _cumsum_rows(x):
    n = x.shape[0]
    row = lax.broadcasted_iota(jnp.int32, x.shape, 0)
    d = 1
    while d < n:
        x = x + jnp.where(row >= d, pltpu.roll(x, d, axis=0), 0.0)
        d *= 2
    return x


def _gdn_kernel(h_ref, w_ref, b_ref, ws_ref, bs_ref, cw_ref, alog_ref, dtb_ref, ng_ref,
                y_ref, xs_ref, state_ref):
    @pl.when(pl.program_id(1) == 0)
    def _():
        state_ref[...] = jnp.zeros_like(state_ref)
        xs_ref[0:CARRY_ROWS, :] = jnp.zeros((CARRY_ROWS, xs_ref.shape[1]), F32)

    width = N_HEADS * HEAD_DIM
    c = GDN_CHUNK
    hb = h_ref[0].astype(BF16)
    t = hb.shape[0]
    proj = _dot(hb, w_ref[...]) + b_ref[...]
    small = _dot(hb, ws_ref[...]) + bs_ref[...]
    qkv = _silu(_causal_conv(proj[:, :3 * width], xs_ref, cw_ref))
    beta_all = _sigmoid(small)
    la_all = -jnp.exp(alog_ref[...]) * _softplus(small + dtb_ref[...])

    ri = lax.broadcasted_iota(jnp.int32, (c, c), 0)
    ci = lax.broadcasted_iota(jnp.int32, (c, c), 1)
    lower = ri >= ci
    strict = ri > ci
    eye = (ri == ci).astype(F32)

    for n in range(t // c):
        r0 = n * c
        la_c = la_all[r0:r0 + c, :]
        gc_c = _cumsum_rows(la_c)
        for hh in range(N_HEADS):
            lo = hh * HEAD_DIM
            q = qkv[r0:r0 + c, lo:lo + HEAD_DIM]
            k = qkv[r0:r0 + c, width + lo:width + lo + HEAD_DIM]
            v = qkv[r0:r0 + c, 2 * width + lo:2 * width + lo + HEAD_DIM]
            q = q * lax.rsqrt(jnp.sum(q * q, axis=-1, keepdims=True) + 1e-6) * (HEAD_DIM ** -0.5)
            k = k * lax.rsqrt(jnp.sum(k * k, axis=-1, keepdims=True) + 1e-6)
            beta = beta_all[r0:r0 + c, hh:hh + 1]
            la = la_c[:, N_HEADS + hh:N_HEADS + hh + 1]
            gc = gc_c[:, N_HEADS + hh:N_HEADS + hh + 1]
            gc_row = jnp.sum(jnp.where(ri <= ci, jnp.broadcast_to(la, (c, c)), 0.0), axis=0, keepdims=True)
            gc_last = gc_row[:, c - 1:c]
            decay = jnp.where(lower, jnp.exp(jnp.where(lower, gc - gc_row, 0.0)), 0.0)
            kb = k * beta
            kbf = k.astype(BF16)
            a_neg = jnp.where(strict, -(_dot_nt(kb.astype(BF16), kbf) * decay), 0.0)
            t_inv = eye + a_neg
            pw = a_neg
            m = 2
            while m < c:
                pw = _dot_f32(pw, pw)
                t_inv = t_inv + _dot_f32(t_inv, pw)
                m *= 2
            rhs = jnp.concatenate([v * beta, kb * jnp.exp(gc)], axis=1)
            uw = _dot_f32(t_inv, rhs)
            u = uw[:, :HEAD_DIM]
            w = uw[:, HEAD_DIM:]
            qk = _dot_nt(q.astype(BF16), kbf) * decay
            q_dec = q * jnp.exp(gc)
            k_dec = k * jnp.exp(gc_last - gc)
            state = state_ref[hh]
            sb = state.astype(BF16)
            v_new = u - _dot(w.astype(BF16), sb)
            vnb = v_new.astype(BF16)
            o = _dot(q_dec.astype(BF16), sb) + _dot(qk.astype(BF16), vnb)
            state_ref[hh] = state * jnp.exp(gc_last) + _dot_tn(k_dec.astype(BF16), vnb)
            o = o * lax.rsqrt(jnp.mean(o * o, axis=-1, keepdims=True) + 1e-6) * ng_ref[...]
            og = proj[r0:r0 + c, 3 * width + lo:3 * width + lo + HEAD_DIM]
            y_ref[0, r0:r0 + c, lo:lo + HEAD_DIM] = (o * _silu(og)).astype(y_ref.dtype)


def _gdn_branch(h, w, b, w_small, b_small, conv_w, a_log, dt_bias, norm_g):
    bsz, seq, d = h.shape
    t = min(SEQ_TILE, seq)
    width = N_HEADS * HEAD_DIM
    lanes = w_small.shape[1]
    pad_row = lambda vec: jnp.zeros((1, lanes), F32).at[0, N_HEADS:2 * N_HEADS].set(vec.astype(F32))
    return pl.pallas_call(
        _gdn_kernel,
        grid=(bsz, seq // t),
        in_specs=[
            pl.BlockSpec((1, t, d), lambda i, j: (i, j, 0)),
            _const_spec((d, 4 * width)),
            _const_spec((1, 4 * width)),
            _const_spec((d, lanes)),
            _const_spec((1, lanes)),
            _const_spec((CONV_WIDTH, 3 * width)),
            _const_spec((1, lanes)),
            _const_spec((1, lanes)),
            _const_spec((1, HEAD_DIM)),
        ],
        out_specs=pl.BlockSpec((1, t, width), lambda i, j: (i, j, 0)),
        out_shape=jax.ShapeDtypeStruct((bsz, seq, width), BF16),
        scratch_shapes=[pltpu.VMEM((CARRY_ROWS + t, 3 * width), F32),
                        pltpu.VMEM((N_HEADS, HEAD_DIM, HEAD_DIM), F32)],
        compiler_params=_params("parallel", "arbitrary"),
        name="gdn_branch",
    )(h, w, b.reshape(1, -1), w_small, b_small.reshape(1, -1), conv_w, pad_row(a_log), pad_row(dt_bias),
      norm_g.reshape(1, -1))


def _lru_kernel(h_ref, w_ref, b_ref, cw_ref, cb_ref, wr_ref, br_ref, wi_ref, bi_ref, lam_ref,
                y_ref, xs_ref, carry_ref):
    @pl.when(pl.program_id(1) == 0)
    def _():
        carry_ref[...] = jnp.zeros_like(carry_ref)
        xs_ref[0:CARRY_ROWS, :] = jnp.zeros((CARRY_ROWS, xs_ref.shape[1]), F32)

    width = cw_ref.shape[1]
    hb = h_ref[0].astype(BF16)
    t = hb.shape[0]
    proj = _dot(hb, w_ref[...]) + b_ref[...]
    xc = _causal_conv(proj[:, :width], xs_ref, cw_ref) + cb_ref[...]
    xcb = xc.astype(BF16)
    r = _sigmoid(_dot(xcb, wr_ref[...]) + br_ref[...])
    gi = _sigmoid(_dot(xcb, wi_ref[...]) + bi_ref[...])
    log_a = -LRU_C * r * _softplus(-lam_ref[...])
    a = jnp.exp(log_a)
    th = jnp.tanh(log_a)
    hs = jnp.sqrt(-2.0 * th / (1.0 - th)) * (gi * xc)
    row = lax.broadcasted_iota(jnp.int32, (t, width), 0)
    d = 1
    while d < t:
        keep = row >= d
        hs = hs + a * jnp.where(keep, pltpu.roll(hs, d, axis=0), 0.0)
        a = a * jnp.where(keep, pltpu.roll(a, d, axis=0), 1.0)
        d *= 2
    hs = hs + a * carry_ref[...]
    carry_ref[...] = hs[t - 1:t, :]
    y_ref[0] = (jax.nn.gelu(proj[:, width:], approximate=True) * hs).astype(y_ref.dtype)


def _lru_branch(h, w, b, conv_w, conv_b, w_r, b_r, w_i, b_i, lam):
    bsz, seq, d = h.shape
    t = min(SEQ_TILE, seq)
    width = conv_w.shape[1]
    row = lambda vec: vec.reshape(1, -1)
    return pl.pallas_call(
        _lru_kernel,
        grid=(bsz, seq // t),
        in_specs=[
            pl.BlockSpec((1, t, d), lambda i, j: (i, j, 0)),
            _const_spec((d, 2 * width)),
            _const_spec((1, 2 * width)),
            _const_spec((CONV_WIDTH, width)),
            _const_spec((1, width)),
            _const_spec((width, width)),
            _const_spec((1, width)),
            _const_spec((width, width)),
            _const_spec((1, width)),
            _const_spec((1, width)),
        ],
        out_specs=pl.BlockSpec((1, t, width), lambda i, j: (i, j, 0)),
        out_shape=jax.ShapeDtypeStruct((bsz, seq, width), BF16),
        scratch_shapes=[pltpu.VMEM((CARRY_ROWS + t, width), F32), pltpu.VMEM((1, width), F32)],
        compiler_params=_params("parallel", "arbitrary"),
        name="lru_branch",
    )(h, w, row(b), conv_w, row(conv_b), w_r, row(b_r), w_i, row(b_i), row(lam))


def _first_index_of_max(x, idx, size):
    m = jnp.max(x, axis=0, keepdims=True)
    first = jnp.min(jnp.where(x == m, idx, size), axis=0, keepdims=True)
    return m, idx == first


def _route(logits_t, bias_col):
    n_exp, t = logits_t.shape
    per_group = n_exp // N_GROUPS
    scores = _sigmoid(logits_t)
    sel = scores + bias_col
    idx_g = lax.broadcasted_iota(jnp.int32, (per_group, t), 0)
    group_scores = []
    for g in range(N_GROUPS):
        x = sel[g * per_group:(g + 1) * per_group, :]
        m1, hit = _first_index_of_max(x, idx_g, per_group)
        m2 = jnp.max(jnp.where(hit, NEG_INF, x), axis=0, keepdims=True)
        group_scores.append(m1 + m2)
    gsc = jnp.concatenate(group_scores, axis=0)
    idx_n = lax.broadcasted_iota(jnp.int32, (N_GROUPS, t), 0)
    gmask = jnp.zeros((N_GROUPS, t), F32)
    for _ in range(TOPK_GROUPS):
        _, hit = _first_index_of_max(gsc, idx_n, N_GROUPS)
        gmask = jnp.where(hit, 1.0, gmask)
        gsc = jnp.where(hit, NEG_INF, gsc)
    emask = jnp.concatenate([jnp.broadcast_to(gmask[g:g + 1, :], (per_group, t)) for g in range(N_GROUPS)], axis=0)
    cand = jnp.where(emask > 0.0, sel, NEG_INF)
    idx_e = lax.broadcasted_iota(jnp.int32, (n_exp, t), 0)
    picked = jnp.zeros((n_exp, t), F32)
    for _ in range(TOP_K):
        _, hit = _first_index_of_max(cand, idx_e, n_exp)
        picked = jnp.where(hit, 1.0, picked)
        cand = jnp.where(hit, NEG_INF, cand)
    gw = jnp.where(picked > 0.0, scores, 0.0)
    return gw / jnp.sum(gw, axis=0, keepdims=True) * ROUTED_SCALE


def _merge_kernel(h_ref, yr_ref, yg_ref, yl_ref, wmg_ref, bmg_ref, wbr_ref, wout_ref, g_ref, b_ref,
                  rwt_ref, rb_ref, eye_ref, h1_ref, h1b_ref, gate_ref, *, alpha):
    h = h_ref[...]
    d = h.shape[1]
    hb = h.astype(BF16)
    gates = _sigmoid(_dot(hb, wmg_ref[...]) + bmg_ref[...])
    mixed = None
    for n, y_ref in enumerate((yr_ref, yg_ref, yl_ref)):
        term = gates[:, n * d:(n + 1) * d] * _dot(y_ref[...], wbr_ref[n])
        mixed = term if mixed is None else mixed + term
    mix = _dot(mixed.astype(BF16), wout_ref[...])
    h1 = _layer_norm(alpha * h + mix, g_ref[...], b_ref[...])
    h1b = h1.astype(BF16)
    h1_ref[...] = h1
    h1b_ref[...] = h1b
    combine_t = _route(_dot_nt(rwt_ref[...], h1b), rb_ref[...])
    gate_ref[...] = lax.dot_general(eye_ref[...], combine_t, (((1,), (1,)), ((), ())),
                                    preferred_element_type=F32, precision=lax.Precision.HIGHEST)


def _merge(h2, y_ret, y_gdn, y_lru, w_mg, b_mg, w_branch, w_out, ln_g, ln_b, router_wt, router_b, alpha):
    n, d = h2.shape
    t = min(ROW_TILE, n)
    width = y_ret.shape[1]
    n_exp = router_wt.shape[0]
    tile = lambda cols: pl.BlockSpec((t, cols), lambda i: (i, 0))
    return pl.pallas_call(
        functools.partial(_merge_kernel, alpha=alpha),
        grid=(n // t,),
        in_specs=[
            tile(d), tile(width), tile(width), tile(width),
            _const_spec((d, N_BRANCHES * d)),
            _const_spec((1, N_BRANCHES * d)),
            _const_spec((N_BRANCHES, width, d)),
            _const_spec((d, d)),
            _const_spec((1, d)),
            _const_spec((1, d)),
            _const_spec((n_exp, d)),
            _const_spec((n_exp, 1)),
            _const_spec((t, t)),
        ],
        out_specs=[tile(d), tile(d), tile(n_exp)],
        out_shape=[jax.ShapeDtypeStruct((n, d), F32), jax.ShapeDtypeStruct((n, d), BF16),
                   jax.ShapeDtypeStruct((n, n_exp), F32)],
        compiler_params=_params("parallel"),
        name="merge_route",
    )(h2, y_ret, y_gdn, y_lru, w_mg, b_mg.reshape(1, -1), w_branch, w_out, ln_g.reshape(1, -1),
      ln_b.reshape(1, -1), router_wt, router_b.reshape(-1, 1), jnp.eye(t, dtype=F32))


def _moe_kernel(x_ref, gate_ref, wgu_ref, wd_ref, o_ref, acc_ref):
    e = pl.program_id(1)

    @pl.when(e == 0)
    def _():
        acc_ref[...] = jnp.zeros_like(acc_ref)

    ff = wd_ref.shape[1]
    gu = _dot(x_ref[...], wgu_ref[0])
    lane = lax.broadcasted_iota(jnp.int32, gate_ref.shape, 1)
    wt = jnp.sum(jnp.where(lane == e, gate_ref[...], 0.0), axis=1, keepdims=True)
    mid = _silu(gu[:, :ff]) * gu[:, ff:] * wt
    acc_ref[...] += _dot(mid.astype(BF16), wd_ref[0])

    @pl.when(e == pl.num_programs(1) - 1)
    def _():
        o_ref[...] = acc_ref[...]


def _routed_experts(xb, gate, w_gu, w_down):
    n, d = xb.shape
    n_exp, _, ff2 = w_gu.shape
    t = min(MOE_TILE, n)
    return pl.pallas_call(
        _moe_kernel,
        grid=(n // t, n_exp),
        in_specs=[
            pl.BlockSpec((t, d), lambda i, e: (i, 0)),
            pl.BlockSpec((t, n_exp), lambda i, e: (i, 0)),
            pl.BlockSpec((1, d, ff2), lambda i, e: (e, 0, 0)),
            pl.BlockSpec((1, ff2 // 2, d), lambda i, e: (e, 0, 0)),
        ],
        out_specs=pl.BlockSpec((t, d), lambda i, e: (i, 0)),
        out_shape=jax.ShapeDtypeStruct((n, d), F32),
        scratch_shapes=[pltpu.VMEM((t, d), F32)],
        compiler_params=_params("parallel", "arbitrary"),
        name="routed_experts",
    )(xb, gate, w_gu, w_down)


def _final_kernel(h1_ref, h1b_ref, routed_ref, p_ref, wgu_ref, wd_ref, wpg_ref, bpg_ref, wpe_ref, g_ref, b_ref,
                  o_ref, *, alpha):
    xb = h1b_ref[...]
    ff = wd_ref.shape[0]
    gu = _dot(xb, wgu_ref[...])
    shared = _dot((_silu(gu[:, :ff]) * gu[:, ff:]).astype(BF16), wd_ref[...])
    ple = _sigmoid(_dot(xb, wpg_ref[...]) + bpg_ref[...]) * _dot(p_ref[...].astype(BF16), wpe_ref[...])
    o_ref[...] = _layer_norm(alpha * h1_ref[...] + (routed_ref[...] + shared) + ple, g_ref[...], b_ref[...])


def _final(h1, h1b, routed, p2, sh_w_gu, sh_w_down, ple_w_g, ple_b_g, ple_w_e, ln_g, ln_b, alpha):
    n, d = h1.shape
    t = min(ROW_TILE, n)
    pdim = p2.shape[1]
    ff2 = sh_w_gu.shape[1]
    tile = lambda cols: pl.BlockSpec((t, cols), lambda i: (i, 0))
    return pl.pallas_call(
        functools.partial(_final_kernel, alpha=alpha),
        grid=(n // t,),
        in_specs=[
            tile(d), tile(d), tile(d), tile(pdim),
            _const_spec((d, ff2)),
            _const_spec((ff2 // 2, d)),
            _const_spec((d, d)),
            _const_spec((1, d)),
            _const_spec((pdim, d)),
            _const_spec((1, d)),
            _const_spec((1, d)),
        ],
        out_specs=tile(d),
        out_shape=jax.ShapeDtypeStruct((n, d), F32),
        compiler_params=_params("parallel"),
        name="shared_ple_norm",
    )(h1, h1b, routed, p2, sh_w_gu, sh_w_down, ple_w_g, ple_b_g.reshape(1, -1), ple_w_e, ln_g.reshape(1, -1),
      ln_b.reshape(1, -1))


def _block_diag(w):
    g, i, j = w.shape
    eye = jnp.eye(g, dtype=w.dtype)
    return (eye[:, None, :, None] * w[:, :, None, :]).reshape(g * i, g * j)


def kernel(x, p, ln_in_g, ln_in_b, w_in, b_in, ret_norm_g, ret_norm_b, gdn_conv_w, gdn_a_log, gdn_dt_bias, gdn_norm_g, lru_conv_w, lru_conv_b, lru_w_r, lru_b_r, lru_w_i, lru_b_i, lru_lambda, w_branch, w_out, ln1_g, ln1_b, router_w, router_b, exp_w_gu, exp_w_down, sh_w_gu, sh_w_down, ple_w_e, ple_w_g, ple_b_g, ln2_g, ln2_b):
    bsz, seq, d = x.shape
    depth = w_in.shape[0]
    n = bsz * seq
    width = N_HEADS * HEAD_DIM
    alpha = (2 * depth) ** 0.25
    lanes = 128
    o_ret = 0
    o_gdn = o_ret + 4 * width
    o_small = o_gdn + 4 * width
    o_lru = o_small + 2 * N_HEADS
    o_mg = o_lru + 2 * width

    h = _entry_norm(x.reshape(n, d), ln_in_g, ln_in_b)
    for l in range(depth):
        wl, bl = w_in[l], b_in[l]
        bf = lambda a: a.astype(BF16)
        w_small = jnp.zeros((d, lanes), F32).at[:, :2 * N_HEADS].set(wl[:, o_small:o_lru])
        b_small = jnp.zeros((lanes,), F32).at[:2 * N_HEADS].set(bl[o_small:o_lru])
        h3 = h.reshape(bsz, seq, d)
        y_ret = _retention_branch(h3, bf(wl[:, o_ret:o_gdn]), bl[o_ret:o_gdn], ret_norm_g[l], ret_norm_b[l])
        y_gdn = _gdn_branch(h3, bf(wl[:, o_gdn:o_small]), bl[o_gdn:o_small], bf(w_small), b_small,
                            gdn_conv_w[l], gdn_a_log[l], gdn_dt_bias[l], gdn_norm_g[l])
        y_lru = _lru_branch(h3, bf(wl[:, o_lru:o_mg]), bl[o_lru:o_mg], lru_conv_w[l], lru_conv_b[l],
                            bf(_block_diag(lru_w_r[l])), lru_b_r[l], bf(_block_diag(lru_w_i[l])), lru_b_i[l],
                            lru_lambda[l])
        h1, h1b, gate = _merge(h, y_ret.reshape(n, width), y_gdn.reshape(n, width), y_lru.reshape(n, width),
                               bf(wl[:, o_mg:]), bl[o_mg:], bf(w_branch[l]), bf(w_out[l]), ln1_g[l], ln1_b[l],
                               bf(router_w[l].T), router_b[l], alpha)
        routed = _routed_experts(h1b, gate, bf(exp_w_gu[l]), bf(exp_w_down[l]))
        h = _final(h1, h1b, routed, p[l].reshape(n, -1), bf(sh_w_gu[l]), bf(sh_w_down[l]), bf(ple_w_g[l]),
                   ple_b_g[l], bf(ple_w_e[l]), ln2_g[l], ln2_b[l], alpha)
    return h.reshape(bsz, seq, d)
```

```python
import functools
import math

import numpy as np
import jax
import jax.numpy as jnp
from jax import lax
from jax.experimental import pallas as pl
from jax.experimental.pallas import tpu as pltpu

F32 = jnp.float32
BF16 = jnp.bfloat16

HEAD_DIM = 128
N_HEADS = 4
LRU_BLOCKS = 8
LRU_C = 8.0
CONV_WIDTH = 4
N_BRANCHES = 3
ROPE_BASE = 10000.0
N_GROUPS = 8
TOPK_GROUPS = 4
TOP_K = 8
ROUTED_SCALE = 2.5
LN_EPS = 1e-5
GDN_CHUNK = 64
SEQ_TILE = 256
ROW_TILE = 256
MOE_TILE = 1024
CARRY_ROWS = 8
VMEM_LIMIT_BYTES = 56 * 1024 * 1024
NEG_INF = float("-inf")


def _const_spec(shape):
    nd = len(shape)
    return pl.BlockSpec(shape, lambda *_: (0,) * nd, pipeline_mode=pl.Buffered(1))


def _params(*sem):
    return pltpu.CompilerParams(dimension_semantics=sem, vmem_limit_bytes=VMEM_LIMIT_BYTES)


def _layer_norm(x, g, b):
    mu = jnp.mean(x, axis=-1, keepdims=True)
    xc = x - mu
    var = jnp.mean(xc * xc, axis=-1, keepdims=True)
    return xc * lax.rsqrt(var + LN_EPS) * g + b


def _sigmoid(x):
    return 1.0 / (1.0 + jnp.exp(-x))


def _silu(x):
    return x * _sigmoid(x)


def _softplus(x):
    return jnp.maximum(x, 0.0) + jnp.log1p(jnp.exp(-jnp.abs(x)))


def _dot(a, b):
    return jnp.dot(a, b, preferred_element_type=F32)


def _dot_nt(a, b):
    return lax.dot_general(a, b, (((1,), (1,)), ((), ())), preferred_element_type=F32)


def _dot_tn(a, b):
    return lax.dot_general(a, b, (((0,), (0,)), ((), ())), preferred_element_type=F32)


def _dot_f32(a, b):
    return jnp.dot(a, b, preferred_element_type=F32, precision=lax.Precision.HIGHEST)


def _ln_kernel(x_ref, g_ref, b_ref, o_ref):
    o_ref[...] = _layer_norm(x_ref[...], g_ref[...], b_ref[...])


def _entry_norm(x2, g, b):
    n, d = x2.shape
    t = min(1024, n)
    return pl.pallas_call(
        _ln_kernel,
        grid=(n // t,),
        in_specs=[pl.BlockSpec((t, d), lambda i: (i, 0)), _const_spec((1, d)), _const_spec((1, d))],
        out_specs=pl.BlockSpec((t, d), lambda i: (i, 0)),
        out_shape=jax.ShapeDtypeStruct((n, d), F32),
        compiler_params=_params("parallel"),
        name="entry_norm",
    )(x2, g.reshape(1, d), b.reshape(1, d))


def _retention_kernel(h_ref, w_ref, b_ref, cos_ref, sin_ref, dmat_ref, qd_ref, kd_ref, ng_ref, nb_ref,
                      y_ref, state_ref, *, chunk_decay):
    @pl.when(pl.program_id(1) == 0)
    def _():
        state_ref[...] = jnp.zeros_like(state_ref)

    width = N_HEADS * HEAD_DIM
    hb = h_ref[0].astype(BF16)
    proj = _dot(hb, w_ref[...]) + b_ref[...]
    cos = cos_ref[...]
    sin = sin_ref[...]
    heads = range(N_HEADS)
    qs, ks, vbs = [], [], []
    for hh in heads:
        lo = hh * HEAD_DIM
        q = proj[:, lo:lo + HEAD_DIM]
        k = proj[:, width + lo:width + lo + HEAD_DIM]
        qs.append(q * cos + pltpu.roll(q, HEAD_DIM // 2, axis=1) * sin)
        ks.append((k * cos + pltpu.roll(k, HEAD_DIM // 2, axis=1) * sin) * (HEAD_DIM ** -0.5))
        vbs.append(proj[:, 2 * width + lo:2 * width + lo + HEAD_DIM].astype(BF16))
    states = [state_ref[hh] for hh in heads]
    scores = [(_dot_nt(qs[hh].astype(BF16), ks[hh].astype(BF16)) * dmat_ref[hh]).astype(BF16) for hh in heads]
    inter = [_dot((qs[hh] * qd_ref[hh]).astype(BF16), states[hh].astype(BF16)) for hh in heads]
    for hh in heads:
        state_ref[hh] = states[hh] * chunk_decay[hh] + _dot_tn((ks[hh] * kd_ref[hh]).astype(BF16), vbs[hh])
    outs = [_dot(scores[hh], vbs[hh]) + inter[hh] for hh in heads]
    for hh in heads:
        lo = hh * HEAD_DIM
        o = outs[hh]
        gate = proj[:, 3 * width + lo:3 * width + lo + HEAD_DIM]
        mu = jnp.mean(o, axis=-1, keepdims=True)
        oc = o - mu
        var = jnp.mean(oc * oc, axis=-1, keepdims=True)
        on = oc * lax.rsqrt(var + LN_EPS) * ng_ref[:, lo:lo + HEAD_DIM] + nb_ref[:, lo:lo + HEAD_DIM]
        y_ref[0, :, lo:lo + HEAD_DIM] = (_silu(gate) * on).astype(y_ref.dtype)


def _retention_tables(seq, tile):
    half = HEAD_DIM // 2
    inv_freq = ROPE_BASE ** (-np.linspace(0.0, 1.0, half))
    ang = np.arange(seq)[:, None] * inv_freq[None, :]
    cos = np.concatenate([np.cos(ang), np.cos(ang)], axis=1)
    sin = np.concatenate([-np.sin(ang), np.sin(ang)], axis=1)
    log_gamma = np.log1p(-np.exp2(-5.0 - np.arange(N_HEADS)))
    pos = np.arange(tile)
    diff = pos[:, None] - pos[None, :]
    dmat = np.where(diff >= 0, np.exp(log_gamma[:, None, None] * np.maximum(diff, 0)), 0.0)
    qd = np.exp(log_gamma[:, None] * (pos + 1.0))[:, :, None] * np.ones((1, 1, HEAD_DIM))
    kd = np.exp(log_gamma[:, None] * (tile - 1.0 - pos))[:, :, None] * np.ones((1, 1, HEAD_DIM))
    chunk_decay = tuple(float(c) for c in np.exp(log_gamma * tile))
    as32 = lambda a: jnp.asarray(a, F32)
    return as32(cos), as32(sin), as32(dmat), as32(qd), as32(kd), chunk_decay


def _retention_branch(h, w, b, norm_g, norm_b):
    bsz, seq, d = h.shape
    t = min(SEQ_TILE, seq)
    width = N_HEADS * HEAD_DIM
    cos, sin, dmat, qd, kd, chunk_decay = _retention_tables(seq, t)
    return pl.pallas_call(
        functools.partial(_retention_kernel, chunk_decay=chunk_decay),
        grid=(bsz, seq // t),
        in_specs=[
            pl.BlockSpec((1, t, d), lambda i, j: (i, j, 0)),
            _const_spec((d, 4 * width)),
            _const_spec((1, 4 * width)),
            pl.BlockSpec((t, HEAD_DIM), lambda i, j: (j, 0)),
            pl.BlockSpec((t, HEAD_DIM), lambda i, j: (j, 0)),
            _const_spec((N_HEADS, t, t)),
            _const_spec((N_HEADS, t, HEAD_DIM)),
            _const_spec((N_HEADS, t, HEAD_DIM)),
            _const_spec((1, width)),
            _const_spec((1, width)),
        ],
        out_specs=pl.BlockSpec((1, t, width), lambda i, j: (i, j, 0)),
        out_shape=jax.ShapeDtypeStruct((bsz, seq, width), BF16),
        scratch_shapes=[pltpu.VMEM((N_HEADS, HEAD_DIM, HEAD_DIM), F32)],
        compiler_params=_params("parallel", "arbitrary"),
        name="retention_branch",
    )(h, w, b.reshape(1, -1), cos, sin, dmat, qd, kd, norm_g.reshape(1, -1), norm_b.reshape(1, -1))


def _causal_conv(x, xs_ref, cw_ref):
    t = x.shape[0]
    xs_ref[CARRY_ROWS:, :] = x
    acc = None
    for j in range(CONV_WIDTH):
        start = CARRY_ROWS - (CONV_WIDTH - 1) + j
        term = xs_ref[start:start + t, :] * cw_ref[j:j + 1, :]
        acc = term if acc is None else acc + term
    xs_ref[0:CARRY_ROWS, :] = xs_ref[t:t + CARRY_ROWS, :]
    return acc


def _cumsum_rows(x):
    n = x.shape[0]
    row = lax.broadcasted_iota(jnp.int32, x.shape, 0)
    d = 1
    while d < n:
        x = x + jnp.where(row >= d, pltpu.roll(x, d, axis=0), 0.0)
        d *= 2
    return x


def _gdn_kernel(h_ref, w_ref, b_ref, ws_ref, bs_ref, cw_ref, alog_ref, dtb_ref, ng_ref,
                y_ref, xs_ref, state_ref, u_ref, wf_ref, w_s_ref, qd_ref, kd_ref, qk_ref):
    @pl.when(pl.program_id(1) == 0)
    def _():
        state_ref[...] = jnp.zeros_like(state_ref)
        xs_ref[0:CARRY_ROWS, :] = jnp.zeros((CARRY_ROWS, xs_ref.shape[1]), F32)

    width = N_HEADS * HEAD_DIM
    c = GDN_CHUNK
    hb = h_ref[0].astype(BF16)
    t = hb.shape[0]
    proj = _dot(hb, w_ref[...]) + b_ref[...]
    small = _dot(hb, ws_ref[...]) + bs_ref[...]
    qkv = _silu(_causal_conv(proj[:, :3 * width], xs_ref, cw_ref))
    beta_all = _sigmoid(small)
    la_all = -jnp.exp(alog_ref[...]) * _softplus(small + dtb_ref[...])

    ri = lax.broadcasted_iota(jnp.int32, (c, c), 0)
    ci = lax.broadcasted_iota(jnp.int32, (c, c), 1)
    lower = ri >= ci
    strict = ri > ci

    items = [(n, hh) for n in range(t // c) for hh in range(N_HEADS)]
    gcs = {}
    for n in range(t // c):
        la_c = la_all[n * c:(n + 1) * c, :]
        gc_c = _cumsum_rows(la_c)
        gcs[n] = (la_c, gc_c, jnp.exp(gc_c))
    g_last, pws, rems = {}, {}, {}
    for n, hh in items:
        r0, lo = n * c, hh * HEAD_DIM
        rows, cols = slice(r0, r0 + c), slice(lo, lo + HEAD_DIM)
        la_c, gc_c, egc_c = gcs[n]
        q = qkv[rows, lo:lo + HEAD_DIM]
        k = qkv[rows, width + lo:width + lo + HEAD_DIM]
        v = qkv[rows, 2 * width + lo:2 * width + lo + HEAD_DIM]
        q = q * lax.rsqrt(jnp.sum(q * q, axis=-1, keepdims=True) + 1e-6) * (HEAD_DIM ** -0.5)
        k = k * lax.rsqrt(jnp.sum(k * k, axis=-1, keepdims=True) + 1e-6)
        beta = beta_all[rows, hh:hh + 1]
        la = la_c[:, N_HEADS + hh:N_HEADS + hh + 1]
        gc = gc_c[:, N_HEADS + hh:N_HEADS + hh + 1]
        egc = egc_c[:, N_HEADS + hh:N_HEADS + hh + 1]
        gc_row = jnp.sum(jnp.where(ri <= ci, jnp.broadcast_to(la, (c, c)), 0.0), axis=0, keepdims=True)
        gc_last = gc_row[:, c - 1:c]
        decay = jnp.where(lower, jnp.exp(jnp.where(lower, gc - gc_row, 0.0)), 0.0)
        kb = k * beta
        kbf = k.astype(BF16)
        a_neg = jnp.where(strict, -(_dot_nt(kb.astype(BF16), kbf) * decay), 0.0)
        pws[n, hh] = a_neg
        rems[n, hh] = a_neg
        u_ref[rows, cols] = v * beta
        wf_ref[rows, cols] = kb * egc
        qk_ref[hh, rows, :] = (_dot_nt(q.astype(BF16), kbf) * decay).astype(BF16)
        qd_ref[rows, cols] = (q * egc).astype(BF16)
        kd_ref[rows, cols] = (k * jnp.exp(gc_last - gc)).astype(BF16)
        g_last[n, hh] = jnp.exp(gc_last)
    m = 2
    while m < c:
        for it in items:
            pwb = pws[it].astype(BF16)
            pws[it] = _dot(pwb, pwb)
        for it in items:
            rems[it] = rems[it] + pws[it] + _dot(rems[it].astype(BF16), pws[it].astype(BF16))
        m *= 2
    for n, hh in items:
        rows, cols = slice(n * c, (n + 1) * c), slice(hh * HEAD_DIM, (hh + 1) * HEAD_DIM)
        remb = rems[n, hh].astype(BF16)
        u_ref[rows, cols] = u_ref[rows, cols] + _dot(remb, u_ref[rows, cols].astype(BF16))
        w_s_ref[rows, cols] = (wf_ref[rows, cols] + _dot(remb, wf_ref[rows, cols].astype(BF16))).astype(BF16)

    heads = range(N_HEADS)
    for n in range(t // c):
        rows = slice(n * c, (n + 1) * c)
        cols = [slice(hh * HEAD_DIM, (hh + 1) * HEAD_DIM) for hh in heads]
        states = [state_ref[hh] for hh in heads]
        sbs = [s.astype(BF16) for s in states]
        vnbs = [(u_ref[rows, cols[hh]] - _dot(w_s_ref[rows, cols[hh]], sbs[hh])).astype(BF16) for hh in heads]
        outs = [_dot(qd_ref[rows, cols[hh]], sbs[hh]) + _dot(qk_ref[hh, rows, :], vnbs[hh]) for hh in heads]
        for hh in heads:
            state_ref[hh] = states[hh] * g_last[n, hh] + _dot_tn(kd_ref[rows, cols[hh]], vnbs[hh])
        for hh in heads:
            o = outs[hh]
            o = o * lax.rsqrt(jnp.mean(o * o, axis=-1, keepdims=True) + 1e-6) * ng_ref[...]
            og = proj[rows, 3 * width + hh * HEAD_DIM:3 * width + (hh + 1) * HEAD_DIM]
            y_ref[0, rows, cols[hh]] = (o * _silu(og)).astype(y_ref.dtype)


def _gdn_branch(h, w, b, w_small, b_small, conv_w, a_log, dt_bias, norm_g):
    bsz, seq, d = h.shape
    t = min(SEQ_TILE, seq)
    width = N_HEADS * HEAD_DIM
    lanes = w_small.shape[1]
    pad_row = lambda vec: jnp.zeros((1, lanes), F32).at[0, N_HEADS:2 * N_HEADS].set(vec.astype(F32))
    return pl.pallas_call(
        _gdn_kernel,
        grid=(bsz, seq // t),
        in_specs=[
            pl.BlockSpec((1, t, d), lambda i, j: (i, j, 0)),
            _const_spec((d, 4 * width)),
            _const_spec((1, 4 * width)),
            _const_spec((d, lanes)),
            _const_spec((1, lanes)),
            _const_spec((CONV_WIDTH, 3 * width)),
            _const_spec((1, lanes)),
            _const_spec((1, lanes)),
            _const_spec((1, HEAD_DIM)),
        ],
        out_specs=pl.BlockSpec((1, t, width), lambda i, j: (i, j, 0)),
        out_shape=jax.ShapeDtypeStruct((bsz, seq, width), BF16),
        scratch_shapes=[pltpu.VMEM((CARRY_ROWS + t, 3 * width), F32),
                        pltpu.VMEM((N_HEADS, HEAD_DIM, HEAD_DIM), F32),
                        pltpu.VMEM((t, width), F32),
                        pltpu.VMEM((t, width), F32),
                        pltpu.VMEM((t, width), BF16),
                        pltpu.VMEM((t, width), BF16),
                        pltpu.VMEM((t, width), BF16),
                        pltpu.VMEM((N_HEADS, t, GDN_CHUNK), BF16)],
        compiler_params=_params("parallel", "arbitrary"),
        name="gdn_branch",
    )(h, w, b.reshape(1, -1), w_small, b_small.reshape(1, -1), conv_w, pad_row(a_log), pad_row(dt_bias),
      norm_g.reshape(1, -1))


def _lru_kernel(h_ref, w_ref, b_ref, cw_ref, cb_ref, wr_ref, br_ref, wi_ref, bi_ref, lam_ref,
                y_ref, xs_ref, carry_ref):
    @pl.when(pl.program_id(1) == 0)
    def _():
        carry_ref[...] = jnp.zeros_like(carry_ref)
        xs_ref[0:CARRY_ROWS, :] = jnp.zeros((CARRY_ROWS, xs_ref.shape[1]), F32)

    width = cw_ref.shape[1]
    hb = h_ref[0].astype(BF16)
    t = hb.shape[0]
    proj = _dot(hb, w_ref[...]) + b_ref[...]
    xc = _causal_conv(proj[:, :width], xs_ref, cw_ref) + cb_ref[...]
    xcb = xc.astype(BF16)
    r = _sigmoid(_dot(xcb, wr_ref[...]) + br_ref[...])
    gi = _sigmoid(_dot(xcb, wi_ref[...]) + bi_ref[...])
    log_a = -LRU_C * r * _softplus(-lam_ref[...])
    a = jnp.exp(log_a)
    th = jnp.tanh(log_a)
    hs = jnp.sqrt(-2.0 * th / (1.0 - th)) * (gi * xc)
    row = lax.broadcasted_iota(jnp.int32, (t, width), 0)
    d = 1
    while d < t:
        keep = row >= d
        hs = hs + a * jnp.where(keep, pltpu.roll(hs, d, axis=0), 0.0)
        a = a * jnp.where(keep, pltpu.roll(a, d, axis=0), 1.0)
        d *= 2
    hs = hs + a * carry_ref[...]
    carry_ref[...] = hs[t - 1:t, :]
    y_ref[0] = (jax.nn.gelu(proj[:, width:], approximate=True) * hs).astype(y_ref.dtype)


def _lru_branch(h, w, b, conv_w, conv_b, w_r, b_r, w_i, b_i, lam):
    bsz, seq, d = h.shape
    t = min(SEQ_TILE, seq)
    width = conv_w.shape[1]
    row = lambda vec: vec.reshape(1, -1)
    return pl.pallas_call(
        _lru_kernel,
        grid=(bsz, seq // t),
        in_specs=[
            pl.BlockSpec((1, t, d), lambda i, j: (i, j, 0)),
            _const_spec((d, 2 * width)),
            _const_spec((1, 2 * width)),
            _const_spec((CONV_WIDTH, width)),
            _const_spec((1, width)),
            _const_spec((width, width)),
            _const_spec((1, width)),
            _const_spec((width, width)),
            _const_spec((1, width)),
            _const_spec((1, width)),
        ],
        out_specs=pl.BlockSpec((1, t, width), lambda i, j: (i, j, 0)),
        out_shape=jax.ShapeDtypeStruct((bsz, seq, width), BF16),
        scratch_shapes=[pltpu.VMEM((CARRY_ROWS + t, width), F32), pltpu.VMEM((1, width), F32)],
        compiler_params=_params("parallel", "arbitrary"),
        name="lru_branch",
    )(h, w, row(b), conv_w, row(conv_b), w_r, row(b_r), w_i, row(b_i), row(lam))


def _first_index_of_max(x, idx, size):
    m = jnp.max(x, axis=0, keepdims=True)
    first = jnp.min(jnp.where(x == m, idx, size), axis=0, keepdims=True)
    return m, idx == first


def _route(logits_t, bias_col):
    n_exp, t = logits_t.shape
    per_group = n_exp // N_GROUPS
    scores = _sigmoid(logits_t)
    sel = scores + bias_col
    idx_g = lax.broadcasted_iota(jnp.int32, (per_group, t), 0)
    group_scores = []
    for g in range(N_GROUPS):
        x = sel[g * per_group:(g + 1) * per_group, :]
        m1, hit = _first_index_of_max(x, idx_g, per_group)
        m2 = jnp.max(jnp.where(hit, NEG_INF, x), axis=0, keepdims=True)
        group_scores.append(m1 + m2)
    gsc = jnp.concatenate(group_scores, axis=0)
    idx_n = lax.broadcasted_iota(jnp.int32, (N_GROUPS, t), 0)
    gmask = jnp.zeros((N_GROUPS, t), F32)
    for _ in range(TOPK_GROUPS):
        _, hit = _first_index_of_max(gsc, idx_n, N_GROUPS)
        gmask = jnp.where(hit, 1.0, gmask)
        gsc = jnp.where(hit, NEG_INF, gsc)
    emask = jnp.concatenate([jnp.broadcast_to(gmask[g:g + 1, :], (per_group, t)) for g in range(N_GROUPS)], axis=0)
    cand = jnp.where(emask > 0.0, sel, NEG_INF)
    idx_e = lax.broadcasted_iota(jnp.int32, (n_exp, t), 0)
    picked = jnp.zeros((n_exp, t), F32)
    for _ in range(TOP_K):
        _, hit = _first_index_of_max(cand, idx_e, n_exp)
        picked = jnp.where(hit, 1.0, picked)
        cand = jnp.where(hit, NEG_INF, cand)
    gw = jnp.where(picked > 0.0, scores, 0.0)
    return gw / jnp.sum(gw, axis=0, keepdims=True) * ROUTED_SCALE


def _merge_kernel(h_ref, yr_ref, yg_ref, yl_ref, wmg_ref, bmg_ref, wbr_ref, wout_ref, g_ref, b_ref,
                  rwt_ref, rb_ref, eye_ref, h1_ref, h1b_ref, gate_ref, *, alpha):
    h = h_ref[...]
    d = h.shape[1]
    hb = h.astype(BF16)
    gates = _sigmoid(_dot(hb, wmg_ref[...]) + bmg_ref[...])
    mixed = None
    for n, y_ref in enumerate((yr_ref, yg_ref, yl_ref)):
        term = gates[:, n * d:(n + 1) * d] * _dot(y_ref[...], wbr_ref[n])
        mixed = term if mixed is None else mixed + term
    mix = _dot(mixed.astype(BF16), wout_ref[...])
    h1 = _layer_norm(alpha * h + mix, g_ref[...], b_ref[...])
    h1b = h1.astype(BF16)
    h1_ref[...] = h1
    h1b_ref[...] = h1b
    combine_t = _route(_dot_nt(rwt_ref[...], h1b), rb_ref[...])
    gate_ref[...] = lax.dot_general(eye_ref[...], combine_t, (((1,), (1,)), ((), ())),
                                    preferred_element_type=F32, precision=lax.Precision.HIGHEST)


def _merge(h2, y_ret, y_gdn, y_lru, w_mg, b_mg, w_branch, w_out, ln_g, ln_b, router_wt, router_b, alpha):
    n, d = h2.shape
    t = min(ROW_TILE, n)
    width = y_ret.shape[1]
    n_exp = router_wt.shape[0]
    tile = lambda cols: pl.BlockSpec((t, cols), lambda i: (i, 0))
    return pl.pallas_call(
        functools.partial(_merge_kernel, alpha=alpha),
        grid=(n // t,),
        in_specs=[
            tile(d), tile(width), tile(width), tile(width),
            _const_spec((d, N_BRANCHES * d)),
            _const_spec((1, N_BRANCHES * d)),
            _const_spec((N_BRANCHES, width, d)),
            _const_spec((d, d)),
            _const_spec((1, d)),
            _const_spec((1, d)),
            _const_spec((n_exp, d)),
            _const_spec((n_exp, 1)),
            _const_spec((t, t)),
        ],
        out_specs=[tile(d), tile(d), tile(n_exp)],
        out_shape=[jax.ShapeDtypeStruct((n, d), F32), jax.ShapeDtypeStruct((n, d), BF16),
                   jax.ShapeDtypeStruct((n, n_exp), F32)],
        compiler_params=_params("parallel"),
        name="merge_route",
    )(h2, y_ret, y_gdn, y_lru, w_mg, b_mg.reshape(1, -1), w_branch, w_out, ln_g.reshape(1, -1),
      ln_b.reshape(1, -1), router_wt, router_b.reshape(-1, 1), jnp.eye(t, dtype=F32))


def _moe_kernel(x_ref, gate_ref, wgu_ref, wd_ref, o_ref, acc_ref):
    e = pl.program_id(1)

    @pl.when(e == 0)
    def _():
        acc_ref[...] = jnp.zeros_like(acc_ref)

    ff = wd_ref.shape[1]
    gu = _dot(x_ref[...], wgu_ref[0])
    lane = lax.broadcasted_iota(jnp.int32, gate_ref.shape, 1)
    wt = jnp.sum(jnp.where(lane == e, gate_ref[...], 0.0), axis=1, keepdims=True)
    mid = _silu(gu[:, :ff]) * gu[:, ff:] * wt
    acc_ref[...] += _dot(mid.astype(BF16), wd_ref[0])

    @pl.when(e == pl.num_programs(1) - 1)
    def _():
        o_ref[...] = acc_ref[...]


def _routed_experts(xb, gate, w_gu, w_down):
    n, d = xb.shape
    n_exp, _, ff2 = w_gu.shape
    t = min(MOE_TILE, n)
    return pl.pallas_call(
        _moe_kernel,
        grid=(n // t, n_exp),
        in_specs=[
            pl.BlockSpec((t, d), lambda i, e: (i, 0)),
            pl.BlockSpec((t, n_exp), lambda i, e: (i, 0)),
            pl.BlockSpec((1, d, ff2), lambda i, e: (e, 0, 0)),
            pl.BlockSpec((1, ff2 // 2, d), lambda i, e: (e, 0, 0)),
        ],
        out_specs=pl.BlockSpec((t, d), lambda i, e: (i, 0)),
        out_shape=jax.ShapeDtypeStruct((n, d), F32),
        scratch_shapes=[pltpu.VMEM((t, d), F32)],
        compiler_params=_params("parallel", "arbitrary"),
        name="routed_experts",
    )(xb, gate, w_gu, w_down)


def _final_kernel(h1_ref, h1b_ref, routed_ref, p_ref, wgu_ref, wd_ref, wpg_ref, bpg_ref, wpe_ref, g_ref, b_ref,
                  o_ref, *, alpha):
    xb = h1b_ref[...]
    ff = wd_ref.shape[0]
    gu = _dot(xb, wgu_ref[...])
    shared = _dot((_silu(gu[:, :ff]) * gu[:, ff:]).astype(BF16), wd_ref[...])
    ple = _sigmoid(_dot(xb, wpg_ref[...]) + bpg_ref[...]) * _dot(p_ref[...].astype(BF16), wpe_ref[...])
    o_ref[...] = _layer_norm(alpha * h1_ref[...] + (routed_ref[...] + shared) + ple, g_ref[...], b_ref[...])


def _final(h1, h1b, routed, p2, sh_w_gu, sh_w_down, ple_w_g, ple_b_g, ple_w_e, ln_g, ln_b, alpha):
    n, d = h1.shape
    t = min(ROW_TILE, n)
    pdim = p2.shape[1]
    ff2 = sh_w_gu.shape[1]
    tile = lambda cols: pl.BlockSpec((t, cols), lambda i: (i, 0))
    return pl.pallas_call(
        functools.partial(_final_kernel, alpha=alpha),
        grid=(n // t,),
        in_specs=[
            tile(d), tile(d), tile(d), tile(pdim),
            _const_spec((d, ff2)),
            _const_spec((ff2 // 2, d)),
            _const_spec((d, d)),
            _const_spec((1, d)),
            _const_spec((pdim, d)),
            _const_spec((1, d)),
            _const_spec((1, d)),
        ],
        out_specs=tile(d),
        out_shape=jax.ShapeDtypeStruct((n, d), F32),
        compiler_params=_params("parallel"),
        name="shared_ple_norm",
    )(h1, h1b, routed, p2, sh_w_gu, sh_w_down, ple_w_g, ple_b_g.reshape(1, -1), ple_w_e, ln_g.reshape(1, -1),
      ln_b.reshape(1, -1))


def _block_diag(w):
    g, i, j = w.shape
    eye = jnp.eye(g, dtype=w.dtype)
    return (eye[:, None, :, None] * w[:, :, None, :]).reshape(g * i, g * j)


def kernel(x, p, ln_in_g, ln_in_b, w_in, b_in, ret_norm_g, ret_norm_b, gdn_conv_w, gdn_a_log, gdn_dt_bias, gdn_norm_g, lru_conv_w, lru_conv_b, lru_w_r, lru_b_r, lru_w_i, lru_b_i, lru_lambda, w_branch, w_out, ln1_g, ln1_b, router_w, router_b, exp_w_gu, exp_w_down, sh_w_gu, sh_w_down, ple_w_e, ple_w_g, ple_b_g, ln2_g, ln2_b):
    bsz, seq, d = x.shape
    depth = w_in.shape[0]
    n = bsz * seq
    width = N_HEADS * HEAD_DIM
    alpha = (2 * depth) ** 0.25
    lanes = 128
    o_ret = 0
    o_gdn = o_ret + 4 * width
    o_small = o_gdn + 4 * width
    o_lru = o_small + 2 * N_HEADS
    o_mg = o_lru + 2 * width

    h = _entry_norm(x.reshape(n, d), ln_in_g, ln_in_b)
    for l in range(depth):
        wl, bl = w_in[l], b_in[l]
        bf = lambda a: a.astype(BF16)
        w_small = jnp.zeros((d, lanes), F32).at[:, :2 * N_HEADS].set(wl[:, o_small:o_lru])
        b_small = jnp.zeros((lanes,), F32).at[:2 * N_HEADS].set(bl[o_small:o_lru])
        h3 = h.reshape(bsz, seq, d)
        y_ret = _retention_branch(h3, bf(wl[:, o_ret:o_gdn]), bl[o_ret:o_gdn], ret_norm_g[l], ret_norm_b[l])
        y_gdn = _gdn_branch(h3, bf(wl[:, o_gdn:o_small]), bl[o_gdn:o_small], bf(w_small), b_small,
                            gdn_conv_w[l], gdn_a_log[l], gdn_dt_bias[l], gdn_norm_g[l])
        y_lru = _lru_branch(h3, bf(wl[:, o_lru:o_mg]), bl[o_lru:o_mg], lru_conv_w[l], lru_conv_b[l],
                            bf(_block_diag(lru_w_r[l])), lru_b_r[l], bf(_block_diag(lru_w_i[l])), lru_b_i[l],
                            lru_lambda[l])
        h1, h1b, gate = _merge(h, y_ret.reshape(n, width), y_gdn.reshape(n, width), y_lru.reshape(n, width),
                               bf(wl[:, o_mg:]), bl[o_mg:], bf(w_branch[l]), bf(w_out[l]), ln1_g[l], ln1_b[l],
                               bf(router_w[l].T), router_b[l], alpha)
        routed = _routed_experts(h1b, gate, bf(exp_w_gu[l]), bf(exp_w_down[l]))
        h = _final(h1, h1b, routed, p[l].reshape(n, -1), bf(sh_w_gu[l]), bf(sh_w_down[l]), bf(ple_w_g[l]),
                   ple_b_g[l], bf(ple_w_e[l]), ln2_g[l], ln2_b[l], alpha)
    return h.reshape(bsz, seq, d)
```

```python
import functools
import math

import numpy as np
import jax
import jax.numpy as jnp
from jax import lax
from jax.experimental import pallas as pl
from jax.experimental.pallas import tpu as pltpu
from jax.experimental.pallas import tpu_sc as plsc

F32 = jnp.float32
BF16 = jnp.bfloat16

HEAD_DIM = 128
N_HEADS = 4
LRU_BLOCKS = 8
LRU_C = 8.0
CONV_WIDTH = 4
N_BRANCHES = 3
ROPE_BASE = 10000.0
N_GROUPS = 8
TOPK_GROUPS = 4
TOP_K = 8
ROUTED_SCALE = 2.5
LN_EPS = 1e-5
GDN_CHUNK = 64
SEQ_TILE = 256
ROW_TILE = 256
MOE_BLOCK = 256
DEST_TILE = 4096
SC_WINDOW = 128
CARRY_ROWS = 8
VMEM_LIMIT_BYTES = 56 * 1024 * 1024
NEG_INF = float("-inf")


def _const_spec(shape):
    nd = len(shape)
    return pl.BlockSpec(shape, lambda *_: (0,) * nd, pipeline_mode=pl.Buffered(1))


def _params(*sem):
    return pltpu.CompilerParams(dimension_semantics=sem, vmem_limit_bytes=VMEM_LIMIT_BYTES)


def _layer_norm(x, g, b):
    mu = jnp.mean(x, axis=-1, keepdims=True)
    xc = x - mu
    var = jnp.mean(xc * xc, axis=-1, keepdims=True)
    return xc * lax.rsqrt(var + LN_EPS) * g + b


def _sigmoid(x):
    return 1.0 / (1.0 + jnp.exp(-x))


def _silu(x):
    return x * _sigmoid(x)


def _softplus(x):
    return jnp.maximum(x, 0.0) + jnp.log1p(jnp.exp(-jnp.abs(x)))


def _dot(a, b):
    return jnp.dot(a, b, preferred_element_type=F32)


def _dot_nt(a, b):
    return lax.dot_general(a, b, (((1,), (1,)), ((), ())), preferred_element_type=F32)


def _dot_tn(a, b):
    return lax.dot_general(a, b, (((0,), (0,)), ((), ())), preferred_element_type=F32)


def _dot_f32(a, b):
    return jnp.dot(a, b, preferred_element_type=F32, precision=lax.Precision.HIGHEST)


def _ln_kernel(x_ref, g_ref, b_ref, o_ref):
    o_ref[...] = _layer_norm(x_ref[...], g_ref[...], b_ref[...])


def _entry_norm(x2, g, b):
    n, d = x2.shape
    t = min(1024, n)
    return pl.pallas_call(
        _ln_kernel,
        grid=(n // t,),
        in_specs=[pl.BlockSpec((t, d), lambda i: (i, 0)), _const_spec((1, d)), _const_spec((1, d))],
        out_specs=pl.BlockSpec((t, d), lambda i: (i, 0)),
        out_shape=jax.ShapeDtypeStruct((n, d), F32),
        compiler_params=_params("parallel"),
        name="entry_norm",
    )(x2, g.reshape(1, d), b.reshape(1, d))


def _retention_kernel(h_ref, w_ref, b_ref, cos_ref, sin_ref, dmat_ref, qd_ref, kd_ref, ng_ref, nb_ref,
                      y_ref, state_ref, *, chunk_decay):
    @pl.when(pl.program_id(1) == 0)
    def _():
        state_ref[...] = jnp.zeros_like(state_ref)

    width = N_HEADS * HEAD_DIM
    hb = h_ref[0].astype(BF16)
    proj = _dot(hb, w_ref[...]) + b_ref[...]
    cos = cos_ref[...]
    sin = sin_ref[...]
    heads = range(N_HEADS)
    qs, ks, vbs = [], [], []
    for hh in heads:
        lo = hh * HEAD_DIM
        q = proj[:, lo:lo + HEAD_DIM]
        k = proj[:, width + lo:width + lo + HEAD_DIM]
        qs.append(q * cos + pltpu.roll(q, HEAD_DIM // 2, axis=1) * sin)
        ks.append((k * cos + pltpu.roll(k, HEAD_DIM // 2, axis=1) * sin) * (HEAD_DIM ** -0.5))
        vbs.append(proj[:, 2 * width + lo:2 * width + lo + HEAD_DIM].astype(BF16))
    states = [state_ref[hh] for hh in heads]
    scores = [(_dot_nt(qs[hh].astype(BF16), ks[hh].astype(BF16)) * dmat_ref[hh]).astype(BF16) for hh in heads]
    inter = [_dot((qs[hh] * qd_ref[hh]).astype(BF16), states[hh].astype(BF16)) for hh in heads]
    for hh in heads:
        state_ref[hh] = states[hh] * chunk_decay[hh] + _dot_tn((ks[hh] * kd_ref[hh]).astype(BF16), vbs[hh])
    outs = [_dot(scores[hh], vbs[hh]) + inter[hh] for hh in heads]
    for hh in heads:
        lo = hh * HEAD_DIM
        o = outs[hh]
        gate = proj[:, 3 * width + lo:3 * width + lo + HEAD_DIM]
        mu = jnp.mean(o, axis=-1, keepdims=True)
        oc = o - mu
        var = jnp.mean(oc * oc, axis=-1, keepdims=True)
        on = oc * lax.rsqrt(var + LN_EPS) * ng_ref[:, lo:lo + HEAD_DIM] + nb_ref[:, lo:lo + HEAD_DIM]
        y_ref[0, :, lo:lo + HEAD_DIM] = (_silu(gate) * on).astype(y_ref.dtype)


def _retention_tables(seq, tile):
    half = HEAD_DIM // 2
    inv_freq = ROPE_BASE ** (-np.linspace(0.0, 1.0, half))
    ang = np.arange(seq)[:, None] * inv_freq[None, :]
    cos = np.concatenate([np.cos(ang), np.cos(ang)], axis=1)
    sin = np.concatenate([-np.sin(ang), np.sin(ang)], axis=1)
    log_gamma = np.log1p(-np.exp2(-5.0 - np.arange(N_HEADS)))
    pos = np.arange(tile)
    diff = pos[:, None] - pos[None, :]
    dmat = np.where(diff >= 0, np.exp(log_gamma[:, None, None] * np.maximum(diff, 0)), 0.0)
    qd = np.exp(log_gamma[:, None] * (pos + 1.0))[:, :, None] * np.ones((1, 1, HEAD_DIM))
    kd = np.exp(log_gamma[:, None] * (tile - 1.0 - pos))[:, :, None] * np.ones((1, 1, HEAD_DIM))
    chunk_decay = tuple(float(c) for c in np.exp(log_gamma * tile))
    as32 = lambda a: jnp.asarray(a, F32)
    return as32(cos), as32(sin), as32(dmat), as32(qd), as32(kd), chunk_decay


def _retention_branch(h, w, b, norm_g, norm_b):
    bsz, seq, d = h.shape
    t = min(SEQ_TILE, seq)
    width = N_HEADS * HEAD_DIM
    cos, sin, dmat, qd, kd, chunk_decay = _retention_tables(seq, t)
    return pl.pallas_call(
        functools.partial(_retention_kernel, chunk_decay=chunk_decay),
        grid=(bsz, seq // t),
        in_specs=[
            pl.BlockSpec((1, t, d), lambda i, j: (i, j, 0)),
            _const_spec((d, 4 * width)),
            _const_spec((1, 4 * width)),
            pl.BlockSpec((t, HEAD_DIM), lambda i, j: (j, 0)),
            pl.BlockSpec((t, HEAD_DIM), lambda i, j: (j, 0)),
            _const_spec((N_HEADS, t, t)),
            _const_spec((N_HEADS, t, HEAD_DIM)),
            _const_spec((N_HEADS, t, HEAD_DIM)),
            _const_spec((1, width)),
            _const_spec((1, width)),
        ],
        out_specs=pl.BlockSpec((1, t, width), lambda i, j: (i, j, 0)),
        out_shape=jax.ShapeDtypeStruct((bsz, seq, width), BF16),
        scratch_shapes=[pltpu.VMEM((N_HEADS, HEAD_DIM, HEAD_DIM), F32)],
        compiler_params=_params("parallel", "arbitrary"),
        name="retention_branch",
    )(h, w, b.reshape(1, -1), cos, sin, dmat, qd, kd, norm_g.reshape(1, -1), norm_b.reshape(1, -1))


def _causal_conv(x, xs_ref, cw_ref):
    t = x.shape[0]
    xs_ref[CARRY_ROWS:, :] = x
    acc = None
    for j in range(CONV_WIDTH):
        start = CARRY_ROWS - (CONV_WIDTH - 1) + j
        term = xs_ref[start:start + t, :] * cw_ref[j:j + 1, :]
        acc = term if acc is None else acc + term
    xs_ref[0:CARRY_ROWS, :] = xs_ref[t:t + CARRY_ROWS, :]
    return acc


def _cumsum_rows(x):
    n = x.shape[0]
    row = lax.broadcasted_iota(jnp.int32, x.shape, 0)
    d = 1
    while d < n:
        x = x + jnp.where(row >= d, pltpu.roll(x, d, axis=0), 0.0)
        d *= 2
    return x


def _gdn_kernel(h_ref, w_ref, b_ref, ws_ref, bs_ref, cw_ref, alog_ref, dtb_ref, ng_ref,
                y_ref, xs_ref, state_ref, u_ref, wf_ref, w_s_ref, qd_ref, kd_ref, qk_ref):
    @pl.when(pl.program_id(1) == 0)
    def _():
        state_ref[...] = jnp.zeros_like(state_ref)
        xs_ref[0:CARRY_ROWS, :] = jnp.zeros((CARRY_ROWS, xs_ref.shape[1]), F32)

    width = N_HEADS * HEAD_DIM
    c = GDN_CHUNK
    hb = h_ref[0].astype(BF16)
    t = hb.shape[0]
    proj = _dot(hb, w_ref[...]) + b_ref[...]
    small = _dot(hb, ws_ref[...]) + bs_ref[...]
    qkv = _silu(_causal_conv(proj[:, :3 * width], xs_ref, cw_ref))
    beta_all = _sigmoid(small)
    la_all = -jnp.exp(alog_ref[...]) * _softplus(small + dtb_ref[...])

    ri = lax.broadcasted_iota(jnp.int32, (c, c), 0)
    ci = lax.broadcasted_iota(jnp.int32, (c, c), 1)
    lower = ri >= ci
    strict = ri > ci

    items = [(n, hh) for n in range(t // c) for hh in range(N_HEADS)]
    gcs = {}
    for n in range(t // c):
        la_c = la_all[n * c:(n + 1) * c, :]
        gc_c = _cumsum_rows(la_c)
        gcs[n] = (la_c, gc_c, jnp.exp(gc_c))
    g_last, pws, rems = {}, {}, {}
    for n, hh in items:
        r0, lo = n * c, hh * HEAD_DIM
        rows, cols = slice(r0, r0 + c), slice(lo, lo + HEAD_DIM)
        la_c, gc_c, egc_c = gcs[n]
        q = qkv[rows, lo:lo + HEAD_DIM]
        k = qkv[rows, width + lo:width + lo + HEAD_DIM]
        v = qkv[rows, 2 * width + lo:2 * width + lo + HEAD_DIM]
        q = q * lax.rsqrt(jnp.sum(q * q, axis=-1, keepdims=True) + 1e-6) * (HEAD_DIM ** -0.5)
        k = k * lax.rsqrt(jnp.sum(k * k, axis=-1, keepdims=True) + 1e-6)
        beta = beta_all[rows, hh:hh + 1]
        la = la_c[:, N_HEADS + hh:N_HEADS + hh + 1]
        gc = gc_c[:, N_HEADS + hh:N_HEADS + hh + 1]
        egc = egc_c[:, N_HEADS + hh:N_HEADS + hh + 1]
        gc_row = jnp.sum(jnp.where(ri <= ci, jnp.broadcast_to(la, (c, c)), 0.0), axis=0, keepdims=True)
        gc_last = gc_row[:, c - 1:c]
        decay = jnp.where(lower, jnp.exp(jnp.where(lower, gc - gc_row, 0.0)), 0.0)
        kb = k * beta
        kbf = k.astype(BF16)
        a_neg = jnp.where(strict, -(_dot_nt(kb.astype(BF16), kbf) * decay), 0.0)
        pws[n, hh] = a_neg
        rems[n, hh] = a_neg
        u_ref[rows, cols] = v * beta
        wf_ref[rows, cols] = kb * egc
        qk_ref[hh, rows, :] = (_dot_nt(q.astype(BF16), kbf) * decay).astype(BF16)
        qd_ref[rows, cols] = (q * egc).astype(BF16)
        kd_ref[rows, cols] = (k * jnp.exp(gc_last - gc)).astype(BF16)
        g_last[n, hh] = jnp.exp(gc_last)
    m = 2
    while m < c:
        for it in items:
            pwb = pws[it].astype(BF16)
            pws[it] = _dot(pwb, pwb)
        for it in items:
            rems[it] = rems[it] + pws[it] + _dot(rems[it].astype(BF16), pws[it].astype(BF16))
        m *= 2
    for n, hh in items:
        rows, cols = slice(n * c, (n + 1) * c), slice(hh * HEAD_DIM, (hh + 1) * HEAD_DIM)
        remb = rems[n, hh].astype(BF16)
        u_ref[rows, cols] = u_ref[rows, cols] + _dot(remb, u_ref[rows, cols].astype(BF16))
        w_s_ref[rows, cols] = (wf_ref[rows, cols] + _dot(remb, wf_ref[rows, cols].astype(BF16))).astype(BF16)

    heads = range(N_HEADS)
    for n in range(t // c):
        rows = slice(n * c, (n + 1) * c)
        cols = [slice(hh * HEAD_DIM, (hh + 1) * HEAD_DIM) for hh in heads]
        states = [state_ref[hh] for hh in heads]
        sbs = [s.astype(BF16) for s in states]
        vnbs = [(u_ref[rows, cols[hh]] - _dot(w_s_ref[rows, cols[hh]], sbs[hh])).astype(BF16) for hh in heads]
        outs = [_dot(qd_ref[rows, cols[hh]], sbs[hh]) + _dot(qk_ref[hh, rows, :], vnbs[hh]) for hh in heads]
        for hh in heads:
            state_ref[hh] = states[hh] * g_last[n, hh] + _dot_tn(kd_ref[rows, cols[hh]], vnbs[hh])
        for hh in heads:
            o = outs[hh]
            o = o * lax.rsqrt(jnp.mean(o * o, axis=-1, keepdims=True) + 1e-6) * ng_ref[...]
            og = proj[rows, 3 * width + hh * HEAD_DIM:3 * width + (hh + 1) * HEAD_DIM]
            y_ref[0, rows, cols[hh]] = (o * _silu(og)).astype(y_ref.dtype)


def _gdn_branch(h, w, b, w_small, b_small, conv_w, a_log, dt_bias, norm_g):
    bsz, seq, d = h.shape
    t = min(SEQ_TILE, seq)
    width = N_HEADS * HEAD_DIM
    lanes = w_small.shape[1]
    pad_row = lambda vec: jnp.zeros((1, lanes), F32).at[0, N_HEADS:2 * N_HEADS].set(vec.astype(F32))
    return pl.pallas_call(
        _gdn_kernel,
        grid=(bsz, seq // t),
        in_specs=[
            pl.BlockSpec((1, t, d), lambda i, j: (i, j, 0)),
            _const_spec((d, 4 * width)),
            _const_spec((1, 4 * width)),
            _const_spec((d, lanes)),
            _const_spec((1, lanes)),
            _const_spec((CONV_WIDTH, 3 * width)),
            _const_spec((1, lanes)),
            _const_spec((1, lanes)),
            _const_spec((1, HEAD_DIM)),
        ],
        out_specs=pl.BlockSpec((1, t, width), lambda i, j: (i, j, 0)),
        out_shape=jax.ShapeDtypeStruct((bsz, seq, width), BF16),
        scratch_shapes=[pltpu.VMEM((CARRY_ROWS + t, 3 * width), F32),
                        pltpu.VMEM((N_HEADS, HEAD_DIM, HEAD_DIM), F32),
                        pltpu.VMEM((t, width), F32),
                        pltpu.VMEM((t, width), F32),
                        pltpu.VMEM((t, width), BF16),
                        pltpu.VMEM((t, width), BF16),
                        pltpu.VMEM((t, width), BF16),
                        pltpu.VMEM((N_HEADS, t, GDN_CHUNK), BF16)],
        compiler_params=_params("parallel", "arbitrary"),
        name="gdn_branch",
    )(h, w, b.reshape(1, -1), w_small, b_small.reshape(1, -1), conv_w, pad_row(a_log), pad_row(dt_bias),
      norm_g.reshape(1, -1))


def _lru_kernel(h_ref, w_ref, b_ref, cw_ref, cb_ref, wr_ref, br_ref, wi_ref, bi_ref, lam_ref,
                y_ref, xs_ref, carry_ref):
    @pl.when(pl.program_id(1) == 0)
    def _():
        carry_ref[...] = jnp.zeros_like(carry_ref)
        xs_ref[0:CARRY_ROWS, :] = jnp.zeros((CARRY_ROWS, xs_ref.shape[1]), F32)

    width = cw_ref.shape[1]
    hb = h_ref[0].astype(BF16)
    t = hb.shape[0]
    proj = _dot(hb, w_ref[...]) + b_ref[...]
    xc = _causal_conv(proj[:, :width], xs_ref, cw_ref) + cb_ref[...]
    xcb = xc.astype(BF16)
    r = _sigmoid(_dot(xcb, wr_ref[...]) + br_ref[...])
    gi = _sigmoid(_dot(xcb, wi_ref[...]) + bi_ref[...])
    log_a = -LRU_C * r * _softplus(-lam_ref[...])
    a = jnp.exp(log_a)
    th = jnp.tanh(log_a)
    hs = jnp.sqrt(-2.0 * th / (1.0 - th)) * (gi * xc)
    row = lax.broadcasted_iota(jnp.int32, (t, width), 0)
    d = 1
    while d < t:
        keep = row >= d
        hs = hs + a * jnp.where(keep, pltpu.roll(hs, d, axis=0), 0.0)
        a = a * jnp.where(keep, pltpu.roll(a, d, axis=0), 1.0)
        d *= 2
    hs = hs + a * carry_ref[...]
    carry_ref[...] = hs[t - 1:t, :]
    y_ref[0] = (jax.nn.gelu(proj[:, width:], approximate=True) * hs).astype(y_ref.dtype)


def _lru_branch(h, w, b, conv_w, conv_b, w_r, b_r, w_i, b_i, lam):
    bsz, seq, d = h.shape
    t = min(SEQ_TILE, seq)
    width = conv_w.shape[1]
    row = lambda vec: vec.reshape(1, -1)
    return pl.pallas_call(
        _lru_kernel,
        grid=(bsz, seq // t),
        in_specs=[
            pl.BlockSpec((1, t, d), lambda i, j: (i, j, 0)),
            _const_spec((d, 2 * width)),
            _const_spec((1, 2 * width)),
            _const_spec((CONV_WIDTH, width)),
            _const_spec((1, width)),
            _const_spec((width, width)),
            _const_spec((1, width)),
            _const_spec((width, width)),
            _const_spec((1, width)),
            _const_spec((1, width)),
        ],
        out_specs=pl.BlockSpec((1, t, width), lambda i, j: (i, j, 0)),
        out_shape=jax.ShapeDtypeStruct((bsz, seq, width), BF16),
        scratch_shapes=[pltpu.VMEM((CARRY_ROWS + t, width), F32), pltpu.VMEM((1, width), F32)],
        compiler_params=_params("parallel", "arbitrary"),
        name="lru_branch",
    )(h, w, row(b), conv_w, row(conv_b), w_r, row(b_r), w_i, row(b_i), row(lam))


def _first_index_of_max(x, idx, size):
    m = jnp.max(x, axis=0, keepdims=True)
    first = jnp.min(jnp.where(x == m, idx, size), axis=0, keepdims=True)
    return m, idx == first


def _route(logits_t, bias_col):
    n_exp, t = logits_t.shape
    per_group = n_exp // N_GROUPS
    scores = _sigmoid(logits_t)
    sel = scores + bias_col
    idx_g = lax.broadcasted_iota(jnp.int32, (per_group, t), 0)
    group_scores = []
    for g in range(N_GROUPS):
        x = sel[g * per_group:(g + 1) * per_group, :]
        m1, hit = _first_index_of_max(x, idx_g, per_group)
        m2 = jnp.max(jnp.where(hit, NEG_INF, x), axis=0, keepdims=True)
        group_scores.append(m1 + m2)
    gsc = jnp.concatenate(group_scores, axis=0)
    idx_n = lax.broadcasted_iota(jnp.int32, (N_GROUPS, t), 0)
    gmask = jnp.zeros((N_GROUPS, t), F32)
    for _ in range(TOPK_GROUPS):
        _, hit = _first_index_of_max(gsc, idx_n, N_GROUPS)
        gmask = jnp.where(hit, 1.0, gmask)
        gsc = jnp.where(hit, NEG_INF, gsc)
    emask = jnp.concatenate([jnp.broadcast_to(gmask[g:g + 1, :], (per_group, t)) for g in range(N_GROUPS)], axis=0)
    cand = jnp.where(emask > 0.0, sel, NEG_INF)
    idx_e = lax.broadcasted_iota(jnp.int32, (n_exp, t), 0)
    picked = jnp.zeros((n_exp, t), F32)
    hits = []
    for _ in range(TOP_K):
        _, hit = _first_index_of_max(cand, idx_e, n_exp)
        hits.append(hit)
        picked = jnp.where(hit, 1.0, picked)
        cand = jnp.where(hit, NEG_INF, cand)
    gw = jnp.where(picked > 0.0, scores, 0.0)
    return gw / jnp.sum(gw, axis=0, keepdims=True) * ROUTED_SCALE, picked, hits, idx_e


def _pack_bf16_pairs(x):
    c = x.shape[1] // 2
    hi = pltpu.bitcast(x[:, :c].astype(BF16).astype(F32), jnp.int32)
    lo = pltpu.bitcast(x[:, c:].astype(BF16).astype(F32), jnp.int32)
    return hi | lax.shift_right_logical(lo, jnp.full(lo.shape, 16, jnp.int32))


def _unpack_bf16_pairs(w):
    hi = pltpu.bitcast(w & jnp.int32(-65536), F32)
    lo = pltpu.bitcast(lax.shift_left(w, jnp.full(w.shape, 16, jnp.int32)), F32)
    return hi, lo


def _merge_kernel(h_ref, yr_ref, yg_ref, yl_ref, wmg_ref, bmg_ref, wbr_ref, wout_ref, g_ref, b_ref,
                  rwt_ref, rb_ref, eye_ref, tri_ref, h1_ref, h1b_ref, xp_ref, eidx_ref, rank_ref, gwt_ref,
                  count_ref, *, alpha):
    @pl.when(pl.program_id(0) == 0)
    def _():
        count_ref[...] = jnp.zeros_like(count_ref)

    h = h_ref[...]
    d = h.shape[1]
    hb = h.astype(BF16)
    gates = _sigmoid(_dot(hb, wmg_ref[...]) + bmg_ref[...])
    mixed = None
    for n, y_ref in enumerate((yr_ref, yg_ref, yl_ref)):
        term = gates[:, n * d:(n + 1) * d] * _dot(y_ref[...], wbr_ref[n])
        mixed = term if mixed is None else mixed + term
    mix = _dot(mixed.astype(BF16), wout_ref[...])
    h1 = _layer_norm(alpha * h + mix, g_ref[...], b_ref[...])
    h1b = h1.astype(BF16)
    h1_ref[...] = h1
    h1b_ref[...] = h1b
    xp_ref[...] = _pack_bf16_pairs(h1)
    combine_t, picked, hits, idx_e = _route(_dot_nt(rwt_ref[...], h1b), rb_ref[...])
    rank_full = count_ref[...] + _dot(picked.astype(BF16), tri_ref[...])
    count_ref[...] += jnp.sum(picked, axis=1, keepdims=True)
    pick = lambda hit, vals, zero: jnp.sum(jnp.where(hit, vals, zero), axis=0, keepdims=True)
    eidx_ref[...] = jnp.concatenate([pick(hit, idx_e, 0) for hit in hits], axis=0)
    rank_ref[...] = jnp.concatenate([pick(hit, rank_full, 0.0) for hit in hits], axis=0).astype(jnp.int32)
    gw = jnp.concatenate([pick(hit, combine_t, 0.0) for hit in hits], axis=0)
    gwt_ref[...] = lax.dot_general(eye_ref[...], gw, (((1,), (1,)), ((), ())),
                                   preferred_element_type=F32, precision=lax.Precision.HIGHEST)


def _merge(h2, y_ret, y_gdn, y_lru, w_mg, b_mg, w_branch, w_out, ln_g, ln_b, router_wt, router_b, alpha):
    n, d = h2.shape
    t = min(ROW_TILE, n)
    width = y_ret.shape[1]
    n_exp = router_wt.shape[0]
    tile = lambda cols: pl.BlockSpec((t, cols), lambda i: (i, 0))
    lane_tile = pl.BlockSpec((TOP_K, t), lambda i: (0, i))
    tri = jnp.triu(jnp.ones((t, t), BF16), 1)
    return pl.pallas_call(
        functools.partial(_merge_kernel, alpha=alpha),
        grid=(n // t,),
        in_specs=[
            tile(d), tile(width), tile(width), tile(width),
            _const_spec((d, N_BRANCHES * d)),
            _const_spec((1, N_BRANCHES * d)),
            _const_spec((N_BRANCHES, width, d)),
            _const_spec((d, d)),
            _const_spec((1, d)),
            _const_spec((1, d)),
            _const_spec((n_exp, d)),
            _const_spec((n_exp, 1)),
            _const_spec((t, t)),
            _const_spec((t, t)),
        ],
        out_specs=[tile(d), tile(d), tile(d // 2), lane_tile, lane_tile, tile(TOP_K),
                   pl.BlockSpec((n_exp, 1), lambda i: (0, 0))],
        out_shape=[jax.ShapeDtypeStruct((n, d), F32), jax.ShapeDtypeStruct((n, d), BF16),
                   jax.ShapeDtypeStruct((n, d // 2), jnp.int32),
                   jax.ShapeDtypeStruct((TOP_K, n), jnp.int32), jax.ShapeDtypeStruct((TOP_K, n), jnp.int32),
                   jax.ShapeDtypeStruct((n, TOP_K), F32), jax.ShapeDtypeStruct((n_exp, 1), F32)],
        compiler_params=_params("arbitrary"),
        name="merge_route",
    )(h2, y_ret, y_gdn, y_lru, w_mg, b_mg.reshape(1, -1), w_branch, w_out, ln_g.reshape(1, -1),
      ln_b.reshape(1, -1), router_wt, router_b.reshape(-1, 1), jnp.eye(t, dtype=F32), tri)


def _dest_kernel(start_ref, eidx_ref, rank_ref, o_ref, *, n_exp):
    eidx = eidx_ref[...]
    dest = rank_ref[...]
    for e in range(n_exp):
        dest = dest + jnp.where(eidx == e, start_ref[e], 0)
    o_ref[...] = dest


def _dest_rows(seg_start, eidx, rank):
    k, n = eidx.shape
    t = min(DEST_TILE, n)
    blk = pl.BlockSpec((k, t), lambda i, s: (0, i))
    return pl.pallas_call(
        functools.partial(_dest_kernel, n_exp=seg_start.shape[0]),
        grid_spec=pltpu.PrefetchScalarGridSpec(num_scalar_prefetch=1, grid=(n // t,), in_specs=[blk, blk],
                                               out_specs=blk),
        out_shape=jax.ShapeDtypeStruct((k, n), jnp.int32),
        compiler_params=_params("parallel"),
        name="dest_rows",
    )(seg_start, eidx, rank)


def _sc_workers():
    info = plsc.get_sparse_core_info()
    return info.num_cores, info.num_subcores


def _sc_mesh():
    return plsc.VectorSubcoreMesh(core_axis_name="c", subcore_axis_name="s")


def _sc_scatter_rows(x, dest, n_rows):
    n, c = x.shape
    k = dest.shape[0]
    n_cores, n_sub = _sc_workers()
    per_worker = n // (n_cores * n_sub)
    steps = per_worker // SC_WINDOW

    @functools.partial(pl.kernel, out_type=jax.ShapeDtypeStruct((n_rows, c), x.dtype), mesh=_sc_mesh(),
                       scratch_types=[pltpu.VMEM((SC_WINDOW,), jnp.int32), pltpu.VMEM((SC_WINDOW, c), x.dtype)],
                       name="dispatch_rows")
    def scatter(x_hbm, d_hbm, o_hbm, idx_v, rows_v):
        base = (lax.axis_index("s") * n_cores + lax.axis_index("c")) * per_worker

        @pl.loop(0, steps)
        def _(j):
            off = base + j * SC_WINDOW
            pltpu.sync_copy(x_hbm.at[pl.ds(off, SC_WINDOW)], rows_v)
            for kk in range(k):
                pltpu.sync_copy(d_hbm.at[pl.ds(kk * n + off, SC_WINDOW)], idx_v)
                pltpu.sync_copy(rows_v, o_hbm.at[idx_v])

    return scatter(x, dest.reshape(k * n))


def _sc_gather_rows(table, idx):
    b = idx.shape[0]
    c = table.shape[1]
    n_cores, n_sub = _sc_workers()
    per_worker = b // (n_cores * n_sub)
    steps = per_worker // SC_WINDOW

    @functools.partial(pl.kernel, out_type=jax.ShapeDtypeStruct((b, c), table.dtype), mesh=_sc_mesh(),
                       scratch_types=[pltpu.VMEM((SC_WINDOW,), jnp.int32), pltpu.VMEM((SC_WINDOW, c), table.dtype)],
                       name="collect_rows")
    def gather(t_hbm, i_hbm, o_hbm, idx_v, rows_v):
        base = (lax.axis_index("s") * n_cores + lax.axis_index("c")) * per_worker

        @pl.loop(0, steps)
        def _(j):
            off = base + j * SC_WINDOW
            pltpu.sync_copy(i_hbm.at[pl.ds(off, SC_WINDOW)], idx_v)
            pltpu.sync_copy(t_hbm.at[idx_v], rows_v)
            pltpu.sync_copy(rows_v, o_hbm.at[pl.ds(off, SC_WINDOW)])

    return gather(table, idx)


def _expert_block_kernel(blk_ref, xs_ref, wgu_ref, wd_ref, ys_ref):
    del blk_ref
    hi, lo = _unpack_bf16_pairs(xs_ref[...])
    half = hi.shape[1]
    ff = wd_ref.shape[1]
    gu = _dot(hi.astype(BF16), wgu_ref[0, :half, :]) + _dot(lo.astype(BF16), wgu_ref[0, half:, :])
    mid = _silu(gu[:, :ff]) * gu[:, ff:]
    ys_ref[...] = _pack_bf16_pairs(_dot(mid.astype(BF16), wd_ref[0]))


def _expert_blocks(blk_expert, xs, w_gu, w_down):
    rows, half = xs.shape
    _, d, ff2 = w_gu.shape
    return pl.pallas_call(
        _expert_block_kernel,
        grid_spec=pltpu.PrefetchScalarGridSpec(
            num_scalar_prefetch=1, grid=(rows // MOE_BLOCK,),
            in_specs=[pl.BlockSpec((MOE_BLOCK, half), lambda i, blk: (i, 0)),
                      pl.BlockSpec((1, d, ff2), lambda i, blk: (blk[i], 0, 0)),
                      pl.BlockSpec((1, ff2 // 2, d), lambda i, blk: (blk[i], 0, 0))],
            out_specs=pl.BlockSpec((MOE_BLOCK, half), lambda i, blk: (i, 0))),
        out_shape=jax.ShapeDtypeStruct((rows, half), jnp.int32),
        compiler_params=_params("parallel"),
        name="expert_blocks",
    )(blk_expert, xs, w_gu, w_down)


def _routed_rows(xp, eidx, rank, counts, w_gu, w_down):
    n = xp.shape[0]
    k = eidx.shape[0]
    n_exp = w_gu.shape[0]
    counts = counts.reshape(n_exp).astype(jnp.int32)
    padded = (counts + MOE_BLOCK - 1) // MOE_BLOCK * MOE_BLOCK
    seg_end = jnp.cumsum(padded)
    n_blocks = k * n // MOE_BLOCK + n_exp
    blk_start = jnp.arange(n_blocks, dtype=jnp.int32) * MOE_BLOCK
    blk_expert = jnp.minimum(jnp.searchsorted(seg_end, blk_start, side="right"), n_exp - 1).astype(jnp.int32)
    dest = _dest_rows(seg_end - padded, eidx, rank)
    xs = _sc_scatter_rows(xp, dest, n_blocks * MOE_BLOCK)
    ys = _expert_blocks(blk_expert, xs, w_gu, w_down)
    return _sc_gather_rows(ys, dest.reshape(k * n)).reshape(k, n, -1)


def _final_kernel(h1_ref, h1b_ref, yk_ref, gwt_ref, p_ref, wgu_ref, wd_ref, wpg_ref, bpg_ref, wpe_ref, g_ref, b_ref,
                  o_ref, *, alpha):
    xb = h1b_ref[...]
    ff = wd_ref.shape[0]
    gu = _dot(xb, wgu_ref[...])
    shared = _dot((_silu(gu[:, :ff]) * gu[:, ff:]).astype(BF16), wd_ref[...])
    ple = _sigmoid(_dot(xb, wpg_ref[...]) + bpg_ref[...]) * _dot(p_ref[...].astype(BF16), wpe_ref[...])
    gwt = gwt_ref[...]
    routed_hi = routed_lo = None
    for k in range(yk_ref.shape[0]):
        hi, lo = _unpack_bf16_pairs(yk_ref[k])
        wk = gwt[:, k:k + 1]
        routed_hi = hi * wk if routed_hi is None else routed_hi + hi * wk
        routed_lo = lo * wk if routed_lo is None else routed_lo + lo * wk
    routed = jnp.concatenate([routed_hi, routed_lo], axis=1)
    o_ref[...] = _layer_norm(alpha * h1_ref[...] + (routed + shared) + ple, g_ref[...], b_ref[...])


def _final(h1, h1b, yk, gwt, p2, sh_w_gu, sh_w_down, ple_w_g, ple_b_g, ple_w_e, ln_g, ln_b, alpha):
    n, d = h1.shape
    t = min(ROW_TILE, n)
    pdim = p2.shape[1]
    ff2 = sh_w_gu.shape[1]
    top_k = yk.shape[0]
    tile = lambda cols: pl.BlockSpec((t, cols), lambda i: (i, 0))
    return pl.pallas_call(
        functools.partial(_final_kernel, alpha=alpha),
        grid=(n // t,),
        in_specs=[
            tile(d), tile(d), pl.BlockSpec((top_k, t, d // 2), lambda i: (0, i, 0)), tile(top_k), tile(pdim),
            _const_spec((d, ff2)),
            _const_spec((ff2 // 2, d)),
            _const_spec((d, d)),
            _const_spec((1, d)),
            _const_spec((pdim, d)),
            _const_spec((1, d)),
            _const_spec((1, d)),
        ],
        out_specs=tile(d),
        out_shape=jax.ShapeDtypeStruct((n, d), F32),
        compiler_params=_params("parallel"),
        name="shared_ple_norm",
    )(h1, h1b, yk, gwt, p2, sh_w_gu, sh_w_down, ple_w_g, ple_b_g.reshape(1, -1), ple_w_e, ln_g.reshape(1, -1),
      ln_b.reshape(1, -1))


def _block_diag(w):
    g, i, j = w.shape
    eye = jnp.eye(g, dtype=w.dtype)
    return (eye[:, None, :, None] * w[:, :, None, :]).reshape(g * i, g * j)


def kernel(x, p, ln_in_g, ln_in_b, w_in, b_in, ret_norm_g, ret_norm_b, gdn_conv_w, gdn_a_log, gdn_dt_bias, gdn_norm_g, lru_conv_w, lru_conv_b, lru_w_r, lru_b_r, lru_w_i, lru_b_i, lru_lambda, w_branch, w_out, ln1_g, ln1_b, router_w, router_b, exp_w_gu, exp_w_down, sh_w_gu, sh_w_down, ple_w_e, ple_w_g, ple_b_g, ln2_g, ln2_b):
    bsz, seq, d = x.shape
    depth = w_in.shape[0]
    n = bsz * seq
    width = N_HEADS * HEAD_DIM
    alpha = (2 * depth) ** 0.25
    lanes = 128
    o_ret = 0
    o_gdn = o_ret + 4 * width
    o_small = o_gdn + 4 * width
    o_lru = o_small + 2 * N_HEADS
    o_mg = o_lru + 2 * width

    h = _entry_norm(x.reshape(n, d), ln_in_g, ln_in_b)
    for l in range(depth):
        wl, bl = w_in[l], b_in[l]
        bf = lambda a: a.astype(BF16)
        w_small = jnp.zeros((d, lanes), F32).at[:, :2 * N_HEADS].set(wl[:, o_small:o_lru])
        b_small = jnp.zeros((lanes,), F32).at[:2 * N_HEADS].set(bl[o_small:o_lru])
        h3 = h.reshape(bsz, seq, d)
        y_ret = _retention_branch(h3, bf(wl[:, o_ret:o_gdn]), bl[o_ret:o_gdn], ret_norm_g[l], ret_norm_b[l])
        y_gdn = _gdn_branch(h3, bf(wl[:, o_gdn:o_small]), bl[o_gdn:o_small], bf(w_small), b_small,
                            gdn_conv_w[l], gdn_a_log[l], gdn_dt_bias[l], gdn_norm_g[l])
        y_lru = _lru_branch(h3, bf(wl[:, o_lru:o_mg]), bl[o_lru:o_mg], lru_conv_w[l], lru_conv_b[l],
                            bf(_block_diag(lru_w_r[l])), lru_b_r[l], bf(_block_diag(lru_w_i[l])), lru_b_i[l],
                            lru_lambda[l])
        h1, h1b, xp, eidx, rank, gwt, counts = _merge(
            h, y_ret.reshape(n, width), y_gdn.reshape(n, width), y_lru.reshape(n, width), bf(wl[:, o_mg:]), bl[o_mg:],
            bf(w_branch[l]), bf(w_out[l]), ln1_g[l], ln1_b[l], bf(router_w[l].T), router_b[l], alpha)
        yk = _routed_rows(xp, eidx, rank, counts, bf(exp_w_gu[l]), bf(exp_w_down[l]))
        h = _final(h1, h1b, yk, gwt, p[l].reshape(n, -1), bf(sh_w_gu[l]), bf(sh_w_down[l]), bf(ple_w_g[l]),
                   ple_b_g[l], bf(ple_w_e[l]), ln2_g[l], ln2_b[l], alpha)
    return h.reshape(bsz, seq, d)
```

```python
import functools
import math

import numpy as np
import jax
import jax.numpy as jnp
from jax import lax
from jax.experimental import pallas as pl
from jax.experimental.pallas import tpu as pltpu
from jax.experimental.pallas import tpu_sc as plsc

F32 = jnp.float32
BF16 = jnp.bfloat16

HEAD_DIM = 128
N_HEADS = 4
LRU_BLOCKS = 8
LRU_C = 8.0
CONV_WIDTH = 4
N_BRANCHES = 3
ROPE_BASE = 10000.0
N_GROUPS = 8
TOPK_GROUPS = 4
TOP_K = 8
ROUTED_SCALE = 2.5
LN_EPS = 1e-5
GDN_CHUNK = 64
SEQ_TILE = 256
MERGE_TILE = 512
MERGE_SUB = 256
ROW_TILE = 512
MOE_BLOCK = 512
DEST_TILE = 4096
SC_WINDOW = 128
CARRY_ROWS = 8
VMEM_LIMIT_BYTES = 56 * 1024 * 1024
NEG_INF = float("-inf")


def _const_spec(shape):
    nd = len(shape)
    return pl.BlockSpec(shape, lambda *_: (0,) * nd, pipeline_mode=pl.Buffered(1))


def _params(*sem):
    return pltpu.CompilerParams(dimension_semantics=sem, vmem_limit_bytes=VMEM_LIMIT_BYTES)


def _layer_norm(x, g, b):
    mu = jnp.mean(x, axis=-1, keepdims=True)
    xc = x - mu
    var = jnp.mean(xc * xc, axis=-1, keepdims=True)
    return xc * lax.rsqrt(var + LN_EPS) * g + b


def _sigmoid(x):
    return 1.0 / (1.0 + jnp.exp(-x))


def _silu(x):
    return x * _sigmoid(x)


def _softplus(x):
    return jnp.maximum(x, 0.0) + jnp.log1p(jnp.exp(-jnp.abs(x)))


def _dot(a, b):
    return jnp.dot(a, b, preferred_element_type=F32)


def _dot_nt(a, b):
    return lax.dot_general(a, b, (((1,), (1,)), ((), ())), preferred_element_type=F32)


def _dot_tn(a, b):
    return lax.dot_general(a, b, (((0,), (0,)), ((), ())), preferred_element_type=F32)


def _dot_f32(a, b):
    return jnp.dot(a, b, preferred_element_type=F32, precision=lax.Precision.HIGHEST)


def _ln_kernel(x_ref, g_ref, b_ref, o_ref):
    o_ref[...] = _layer_norm(x_ref[...], g_ref[...], b_ref[...])


def _entry_norm(x2, g, b):
    n, d = x2.shape
    t = min(1024, n)
    return pl.pallas_call(
        _ln_kernel,
        grid=(n // t,),
        in_specs=[pl.BlockSpec((t, d), lambda i: (i, 0)), _const_spec((1, d)), _const_spec((1, d))],
        out_specs=pl.BlockSpec((t, d), lambda i: (i, 0)),
        out_shape=jax.ShapeDtypeStruct((n, d), F32),
        compiler_params=_params("parallel"),
        name="entry_norm",
    )(x2, g.reshape(1, d), b.reshape(1, d))


def _retention_kernel(h_ref, w_ref, b_ref, cos_ref, sin_ref, dmat_ref, qd_ref, kd_ref, ng_ref, nb_ref,
                      y_ref, state_ref, *, chunk_decay):
    @pl.when(pl.program_id(1) == 0)
    def _():
        state_ref[...] = jnp.zeros_like(state_ref)

    width = N_HEADS * HEAD_DIM
    hb = h_ref[0].astype(BF16)
    proj = _dot(hb, w_ref[...]) + b_ref[...]
    cos = cos_ref[...]
    sin = sin_ref[...]
    heads = range(N_HEADS)
    qs, ks, vbs = [], [], []
    for hh in heads:
        lo = hh * HEAD_DIM
        q = proj[:, lo:lo + HEAD_DIM]
        k = proj[:, width + lo:width + lo + HEAD_DIM]
        qs.append(q * cos + pltpu.roll(q, HEAD_DIM // 2, axis=1) * sin)
        ks.append((k * cos + pltpu.roll(k, HEAD_DIM // 2, axis=1) * sin) * (HEAD_DIM ** -0.5))
        vbs.append(proj[:, 2 * width + lo:2 * width + lo + HEAD_DIM].astype(BF16))
    states = [state_ref[hh] for hh in heads]
    scores = [(_dot_nt(qs[hh].astype(BF16), ks[hh].astype(BF16)) * dmat_ref[hh]).astype(BF16) for hh in heads]
    inter = [_dot((qs[hh] * qd_ref[hh]).astype(BF16), states[hh].astype(BF16)) for hh in heads]
    for hh in heads:
        state_ref[hh] = states[hh] * chunk_decay[hh] + _dot_tn((ks[hh] * kd_ref[hh]).astype(BF16), vbs[hh])
    outs = [_dot(scores[hh], vbs[hh]) + inter[hh] for hh in heads]
    for hh in heads:
        lo = hh * HEAD_DIM
        o = outs[hh]
        gate = proj[:, 3 * width + lo:3 * width + lo + HEAD_DIM]
        mu = jnp.mean(o, axis=-1, keepdims=True)
        oc = o - mu
        var = jnp.mean(oc * oc, axis=-1, keepdims=True)
        on = oc * lax.rsqrt(var + LN_EPS) * ng_ref[:, lo:lo + HEAD_DIM] + nb_ref[:, lo:lo + HEAD_DIM]
        y_ref[0, :, lo:lo + HEAD_DIM] = (_silu(gate) * on).astype(y_ref.dtype)


def _retention_tables(seq, tile):
    half = HEAD_DIM // 2
    inv_freq = ROPE_BASE ** (-np.linspace(0.0, 1.0, half))
    ang = np.arange(seq)[:, None] * inv_freq[None, :]
    cos = np.concatenate([np.cos(ang), np.cos(ang)], axis=1)
    sin = np.concatenate([-np.sin(ang), np.sin(ang)], axis=1)
    log_gamma = np.log1p(-np.exp2(-5.0 - np.arange(N_HEADS)))
    pos = np.arange(tile)
    diff = pos[:, None] - pos[None, :]
    dmat = np.where(diff >= 0, np.exp(log_gamma[:, None, None] * np.maximum(diff, 0)), 0.0)
    qd = np.exp(log_gamma[:, None] * (pos + 1.0))[:, :, None] * np.ones((1, 1, HEAD_DIM))
    kd = np.exp(log_gamma[:, None] * (tile - 1.0 - pos))[:, :, None] * np.ones((1, 1, HEAD_DIM))
    chunk_decay = tuple(float(c) for c in np.exp(log_gamma * tile))
    as32 = lambda a: jnp.asarray(a, F32)
    return as32(cos), as32(sin), as32(dmat), as32(qd), as32(kd), chunk_decay


def _retention_branch(h, w, b, norm_g, norm_b):
    bsz, seq, d = h.shape
    t = min(SEQ_TILE, seq)
    width = N_HEADS * HEAD_DIM
    cos, sin, dmat, qd, kd, chunk_decay = _retention_tables(seq, t)
    return pl.pallas_call(
        functools.partial(_retention_kernel, chunk_decay=chunk_decay),
        grid=(bsz, seq // t),
        in_specs=[
            pl.BlockSpec((1, t, d), lambda i, j: (i, j, 0)),
            _const_spec((d, 4 * width)),
            _const_spec((1, 4 * width)),
            pl.BlockSpec((t, HEAD_DIM), lambda i, j: (j, 0)),
            pl.BlockSpec((t, HEAD_DIM), lambda i, j: (j, 0)),
            _const_spec((N_HEADS, t, t)),
            _const_spec((N_HEADS, t, HEAD_DIM)),
            _const_spec((N_HEADS, t, HEAD_DIM)),
            _const_spec((1, width)),
            _const_spec((1, width)),
        ],
        out_specs=pl.BlockSpec((1, t, width), lambda i, j: (i, j, 0)),
        out_shape=jax.ShapeDtypeStruct((bsz, seq, width), BF16),
        scratch_shapes=[pltpu.VMEM((N_HEADS, HEAD_DIM, HEAD_DIM), F32)],
        compiler_params=_params("parallel", "arbitrary"),
        name="retention_branch",
    )(h, w, b.reshape(1, -1), cos, sin, dmat, qd, kd, norm_g.reshape(1, -1), norm_b.reshape(1, -1))


def _causal_conv(x, xs_ref, cw_ref):
    t = x.shape[0]
    xs_ref[CARRY_ROWS:, :] = x
    acc = None
    for j in range(CONV_WIDTH):
        start = CARRY_ROWS - (CONV_WIDTH - 1) + j
        term = xs_ref[start:start + t, :] * cw_ref[j:j + 1, :]
        acc = term if acc is None else acc + term
    xs_ref[0:CARRY_ROWS, :] = xs_ref[t:t + CARRY_ROWS, :]
    return acc


def _cumsum_rows(x):
    n = x.shape[0]
    row = lax.broadcasted_iota(jnp.int32, x.shape, 0)
    d = 1
    while d < n:
        x = x + jnp.where(row >= d, pltpu.roll(x, d, axis=0), 0.0)
        d *= 2
    return x


def _gdn_kernel(h_ref, w_ref, b_ref, ws_ref, bs_ref, cw_ref, alog_ref, dtb_ref, ng_ref,
                y_ref, xs_ref, state_ref, u_ref, wf_ref, w_s_ref, qd_ref, kd_ref, qk_ref):
    @pl.when(pl.program_id(1) == 0)
    def _():
        state_ref[...] = jnp.zeros_like(state_ref)
        xs_ref[0:CARRY_ROWS, :] = jnp.zeros((CARRY_ROWS, xs_ref.shape[1]), F32)

    width = N_HEADS * HEAD_DIM
    c = GDN_CHUNK
    hb = h_ref[0].astype(BF16)
    t = hb.shape[0]
    proj = _dot(hb, w_ref[...]) + b_ref[...]
    small = _dot(hb, ws_ref[...]) + bs_ref[...]
    qkv = _silu(_causal_conv(proj[:, :3 * width], xs_ref, cw_ref))
    beta_all = _sigmoid(small)
    la_all = -jnp.exp(alog_ref[...]) * _softplus(small + dtb_ref[...])

    ri = lax.broadcasted_iota(jnp.int32, (c, c), 0)
    ci = lax.broadcasted_iota(jnp.int32, (c, c), 1)
    lower = ri >= ci
    strict = ri > ci

    items = [(n, hh) for n in range(t // c) for hh in range(N_HEADS)]
    gcs = {}
    for n in range(t // c):
        la_c = la_all[n * c:(n + 1) * c, :]
        gc_c = _cumsum_rows(la_c)
        gcs[n] = (la_c, gc_c, jnp.exp(gc_c))
    g_last, pws, rems = {}, {}, {}
    for n, hh in items:
        r0, lo = n * c, hh * HEAD_DIM
        rows, cols = slice(r0, r0 + c), slice(lo, lo + HEAD_DIM)
        la_c, gc_c, egc_c = gcs[n]
        q = qkv[rows, lo:lo + HEAD_DIM]
        k = qkv[rows, width + lo:width + lo + HEAD_DIM]
        v = qkv[rows, 2 * width + lo:2 * width + lo + HEAD_DIM]
        q = q * lax.rsqrt(jnp.sum(q * q, axis=-1, keepdims=True) + 1e-6) * (HEAD_DIM ** -0.5)
        k = k * lax.rsqrt(jnp.sum(k * k, axis=-1, keepdims=True) + 1e-6)
        beta = beta_all[rows, hh:hh + 1]
        la = la_c[:, N_HEADS + hh:N_HEADS + hh + 1]
        gc = gc_c[:, N_HEADS + hh:N_HEADS + hh + 1]
        egc = egc_c[:, N_HEADS + hh:N_HEADS + hh + 1]
        gc_row = jnp.sum(jnp.where(ri <= ci, jnp.broadcast_to(la, (c, c)), 0.0), axis=0, keepdims=True)
        gc_last = gc_row[:, c - 1:c]
        decay = jnp.where(lower, jnp.exp(jnp.where(lower, gc - gc_row, 0.0)), 0.0)
        kb = k * beta
        kbf = k.astype(BF16)
        a_neg = jnp.where(strict, -(_dot_nt(kb.astype(BF16), kbf) * decay), 0.0)
        pws[n, hh] = a_neg
        rems[n, hh] = a_neg
        u_ref[rows, cols] = v * beta
        wf_ref[rows, cols] = kb * egc
        qk_ref[hh, rows, :] = (_dot_nt(q.astype(BF16), kbf) * decay).astype(BF16)
        qd_ref[rows, cols] = (q * egc).astype(BF16)
        kd_ref[rows, cols] = (k * jnp.exp(gc_last - gc)).astype(BF16)
        g_last[n, hh] = jnp.exp(gc_last)
    m = 2
    while m < c:
        for it in items:
            pwb = pws[it].astype(BF16)
            pws[it] = _dot(pwb, pwb)
        for it in items:
            rems[it] = rems[it] + pws[it] + _dot(rems[it].astype(BF16), pws[it].astype(BF16))
        m *= 2
    for n, hh in items:
        rows, cols = slice(n * c, (n + 1) * c), slice(hh * HEAD_DIM, (hh + 1) * HEAD_DIM)
        remb = rems[n, hh].astype(BF16)
        u_ref[rows, cols] = u_ref[rows, cols] + _dot(remb, u_ref[rows, cols].astype(BF16))
        w_s_ref[rows, cols] = (wf_ref[rows, cols] + _dot(remb, wf_ref[rows, cols].astype(BF16))).astype(BF16)

    heads = range(N_HEADS)
    for n in range(t // c):
        rows = slice(n * c, (n + 1) * c)
        cols = [slice(hh * HEAD_DIM, (hh + 1) * HEAD_DIM) for hh in heads]
        states = [state_ref[hh] for hh in heads]
        sbs = [s.astype(BF16) for s in states]
        vnbs = [(u_ref[rows, cols[hh]] - _dot(w_s_ref[rows, cols[hh]], sbs[hh])).astype(BF16) for hh in heads]
        outs = [_dot(qd_ref[rows, cols[hh]], sbs[hh]) + _dot(qk_ref[hh, rows, :], vnbs[hh]) for hh in heads]
        for hh in heads:
            state_ref[hh] = states[hh] * g_last[n, hh] + _dot_tn(kd_ref[rows, cols[hh]], vnbs[hh])
        for hh in heads:
            o = outs[hh]
            o = o * lax.rsqrt(jnp.mean(o * o, axis=-1, keepdims=True) + 1e-6) * ng_ref[...]
            og = proj[rows, 3 * width + hh * HEAD_DIM:3 * width + (hh + 1) * HEAD_DIM]
            y_ref[0, rows, cols[hh]] = (o * _silu(og)).astype(y_ref.dtype)


def _gdn_branch(h, w, b, w_small, b_small, conv_w, a_log, dt_bias, norm_g):
    bsz, seq, d = h.shape
    t = min(SEQ_TILE, seq)
    width = N_HEADS * HEAD_DIM
    lanes = w_small.shape[1]
    pad_row = lambda vec: jnp.zeros((1, lanes), F32).at[0, N_HEADS:2 * N_HEADS].set(vec.astype(F32))
    return pl.pallas_call(
        _gdn_kernel,
        grid=(bsz, seq // t),
        in_specs=[
            pl.BlockSpec((1, t, d), lambda i, j: (i, j, 0)),
            _const_spec((d, 4 * width)),
            _const_spec((1, 4 * width)),
            _const_spec((d, lanes)),
            _const_spec((1, lanes)),
            _const_spec((CONV_WIDTH, 3 * width)),
            _const_spec((1, lanes)),
            _const_spec((1, lanes)),
            _const_spec((1, HEAD_DIM)),
        ],
        out_specs=pl.BlockSpec((1, t, width), lambda i, j: (i, j, 0)),
        out_shape=jax.ShapeDtypeStruct((bsz, seq, width), BF16),
        scratch_shapes=[pltpu.VMEM((CARRY_ROWS + t, 3 * width), F32),
                        pltpu.VMEM((N_HEADS, HEAD_DIM, HEAD_DIM), F32),
                        pltpu.VMEM((t, width), F32),
                        pltpu.VMEM((t, width), F32),
                        pltpu.VMEM((t, width), BF16),
                        pltpu.VMEM((t, width), BF16),
                        pltpu.VMEM((t, width), BF16),
                        pltpu.VMEM((N_HEADS, t, GDN_CHUNK), BF16)],
        compiler_params=_params("parallel", "arbitrary"),
        name="gdn_branch",
    )(h, w, b.reshape(1, -1), w_small, b_small.reshape(1, -1), conv_w, pad_row(a_log), pad_row(dt_bias),
      norm_g.reshape(1, -1))


def _lru_kernel(h_ref, w_ref, b_ref, cw_ref, cb_ref, wr_ref, br_ref, wi_ref, bi_ref, lam_ref,
                y_ref, xs_ref, carry_ref):
    @pl.when(pl.program_id(1) == 0)
    def _():
        carry_ref[...] = jnp.zeros_like(carry_ref)
        xs_ref[0:CARRY_ROWS, :] = jnp.zeros((CARRY_ROWS, xs_ref.shape[1]), F32)

    width = cw_ref.shape[1]
    hb = h_ref[0].astype(BF16)
    t = hb.shape[0]
    proj = _dot(hb, w_ref[...]) + b_ref[...]
    xc = _causal_conv(proj[:, :width], xs_ref, cw_ref) + cb_ref[...]
    xcb = xc.astype(BF16)
    r = _sigmoid(_dot(xcb, wr_ref[...]) + br_ref[...])
    gi = _sigmoid(_dot(xcb, wi_ref[...]) + bi_ref[...])
    log_a = -LRU_C * r * _softplus(-lam_ref[...])
    a = jnp.exp(log_a)
    th = jnp.tanh(log_a)
    hs = jnp.sqrt(-2.0 * th / (1.0 - th)) * (gi * xc)
    row = lax.broadcasted_iota(jnp.int32, (t, width), 0)
    d = 1
    while d < t:
        keep = row >= d
        hs = hs + a * jnp.where(keep, pltpu.roll(hs, d, axis=0), 0.0)
        a = a * jnp.where(keep, pltpu.roll(a, d, axis=0), 1.0)
        d *= 2
    hs = hs + a * carry_ref[...]
    carry_ref[...] = hs[t - 1:t, :]
    y_ref[0] = (jax.nn.gelu(proj[:, width:], approximate=True) * hs).astype(y_ref.dtype)


def _lru_branch(h, w, b, conv_w, conv_b, w_r, b_r, w_i, b_i, lam):
    bsz, seq, d = h.shape
    t = min(SEQ_TILE, seq)
    width = conv_w.shape[1]
    row = lambda vec: vec.reshape(1, -1)
    return pl.pallas_call(
        _lru_kernel,
        grid=(bsz, seq // t),
        in_specs=[
            pl.BlockSpec((1, t, d), lambda i, j: (i, j, 0)),
            _const_spec((d, 2 * width)),
            _const_spec((1, 2 * width)),
            _const_spec((CONV_WIDTH, width)),
            _const_spec((1, width)),
            _const_spec((width, width)),
            _const_spec((1, width)),
            _const_spec((width, width)),
            _const_spec((1, width)),
            _const_spec((1, width)),
        ],
        out_specs=pl.BlockSpec((1, t, width), lambda i, j: (i, j, 0)),
        out_shape=jax.ShapeDtypeStruct((bsz, seq, width), BF16),
        scratch_shapes=[pltpu.VMEM((CARRY_ROWS + t, width), F32), pltpu.VMEM((1, width), F32)],
        compiler_params=_params("parallel", "arbitrary"),
        name="lru_branch",
    )(h, w, row(b), conv_w, row(conv_b), w_r, row(b_r), w_i, row(b_i), row(lam))


def _first_index_of_max(x, idx, size):
    m = jnp.max(x, axis=0, keepdims=True)
    first = jnp.min(jnp.where(x == m, idx, size), axis=0, keepdims=True)
    return m, idx == first


def _route(logits_t, bias_col):
    n_exp, t = logits_t.shape
    per_group = n_exp // N_GROUPS
    scores = _sigmoid(logits_t)
    sel = scores + bias_col
    idx_g = lax.broadcasted_iota(jnp.int32, (per_group, t), 0)
    group_scores = []
    for g in range(N_GROUPS):
        x = sel[g * per_group:(g + 1) * per_group, :]
        m1, hit = _first_index_of_max(x, idx_g, per_group)
        m2 = jnp.max(jnp.where(hit, NEG_INF, x), axis=0, keepdims=True)
        group_scores.append(m1 + m2)
    gsc = jnp.concatenate(group_scores, axis=0)
    idx_n = lax.broadcasted_iota(jnp.int32, (N_GROUPS, t), 0)
    gmask = jnp.zeros((N_GROUPS, t), F32)
    for _ in range(TOPK_GROUPS):
        _, hit = _first_index_of_max(gsc, idx_n, N_GROUPS)
        gmask = jnp.where(hit, 1.0, gmask)
        gsc = jnp.where(hit, NEG_INF, gsc)
    emask = jnp.concatenate([jnp.broadcast_to(gmask[g:g + 1, :], (per_group, t)) for g in range(N_GROUPS)], axis=0)
    cand = jnp.where(emask > 0.0, sel, NEG_INF)
    idx_e = lax.broadcasted_iota(jnp.int32, (n_exp, t), 0)
    picked = jnp.zeros((n_exp, t), F32)
    hits = []
    for _ in range(TOP_K):
        _, hit = _first_index_of_max(cand, idx_e, n_exp)
        hits.append(hit)
        picked = jnp.where(hit, 1.0, picked)
        cand = jnp.where(hit, NEG_INF, cand)
    gw = jnp.where(picked > 0.0, scores, 0.0)
    return gw / jnp.sum(gw, axis=0, keepdims=True) * ROUTED_SCALE, picked, hits, idx_e


def _pack_bf16_pairs(x):
    c = x.shape[1] // 2
    hi = pltpu.bitcast(x[:, :c].astype(BF16).astype(F32), jnp.int32)
    lo = pltpu.bitcast(x[:, c:].astype(BF16).astype(F32), jnp.int32)
    return hi | lax.shift_right_logical(lo, jnp.full(lo.shape, 16, jnp.int32))


def _unpack_bf16_pairs(w):
    hi = pltpu.bitcast(w & jnp.int32(-65536), F32)
    lo = pltpu.bitcast(lax.shift_left(w, jnp.full(w.shape, 16, jnp.int32)), F32)
    return hi, lo


def _merge_kernel(h_ref, yr_ref, yg_ref, yl_ref, wmg_ref, bmg_ref, wbr_ref, wout_ref, g_ref, b_ref,
                  rwt_ref, rb_ref, eye_ref, tri_ref, h1_ref, h1b_ref, xp_ref, eidx_ref, rank_ref, gwt_ref,
                  count_ref, *, alpha):
    @pl.when(pl.program_id(0) == 0)
    def _():
        count_ref[...] = jnp.zeros_like(count_ref)

    d = h_ref.shape[1]
    sub = eye_ref.shape[0]
    parts = [slice(s, s + sub) for s in range(0, h_ref.shape[0], sub)]
    logits = []
    for rows in parts:
        h = h_ref[rows, :]
        hb = h.astype(BF16)
        gates = _sigmoid(_dot(hb, wmg_ref[...]) + bmg_ref[...])
        mixed = None
        for n, y_ref in enumerate((yr_ref, yg_ref, yl_ref)):
            term = gates[:, n * d:(n + 1) * d] * _dot(y_ref[rows, :], wbr_ref[n])
            mixed = term if mixed is None else mixed + term
        mix = _dot(mixed.astype(BF16), wout_ref[...])
        h1 = _layer_norm(alpha * h + mix, g_ref[...], b_ref[...])
        h1b = h1.astype(BF16)
        h1_ref[rows, :] = h1
        h1b_ref[rows, :] = h1b
        xp_ref[rows, :] = _pack_bf16_pairs(h1)
        logits.append(_dot_nt(rwt_ref[...], h1b))
    pick = lambda hit, vals, zero: jnp.sum(jnp.where(hit, vals, zero), axis=0, keepdims=True)
    for rows, logit in zip(parts, logits):
        combine_t, picked, hits, idx_e = _route(logit, rb_ref[...])
        rank_full = count_ref[...] + _dot(picked.astype(BF16), tri_ref[...])
        count_ref[...] += jnp.sum(picked, axis=1, keepdims=True)
        eidx_ref[:, rows] = jnp.concatenate([pick(hit, idx_e, 0) for hit in hits], axis=0)
        rank_ref[:, rows] = jnp.concatenate([pick(hit, rank_full, 0.0) for hit in hits], axis=0).astype(jnp.int32)
        gw = jnp.concatenate([pick(hit, combine_t, 0.0) for hit in hits], axis=0)
        gwt_ref[rows, :] = lax.dot_general(eye_ref[...], gw, (((1,), (1,)), ((), ())),
                                           preferred_element_type=F32, precision=lax.Precision.HIGHEST)


def _merge(h2, y_ret, y_gdn, y_lru, w_mg, b_mg, w_branch, w_out, ln_g, ln_b, router_wt, router_b, alpha):
    n, d = h2.shape
    t = min(MERGE_TILE, n)
    width = y_ret.shape[1]
    n_exp = router_wt.shape[0]
    tile = lambda cols: pl.BlockSpec((t, cols), lambda i: (i, 0))
    lane_tile = pl.BlockSpec((TOP_K, t), lambda i: (0, i))
    sub = min(MERGE_SUB, t)
    tri = jnp.triu(jnp.ones((sub, sub), BF16), 1)
    return pl.pallas_call(
        functools.partial(_merge_kernel, alpha=alpha),
        grid=(n // t,),
        in_specs=[
            tile(d), tile(width), tile(width), tile(width),
            _const_spec((d, N_BRANCHES * d)),
            _const_spec((1, N_BRANCHES * d)),
            _const_spec((N_BRANCHES, width, d)),
            _const_spec((d, d)),
            _const_spec((1, d)),
            _const_spec((1, d)),
            _const_spec((n_exp, d)),
            _const_spec((n_exp, 1)),
            _const_spec((sub, sub)),
            _const_spec((sub, sub)),
        ],
        out_specs=[tile(d), tile(d), tile(d // 2), lane_tile, lane_tile, tile(TOP_K),
                   pl.BlockSpec((n_exp, 1), lambda i: (0, 0))],
        out_shape=[jax.ShapeDtypeStruct((n, d), F32), jax.ShapeDtypeStruct((n, d), BF16),
                   jax.ShapeDtypeStruct((n, d // 2), jnp.int32),
                   jax.ShapeDtypeStruct((TOP_K, n), jnp.int32), jax.ShapeDtypeStruct((TOP_K, n), jnp.int32),
                   jax.ShapeDtypeStruct((n, TOP_K), F32), jax.ShapeDtypeStruct((n_exp, 1), F32)],
        compiler_params=_params("arbitrary"),
        name="merge_route",
    )(h2, y_ret, y_gdn, y_lru, w_mg, b_mg.reshape(1, -1), w_branch, w_out, ln_g.reshape(1, -1),
      ln_b.reshape(1, -1), router_wt, router_b.reshape(-1, 1), jnp.eye(sub, dtype=F32), tri)


def _dest_kernel(start_ref, eidx_ref, rank_ref, o_ref, *, n_exp):
    eidx = eidx_ref[...]
    dest = rank_ref[...]
    for e in range(n_exp):
        dest = dest + jnp.where(eidx == e, start_ref[e], 0)
    o_ref[...] = dest


def _dest_rows(seg_start, eidx, rank):
    k, n = eidx.shape
    t = min(DEST_TILE, n)
    blk = pl.BlockSpec((k, t), lambda i, s: (0, i))
    return pl.pallas_call(
        functools.partial(_dest_kernel, n_exp=seg_start.shape[0]),
        grid_spec=pltpu.PrefetchScalarGridSpec(num_scalar_prefetch=1, grid=(n // t,), in_specs=[blk, blk],
                                               out_specs=blk),
        out_shape=jax.ShapeDtypeStruct((k, n), jnp.int32),
        compiler_params=_params("parallel"),
        name="dest_rows",
    )(seg_start, eidx, rank)


def _sc_workers():
    info = plsc.get_sparse_core_info()
    return info.num_cores, info.num_subcores


def _sc_mesh():
    return plsc.VectorSubcoreMesh(core_axis_name="c", subcore_axis_name="s")


def _sc_scatter_rows(x, dest, n_rows):
    n, c = x.shape
    k = dest.shape[0]
    n_cores, n_sub = _sc_workers()
    per_worker = n // (n_cores * n_sub)
    steps = per_worker // SC_WINDOW

    @functools.partial(pl.kernel, out_type=jax.ShapeDtypeStruct((n_rows, c), x.dtype), mesh=_sc_mesh(),
                       scratch_types=[pltpu.VMEM((SC_WINDOW,), jnp.int32), pltpu.VMEM((SC_WINDOW, c), x.dtype)],
                       name="dispatch_rows")
    def scatter(x_hbm, d_hbm, o_hbm, idx_v, rows_v):
        base = (lax.axis_index("s") * n_cores + lax.axis_index("c")) * per_worker

        @pl.loop(0, steps)
        def _(j):
            off = base + j * SC_WINDOW
            pltpu.sync_copy(x_hbm.at[pl.ds(off, SC_WINDOW)], rows_v)
            for kk in range(k):
                pltpu.sync_copy(d_hbm.at[pl.ds(kk * n + off, SC_WINDOW)], idx_v)
                pltpu.sync_copy(rows_v, o_hbm.at[idx_v])

    return scatter(x, dest.reshape(k * n))


def _sc_gather_rows(table, idx):
    b = idx.shape[0]
    c = table.shape[1]
    n_cores, n_sub = _sc_workers()
    per_worker = b // (n_cores * n_sub)
    steps = per_worker // SC_WINDOW

    @functools.partial(pl.kernel, out_type=jax.ShapeDtypeStruct((b, c), table.dtype), mesh=_sc_mesh(),
                       scratch_types=[pltpu.VMEM((SC_WINDOW,), jnp.int32), pltpu.VMEM((SC_WINDOW, c), table.dtype)],
                       name="collect_rows")
    def gather(t_hbm, i_hbm, o_hbm, idx_v, rows_v):
        base = (lax.axis_index("s") * n_cores + lax.axis_index("c")) * per_worker

        @pl.loop(0, steps)
        def _(j):
            off = base + j * SC_WINDOW
            pltpu.sync_copy(i_hbm.at[pl.ds(off, SC_WINDOW)], idx_v)
            pltpu.sync_copy(t_hbm.at[idx_v], rows_v)
            pltpu.sync_copy(rows_v, o_hbm.at[pl.ds(off, SC_WINDOW)])

    return gather(table, idx)


def _expert_block_kernel(meta_ref, xs_ref, wgu_ref, wd_ref, ys_ref, wgu_b, wd_b, *, n_blocks):
    i = pl.program_id(0)

    @pl.when((i == 0) | (meta_ref[i] != meta_ref[jnp.maximum(i - 1, 0)]))
    def _():
        wgu_b[...] = wgu_ref[0].astype(BF16)
        wd_b[...] = wd_ref[0].astype(BF16)

    @pl.when(i < meta_ref[n_blocks])
    def _():
        hi, lo = _unpack_bf16_pairs(xs_ref[...])
        half = hi.shape[1]
        ff = wd_b.shape[0]
        gu = _dot(hi.astype(BF16), wgu_b[:half, :]) + _dot(lo.astype(BF16), wgu_b[half:, :])
        mid = _silu(gu[:, :ff]) * gu[:, ff:]
        ys_ref[...] = _pack_bf16_pairs(_dot(mid.astype(BF16), wd_b[...]))


def _expert_blocks(meta, xs, w_gu, w_down):
    rows, half = xs.shape
    _, d, ff2 = w_gu.shape
    n_blocks = rows // MOE_BLOCK
    row_blk = pl.BlockSpec((MOE_BLOCK, half), lambda i, meta: (jnp.minimum(i, meta[n_blocks] - 1), 0))
    return pl.pallas_call(
        functools.partial(_expert_block_kernel, n_blocks=n_blocks),
        grid_spec=pltpu.PrefetchScalarGridSpec(
            num_scalar_prefetch=1, grid=(n_blocks,),
            in_specs=[row_blk,
                      pl.BlockSpec((1, d, ff2), lambda i, meta: (meta[i], 0, 0)),
                      pl.BlockSpec((1, ff2 // 2, d), lambda i, meta: (meta[i], 0, 0))],
            out_specs=row_blk,
            scratch_shapes=[pltpu.VMEM((d, ff2), BF16), pltpu.VMEM((ff2 // 2, d), BF16)]),
        out_shape=jax.ShapeDtypeStruct((rows, half), jnp.int32),
        compiler_params=_params("arbitrary"),
        name="expert_blocks",
    )(meta, xs, w_gu, w_down)


def _routed_rows(xp, eidx, rank, counts, w_gu, w_down):
    n = xp.shape[0]
    k = eidx.shape[0]
    n_exp = w_gu.shape[0]
    counts = counts.reshape(n_exp).astype(jnp.int32)
    padded = (counts + MOE_BLOCK - 1) // MOE_BLOCK * MOE_BLOCK
    seg_end = jnp.cumsum(padded)
    n_blocks = k * n // MOE_BLOCK + n_exp
    blk_start = jnp.arange(n_blocks, dtype=jnp.int32) * MOE_BLOCK
    blk_expert = jnp.minimum(jnp.sum(seg_end[None, :] <= blk_start[:, None], axis=1), n_exp - 1)
    meta = jnp.concatenate([blk_expert, seg_end[-1:] // MOE_BLOCK]).astype(jnp.int32)
    dest = _dest_rows(seg_end - padded, eidx, rank)
    xs = _sc_scatter_rows(xp, dest, n_blocks * MOE_BLOCK)
    ys = _expert_blocks(meta, xs, w_gu, w_down)
    return _sc_gather_rows(ys, dest.reshape(k * n)).reshape(k, n, -1)


def _shared_ple_kernel(h1_ref, h1b_ref, p_ref, wgu_ref, wd_ref, wpg_ref, bpg_ref, wpe_ref, o_ref, *, alpha):
    xb = h1b_ref[...]
    ff = wd_ref.shape[0]
    gu = _dot(xb, wgu_ref[...])
    shared = _dot((_silu(gu[:, :ff]) * gu[:, ff:]).astype(BF16), wd_ref[...])
    ple = _sigmoid(_dot(xb, wpg_ref[...]) + bpg_ref[...]) * _dot(p_ref[...].astype(BF16), wpe_ref[...])
    o_ref[...] = alpha * h1_ref[...] + shared + ple


def _shared_ple(h1, h1b, p2, sh_w_gu, sh_w_down, ple_w_g, ple_b_g, ple_w_e, alpha):
    n, d = h1.shape
    t = min(ROW_TILE, n)
    pdim = p2.shape[1]
    ff2 = sh_w_gu.shape[1]
    tile = lambda cols: pl.BlockSpec((t, cols), lambda i: (i, 0))
    return pl.pallas_call(
        functools.partial(_shared_ple_kernel, alpha=alpha),
        grid=(n // t,),
        in_specs=[
            tile(d), tile(d), tile(pdim),
            _const_spec((d, ff2)),
            _const_spec((ff2 // 2, d)),
            _const_spec((d, d)),
            _const_spec((1, d)),
            _const_spec((pdim, d)),
        ],
        out_specs=tile(d),
        out_shape=jax.ShapeDtypeStruct((n, d), F32),
        compiler_params=_params("parallel"),
        name="shared_ple",
    )(h1, h1b, p2, sh_w_gu, sh_w_down, ple_w_g, ple_b_g.reshape(1, -1), ple_w_e)


def _combine_norm_kernel(pre_ref, yk_ref, gwt_ref, g_ref, b_ref, o_ref):
    gwt = gwt_ref[...]
    routed_hi = routed_lo = None
    for k in range(yk_ref.shape[0]):
        hi, lo = _unpack_bf16_pairs(yk_ref[k])
        wk = gwt[:, k:k + 1]
        routed_hi = hi * wk if routed_hi is None else routed_hi + hi * wk
        routed_lo = lo * wk if routed_lo is None else routed_lo + lo * wk
    routed = jnp.concatenate([routed_hi, routed_lo], axis=1)
    o_ref[...] = _layer_norm(pre_ref[...] + routed, g_ref[...], b_ref[...])


def _combine_norm(pre, yk, gwt, ln_g, ln_b):
    n, d = pre.shape
    t = min(ROW_TILE, n)
    top_k = yk.shape[0]
    tile = lambda cols: pl.BlockSpec((t, cols), lambda i: (i, 0))
    return pl.pallas_call(
        _combine_norm_kernel,
        grid=(n // t,),
        in_specs=[tile(d), pl.BlockSpec((top_k, t, d // 2), lambda i: (0, i, 0)), tile(top_k),
                  _const_spec((1, d)), _const_spec((1, d))],
        out_specs=tile(d),
        out_shape=jax.ShapeDtypeStruct((n, d), F32),
        compiler_params=_params("parallel"),
        name="combine_norm",
    )(pre, yk, gwt, ln_g.reshape(1, -1), ln_b.reshape(1, -1))


def _block_diag(w):
    g, i, j = w.shape
    eye = jnp.eye(g, dtype=w.dtype)
    return (eye[:, None, :, None] * w[:, :, None, :]).reshape(g * i, g * j)


def kernel(x, p, ln_in_g, ln_in_b, w_in, b_in, ret_norm_g, ret_norm_b, gdn_conv_w, gdn_a_log, gdn_dt_bias, gdn_norm_g, lru_conv_w, lru_conv_b, lru_w_r, lru_b_r, lru_w_i, lru_b_i, lru_lambda, w_branch, w_out, ln1_g, ln1_b, router_w, router_b, exp_w_gu, exp_w_down, sh_w_gu, sh_w_down, ple_w_e, ple_w_g, ple_b_g, ln2_g, ln2_b):
    bsz, seq, d = x.shape
    depth = w_in.shape[0]
    n = bsz * seq
    width = N_HEADS * HEAD_DIM
    alpha = (2 * depth) ** 0.25
    lanes = 128
    o_ret = 0
    o_gdn = o_ret + 4 * width
    o_small = o_gdn + 4 * width
    o_lru = o_small + 2 * N_HEADS
    o_mg = o_lru + 2 * width

    h = _entry_norm(x.reshape(n, d), ln_in_g, ln_in_b)
    for l in range(depth):
        wl, bl = w_in[l], b_in[l]
        bf = lambda a: a.astype(BF16)
        w_small = jnp.zeros((d, lanes), F32).at[:, :2 * N_HEADS].set(wl[:, o_small:o_lru])
        b_small = jnp.zeros((lanes,), F32).at[:2 * N_HEADS].set(bl[o_small:o_lru])
        h3 = h.reshape(bsz, seq, d)
        y_ret = _retention_branch(h3, bf(wl[:, o_ret:o_gdn]), bl[o_ret:o_gdn], ret_norm_g[l], ret_norm_b[l])
        y_gdn = _gdn_branch(h3, bf(wl[:, o_gdn:o_small]), bl[o_gdn:o_small], bf(w_small), b_small,
                            gdn_conv_w[l], gdn_a_log[l], gdn_dt_bias[l], gdn_norm_g[l])
        y_lru = _lru_branch(h3, bf(wl[:, o_lru:o_mg]), bl[o_lru:o_mg], lru_conv_w[l], lru_conv_b[l],
                            bf(_block_diag(lru_w_r[l])), lru_b_r[l], bf(_block_diag(lru_w_i[l])), lru_b_i[l],
                            lru_lambda[l])
        h1, h1b, xp, eidx, rank, gwt, counts = _merge(
            h, y_ret.reshape(n, width), y_gdn.reshape(n, width), y_lru.reshape(n, width), bf(wl[:, o_mg:]), bl[o_mg:],
            bf(w_branch[l]), bf(w_out[l]), ln1_g[l], ln1_b[l], bf(router_w[l].T), router_b[l], alpha)
        pre = _shared_ple(h1, h1b, p[l].reshape(n, -1), bf(sh_w_gu[l]), bf(sh_w_down[l]), bf(ple_w_g[l]), ple_b_g[l],
                          bf(ple_w_e[l]), alpha)
        yk = _routed_rows(xp, eidx, rank, counts, exp_w_gu[l], exp_w_down[l])
        h = _combine_norm(pre, yk, gwt, ln2_g[l], ln2_b[l])
    return h.reshape(bsz, seq, d)
```

```python
import functools

import numpy as np
import jax
import jax.numpy as jnp
from jax import lax
from jax.experimental import pallas as pl
from jax.experimental.pallas import tpu as pltpu
from jax.experimental.pallas import tpu_sc as plsc

F32 = jnp.float32
BF16 = jnp.bfloat16

HEAD_DIM = 128
N_HEADS = 4
LRU_C = 8.0
CONV_WIDTH = 4
N_BRANCHES = 3
ROPE_BASE = 10000.0
N_GROUPS = 8
TOPK_GROUPS = 4
TOP_K = 8
ROUTED_SCALE = 2.5
LN_EPS = 1e-5
GDN_CHUNK = 64
SEQ_TILE = 256
MERGE_TILE = 512
MERGE_SUB = 256
ROW_TILE = 512
MOE_BLOCK = 512
DEST_TILE = 4096
SC_WINDOW = 128
CARRY_ROWS = 8
VMEM_LIMIT_BYTES = 56 * 1024 * 1024
NEG_INF = float("-inf")


def _const_spec(shape):
    nd = len(shape)
    return pl.BlockSpec(shape, lambda *_: (0,) * nd, pipeline_mode=pl.Buffered(1))


def _params(*sem):
    return pltpu.CompilerParams(dimension_semantics=sem, vmem_limit_bytes=VMEM_LIMIT_BYTES)


def _layer_norm(x, g, b):
    mu = jnp.mean(x, axis=-1, keepdims=True)
    xc = x - mu
    var = jnp.mean(xc * xc, axis=-1, keepdims=True)
    return xc * lax.rsqrt(var + LN_EPS) * g + b


def _sigmoid(x):
    return 1.0 / (1.0 + jnp.exp(-x))


def _silu(x):
    return x * _sigmoid(x)


def _softplus(x):
    return jnp.maximum(x, 0.0) + jnp.log1p(jnp.exp(-jnp.abs(x)))


def _dot(a, b):
    return jnp.dot(a, b, preferred_element_type=F32)


def _dot_nt(a, b):
    return lax.dot_general(a, b, (((1,), (1,)), ((), ())), preferred_element_type=F32)


def _dot_tn(a, b):
    return lax.dot_general(a, b, (((0,), (0,)), ((), ())), preferred_element_type=F32)


def _ln_kernel(x_ref, g_ref, b_ref, o_ref):
    o_ref[...] = _layer_norm(x_ref[...], g_ref[...], b_ref[...])


def _entry_norm(x2, g, b):
    n, d = x2.shape
    t = min(1024, n)
    return pl.pallas_call(
        _ln_kernel,
        grid=(n // t,),
        in_specs=[pl.BlockSpec((t, d), lambda i: (i, 0)), _const_spec((1, d)), _const_spec((1, d))],
        out_specs=pl.BlockSpec((t, d), lambda i: (i, 0)),
        out_shape=jax.ShapeDtypeStruct((n, d), F32),
        compiler_params=_params("parallel"),
        name="entry_norm",
    )(x2, g.reshape(1, d), b.reshape(1, d))


def _retention_kernel(h_ref, w_ref, b_ref, cos_ref, sin_ref, dmat_ref, qd_ref, kd_ref, ng_ref, nb_ref,
                      y_ref, state_ref, *, chunk_decay):
    @pl.when(pl.program_id(1) == 0)
    def _():
        state_ref[...] = jnp.zeros_like(state_ref)

    width = N_HEADS * HEAD_DIM
    hb = h_ref[0].astype(BF16)
    proj = _dot(hb, w_ref[...]) + b_ref[...]
    cos = cos_ref[...]
    sin = sin_ref[...]
    heads = range(N_HEADS)
    qs, ks, vbs = [], [], []
    for hh in heads:
        lo = hh * HEAD_DIM
        q = proj[:, lo:lo + HEAD_DIM]
        k = proj[:, width + lo:width + lo + HEAD_DIM]
        qs.append(q * cos + pltpu.roll(q, HEAD_DIM // 2, axis=1) * sin)
        ks.append((k * cos + pltpu.roll(k, HEAD_DIM // 2, axis=1) * sin) * (HEAD_DIM ** -0.5))
        vbs.append(proj[:, 2 * width + lo:2 * width + lo + HEAD_DIM].astype(BF16))
    states = [state_ref[hh] for hh in heads]
    scores = [(_dot_nt(qs[hh].astype(BF16), ks[hh].astype(BF16)) * dmat_ref[hh]).astype(BF16) for hh in heads]
    inter = [_dot((qs[hh] * qd_ref[hh]).astype(BF16), states[hh].astype(BF16)) for hh in heads]
    for hh in heads:
        state_ref[hh] = states[hh] * chunk_decay[hh] + _dot_tn((ks[hh] * kd_ref[hh]).astype(BF16), vbs[hh])
    outs = [_dot(scores[hh], vbs[hh]) + inter[hh] for hh in heads]
    for hh in heads:
        lo = hh * HEAD_DIM
        o = outs[hh]
        gate = proj[:, 3 * width + lo:3 * width + lo + HEAD_DIM]
        mu = jnp.mean(o, axis=-1, keepdims=True)
        oc = o - mu
        var = jnp.mean(oc * oc, axis=-1, keepdims=True)
        on = oc * lax.rsqrt(var + LN_EPS) * ng_ref[:, lo:lo + HEAD_DIM] + nb_ref[:, lo:lo + HEAD_DIM]
        y_ref[0, :, lo:lo + HEAD_DIM] = (_silu(gate) * on).astype(y_ref.dtype)


def _retention_tables(seq, tile):
    half = HEAD_DIM // 2
    inv_freq = ROPE_BASE ** (-np.linspace(0.0, 1.0, half))
    ang = np.arange(seq)[:, None] * inv_freq[None, :]
    cos = np.concatenate([np.cos(ang), np.cos(ang)], axis=1)
    sin = np.concatenate([-np.sin(ang), np.sin(ang)], axis=1)
    log_gamma = np.log1p(-np.exp2(-5.0 - np.arange(N_HEADS)))
    pos = np.arange(tile)
    diff = pos[:, None] - pos[None, :]
    dmat = np.where(diff >= 0, np.exp(log_gamma[:, None, None] * np.maximum(diff, 0)), 0.0)
    qd = np.exp(log_gamma[:, None] * (pos + 1.0))[:, :, None] * np.ones((1, 1, HEAD_DIM))
    kd = np.exp(log_gamma[:, None] * (tile - 1.0 - pos))[:, :, None] * np.ones((1, 1, HEAD_DIM))
    chunk_decay = tuple(float(c) for c in np.exp(log_gamma * tile))
    as32 = lambda a: jnp.asarray(a, F32)
    return as32(cos), as32(sin), as32(dmat), as32(qd), as32(kd), chunk_decay


def _retention_branch(h, w, b, norm_g, norm_b):
    bsz, seq, d = h.shape
    t = min(SEQ_TILE, seq)
    width = N_HEADS * HEAD_DIM
    cos, sin, dmat, qd, kd, chunk_decay = _retention_tables(seq, t)
    return pl.pallas_call(
        functools.partial(_retention_kernel, chunk_decay=chunk_decay),
        grid=(bsz, seq // t),
        in_specs=[
            pl.BlockSpec((1, t, d), lambda i, j: (i, j, 0)),
            _const_spec((d, 4 * width)),
            _const_spec((1, 4 * width)),
            pl.BlockSpec((t, HEAD_DIM), lambda i, j: (j, 0)),
            pl.BlockSpec((t, HEAD_DIM), lambda i, j: (j, 0)),
            _const_spec((N_HEADS, t, t)),
            _const_spec((N_HEADS, t, HEAD_DIM)),
            _const_spec((N_HEADS, t, HEAD_DIM)),
            _const_spec((1, width)),
            _const_spec((1, width)),
        ],
        out_specs=pl.BlockSpec((1, t, width), lambda i, j: (i, j, 0)),
        out_shape=jax.ShapeDtypeStruct((bsz, seq, width), BF16),
        scratch_shapes=[pltpu.VMEM((N_HEADS, HEAD_DIM, HEAD_DIM), F32)],
        compiler_params=_params("parallel", "arbitrary"),
        name="retention_branch",
    )(h, w, b.reshape(1, -1), cos, sin, dmat, qd, kd, norm_g.reshape(1, -1), norm_b.reshape(1, -1))


def _causal_conv(x, xs_ref, cw_ref):
    t = x.shape[0]
    xs_ref[CARRY_ROWS:, :] = x
    acc = None
    for j in range(CONV_WIDTH):
        start = CARRY_ROWS - (CONV_WIDTH - 1) + j
        term = xs_ref[start:start + t, :] * cw_ref[j:j + 1, :]
        acc = term if acc is None else acc + term
    xs_ref[0:CARRY_ROWS, :] = xs_ref[t:t + CARRY_ROWS, :]
    return acc


def _cumsum_rows(x):
    n = x.shape[0]
    row = lax.broadcasted_iota(jnp.int32, x.shape, 0)
    d = 1
    while d < n:
        x = x + jnp.where(row >= d, pltpu.roll(x, d, axis=0), 0.0)
        d *= 2
    return x


def _gdn_kernel(h_ref, w_ref, b_ref, ws_ref, bs_ref, cw_ref, alog_ref, dtb_ref, ng_ref,
                y_ref, xs_ref, state_ref, u_ref, wf_ref, w_s_ref, qd_ref, kd_ref, qk_ref):
    @pl.when(pl.program_id(1) == 0)
    def _():
        state_ref[...] = jnp.zeros_like(state_ref)
        xs_ref[0:CARRY_ROWS, :] = jnp.zeros((CARRY_ROWS, xs_ref.shape[1]), F32)

    width = N_HEADS * HEAD_DIM
    c = GDN_CHUNK
    hb = h_ref[0].astype(BF16)
    t = hb.shape[0]
    proj = _dot(hb, w_ref[...]) + b_ref[...]
    small = _dot(hb, ws_ref[...]) + bs_ref[...]
    qkv = _silu(_causal_conv(proj[:, :3 * width], xs_ref, cw_ref))
    beta_all = _sigmoid(small)
    la_all = -jnp.exp(alog_ref[...]) * _softplus(small + dtb_ref[...])

    ri = lax.broadcasted_iota(jnp.int32, (c, c), 0)
    ci = lax.broadcasted_iota(jnp.int32, (c, c), 1)
    lower = ri >= ci
    strict = ri > ci

    items = [(n, hh) for n in range(t // c) for hh in range(N_HEADS)]
    gcs = {}
    for n in range(t // c):
        la_c = la_all[n * c:(n + 1) * c, :]
        gc_c = _cumsum_rows(la_c)
        gcs[n] = (la_c, gc_c, jnp.exp(gc_c))
    g_last, pws, rems = {}, {}, {}
    for n, hh in items:
        r0, lo = n * c, hh * HEAD_DIM
        rows, cols = slice(r0, r0 + c), slice(lo, lo + HEAD_DIM)
        la_c, gc_c, egc_c = gcs[n]
        q = qkv[rows, lo:lo + HEAD_DIM]
        k = qkv[rows, width + lo:width + lo + HEAD_DIM]
        v = qkv[rows, 2 * width + lo:2 * width + lo + HEAD_DIM]
        q = q * lax.rsqrt(jnp.sum(q * q, axis=-1, keepdims=True) + 1e-6) * (HEAD_DIM ** -0.5)
        k = k * lax.rsqrt(jnp.sum(k * k, axis=-1, keepdims=True) + 1e-6)
        beta = beta_all[rows, hh:hh + 1]
        la = la_c[:, N_HEADS + hh:N_HEADS + hh + 1]
        gc = gc_c[:, N_HEADS + hh:N_HEADS + hh + 1]
        egc = egc_c[:, N_HEADS + hh:N_HEADS + hh + 1]
        gc_row = jnp.sum(jnp.where(ri <= ci, jnp.broadcast_to(la, (c, c)), 0.0), axis=0, keepdims=True)
        gc_last = gc_row[:, c - 1:c]
        decay = jnp.where(lower, jnp.exp(jnp.where(lower, gc - gc_row, 0.0)), 0.0)
        kb = k * beta
        kbf = k.astype(BF16)
        a_neg = jnp.where(strict, -(_dot_nt(kb.astype(BF16), kbf) * decay), 0.0)
        pws[n, hh] = a_neg
        rems[n, hh] = a_neg
        u_ref[rows, cols] = v * beta
        wf_ref[rows, cols] = kb * egc
        qk_ref[hh, rows, :] = (_dot_nt(q.astype(BF16), kbf) * decay).astype(BF16)
        qd_ref[rows, cols] = (q * egc).astype(BF16)
        kd_ref[rows, cols] = (k * jnp.exp(gc_last - gc)).astype(BF16)
        g_last[n, hh] = jnp.exp(gc_last)
    m = 2
    while m < c:
        for it in items:
            pwb = pws[it].astype(BF16)
            pws[it] = _dot(pwb, pwb)
        for it in items:
            rems[it] = rems[it] + pws[it] + _dot(rems[it].astype(BF16), pws[it].astype(BF16))
        m *= 2
    for n, hh in items:
        rows, cols = slice(n * c, (n + 1) * c), slice(hh * HEAD_DIM, (hh + 1) * HEAD_DIM)
        remb = rems[n, hh].astype(BF16)
        u_ref[rows, cols] = u_ref[rows, cols] + _dot(remb, u_ref[rows, cols].astype(BF16))
        w_s_ref[rows, cols] = (wf_ref[rows, cols] + _dot(remb, wf_ref[rows, cols].astype(BF16))).astype(BF16)

    heads = range(N_HEADS)
    for n in range(t // c):
        rows = slice(n * c, (n + 1) * c)
        cols = [slice(hh * HEAD_DIM, (hh + 1) * HEAD_DIM) for hh in heads]
        states = [state_ref[hh] for hh in heads]
        sbs = [s.astype(BF16) for s in states]
        vnbs = [(u_ref[rows, cols[hh]] - _dot(w_s_ref[rows, cols[hh]], sbs[hh])).astype(BF16) for hh in heads]
        outs = [_dot(qd_ref[rows, cols[hh]], sbs[hh]) + _dot(qk_ref[hh, rows, :], vnbs[hh]) for hh in heads]
        for hh in heads:
            state_ref[hh] = states[hh] * g_last[n, hh] + _dot_tn(kd_ref[rows, cols[hh]], vnbs[hh])
        for hh in heads:
            o = outs[hh]
            o = o * lax.rsqrt(jnp.mean(o * o, axis=-1, keepdims=True) + 1e-6) * ng_ref[...]
            og = proj[rows, 3 * width + hh * HEAD_DIM:3 * width + (hh + 1) * HEAD_DIM]
            y_ref[0, rows, cols[hh]] = (o * _silu(og)).astype(y_ref.dtype)


def _gdn_branch(h, w, b, w_small, b_small, conv_w, a_log, dt_bias, norm_g):
    bsz, seq, d = h.shape
    t = min(SEQ_TILE, seq)
    width = N_HEADS * HEAD_DIM
    lanes = w_small.shape[1]
    pad_row = lambda vec: jnp.zeros((1, lanes), F32).at[0, N_HEADS:2 * N_HEADS].set(vec.astype(F32))
    return pl.pallas_call(
        _gdn_kernel,
        grid=(bsz, seq // t),
        in_specs=[
            pl.BlockSpec((1, t, d), lambda i, j: (i, j, 0)),
            _const_spec((d, 4 * width)),
            _const_spec((1, 4 * width)),
            _const_spec((d, lanes)),
            _const_spec((1, lanes)),
            _const_spec((CONV_WIDTH, 3 * width)),
            _const_spec((1, lanes)),
            _const_spec((1, lanes)),
            _const_spec((1, HEAD_DIM)),
        ],
        out_specs=pl.BlockSpec((1, t, width), lambda i, j: (i, j, 0)),
        out_shape=jax.ShapeDtypeStruct((bsz, seq, width), BF16),
        scratch_shapes=[pltpu.VMEM((CARRY_ROWS + t, 3 * width), F32),
                        pltpu.VMEM((N_HEADS, HEAD_DIM, HEAD_DIM), F32),
                        pltpu.VMEM((t, width), F32),
                        pltpu.VMEM((t, width), F32),
                        pltpu.VMEM((t, width), BF16),
                        pltpu.VMEM((t, width), BF16),
                        pltpu.VMEM((t, width), BF16),
                        pltpu.VMEM((N_HEADS, t, GDN_CHUNK), BF16)],
        compiler_params=_params("parallel", "arbitrary"),
        name="gdn_branch",
    )(h, w, b.reshape(1, -1), w_small, b_small.reshape(1, -1), conv_w, pad_row(a_log), pad_row(dt_bias),
      norm_g.reshape(1, -1))


def _lru_kernel(h_ref, w_ref, b_ref, cw_ref, cb_ref, wr_ref, br_ref, wi_ref, bi_ref, lam_ref,
                y_ref, xs_ref, carry_ref):
    @pl.when(pl.program_id(1) == 0)
    def _():
        carry_ref[...] = jnp.zeros_like(carry_ref)
        xs_ref[0:CARRY_ROWS, :] = jnp.zeros((CARRY_ROWS, xs_ref.shape[1]), F32)

    width = cw_ref.shape[1]
    hb = h_ref[0].astype(BF16)
    t = hb.shape[0]
    proj = _dot(hb, w_ref[...]) + b_ref[...]
    xc = _causal_conv(proj[:, :width], xs_ref, cw_ref) + cb_ref[...]
    xcb = xc.astype(BF16)
    r = _sigmoid(_dot(xcb, wr_ref[...]) + br_ref[...])
    gi = _sigmoid(_dot(xcb, wi_ref[...]) + bi_ref[...])
    log_a = -LRU_C * r * _softplus(-lam_ref[...])
    a = jnp.exp(log_a)
    th = jnp.tanh(log_a)
    hs = jnp.sqrt(-2.0 * th / (1.0 - th)) * (gi * xc)
    row = lax.broadcasted_iota(jnp.int32, (t, width), 0)
    d = 1
    while d < t:
        keep = row >= d
        hs = hs + a * jnp.where(keep, pltpu.roll(hs, d, axis=0), 0.0)
        a = a * jnp.where(keep, pltpu.roll(a, d, axis=0), 1.0)
        d *= 2
    hs = hs + a * carry_ref[...]
    carry_ref[...] = hs[t - 1:t, :]
    y_ref[0] = (jax.nn.gelu(proj[:, width:], approximate=True) * hs).astype(y_ref.dtype)


def _lru_branch(h, w, b, conv_w, conv_b, w_r, b_r, w_i, b_i, lam):
    bsz, seq, d = h.shape
    t = min(SEQ_TILE, seq)
    width = conv_w.shape[1]
    row = lambda vec: vec.reshape(1, -1)
    return pl.pallas_call(
        _lru_kernel,
        grid=(bsz, seq // t),
        in_specs=[
            pl.BlockSpec((1, t, d), lambda i, j: (i, j, 0)),
            _const_spec((d, 2 * width)),
            _const_spec((1, 2 * width)),
            _const_spec((CONV_WIDTH, width)),
            _const_spec((1, width)),
            _const_spec((width, width)),
            _const_spec((1, width)),
            _const_spec((width, width)),
            _const_spec((1, width)),
            _const_spec((1, width)),
        ],
        out_specs=pl.BlockSpec((1, t, width), lambda i, j: (i, j, 0)),
        out_shape=jax.ShapeDtypeStruct((bsz, seq, width), BF16),
        scratch_shapes=[pltpu.VMEM((CARRY_ROWS + t, width), F32), pltpu.VMEM((1, width), F32)],
        compiler_params=_params("parallel", "arbitrary"),
        name="lru_branch",
    )(h, w, row(b), conv_w, row(conv_b), w_r, row(b_r), w_i, row(b_i), row(lam))


def _first_index_of_max(x, idx, size):
    m = jnp.max(x, axis=0, keepdims=True)
    first = jnp.min(jnp.where(x == m, idx, size), axis=0, keepdims=True)
    return m, idx == first


def _route(logits_t, bias_col):
    n_exp, t = logits_t.shape
    per_group = n_exp // N_GROUPS
    scores = _sigmoid(logits_t)
    sel = scores + bias_col
    idx_g = lax.broadcasted_iota(jnp.int32, (per_group, t), 0)
    group_scores = []
    for g in range(N_GROUPS):
        x = sel[g * per_group:(g + 1) * per_group, :]
        m1, hit = _first_index_of_max(x, idx_g, per_group)
        m2 = jnp.max(jnp.where(hit, NEG_INF, x), axis=0, keepdims=True)
        group_scores.append(m1 + m2)
    gsc = jnp.concatenate(group_scores, axis=0)
    idx_n = lax.broadcasted_iota(jnp.int32, (N_GROUPS, t), 0)
    gmask = jnp.zeros((N_GROUPS, t), F32)
    for _ in range(TOPK_GROUPS):
        _, hit = _first_index_of_max(gsc, idx_n, N_GROUPS)
        gmask = jnp.where(hit, 1.0, gmask)
        gsc = jnp.where(hit, NEG_INF, gsc)
    emask = jnp.concatenate([jnp.broadcast_to(gmask[g:g + 1, :], (per_group, t)) for g in range(N_GROUPS)], axis=0)
    cand = jnp.where(emask > 0.0, sel, NEG_INF)
    idx_e = lax.broadcasted_iota(jnp.int32, (n_exp, t), 0)
    picked = jnp.zeros((n_exp, t), F32)
    hits = []
    for _ in range(TOP_K):
        _, hit = _first_index_of_max(cand, idx_e, n_exp)
        hits.append(hit)
        picked = jnp.where(hit, 1.0, picked)
        cand = jnp.where(hit, NEG_INF, cand)
    gw = jnp.where(picked > 0.0, scores, 0.0)
    return gw / jnp.sum(gw, axis=0, keepdims=True) * ROUTED_SCALE, picked, hits, idx_e


def _pack_bf16_pairs(x):
    c = x.shape[1] // 2
    hi = pltpu.bitcast(x[:, :c].astype(BF16).astype(F32), jnp.int32)
    lo = pltpu.bitcast(x[:, c:].astype(BF16).astype(F32), jnp.int32)
    return hi | lax.shift_right_logical(lo, jnp.full(lo.shape, 16, jnp.int32))


def _unpack_bf16_pairs(w):
    hi = pltpu.bitcast(w & jnp.int32(-65536), F32)
    lo = pltpu.bitcast(lax.shift_left(w, jnp.full(w.shape, 16, jnp.int32)), F32)
    return hi, lo


def _merge_kernel(h_ref, yr_ref, yg_ref, yl_ref, wmg_ref, bmg_ref, wbr_ref, wout_ref, g_ref, b_ref,
                  rwt_ref, rb_ref, eye_ref, tri_ref, h1_ref, h1b_ref, xp_ref, eidx_ref, rank_ref, gwt_ref,
                  count_ref, *, alpha):
    @pl.when(pl.program_id(0) == 0)
    def _():
        count_ref[...] = jnp.zeros_like(count_ref)

    d = h_ref.shape[1]
    sub = eye_ref.shape[0]
    parts = [slice(s, s + sub) for s in range(0, h_ref.shape[0], sub)]
    logits = []
    for rows in parts:
        h = h_ref[rows, :]
        hb = h.astype(BF16)
        gates = _sigmoid(_dot(hb, wmg_ref[...]) + bmg_ref[...])
        mixed = None
        for n, y_ref in enumerate((yr_ref, yg_ref, yl_ref)):
            term = gates[:, n * d:(n + 1) * d] * _dot(y_ref[rows, :], wbr_ref[n])
            mixed = term if mixed is None else mixed + term
        mix = _dot(mixed.astype(BF16), wout_ref[...])
        h1 = _layer_norm(alpha * h + mix, g_ref[...], b_ref[...])
        h1b = h1.astype(BF16)
        h1_ref[rows, :] = h1
        h1b_ref[rows, :] = h1b
        xp_ref[rows, :] = _pack_bf16_pairs(h1)
        logits.append(_dot_nt(rwt_ref[...], h1b))
    pick = lambda hit, vals, zero: jnp.sum(jnp.where(hit, vals, zero), axis=0, keepdims=True)
    for rows, logit in zip(parts, logits):
        combine_t, picked, hits, idx_e = _route(logit, rb_ref[...])
        rank_full = count_ref[...] + _dot(picked.astype(BF16), tri_ref[...])
        count_ref[...] += jnp.sum(picked, axis=1, keepdims=True)
        eidx_ref[:, rows] = jnp.concatenate([pick(hit, idx_e, 0) for hit in hits], axis=0)
        rank_ref[:, rows] = jnp.concatenate([pick(hit, rank_full, 0.0) for hit in hits], axis=0).astype(jnp.int32)
        gw = jnp.concatenate([pick(hit, combine_t, 0.0) for hit in hits], axis=0)
        gwt_ref[rows, :] = lax.dot_general(eye_ref[...], gw, (((1,), (1,)), ((), ())),
                                           preferred_element_type=F32, precision=lax.Precision.HIGHEST)


def _merge(h2, y_ret, y_gdn, y_lru, w_mg, b_mg, w_branch, w_out, ln_g, ln_b, router_wt, router_b, alpha):
    n, d = h2.shape
    t = min(MERGE_TILE, n)
    width = y_ret.shape[1]
    n_exp = router_wt.shape[0]
    tile = lambda cols: pl.BlockSpec((t, cols), lambda i: (i, 0))
    lane_tile = pl.BlockSpec((TOP_K, t), lambda i: (0, i))
    sub = min(MERGE_SUB, t)
    tri = jnp.triu(jnp.ones((sub, sub), BF16), 1)
    return pl.pallas_call(
        functools.partial(_merge_kernel, alpha=alpha),
        grid=(n // t,),
        in_specs=[
            tile(d), tile(width), tile(width), tile(width),
            _const_spec((d, N_BRANCHES * d)),
            _const_spec((1, N_BRANCHES * d)),
            _const_spec((N_BRANCHES, width, d)),
            _const_spec((d, d)),
            _const_spec((1, d)),
            _const_spec((1, d)),
            _const_spec((n_exp, d)),
            _const_spec((n_exp, 1)),
            _const_spec((sub, sub)),
            _const_spec((sub, sub)),
        ],
        out_specs=[tile(d), tile(d), tile(d // 2), lane_tile, lane_tile, tile(TOP_K),
                   pl.BlockSpec((n_exp, 1), lambda i: (0, 0))],
        out_shape=[jax.ShapeDtypeStruct((n, d), F32), jax.ShapeDtypeStruct((n, d), BF16),
                   jax.ShapeDtypeStruct((n, d // 2), jnp.int32),
                   jax.ShapeDtypeStruct((TOP_K, n), jnp.int32), jax.ShapeDtypeStruct((TOP_K, n), jnp.int32),
                   jax.ShapeDtypeStruct((n, TOP_K), F32), jax.ShapeDtypeStruct((n_exp, 1), F32)],
        compiler_params=_params("arbitrary"),
        name="merge_route",
    )(h2, y_ret, y_gdn, y_lru, w_mg, b_mg.reshape(1, -1), w_branch, w_out, ln_g.reshape(1, -1),
      ln_b.reshape(1, -1), router_wt, router_b.reshape(-1, 1), jnp.eye(sub, dtype=F32), tri)


def _dest_kernel(start_ref, eidx_ref, rank_ref, o_ref, *, n_exp):
    eidx = eidx_ref[...]
    dest = rank_ref[...]
    for e in range(n_exp):
        dest = dest + jnp.where(eidx == e, start_ref[e], 0)
    o_ref[...] = dest


def _dest_rows(seg_start, eidx, rank):
    k, n = eidx.shape
    t = min(DEST_TILE, n)
    blk = pl.BlockSpec((k, t), lambda i, s: (0, i))
    return pl.pallas_call(
        functools.partial(_dest_kernel, n_exp=seg_start.shape[0]),
        grid_spec=pltpu.PrefetchScalarGridSpec(num_scalar_prefetch=1, grid=(n // t,), in_specs=[blk, blk],
                                               out_specs=blk),
        out_shape=jax.ShapeDtypeStruct((k, n), jnp.int32),
        compiler_params=_params("parallel"),
        name="dest_rows",
    )(seg_start, eidx, rank)


def _sc_workers():
    info = plsc.get_sparse_core_info()
    return info.num_cores, info.num_subcores


def _sc_mesh():
    return plsc.VectorSubcoreMesh(core_axis_name="c", subcore_axis_name="s")


def _sc_scatter_rows(x, dest, n_rows):
    n, c = x.shape
    k = dest.shape[0]
    n_cores, n_sub = _sc_workers()
    per_worker = n // (n_cores * n_sub)
    steps = per_worker // SC_WINDOW

    @functools.partial(pl.kernel, out_type=jax.ShapeDtypeStruct((n_rows, c), x.dtype), mesh=_sc_mesh(),
                       scratch_types=[pltpu.VMEM((k, SC_WINDOW), jnp.int32), pltpu.VMEM((SC_WINDOW, c), x.dtype),
                                      pltpu.SemaphoreType.DMA],
                       name="dispatch_rows")
    def scatter(x_hbm, d_hbm, o_hbm, idx_v, rows_v, sem):
        base = (lax.axis_index("s") * n_cores + lax.axis_index("c")) * per_worker

        @pl.loop(0, steps)
        def _(j):
            off = base + j * SC_WINDOW
            pltpu.sync_copy(x_hbm.at[pl.ds(off, SC_WINDOW)], rows_v)
            for kk in range(k):
                pltpu.sync_copy(d_hbm.at[pl.ds(kk * n + off, SC_WINDOW)], idx_v.at[kk])
            copies = [pltpu.async_copy(rows_v, o_hbm.at[idx_v.at[kk]], sem) for kk in range(k)]
            for cp in copies:
                cp.wait()

    return scatter(x, dest.reshape(k * n))


def _sc_gather_rows(table, idx):
    b = idx.shape[0]
    c = table.shape[1]
    n_cores, n_sub = _sc_workers()
    per_worker = b // (n_cores * n_sub)
    steps = per_worker // SC_WINDOW

    half = SC_WINDOW // 2
    half_buf = lambda dtype, *shape: pltpu.VMEM((half,) + shape, dtype)

    @functools.partial(pl.kernel, out_type=jax.ShapeDtypeStruct((b, c), table.dtype), mesh=_sc_mesh(),
                       scratch_types=[half_buf(jnp.int32), half_buf(jnp.int32), half_buf(table.dtype, c),
                                      half_buf(table.dtype, c), pltpu.SemaphoreType.DMA, pltpu.SemaphoreType.DMA,
                                      pltpu.SemaphoreType.DMA, pltpu.SemaphoreType.DMA],
                       name="collect_rows")
    def gather(t_hbm, i_hbm, o_hbm, idx_a, idx_b, rows_a, rows_b, gat_a, gat_b, put_a, put_b):
        base = (lax.axis_index("s") * n_cores + lax.axis_index("c")) * per_worker

        @pl.loop(0, steps)
        def _(j):
            off_a = base + j * SC_WINDOW
            off_b = off_a + half
            pltpu.sync_copy(i_hbm.at[pl.ds(off_a, half)], idx_a)
            in_a = pltpu.async_copy(t_hbm.at[idx_a], rows_a, gat_a)
            pltpu.sync_copy(i_hbm.at[pl.ds(off_b, half)], idx_b)
            in_b = pltpu.async_copy(t_hbm.at[idx_b], rows_b, gat_b)
            in_a.wait()
            out_a = pltpu.async_copy(rows_a, o_hbm.at[pl.ds(off_a, half)], put_a)
            in_b.wait()
            out_b = pltpu.async_copy(rows_b, o_hbm.at[pl.ds(off_b, half)], put_b)
            out_a.wait()
            out_b.wait()

    return gather(table, idx)


def _expert_block_kernel(meta_ref, xs_ref, wgu_ref, wd_ref, ys_ref, wgu_b, wd_b, *, n_blocks):
    i = pl.program_id(0)

    @pl.when((i == 0) | (meta_ref[i] != meta_ref[jnp.maximum(i - 1, 0)]))
    def _():
        wgu_b[...] = wgu_ref[0, 0].astype(BF16)
        wd_b[...] = wd_ref[0, 0].astype(BF16)

    @pl.when(i < meta_ref[n_blocks])
    def _():
        half = xs_ref.shape[1]
        ff = wd_b.shape[0]
        sub = xs_ref.shape[0] // 2
        parts = [slice(0, sub), slice(sub, 2 * sub)]
        xs = [_unpack_bf16_pairs(xs_ref[rows, :]) for rows in parts]
        gus = [_dot(hi.astype(BF16), wgu_b[:half, :]) + _dot(lo.astype(BF16), wgu_b[half:, :]) for hi, lo in xs]
        mids = [(_silu(gu[:, :ff]) * gu[:, ff:]).astype(BF16) for gu in gus]
        outs = [_dot(mid, wd_b[...]) for mid in mids]
        for rows, out in zip(parts, outs):
            ys_ref[rows, :] = _pack_bf16_pairs(out)


def _expert_blocks(meta, xs, w_gu, w_down, layer):
    rows, half = xs.shape
    _, _, d, ff2 = w_gu.shape
    n_blocks = rows // MOE_BLOCK
    row_blk = pl.BlockSpec((MOE_BLOCK, half), lambda i, meta: (jnp.minimum(i, meta[n_blocks] - 1), 0))
    return pl.pallas_call(
        functools.partial(_expert_block_kernel, n_blocks=n_blocks),
        grid_spec=pltpu.PrefetchScalarGridSpec(
            num_scalar_prefetch=1, grid=(n_blocks,),
            in_specs=[row_blk,
                      pl.BlockSpec((1, 1, d, ff2), lambda i, meta: (layer, meta[i], 0, 0)),
                      pl.BlockSpec((1, 1, ff2 // 2, d), lambda i, meta: (layer, meta[i], 0, 0))],
            out_specs=row_blk,
            scratch_shapes=[pltpu.VMEM((d, ff2), BF16), pltpu.VMEM((ff2 // 2, d), BF16)]),
        out_shape=jax.ShapeDtypeStruct((rows, half), jnp.int32),
        compiler_params=_params("arbitrary"),
        name="expert_blocks",
    )(meta, xs, w_gu, w_down)


def _routed_rows(xp, eidx, rank, counts, w_gu, w_down, layer):
    n = xp.shape[0]
    k = eidx.shape[0]
    n_exp = w_gu.shape[1]
    counts = counts.reshape(n_exp).astype(jnp.int32)
    padded = (counts + MOE_BLOCK - 1) // MOE_BLOCK * MOE_BLOCK
    seg_end = jnp.cumsum(padded)
    n_blocks = k * n // MOE_BLOCK + n_exp
    blk_start = jnp.arange(n_blocks, dtype=jnp.int32) * MOE_BLOCK
    blk_expert = jnp.minimum(jnp.sum(seg_end[None, :] <= blk_start[:, None], axis=1), n_exp - 1)
    meta = jnp.concatenate([blk_expert, seg_end[-1:] // MOE_BLOCK]).astype(jnp.int32)
    dest = _dest_rows(seg_end - padded, eidx, rank)
    xs = _sc_scatter_rows(xp, dest, n_blocks * MOE_BLOCK)
    ys = _expert_blocks(meta, xs, w_gu, w_down, layer)
    return _sc_gather_rows(ys, dest.reshape(k * n)).reshape(k, n, -1)


def _shared_ple_kernel(h1_ref, h1b_ref, p_ref, wgu_ref, wd_ref, wpg_ref, bpg_ref, wpe_ref, o_ref, *, alpha):
    xb = h1b_ref[...]
    ff = wd_ref.shape[0]
    gu = _dot(xb, wgu_ref[...])
    shared = _dot((_silu(gu[:, :ff]) * gu[:, ff:]).astype(BF16), wd_ref[...])
    ple = _sigmoid(_dot(xb, wpg_ref[...]) + bpg_ref[...]) * _dot(p_ref[0].astype(BF16), wpe_ref[...])
    o_ref[...] = alpha * h1_ref[...] + shared + ple


def _shared_ple(h1, h1b, p3, layer, sh_w_gu, sh_w_down, ple_w_g, ple_b_g, ple_w_e, alpha):
    n, d = h1.shape
    t = min(ROW_TILE, n)
    pdim = p3.shape[2]
    ff2 = sh_w_gu.shape[1]
    tile = lambda cols: pl.BlockSpec((t, cols), lambda i: (i, 0))
    return pl.pallas_call(
        functools.partial(_shared_ple_kernel, alpha=alpha),
        grid=(n // t,),
        in_specs=[
            tile(d), tile(d), pl.BlockSpec((1, t, pdim), lambda i: (layer, i, 0)),
            _const_spec((d, ff2)),
            _const_spec((ff2 // 2, d)),
            _const_spec((d, d)),
            _const_spec((1, d)),
            _const_spec((pdim, d)),
        ],
        out_specs=tile(d),
        out_shape=jax.ShapeDtypeStruct((n, d), F32),
        compiler_params=_params("parallel"),
        name="shared_ple",
    )(h1, h1b, p3, sh_w_gu, sh_w_down, ple_w_g, ple_b_g.reshape(1, -1), ple_w_e)


def _combine_norm_kernel(pre_ref, yk_ref, gwt_ref, g_ref, b_ref, o_ref):
    gwt = gwt_ref[...]
    routed_hi = routed_lo = None
    for k in range(yk_ref.shape[0]):
        hi, lo = _unpack_bf16_pairs(yk_ref[k])
        wk = gwt[:, k:k + 1]
        routed_hi = hi * wk if routed_hi is None else routed_hi + hi * wk
        routed_lo = lo * wk if routed_lo is None else routed_lo + lo * wk
    routed = jnp.concatenate([routed_hi, routed_lo], axis=1)
    o_ref[...] = _layer_norm(pre_ref[...] + routed, g_ref[...], b_ref[...])


def _combine_norm(pre, yk, gwt, ln_g, ln_b):
    n, d = pre.shape
    t = min(ROW_TILE, n)
    top_k = yk.shape[0]
    tile = lambda cols: pl.BlockSpec((t, cols), lambda i: (i, 0))
    return pl.pallas_call(
        _combine_norm_kernel,
        grid=(n // t,),
        in_specs=[tile(d), pl.BlockSpec((top_k, t, d // 2), lambda i: (0, i, 0)), tile(top_k),
                  _const_spec((1, d)), _const_spec((1, d))],
        out_specs=tile(d),
        out_shape=jax.ShapeDtypeStruct((n, d), F32),
        compiler_params=_params("parallel"),
        name="combine_norm",
    )(pre, yk, gwt, ln_g.reshape(1, -1), ln_b.reshape(1, -1))


def _block_diag(w):
    g, i, j = w.shape
    eye = jnp.eye(g, dtype=w.dtype)
    return (eye[:, None, :, None] * w[:, :, None, :]).reshape(g * i, g * j)


def kernel(x, p, ln_in_g, ln_in_b, w_in, b_in, ret_norm_g, ret_norm_b, gdn_conv_w, gdn_a_log, gdn_dt_bias, gdn_norm_g, lru_conv_w, lru_conv_b, lru_w_r, lru_b_r, lru_w_i, lru_b_i, lru_lambda, w_branch, w_out, ln1_g, ln1_b, router_w, router_b, exp_w_gu, exp_w_down, sh_w_gu, sh_w_down, ple_w_e, ple_w_g, ple_b_g, ln2_g, ln2_b):
    bsz, seq, d = x.shape
    depth = w_in.shape[0]
    n = bsz * seq
    width = N_HEADS * HEAD_DIM
    alpha = (2 * depth) ** 0.25
    lanes = 128
    o_ret = 0
    o_gdn = o_ret + 4 * width
    o_small = o_gdn + 4 * width
    o_lru = o_small + 2 * N_HEADS
    o_mg = o_lru + 2 * width

    h = _entry_norm(x.reshape(n, d), ln_in_g, ln_in_b)
    for l in range(depth):
        wl, bl = w_in[l], b_in[l]
        bf = lambda a: a.astype(BF16)
        w_small = jnp.zeros((d, lanes), F32).at[:, :2 * N_HEADS].set(wl[:, o_small:o_lru])
        b_small = jnp.zeros((lanes,), F32).at[:2 * N_HEADS].set(bl[o_small:o_lru])
        h3 = h.reshape(bsz, seq, d)
        y_ret = _retention_branch(h3, bf(wl[:, o_ret:o_gdn]), bl[o_ret:o_gdn], ret_norm_g[l], ret_norm_b[l])
        y_gdn = _gdn_branch(h3, bf(wl[:, o_gdn:o_small]), bl[o_gdn:o_small], bf(w_small), b_small,
                            gdn_conv_w[l], gdn_a_log[l], gdn_dt_bias[l], gdn_norm_g[l])
        y_lru = _lru_branch(h3, bf(wl[:, o_lru:o_mg]), bl[o_lru:o_mg], lru_conv_w[l], lru_conv_b[l],
                            bf(_block_diag(lru_w_r[l])), lru_b_r[l], bf(_block_diag(lru_w_i[l])), lru_b_i[l],
                            lru_lambda[l])
        h1, h1b, xp, eidx, rank, gwt, counts = _merge(
            h, y_ret.reshape(n, width), y_gdn.reshape(n, width), y_lru.reshape(n, width), bf(wl[:, o_mg:]), bl[o_mg:],
            bf(w_branch[l]), bf(w_out[l]), ln1_g[l], ln1_b[l], bf(router_w[l].T), router_b[l], alpha)
        pre = _shared_ple(h1, h1b, p.reshape(depth, n, -1), l, bf(sh_w_gu[l]), bf(sh_w_down[l]), bf(ple_w_g[l]),
                          ple_b_g[l], bf(ple_w_e[l]), alpha)
        yk = _routed_rows(xp, eidx, rank, counts, exp_w_gu, exp_w_down, l)
        h = _combine_norm(pre, yk, gwt, ln2_g[l], ln2_b[l])
    return h.reshape(bsz, seq, d)
```

```python
import functools

import numpy as np
import jax
import jax.numpy as jnp
from jax import lax
from jax.experimental import pallas as pl
from jax.experimental.pallas import tpu as pltpu
from jax.experimental.pallas import tpu_sc as plsc

F32 = jnp.float32
BF16 = jnp.bfloat16

HEAD_DIM = 128
N_HEADS = 4
LRU_C = 8.0
CONV_WIDTH = 4
N_BRANCHES = 3
ROPE_BASE = 10000.0
N_GROUPS = 8
TOPK_GROUPS = 4
TOP_K = 8
ROUTED_SCALE = 2.5
LN_EPS = 1e-5
GDN_CHUNK = 64
SEQ_TILE = 256
MERGE_TILE = 512
MERGE_SUB = 256
ROW_TILE = 512
MOE_BLOCK = 512
DEST_TILE = 4096
N_PARTS = 2
SC_WINDOW = 128
CARRY_ROWS = 8
VMEM_LIMIT_BYTES = 56 * 1024 * 1024
NEG_INF = float("-inf")


def _const_spec(shape):
    nd = len(shape)
    return pl.BlockSpec(shape, lambda *_: (0,) * nd, pipeline_mode=pl.Buffered(1))


def _params(*sem):
    return pltpu.CompilerParams(dimension_semantics=sem, vmem_limit_bytes=VMEM_LIMIT_BYTES)


def _layer_norm(x, g, b):
    mu = jnp.mean(x, axis=-1, keepdims=True)
    xc = x - mu
    var = jnp.mean(xc * xc, axis=-1, keepdims=True)
    return xc * lax.rsqrt(var + LN_EPS) * g + b


def _sigmoid(x):
    return 1.0 / (1.0 + jnp.exp(-x))


def _silu(x):
    return x * _sigmoid(x)


def _softplus(x):
    return jnp.maximum(x, 0.0) + jnp.log1p(jnp.exp(-jnp.abs(x)))


def _dot(a, b):
    return jnp.dot(a, b, preferred_element_type=F32)


def _dot_nt(a, b):
    return lax.dot_general(a, b, (((1,), (1,)), ((), ())), preferred_element_type=F32)


def _dot_tn(a, b):
    return lax.dot_general(a, b, (((0,), (0,)), ((), ())), preferred_element_type=F32)


def _ln_kernel(x_ref, g_ref, b_ref, o_ref):
    o_ref[...] = _layer_norm(x_ref[...], g_ref[...], b_ref[...])


def _entry_norm(x2, g, b, part, n_part):
    d = x2.shape[1]
    t = min(1024, n_part)
    steps = n_part // t
    return pl.pallas_call(
        _ln_kernel,
        grid=(steps,),
        in_specs=[pl.BlockSpec((t, d), lambda i: (part * steps + i, 0)), _const_spec((1, d)), _const_spec((1, d))],
        out_specs=pl.BlockSpec((t, d), lambda i: (i, 0)),
        out_shape=jax.ShapeDtypeStruct((n_part, d), F32),
        compiler_params=_params("parallel"),
        name="entry_norm",
    )(x2, g.reshape(1, d), b.reshape(1, d))


def _retention_kernel(h_ref, w_ref, b_ref, cos_ref, sin_ref, dmat_ref, qd_ref, kd_ref, ng_ref, nb_ref,
                      y_ref, state_ref, *, chunk_decay):
    @pl.when(pl.program_id(1) == 0)
    def _():
        state_ref[...] = jnp.zeros_like(state_ref)

    width = N_HEADS * HEAD_DIM
    hb = h_ref[0].astype(BF16)
    proj = _dot(hb, w_ref[...]) + b_ref[...]
    cos = cos_ref[...]
    sin = sin_ref[...]
    heads = range(N_HEADS)
    qs, ks, vbs = [], [], []
    for hh in heads:
        lo = hh * HEAD_DIM
        q = proj[:, lo:lo + HEAD_DIM]
        k = proj[:, width + lo:width + lo + HEAD_DIM]
        qs.append(q * cos + pltpu.roll(q, HEAD_DIM // 2, axis=1) * sin)
        ks.append((k * cos + pltpu.roll(k, HEAD_DIM // 2, axis=1) * sin) * (HEAD_DIM ** -0.5))
        vbs.append(proj[:, 2 * width + lo:2 * width + lo + HEAD_DIM].astype(BF16))
    states = [state_ref[hh] for hh in heads]
    scores = [(_dot_nt(qs[hh].astype(BF16), ks[hh].astype(BF16)) * dmat_ref[hh]).astype(BF16) for hh in heads]
    inter = [_dot((qs[hh] * qd_ref[hh]).astype(BF16), states[hh].astype(BF16)) for hh in heads]
    for hh in heads:
        state_ref[hh] = states[hh] * chunk_decay[hh] + _dot_tn((ks[hh] * kd_ref[hh]).astype(BF16), vbs[hh])
    outs = [_dot(scores[hh], vbs[hh]) + inter[hh] for hh in heads]
    for hh in heads:
        lo = hh * HEAD_DIM
        o = outs[hh]
        gate = proj[:, 3 * width + lo:3 * width + lo + HEAD_DIM]
        mu = jnp.mean(o, axis=-1, keepdims=True)
        oc = o - mu
        var = jnp.mean(oc * oc, axis=-1, keepdims=True)
        on = oc * lax.rsqrt(var + LN_EPS) * ng_ref[:, lo:lo + HEAD_DIM] + nb_ref[:, lo:lo + HEAD_DIM]
        y_ref[0, :, lo:lo + HEAD_DIM] = (_silu(gate) * on).astype(y_ref.dtype)


def _retention_tables(seq, tile):
    half = HEAD_DIM // 2
    inv_freq = ROPE_BASE ** (-np.linspace(0.0, 1.0, half))
    ang = np.arange(seq)[:, None] * inv_freq[None, :]
    cos = np.concatenate([np.cos(ang), np.cos(ang)], axis=1)
    sin = np.concatenate([-np.sin(ang), np.sin(ang)], axis=1)
    log_gamma = np.log1p(-np.exp2(-5.0 - np.arange(N_HEADS)))
    pos = np.arange(tile)
    diff = pos[:, None] - pos[None, :]
    dmat = np.where(diff >= 0, np.exp(log_gamma[:, None, None] * np.maximum(diff, 0)), 0.0)
    qd = np.exp(log_gamma[:, None] * (pos + 1.0))[:, :, None] * np.ones((1, 1, HEAD_DIM))
    kd = np.exp(log_gamma[:, None] * (tile - 1.0 - pos))[:, :, None] * np.ones((1, 1, HEAD_DIM))
    chunk_decay = tuple(float(c) for c in np.exp(log_gamma * tile))
    as32 = lambda a: jnp.asarray(a, F32)
    return as32(cos), as32(sin), as32(dmat), as32(qd), as32(kd), chunk_decay


def _retention_branch(h, w, b, norm_g, norm_b):
    bsz, seq, d = h.shape
    t = min(SEQ_TILE, seq)
    width = N_HEADS * HEAD_DIM
    cos, sin, dmat, qd, kd, chunk_decay = _retention_tables(seq, t)
    return pl.pallas_call(
        functools.partial(_retention_kernel, chunk_decay=chunk_decay),
        grid=(bsz, seq // t),
        in_specs=[
            pl.BlockSpec((1, t, d), lambda i, j: (i, j, 0)),
            _const_spec((d, 4 * width)),
            _const_spec((1, 4 * width)),
            pl.BlockSpec((t, HEAD_DIM), lambda i, j: (j, 0)),
            pl.BlockSpec((t, HEAD_DIM), lambda i, j: (j, 0)),
            _const_spec((N_HEADS, t, t)),
            _const_spec((N_HEADS, t, HEAD_DIM)),
            _const_spec((N_HEADS, t, HEAD_DIM)),
            _const_spec((1, width)),
            _const_spec((1, width)),
        ],
        out_specs=pl.BlockSpec((1, t, width), lambda i, j: (i, j, 0)),
        out_shape=jax.ShapeDtypeStruct((bsz, seq, width), BF16),
        scratch_shapes=[pltpu.VMEM((N_HEADS, HEAD_DIM, HEAD_DIM), F32)],
        compiler_params=_params("parallel", "arbitrary"),
        name="retention_branch",
    )(h, w, b.reshape(1, -1), cos, sin, dmat, qd, kd, norm_g.reshape(1, -1), norm_b.reshape(1, -1))


def _causal_conv(x, xs_ref, cw_ref):
    t = x.shape[0]
    xs_ref[CARRY_ROWS:, :] = x
    acc = None
    for j in range(CONV_WIDTH):
        start = CARRY_ROWS - (CONV_WIDTH - 1) + j
        term = xs_ref[start:start + t, :] * cw_ref[j:j + 1, :]
        acc = term if acc is None else acc + term
    xs_ref[0:CARRY_ROWS, :] = xs_ref[t:t + CARRY_ROWS, :]
    return acc


def _cumsum_rows(x):
    n = x.shape[0]
    row = lax.broadcasted_iota(jnp.int32, x.shape, 0)
    d = 1
    while d < n:
        x = x + jnp.where(row >= d, pltpu.roll(x, d, axis=0), 0.0)
        d *= 2
    return x


def _gdn_kernel(h_ref, w_ref, b_ref, ws_ref, bs_ref, cw_ref, alog_ref, dtb_ref, ng_ref,
                y_ref, xs_ref, state_ref, u_ref, wf_ref, w_s_ref, qd_ref, kd_ref, qk_ref):
    @pl.when(pl.program_id(1) == 0)
    def _():
        state_ref[...] = jnp.zeros_like(state_ref)
        xs_ref[0:CARRY_ROWS, :] = jnp.zeros((CARRY_ROWS, xs_ref.shape[1]), F32)

    width = N_HEADS * HEAD_DIM
    c = GDN_CHUNK
    hb = h_ref[0].astype(BF16)
    t = hb.shape[0]
    proj = _dot(hb, w_ref[...]) + b_ref[...]
    small = _dot(hb, ws_ref[...]) + bs_ref[...]
    qkv = _silu(_causal_conv(proj[:, :3 * width], xs_ref, cw_ref))
    beta_all = _sigmoid(small)
    la_all = -jnp.exp(alog_ref[...]) * _softplus(small + dtb_ref[...])

    ri = lax.broadcasted_iota(jnp.int32, (c, c), 0)
    ci = lax.broadcasted_iota(jnp.int32, (c, c), 1)
    lower = ri >= ci
    strict = ri > ci

    items = [(n, hh) for n in range(t // c) for hh in range(N_HEADS)]
    gcs = {}
    for n in range(t // c):
        la_c = la_all[n * c:(n + 1) * c, :]
        gc_c = _cumsum_rows(la_c)
        gcs[n] = (la_c, gc_c, jnp.exp(gc_c))
    g_last, pws, rems = {}, {}, {}
    for n, hh in items:
        r0, lo = n * c, hh * HEAD_DIM
        rows, cols = slice(r0, r0 + c), slice(lo, lo + HEAD_DIM)
        la_c, gc_c, egc_c = gcs[n]
        q = qkv[rows, lo:lo + HEAD_DIM]
        k = qkv[rows, width + lo:width + lo + HEAD_DIM]
        v = qkv[rows, 2 * width + lo:2 * width + lo + HEAD_DIM]
        q = q * lax.rsqrt(jnp.sum(q * q, axis=-1, keepdims=True) + 1e-6) * (HEAD_DIM ** -0.5)
        k = k * lax.rsqrt(jnp.sum(k * k, axis=-1, keepdims=True) + 1e-6)
        beta = beta_all[rows, hh:hh + 1]
        la = la_c[:, N_HEADS + hh:N_HEADS + hh + 1]
        gc = gc_c[:, N_HEADS + hh:N_HEADS + hh + 1]
        egc = egc_c[:, N_HEADS + hh:N_HEADS + hh + 1]
        gc_row = jnp.sum(jnp.where(ri <= ci, jnp.broadcast_to(la, (c, c)), 0.0), axis=0, keepdims=True)
        gc_last = gc_row[:, c - 1:c]
        decay = jnp.where(lower, jnp.exp(jnp.where(lower, gc - gc_row, 0.0)), 0.0)
        kb = k * beta
        kbf = k.astype(BF16)
        a_neg = jnp.where(strict, -(_dot_nt(kb.astype(BF16), kbf) * decay), 0.0)
        pws[n, hh] = a_neg
        rems[n, hh] = a_neg
        u_ref[rows, cols] = v * beta
        wf_ref[rows, cols] = kb * egc
        qk_ref[hh, rows, :] = (_dot_nt(q.astype(BF16), kbf) * decay).astype(BF16)
        qd_ref[rows, cols] = (q * egc).astype(BF16)
        kd_ref[rows, cols] = (k * jnp.exp(gc_last - gc)).astype(BF16)
        g_last[n, hh] = jnp.exp(gc_last)
    m = 2
    while m < c:
        for it in items:
            pwb = pws[it].astype(BF16)
            pws[it] = _dot(pwb, pwb)
        for it in items:
            rems[it] = rems[it] + pws[it] + _dot(rems[it].astype(BF16), pws[it].astype(BF16))
        m *= 2
    for n, hh in items:
        rows, cols = slice(n * c, (n + 1) * c), slice(hh * HEAD_DIM, (hh + 1) * HEAD_DIM)
        remb = rems[n, hh].astype(BF16)
        u_ref[rows, cols] = u_ref[rows, cols] + _dot(remb, u_ref[rows, cols].astype(BF16))
        w_s_ref[rows, cols] = (wf_ref[rows, cols] + _dot(remb, wf_ref[rows, cols].astype(BF16))).astype(BF16)

    heads = range(N_HEADS)
    for n in range(t // c):
        rows = slice(n * c, (n + 1) * c)
        cols = [slice(hh * HEAD_DIM, (hh + 1) * HEAD_DIM) for hh in heads]
        states = [state_ref[hh] for hh in heads]
        sbs = [s.astype(BF16) for s in states]
        vnbs = [(u_ref[rows, cols[hh]] - _dot(w_s_ref[rows, cols[hh]], sbs[hh])).astype(BF16) for hh in heads]
        outs = [_dot(qd_ref[rows, cols[hh]], sbs[hh]) + _dot(qk_ref[hh, rows, :], vnbs[hh]) for hh in heads]
        for hh in heads:
            state_ref[hh] = states[hh] * g_last[n, hh] + _dot_tn(kd_ref[rows, cols[hh]], vnbs[hh])
        for hh in heads:
            o = outs[hh]
            o = o * lax.rsqrt(jnp.mean(o * o, axis=-1, keepdims=True) + 1e-6) * ng_ref[...]
            og = proj[rows, 3 * width + hh * HEAD_DIM:3 * width + (hh + 1) * HEAD_DIM]
            y_ref[0, rows, cols[hh]] = (o * _silu(og)).astype(y_ref.dtype)


def _gdn_branch(h, w, b, w_small, b_small, conv_w, a_log, dt_bias, norm_g):
    bsz, seq, d = h.shape
    t = min(SEQ_TILE, seq)
    width = N_HEADS * HEAD_DIM
    lanes = w_small.shape[1]
    pad_row = lambda vec: jnp.zeros((1, lanes), F32).at[0, N_HEADS:2 * N_HEADS].set(vec.astype(F32))
    return pl.pallas_call(
        _gdn_kernel,
        grid=(bsz, seq // t),
        in_specs=[
            pl.BlockSpec((1, t, d), lambda i, j: (i, j, 0)),
            _const_spec((d, 4 * width)),
            _const_spec((1, 4 * width)),
            _const_spec((d, lanes)),
            _const_spec((1, lanes)),
            _const_spec((CONV_WIDTH, 3 * width)),
            _const_spec((1, lanes)),
            _const_spec((1, lanes)),
            _const_spec((1, HEAD_DIM)),
        ],
        out_specs=pl.BlockSpec((1, t, width), lambda i, j: (i, j, 0)),
        out_shape=jax.ShapeDtypeStruct((bsz, seq, width), BF16),
        scratch_shapes=[pltpu.VMEM((CARRY_ROWS + t, 3 * width), F32),
                        pltpu.VMEM((N_HEADS, HEAD_DIM, HEAD_DIM), F32),
                        pltpu.VMEM((t, width), F32),
                        pltpu.VMEM((t, width), F32),
                        pltpu.VMEM((t, width), BF16),
                        pltpu.VMEM((t, width), BF16),
                        pltpu.VMEM((t, width), BF16),
                        pltpu.VMEM((N_HEADS, t, GDN_CHUNK), BF16)],
        compiler_params=_params("parallel", "arbitrary"),
        name="gdn_branch",
    )(h, w, b.reshape(1, -1), w_small, b_small.reshape(1, -1), conv_w, pad_row(a_log), pad_row(dt_bias),
      norm_g.reshape(1, -1))


def _lru_kernel(h_ref, w_ref, b_ref, cw_ref, cb_ref, wr_ref, br_ref, wi_ref, bi_ref, lam_ref,
                y_ref, xs_ref, carry_ref):
    @pl.when(pl.program_id(1) == 0)
    def _():
        carry_ref[...] = jnp.zeros_like(carry_ref)
        xs_ref[0:CARRY_ROWS, :] = jnp.zeros((CARRY_ROWS, xs_ref.shape[1]), F32)

    width = cw_ref.shape[1]
    hb = h_ref[0].astype(BF16)
    t = hb.shape[0]
    proj = _dot(hb, w_ref[...]) + b_ref[...]
    xc = _causal_conv(proj[:, :width], xs_ref, cw_ref) + cb_ref[...]
    xcb = xc.astype(BF16)
    r = _sigmoid(_dot(xcb, wr_ref[...]) + br_ref[...])
    gi = _sigmoid(_dot(xcb, wi_ref[...]) + bi_ref[...])
    log_a = -LRU_C * r * _softplus(-lam_ref[...])
    a = jnp.exp(log_a)
    th = jnp.tanh(log_a)
    hs = jnp.sqrt(-2.0 * th / (1.0 - th)) * (gi * xc)
    row = lax.broadcasted_iota(jnp.int32, (t, width), 0)
    d = 1
    while d < t:
        keep = row >= d
        hs = hs + a * jnp.where(keep, pltpu.roll(hs, d, axis=0), 0.0)
        a = a * jnp.where(keep, pltpu.roll(a, d, axis=0), 1.0)
        d *= 2
    hs = hs + a * carry_ref[...]
    carry_ref[...] = hs[t - 1:t, :]
    y_ref[0] = (jax.nn.gelu(proj[:, width:], approximate=True) * hs).astype(y_ref.dtype)


def _lru_branch(h, w, b, conv_w, conv_b, w_r, b_r, w_i, b_i, lam):
    bsz, seq, d = h.shape
    t = min(SEQ_TILE, seq)
    width = conv_w.shape[1]
    row = lambda vec: vec.reshape(1, -1)
    return pl.pallas_call(
        _lru_kernel,
        grid=(bsz, seq // t),
        in_specs=[
            pl.BlockSpec((1, t, d), lambda i, j: (i, j, 0)),
            _const_spec((d, 2 * width)),
            _const_spec((1, 2 * width)),
            _const_spec((CONV_WIDTH, width)),
            _const_spec((1, width)),
            _const_spec((width, width)),
            _const_spec((1, width)),
            _const_spec((width, width)),
            _const_spec((1, width)),
            _const_spec((1, width)),
        ],
        out_specs=pl.BlockSpec((1, t, width), lambda i, j: (i, j, 0)),
        out_shape=jax.ShapeDtypeStruct((bsz, seq, width), BF16),
        scratch_shapes=[pltpu.VMEM((CARRY_ROWS + t, width), F32), pltpu.VMEM((1, width), F32)],
        compiler_params=_params("parallel", "arbitrary"),
        name="lru_branch",
    )(h, w, row(b), conv_w, row(conv_b), w_r, row(b_r), w_i, row(b_i), row(lam))


def _first_index_of_max(x, idx, size):
    m = jnp.max(x, axis=0, keepdims=True)
    first = jnp.min(jnp.where(x == m, idx, size), axis=0, keepdims=True)
    return m, idx == first


def _route(logits_t, bias_col):
    n_exp, t = logits_t.shape
    per_group = n_exp // N_GROUPS
    scores = _sigmoid(logits_t)
    sel = scores + bias_col
    idx_g = lax.broadcasted_iota(jnp.int32, (per_group, t), 0)
    group_scores = []
    for g in range(N_GROUPS):
        x = sel[g * per_group:(g + 1) * per_group, :]
        m1, hit = _first_index_of_max(x, idx_g, per_group)
        m2 = jnp.max(jnp.where(hit, NEG_INF, x), axis=0, keepdims=True)
        group_scores.append(m1 + m2)
    gsc = jnp.concatenate(group_scores, axis=0)
    idx_n = lax.broadcasted_iota(jnp.int32, (N_GROUPS, t), 0)
    gmask = jnp.zeros((N_GROUPS, t), F32)
    for _ in range(TOPK_GROUPS):
        _, hit = _first_index_of_max(gsc, idx_n, N_GROUPS)
        gmask = jnp.where(hit, 1.0, gmask)
        gsc = jnp.where(hit, NEG_INF, gsc)
    emask = jnp.concatenate([jnp.broadcast_to(gmask[g:g + 1, :], (per_group, t)) for g in range(N_GROUPS)], axis=0)
    cand = jnp.where(emask > 0.0, sel, NEG_INF)
    idx_e = lax.broadcasted_iota(jnp.int32, (n_exp, t), 0)
    picked = jnp.zeros((n_exp, t), F32)
    hits = []
    for _ in range(TOP_K):
        _, hit = _first_index_of_max(cand, idx_e, n_exp)
        hits.append(hit)
        picked = jnp.where(hit, 1.0, picked)
        cand = jnp.where(hit, NEG_INF, cand)
    gw = jnp.where(picked > 0.0, scores, 0.0)
    return gw / jnp.sum(gw, axis=0, keepdims=True) * ROUTED_SCALE, picked, hits, idx_e


def _pack_bf16_pairs(x):
    c = x.shape[1] // 2
    hi = pltpu.bitcast(x[:, :c].astype(BF16).astype(F32), jnp.int32)
    lo = pltpu.bitcast(x[:, c:].astype(BF16).astype(F32), jnp.int32)
    return hi | lax.shift_right_logical(lo, jnp.full(lo.shape, 16, jnp.int32))


def _unpack_bf16_pairs(w):
    hi = pltpu.bitcast(w & jnp.int32(-65536), F32)
    lo = pltpu.bitcast(lax.shift_left(w, jnp.full(w.shape, 16, jnp.int32)), F32)
    return hi, lo


def _merge_kernel(h_ref, yr_ref, yg_ref, yl_ref, wmg_ref, bmg_ref, wbr_ref, wout_ref, g_ref, b_ref,
                  rwt_ref, rb_ref, eye_ref, tri_ref, h1_ref, h1b_ref, xp_ref, eidx_ref, rank_ref, gwt_ref,
                  count_ref, *, alpha):
    @pl.when(pl.program_id(0) == 0)
    def _():
        count_ref[...] = jnp.zeros_like(count_ref)

    d = h_ref.shape[1]
    sub = eye_ref.shape[0]
    parts = [slice(s, s + sub) for s in range(0, h_ref.shape[0], sub)]
    logits = []
    for rows in parts:
        h = h_ref[rows, :]
        hb = h.astype(BF16)
        gates = _sigmoid(_dot(hb, wmg_ref[...]) + bmg_ref[...])
        mixed = None
        for n, y_ref in enumerate((yr_ref, yg_ref, yl_ref)):
            term = gates[:, n * d:(n + 1) * d] * _dot(y_ref[rows, :], wbr_ref[n])
            mixed = term if mixed is None else mixed + term
        mix = _dot(mixed.astype(BF16), wout_ref[...])
        h1 = _layer_norm(alpha * h + mix, g_ref[...], b_ref[...])
        h1b = h1.astype(BF16)
        h1_ref[rows, :] = h1
        h1b_ref[rows, :] = h1b
        xp_ref[rows, :] = _pack_bf16_pairs(h1)
        logits.append(_dot_nt(rwt_ref[...], h1b))
    pick = lambda hit, vals, zero: jnp.sum(jnp.where(hit, vals, zero), axis=0, keepdims=True)
    for rows, logit in zip(parts, logits):
        combine_t, picked, hits, idx_e = _route(logit, rb_ref[...])
        rank_full = count_ref[...] + _dot(picked.astype(BF16), tri_ref[...])
        count_ref[...] += jnp.sum(picked, axis=1, keepdims=True)
        eidx_ref[:, rows] = jnp.concatenate([pick(hit, idx_e, 0) for hit in hits], axis=0)
        rank_ref[:, rows] = jnp.concatenate([pick(hit, rank_full, 0.0) for hit in hits], axis=0).astype(jnp.int32)
        gw = jnp.concatenate([pick(hit, combine_t, 0.0) for hit in hits], axis=0)
        gwt_ref[rows, :] = lax.dot_general(eye_ref[...], gw, (((1,), (1,)), ((), ())),
                                           preferred_element_type=F32, precision=lax.Precision.HIGHEST)


def _merge(h2, y_ret, y_gdn, y_lru, w_mg, b_mg, w_branch, w_out, ln_g, ln_b, router_wt, router_b, alpha):
    n, d = h2.shape
    t = min(MERGE_TILE, n)
    width = y_ret.shape[1]
    n_exp = router_wt.shape[0]
    tile = lambda cols: pl.BlockSpec((t, cols), lambda i: (i, 0))
    lane_tile = pl.BlockSpec((TOP_K, t), lambda i: (0, i))
    sub = min(MERGE_SUB, t)
    tri = jnp.triu(jnp.ones((sub, sub), BF16), 1)
    return pl.pallas_call(
        functools.partial(_merge_kernel, alpha=alpha),
        grid=(n // t,),
        in_specs=[
            tile(d), tile(width), tile(width), tile(width),
            _const_spec((d, N_BRANCHES * d)),
            _const_spec((1, N_BRANCHES * d)),
            _const_spec((N_BRANCHES, width, d)),
            _const_spec((d, d)),
            _const_spec((1, d)),
            _const_spec((1, d)),
            _const_spec((n_exp, d)),
            _const_spec((n_exp, 1)),
            _const_spec((sub, sub)),
            _const_spec((sub, sub)),
        ],
        out_specs=[tile(d), tile(d), tile(d // 2), lane_tile, lane_tile, tile(TOP_K),
                   pl.BlockSpec((n_exp, 1), lambda i: (0, 0))],
        out_shape=[jax.ShapeDtypeStruct((n, d), F32), jax.ShapeDtypeStruct((n, d), BF16),
                   jax.ShapeDtypeStruct((n, d // 2), jnp.int32),
                   jax.ShapeDtypeStruct((TOP_K, n), jnp.int32), jax.ShapeDtypeStruct((TOP_K, n), jnp.int32),
                   jax.ShapeDtypeStruct((n, TOP_K), F32), jax.ShapeDtypeStruct((n_exp, 1), F32)],
        compiler_params=_params("arbitrary"),
        name="merge_route",
    )(h2, y_ret, y_gdn, y_lru, w_mg, b_mg.reshape(1, -1), w_branch, w_out, ln_g.reshape(1, -1),
      ln_b.reshape(1, -1), router_wt, router_b.reshape(-1, 1), jnp.eye(sub, dtype=F32), tri)


def _dest_kernel(start_ref, eidx_ref, rank_ref, o_ref, *, n_exp):
    eidx = eidx_ref[...]
    dest = rank_ref[...]
    for e in range(n_exp):
        dest = dest + jnp.where(eidx == e, start_ref[e], 0)
    o_ref[...] = dest


def _dest_rows(seg_start, eidx, rank):
    k, n = eidx.shape
    t = min(DEST_TILE, n)
    blk = pl.BlockSpec((k, t), lambda i, s: (0, i))
    return pl.pallas_call(
        functools.partial(_dest_kernel, n_exp=seg_start.shape[0]),
        grid_spec=pltpu.PrefetchScalarGridSpec(num_scalar_prefetch=1, grid=(n // t,), in_specs=[blk, blk],
                                               out_specs=blk),
        out_shape=jax.ShapeDtypeStruct((k, n), jnp.int32),
        compiler_params=_params("parallel"),
        name="dest_rows",
    )(seg_start, eidx, rank)


def _sc_workers():
    info = plsc.get_sparse_core_info()
    return info.num_cores, info.num_subcores


def _sc_mesh():
    return plsc.VectorSubcoreMesh(core_axis_name="c", subcore_axis_name="s")


def _sc_scatter_rows(x, dest, n_rows):
    n, c = x.shape
    k = dest.shape[0]
    n_cores, n_sub = _sc_workers()
    per_worker = n // (n_cores * n_sub)
    steps = per_worker // SC_WINDOW

    @functools.partial(pl.kernel, out_type=jax.ShapeDtypeStruct((n_rows, c), x.dtype), mesh=_sc_mesh(),
                       scratch_types=[pltpu.VMEM((k, SC_WINDOW), jnp.int32), pltpu.VMEM((SC_WINDOW, c), x.dtype),
                                      pltpu.SemaphoreType.DMA, pltpu.SemaphoreType.DMA],
                       name="dispatch_rows")
    def scatter(x_hbm, d_hbm, o_hbm, idx_v, rows_v, sem, row_sem):
        base = (lax.axis_index("s") * n_cores + lax.axis_index("c")) * per_worker

        @pl.loop(0, steps)
        def _(j):
            off = base + j * SC_WINDOW
            loads = [pltpu.async_copy(x_hbm.at[pl.ds(off, SC_WINDOW)], rows_v, row_sem)]
            loads += [pltpu.async_copy(d_hbm.at[pl.ds(kk * n + off, SC_WINDOW)], idx_v.at[kk], sem) for kk in range(k)]
            for cp in loads:
                cp.wait()
            copies = [pltpu.async_copy(rows_v, o_hbm.at[idx_v.at[kk]], sem) for kk in range(k)]
            for cp in copies:
                cp.wait()

    return scatter(x, dest.reshape(k * n))


def _sc_gather_rows(table, idx):
    b = idx.shape[0]
    c = table.shape[1]
    n_cores, n_sub = _sc_workers()
    per_worker = b // (n_cores * n_sub)
    steps = per_worker // SC_WINDOW

    half = SC_WINDOW // 2
    half_buf = lambda dtype, *shape: pltpu.VMEM((half,) + shape, dtype)

    @functools.partial(pl.kernel, out_type=jax.ShapeDtypeStruct((b, c), table.dtype), mesh=_sc_mesh(),
                       scratch_types=[half_buf(jnp.int32), half_buf(jnp.int32), half_buf(table.dtype, c),
                                      half_buf(table.dtype, c), pltpu.SemaphoreType.DMA, pltpu.SemaphoreType.DMA,
                                      pltpu.SemaphoreType.DMA, pltpu.SemaphoreType.DMA],
                       name="collect_rows")
    def gather(t_hbm, i_hbm, o_hbm, idx_a, idx_b, rows_a, rows_b, gat_a, gat_b, put_a, put_b):
        base = (lax.axis_index("s") * n_cores + lax.axis_index("c")) * per_worker

        @pl.loop(0, steps)
        def _(j):
            off_a = base + j * SC_WINDOW
            off_b = off_a + half
            pltpu.sync_copy(i_hbm.at[pl.ds(off_a, half)], idx_a)
            in_a = pltpu.async_copy(t_hbm.at[idx_a], rows_a, gat_a)
            pltpu.sync_copy(i_hbm.at[pl.ds(off_b, half)], idx_b)
            in_b = pltpu.async_copy(t_hbm.at[idx_b], rows_b, gat_b)
            in_a.wait()
            out_a = pltpu.async_copy(rows_a, o_hbm.at[pl.ds(off_a, half)], put_a)
            in_b.wait()
            out_b = pltpu.async_copy(rows_b, o_hbm.at[pl.ds(off_b, half)], put_b)
            out_a.wait()
            out_b.wait()

    return gather(table, idx)


def _expert_block_kernel(meta_ref, xs_ref, wgu_ref, wd_ref, ys_ref, wgu_b, wd_b, *, n_blocks):
    i = pl.program_id(0)

    @pl.when((i == 0) | (meta_ref[i] != meta_ref[jnp.maximum(i - 1, 0)]))
    def _():
        wgu_b[...] = wgu_ref[0, 0].astype(BF16)
        wd_b[...] = wd_ref[0, 0].astype(BF16)

    @pl.when(i < meta_ref[n_blocks])
    def _():
        half = xs_ref.shape[1]
        ff = wd_b.shape[0]
        sub = xs_ref.shape[0] // 2
        parts = [slice(0, sub), slice(sub, 2 * sub)]
        xs = [_unpack_bf16_pairs(xs_ref[rows, :]) for rows in parts]
        gus = [_dot(hi.astype(BF16), wgu_b[:half, :]) + _dot(lo.astype(BF16), wgu_b[half:, :]) for hi, lo in xs]
        mids = [(_silu(gu[:, :ff]) * gu[:, ff:]).astype(BF16) for gu in gus]
        outs = [_dot(mid, wd_b[...]) for mid in mids]
        for rows, out in zip(parts, outs):
            ys_ref[rows, :] = _pack_bf16_pairs(out)


def _expert_blocks(meta, xs, w_gu, w_down, layer):
    rows, half = xs.shape
    _, _, d, ff2 = w_gu.shape
    n_blocks = rows // MOE_BLOCK
    row_blk = pl.BlockSpec((MOE_BLOCK, half), lambda i, meta: (jnp.minimum(i, meta[n_blocks] - 1), 0))
    return pl.pallas_call(
        functools.partial(_expert_block_kernel, n_blocks=n_blocks),
        grid_spec=pltpu.PrefetchScalarGridSpec(
            num_scalar_prefetch=1, grid=(n_blocks,),
            in_specs=[row_blk,
                      pl.BlockSpec((1, 1, d, ff2), lambda i, meta: (layer, meta[i], 0, 0)),
                      pl.BlockSpec((1, 1, ff2 // 2, d), lambda i, meta: (layer, meta[i], 0, 0))],
            out_specs=row_blk,
            scratch_shapes=[pltpu.VMEM((d, ff2), BF16), pltpu.VMEM((ff2 // 2, d), BF16)]),
        out_shape=jax.ShapeDtypeStruct((rows, half), jnp.int32),
        compiler_params=_params("arbitrary"),
        name="expert_blocks",
    )(meta, xs, w_gu, w_down)


def _routed_rows(xp, eidx, rank, counts, w_gu, w_down, layer):
    n = xp.shape[0]
    k = eidx.shape[0]
    n_exp = w_gu.shape[1]
    counts = counts.reshape(n_exp).astype(jnp.int32)
    padded = (counts + MOE_BLOCK - 1) // MOE_BLOCK * MOE_BLOCK
    seg_end = jnp.cumsum(padded)
    n_blocks = k * n // MOE_BLOCK + n_exp
    blk_start = jnp.arange(n_blocks, dtype=jnp.int32) * MOE_BLOCK
    blk_expert = jnp.minimum(jnp.sum(seg_end[None, :] <= blk_start[:, None], axis=1), n_exp - 1)
    meta = jnp.concatenate([blk_expert, seg_end[-1:] // MOE_BLOCK]).astype(jnp.int32)
    dest = _dest_rows(seg_end - padded, eidx, rank)
    xs = _sc_scatter_rows(xp, dest, n_blocks * MOE_BLOCK)
    ys = _expert_blocks(meta, xs, w_gu, w_down, layer)
    return _sc_gather_rows(ys, dest.reshape(k * n)).reshape(k, n, -1)


def _shared_ple_kernel(h1_ref, h1b_ref, p_ref, wgu_ref, wd_ref, wpg_ref, bpg_ref, wpe_ref, o_ref, *, alpha):
    xb = h1b_ref[...]
    ff = wd_ref.shape[0]
    gu = _dot(xb, wgu_ref[...])
    shared = _dot((_silu(gu[:, :ff]) * gu[:, ff:]).astype(BF16), wd_ref[...])
    ple = _sigmoid(_dot(xb, wpg_ref[...]) + bpg_ref[...]) * _dot(p_ref[0].astype(BF16), wpe_ref[...])
    o_ref[...] = alpha * h1_ref[...] + shared + ple


def _shared_ple(h1, h1b, p3, layer, part, sh_w_gu, sh_w_down, ple_w_g, ple_b_g, ple_w_e, alpha):
    n, d = h1.shape
    t = min(ROW_TILE, n)
    steps = n // t
    pdim = p3.shape[2]
    ff2 = sh_w_gu.shape[1]
    tile = lambda cols: pl.BlockSpec((t, cols), lambda i: (i, 0))
    return pl.pallas_call(
        functools.partial(_shared_ple_kernel, alpha=alpha),
        grid=(n // t,),
        in_specs=[
            tile(d), tile(d), pl.BlockSpec((1, t, pdim), lambda i: (layer, part * steps + i, 0)),
            _const_spec((d, ff2)),
            _const_spec((ff2 // 2, d)),
            _const_spec((d, d)),
            _const_spec((1, d)),
            _const_spec((pdim, d)),
        ],
        out_specs=tile(d),
        out_shape=jax.ShapeDtypeStruct((n, d), F32),
        compiler_params=_params("parallel"),
        name="shared_ple",
    )(h1, h1b, p3, sh_w_gu, sh_w_down, ple_w_g, ple_b_g.reshape(1, -1), ple_w_e)


def _combine_norm_kernel(pre_ref, yk_ref, gwt_ref, g_ref, b_ref, o_ref):
    gwt = gwt_ref[...]
    routed_hi = routed_lo = None
    for k in range(yk_ref.shape[0]):
        hi, lo = _unpack_bf16_pairs(yk_ref[k])
        wk = gwt[:, k:k + 1]
        routed_hi = hi * wk if routed_hi is None else routed_hi + hi * wk
        routed_lo = lo * wk if routed_lo is None else routed_lo + lo * wk
    routed = jnp.concatenate([routed_hi, routed_lo], axis=1)
    o_ref[...] = _layer_norm(pre_ref[...] + routed, g_ref[...], b_ref[...])


def _combine_norm(pre, yk, gwt, ln_g, ln_b):
    n, d = pre.shape
    t = min(ROW_TILE, n)
    top_k = yk.shape[0]
    tile = lambda cols: pl.BlockSpec((t, cols), lambda i: (i, 0))
    return pl.pallas_call(
        _combine_norm_kernel,
        grid=(n // t,),
        in_specs=[tile(d), pl.BlockSpec((top_k, t, d // 2), lambda i: (0, i, 0)), tile(top_k),
                  _const_spec((1, d)), _const_spec((1, d))],
        out_specs=tile(d),
        out_shape=jax.ShapeDtypeStruct((n, d), F32),
        compiler_params=_params("parallel"),
        name="combine_norm",
    )(pre, yk, gwt, ln_g.reshape(1, -1), ln_b.reshape(1, -1))


def _block_diag(w):
    g, i, j = w.shape
    eye = jnp.eye(g, dtype=w.dtype)
    return (eye[:, None, :, None] * w[:, :, None, :]).reshape(g * i, g * j)


def kernel(x, p, ln_in_g, ln_in_b, w_in, b_in, ret_norm_g, ret_norm_b, gdn_conv_w, gdn_a_log, gdn_dt_bias, gdn_norm_g, lru_conv_w, lru_conv_b, lru_w_r, lru_b_r, lru_w_i, lru_b_i, lru_lambda, w_branch, w_out, ln1_g, ln1_b, router_w, router_b, exp_w_gu, exp_w_down, sh_w_gu, sh_w_down, ple_w_e, ple_w_g, ple_b_g, ln2_g, ln2_b):
    bsz, seq, d = x.shape
    depth = w_in.shape[0]
    n = bsz * seq
    width = N_HEADS * HEAD_DIM
    alpha = (2 * depth) ** 0.25
    lanes = 128
    o_ret = 0
    o_gdn = o_ret + 4 * width
    o_small = o_gdn + 4 * width
    o_lru = o_small + 2 * N_HEADS
    o_mg = o_lru + 2 * width

    parts = range(N_PARTS if bsz % N_PARTS == 0 else 1)
    bsz_p = bsz // len(parts)
    n_p = bsz_p * seq
    p3 = p.reshape(depth, n, -1)
    bf = lambda a: a.astype(BF16)
    hs = [_entry_norm(x.reshape(n, d), ln_in_g, ln_in_b, part, n_p) for part in parts]
    for l in range(depth):
        wl, bl = w_in[l], b_in[l]
        w_small = bf(jnp.zeros((d, lanes), F32).at[:, :2 * N_HEADS].set(wl[:, o_small:o_lru]))
        b_small = jnp.zeros((lanes,), F32).at[:2 * N_HEADS].set(bl[o_small:o_lru])
        w_ret, w_gdn, w_lru, w_mg = bf(wl[:, o_ret:o_gdn]), bf(wl[:, o_gdn:o_small]), bf(wl[:, o_lru:o_mg]), bf(wl[:, o_mg:])
        w_r, w_i = bf(_block_diag(lru_w_r[l])), bf(_block_diag(lru_w_i[l]))
        w_br, w_o, r_wt = bf(w_branch[l]), bf(w_out[l]), bf(router_w[l].T)
        w_sgu, w_sd, w_pg, w_pe = bf(sh_w_gu[l]), bf(sh_w_down[l]), bf(ple_w_g[l]), bf(ple_w_e[l])
        for part in parts:
            h = hs[part]
            h3 = h.reshape(bsz_p, seq, d)
            y_ret = _retention_branch(h3, w_ret, bl[o_ret:o_gdn], ret_norm_g[l], ret_norm_b[l])
            y_gdn = _gdn_branch(h3, w_gdn, bl[o_gdn:o_small], w_small, b_small, gdn_conv_w[l], gdn_a_log[l],
                                gdn_dt_bias[l], gdn_norm_g[l])
            y_lru = _lru_branch(h3, w_lru, bl[o_lru:o_mg], lru_conv_w[l], lru_conv_b[l], w_r, lru_b_r[l], w_i,
                                lru_b_i[l], lru_lambda[l])
            h1, h1b, xp, eidx, rank, gwt, counts = _merge(
                h, y_ret.reshape(n_p, width), y_gdn.reshape(n_p, width), y_lru.reshape(n_p, width), w_mg, bl[o_mg:],
                w_br, w_o, ln1_g[l], ln1_b[l], r_wt, router_b[l], alpha)
            pre = _shared_ple(h1, h1b, p3, l, part, w_sgu, w_sd, w_pg, ple_b_g[l], w_pe, alpha)
            yk = _routed_rows(xp, eidx, rank, counts, exp_w_gu, exp_w_down, l)
            hs[part] = _combine_norm(pre, yk, gwt, ln2_g[l], ln2_b[l])
    return jnp.concatenate([h.reshape(bsz_p, seq, d) for h in hs], axis=0)
```

```python
import functools

import numpy as np
import jax
import jax.numpy as jnp
from jax import lax
from jax.experimental import pallas as pl
from jax.experimental.pallas import tpu as pltpu
from jax.experimental.pallas import tpu_sc as plsc

F32 = jnp.float32
BF16 = jnp.bfloat16

HEAD_DIM = 128
N_HEADS = 4
LRU_C = 8.0
CONV_WIDTH = 4
N_BRANCHES = 3
ROPE_BASE = 10000.0
N_GROUPS = 8
TOPK_GROUPS = 4
TOP_K = 8
ROUTED_SCALE = 2.5
LN_EPS = 1e-5
GDN_CHUNK = 64
SEQ_TILE = 256
MERGE_TILE = 512
MERGE_SUB = 256
ROW_TILE = 512
MOE_BLOCK = 512
DEST_TILE = 4096
N_PARTS = 1
SC_WINDOW = 128
CARRY_ROWS = 8
SCAN_GROUP = 16
VMEM_LIMIT_BYTES = 56 * 1024 * 1024
NEG_INF = float("-inf")


def _const_spec(shape):
    nd = len(shape)
    return pl.BlockSpec(shape, lambda *_: (0,) * nd, pipeline_mode=pl.Buffered(1))


def _params(*sem):
    return pltpu.CompilerParams(dimension_semantics=sem, vmem_limit_bytes=VMEM_LIMIT_BYTES)


def _layer_norm(x, g, b):
    mu = jnp.mean(x, axis=-1, keepdims=True)
    xc = x - mu
    var = jnp.mean(xc * xc, axis=-1, keepdims=True)
    return xc * lax.rsqrt(var + LN_EPS) * g + b


def _sigmoid(x):
    return 1.0 / (1.0 + jnp.exp(-x))


def _silu(x):
    return x * _sigmoid(x)


def _softplus(x):
    return jnp.maximum(x, 0.0) + jnp.log1p(jnp.exp(-jnp.abs(x)))


def _dot(a, b):
    return jnp.dot(a, b, preferred_element_type=F32)


def _dot_nt(a, b):
    return lax.dot_general(a, b, (((1,), (1,)), ((), ())), preferred_element_type=F32)


def _dot_tn(a, b):
    return lax.dot_general(a, b, (((0,), (0,)), ((), ())), preferred_element_type=F32)


def _ln_kernel(x_ref, g_ref, b_ref, o_ref):
    o_ref[...] = _layer_norm(x_ref[...], g_ref[...], b_ref[...])


def _entry_norm(x2, g, b, part, n_part):
    d = x2.shape[1]
    t = min(1024, n_part)
    steps = n_part // t
    return pl.pallas_call(
        _ln_kernel,
        grid=(steps,),
        in_specs=[pl.BlockSpec((t, d), lambda i: (part * steps + i, 0)), _const_spec((1, d)), _const_spec((1, d))],
        out_specs=pl.BlockSpec((t, d), lambda i: (i, 0)),
        out_shape=jax.ShapeDtypeStruct((n_part, d), F32),
        compiler_params=_params("parallel"),
        name="entry_norm",
    )(x2, g.reshape(1, d), b.reshape(1, d))


def _retention_body(hb, w_ref, b_ref, cos_ref, sin_ref, dmat_ref, qd_ref, kd_ref, ng_ref, nb_ref,
                    y_ref, state_ref, *, chunk_decay):
    width = N_HEADS * HEAD_DIM
    proj = _dot(hb, w_ref[...]) + b_ref[...]
    yield
    cos = cos_ref[...]
    sin = sin_ref[...]
    heads = range(N_HEADS)
    qs, ks, vbs = [], [], []
    for hh in heads:
        lo = hh * HEAD_DIM
        q = proj[:, lo:lo + HEAD_DIM]
        k = proj[:, width + lo:width + lo + HEAD_DIM]
        qs.append(q * cos + pltpu.roll(q, HEAD_DIM // 2, axis=1) * sin)
        ks.append((k * cos + pltpu.roll(k, HEAD_DIM // 2, axis=1) * sin) * (HEAD_DIM ** -0.5))
        vbs.append(proj[:, 2 * width + lo:2 * width + lo + HEAD_DIM].astype(BF16))
        yield
    states = [state_ref[hh] for hh in heads]
    scores, inter, outs = [], [], []
    for hh in heads:
        scores.append((_dot_nt(qs[hh].astype(BF16), ks[hh].astype(BF16)) * dmat_ref[hh]).astype(BF16))
        yield
    for hh in heads:
        inter.append(_dot((qs[hh] * qd_ref[hh]).astype(BF16), states[hh].astype(BF16)))
        yield
    for hh in heads:
        state_ref[hh] = states[hh] * chunk_decay[hh] + _dot_tn((ks[hh] * kd_ref[hh]).astype(BF16), vbs[hh])
        yield
    for hh in heads:
        outs.append(_dot(scores[hh], vbs[hh]) + inter[hh])
        yield
    for hh in heads:
        lo = hh * HEAD_DIM
        o = outs[hh]
        gate = proj[:, 3 * width + lo:3 * width + lo + HEAD_DIM]
        mu = jnp.mean(o, axis=-1, keepdims=True)
        oc = o - mu
        var = jnp.mean(oc * oc, axis=-1, keepdims=True)
        on = oc * lax.rsqrt(var + LN_EPS) * ng_ref[:, lo:lo + HEAD_DIM] + nb_ref[:, lo:lo + HEAD_DIM]
        y_ref[:, lo:lo + HEAD_DIM] = (_silu(gate) * on).astype(y_ref.dtype)
        yield


def _retention_tables(seq, tile):
    half = HEAD_DIM // 2
    inv_freq = ROPE_BASE ** (-np.linspace(0.0, 1.0, half))
    ang = np.arange(seq)[:, None] * inv_freq[None, :]
    cos = np.concatenate([np.cos(ang), np.cos(ang)], axis=1)
    sin = np.concatenate([-np.sin(ang), np.sin(ang)], axis=1)
    log_gamma = np.log1p(-np.exp2(-5.0 - np.arange(N_HEADS)))
    pos = np.arange(tile)
    diff = pos[:, None] - pos[None, :]
    dmat = np.where(diff >= 0, np.exp(log_gamma[:, None, None] * np.maximum(diff, 0)), 0.0)
    qd = np.exp(log_gamma[:, None] * (pos + 1.0))[:, :, None] * np.ones((1, 1, HEAD_DIM))
    kd = np.exp(log_gamma[:, None] * (tile - 1.0 - pos))[:, :, None] * np.ones((1, 1, HEAD_DIM))
    chunk_decay = tuple(float(c) for c in np.exp(log_gamma * tile))
    as32 = lambda a: jnp.asarray(a, F32)
    return as32(cos), as32(sin), as32(dmat), as32(qd), as32(kd), chunk_decay


def _causal_conv(x, xs_ref, cw_ref):
    t = x.shape[0]
    xs_ref[CARRY_ROWS:, :] = x
    acc = None
    for j in range(CONV_WIDTH):
        start = CARRY_ROWS - (CONV_WIDTH - 1) + j
        term = xs_ref[start:start + t, :] * cw_ref[j:j + 1, :]
        acc = term if acc is None else acc + term
    xs_ref[0:CARRY_ROWS, :] = xs_ref[t:t + CARRY_ROWS, :]
    return acc


def _cumsum_rows(x):
    n = x.shape[0]
    row = lax.broadcasted_iota(jnp.int32, x.shape, 0)
    d = 1
    while d < n:
        x = x + jnp.where(row >= d, pltpu.roll(x, d, axis=0), 0.0)
        d *= 2
    return x


def _gdn_body(hb, w_ref, b_ref, ws_ref, bs_ref, cw_ref, alog_ref, dtb_ref, ng_ref,
              y_ref, xs_ref, state_ref, u_ref, wf_ref, w_s_ref, qd_ref, kd_ref, qk_ref):
    width = N_HEADS * HEAD_DIM
    c = GDN_CHUNK
    t = hb.shape[0]
    proj = _dot(hb, w_ref[...]) + b_ref[...]
    small = _dot(hb, ws_ref[...]) + bs_ref[...]
    yield
    qkv = _silu(_causal_conv(proj[:, :3 * width], xs_ref, cw_ref))
    yield
    beta_all = _sigmoid(small)
    la_all = -jnp.exp(alog_ref[...]) * _softplus(small + dtb_ref[...])

    ri = lax.broadcasted_iota(jnp.int32, (c, c), 0)
    ci = lax.broadcasted_iota(jnp.int32, (c, c), 1)
    lower = ri >= ci
    strict = ri > ci

    items = [(n, hh) for n in range(t // c) for hh in range(N_HEADS)]
    gcs = {}
    for n in range(t // c):
        la_c = la_all[n * c:(n + 1) * c, :]
        gc_c = _cumsum_rows(la_c)
        gcs[n] = (la_c, gc_c, jnp.exp(gc_c))
    g_last, pws, rems = {}, {}, {}
    for n, hh in items:
        r0, lo = n * c, hh * HEAD_DIM
        rows, cols = slice(r0, r0 + c), slice(lo, lo + HEAD_DIM)
        la_c, gc_c, egc_c = gcs[n]
        q = qkv[rows, lo:lo + HEAD_DIM]
        k = qkv[rows, width + lo:width + lo + HEAD_DIM]
        v = qkv[rows, 2 * width + lo:2 * width + lo + HEAD_DIM]
        q = q * lax.rsqrt(jnp.sum(q * q, axis=-1, keepdims=True) + 1e-6) * (HEAD_DIM ** -0.5)
        k = k * lax.rsqrt(jnp.sum(k * k, axis=-1, keepdims=True) + 1e-6)
        beta = beta_all[rows, hh:hh + 1]
        la = la_c[:, N_HEADS + hh:N_HEADS + hh + 1]
        gc = gc_c[:, N_HEADS + hh:N_HEADS + hh + 1]
        egc = egc_c[:, N_HEADS + hh:N_HEADS + hh + 1]
        gc_row = jnp.sum(jnp.where(ri <= ci, jnp.broadcast_to(la, (c, c)), 0.0), axis=0, keepdims=True)
        gc_last = gc_row[:, c - 1:c]
        decay = jnp.where(lower, jnp.exp(jnp.where(lower, gc - gc_row, 0.0)), 0.0)
        kb = k * beta
        kbf = k.astype(BF16)
        a_neg = jnp.where(strict, -(_dot_nt(kb.astype(BF16), kbf) * decay), 0.0)
        pws[n, hh] = a_neg
        rems[n, hh] = a_neg
        u_ref[rows, cols] = v * beta
        wf_ref[rows, cols] = kb * egc
        qk_ref[hh, rows, :] = (_dot_nt(q.astype(BF16), kbf) * decay).astype(BF16)
        qd_ref[rows, cols] = (q * egc).astype(BF16)
        kd_ref[rows, cols] = (k * jnp.exp(gc_last - gc)).astype(BF16)
        g_last[n, hh] = jnp.exp(gc_last)
        yield
    m = 2
    while m < c:
        for it in items:
            pwb = pws[it].astype(BF16)
            pws[it] = _dot(pwb, pwb)
        yield
        for it in items:
            rems[it] = rems[it] + pws[it] + _dot(rems[it].astype(BF16), pws[it].astype(BF16))
        yield
        m *= 2
    for n, hh in items:
        rows, cols = slice(n * c, (n + 1) * c), slice(hh * HEAD_DIM, (hh + 1) * HEAD_DIM)
        remb = rems[n, hh].astype(BF16)
        u_ref[rows, cols] = u_ref[rows, cols] + _dot(remb, u_ref[rows, cols].astype(BF16))
        w_s_ref[rows, cols] = (wf_ref[rows, cols] + _dot(remb, wf_ref[rows, cols].astype(BF16))).astype(BF16)
    yield

    heads = range(N_HEADS)
    for n in range(t // c):
        rows = slice(n * c, (n + 1) * c)
        cols = [slice(hh * HEAD_DIM, (hh + 1) * HEAD_DIM) for hh in heads]
        states = [state_ref[hh] for hh in heads]
        sbs = [s.astype(BF16) for s in states]
        vnbs = [(u_ref[rows, cols[hh]] - _dot(w_s_ref[rows, cols[hh]], sbs[hh])).astype(BF16) for hh in heads]
        yield
        outs = [_dot(qd_ref[rows, cols[hh]], sbs[hh]) + _dot(qk_ref[hh, rows, :], vnbs[hh]) for hh in heads]
        for hh in heads:
            state_ref[hh] = states[hh] * g_last[n, hh] + _dot_tn(kd_ref[rows, cols[hh]], vnbs[hh])
        yield
        for hh in heads:
            o = outs[hh]
            o = o * lax.rsqrt(jnp.mean(o * o, axis=-1, keepdims=True) + 1e-6) * ng_ref[...]
            og = proj[rows, 3 * width + hh * HEAD_DIM:3 * width + (hh + 1) * HEAD_DIM]
            y_ref[rows, cols[hh]] = (o * _silu(og)).astype(y_ref.dtype)
        yield


def _lru_body(hb, w_ref, b_ref, cw_ref, cb_ref, wr_ref, br_ref, wi_ref, bi_ref, lam_ref,
              y_ref, xs_ref, carry_ref):
    width = cw_ref.shape[1]
    t = hb.shape[0]
    proj = _dot(hb, w_ref[...]) + b_ref[...]
    yield
    xc = _causal_conv(proj[:, :width], xs_ref, cw_ref) + cb_ref[...]
    xcb = xc.astype(BF16)
    yield
    r = _sigmoid(_dot(xcb, wr_ref[...]) + br_ref[...])
    gi = _sigmoid(_dot(xcb, wi_ref[...]) + bi_ref[...])
    yield
    log_a = -LRU_C * r * _softplus(-lam_ref[...])
    a = jnp.exp(log_a)
    th = jnp.tanh(log_a)
    hs = jnp.sqrt(-2.0 * th / (1.0 - th)) * (gi * xc)
    row = lax.broadcasted_iota(jnp.int32, (t, width), 0) % SCAN_GROUP
    d = 1
    while d < SCAN_GROUP:
        keep = row >= d
        hs = hs + a * jnp.where(keep, pltpu.roll(hs, d, axis=0), 0.0)
        a = a * jnp.where(keep, pltpu.roll(a, d, axis=0), 1.0)
        d *= 2
        yield
    gate = jax.nn.gelu(proj[:, width:], approximate=True)
    carry = carry_ref[...]
    for g in range(t // SCAN_GROUP):
        rows = slice(g * SCAN_GROUP, (g + 1) * SCAN_GROUP)
        hg = hs[rows, :] + a[rows, :] * carry
        carry = hg[SCAN_GROUP - 1:SCAN_GROUP, :]
        y_ref[rows, :] = (gate[rows, :] * hg).astype(y_ref.dtype)
        if g % 8 == 7:
            yield
    carry_ref[...] = carry


N_RET_IN, N_GDN_IN, N_LRU_IN = 9, 8, 9
MIX_STRIDE = (1, 2, 4)


def _mixers_kernel(h_ref, *refs, chunk_decay):
    ret_in, refs = refs[:N_RET_IN], refs[N_RET_IN:]
    gdn_in, refs = refs[:N_GDN_IN], refs[N_GDN_IN:]
    lru_in, refs = refs[:N_LRU_IN], refs[N_LRU_IN:]
    yr_ref, yg_ref, yl_ref = refs[:3]
    ret_state, gdn_xs, gdn_state, u_ref, wf_ref, w_s_ref, qd_ref, kd_ref, qk_ref, lru_xs, lru_carry = refs[3:]

    @pl.when(pl.program_id(1) == 0)
    def _():
        ret_state[...] = jnp.zeros_like(ret_state)
        gdn_state[...] = jnp.zeros_like(gdn_state)
        lru_carry[...] = jnp.zeros_like(lru_carry)
        gdn_xs[0:CARRY_ROWS, :] = jnp.zeros((CARRY_ROWS, gdn_xs.shape[1]), F32)
        lru_xs[0:CARRY_ROWS, :] = jnp.zeros((CARRY_ROWS, lru_xs.shape[1]), F32)

    hb = h_ref[0].astype(BF16)
    branches = [
        (_gdn_body(hb, *gdn_in, yg_ref.at[0], gdn_xs, gdn_state, u_ref, wf_ref, w_s_ref, qd_ref, kd_ref, qk_ref),
         MIX_STRIDE[0]),
        (_retention_body(hb, *ret_in, yr_ref.at[0], ret_state, chunk_decay=chunk_decay), MIX_STRIDE[1]),
        (_lru_body(hb, *lru_in, yl_ref.at[0], lru_xs, lru_carry), MIX_STRIDE[2]),
    ]
    tick = 0
    while branches:
        for gen, stride in list(branches):
            if tick % stride == 0 and next(gen, StopIteration) is StopIteration:
                branches.remove((gen, stride))
        tick += 1


def _mixers(h, ret_args, gdn_args, lru_args):
    bsz, seq, d = h.shape
    t = min(SEQ_TILE, seq)
    width = N_HEADS * HEAD_DIM
    row = lambda vec: vec.reshape(1, -1)
    w_ret, b_ret, ret_g, ret_b = ret_args
    w_gdn, b_gdn, w_small, b_small, gdn_cw, a_log, dt_bias, gdn_g = gdn_args
    w_lru, b_lru, lru_cw, lru_cb, w_r, b_r, w_i, b_i, lam = lru_args
    lanes = w_small.shape[1]
    lru_w = lru_cw.shape[1]
    cos, sin, dmat, qd, kd, chunk_decay = _retention_tables(seq, t)
    pad_row = lambda vec: jnp.zeros((1, lanes), F32).at[0, N_HEADS:2 * N_HEADS].set(vec.astype(F32))
    seq_tile = lambda cols: pl.BlockSpec((1, t, cols), lambda i, j: (i, j, 0))
    pos_tile = pl.BlockSpec((t, HEAD_DIM), lambda i, j: (j, 0))
    ret_specs = [_const_spec((d, 4 * width)), _const_spec((1, 4 * width)), pos_tile, pos_tile,
                 _const_spec((N_HEADS, t, t)), _const_spec((N_HEADS, t, HEAD_DIM)), _const_spec((N_HEADS, t, HEAD_DIM)),
                 _const_spec((1, width)), _const_spec((1, width))]
    gdn_specs = [_const_spec((d, 4 * width)), _const_spec((1, 4 * width)), _const_spec((d, lanes)),
                 _const_spec((1, lanes)), _const_spec((CONV_WIDTH, 3 * width)), _const_spec((1, lanes)),
                 _const_spec((1, lanes)), _const_spec((1, HEAD_DIM))]
    lru_specs = [_const_spec((d, 2 * lru_w)), _const_spec((1, 2 * lru_w)), _const_spec((CONV_WIDTH, lru_w)),
                 _const_spec((1, lru_w)), _const_spec((lru_w, lru_w)), _const_spec((1, lru_w)),
                 _const_spec((lru_w, lru_w)), _const_spec((1, lru_w)), _const_spec((1, lru_w))]
    assert (len(ret_specs), len(gdn_specs), len(lru_specs)) == (N_RET_IN, N_GDN_IN, N_LRU_IN)
    out = jax.ShapeDtypeStruct((bsz, seq, width), BF16)
    return pl.pallas_call(
        functools.partial(_mixers_kernel, chunk_decay=chunk_decay),
        grid=(bsz, seq // t),
        in_specs=[seq_tile(d)] + ret_specs + gdn_specs + lru_specs,
        out_specs=[seq_tile(width), seq_tile(width), seq_tile(lru_w)],
        out_shape=[out, out, jax.ShapeDtypeStruct((bsz, seq, lru_w), BF16)],
        scratch_shapes=[pltpu.VMEM((N_HEADS, HEAD_DIM, HEAD_DIM), F32),
                        pltpu.VMEM((CARRY_ROWS + t, 3 * width), F32),
                        pltpu.VMEM((N_HEADS, HEAD_DIM, HEAD_DIM), F32),
                        pltpu.VMEM((t, width), F32),
                        pltpu.VMEM((t, width), F32),
                        pltpu.VMEM((t, width), BF16),
                        pltpu.VMEM((t, width), BF16),
                        pltpu.VMEM((t, width), BF16),
                        pltpu.VMEM((N_HEADS, t, GDN_CHUNK), BF16),
                        pltpu.VMEM((CARRY_ROWS + t, lru_w), F32),
                        pltpu.VMEM((1, lru_w), F32)],
        compiler_params=_params("parallel", "arbitrary"),
        name="token_mixers",
    )(h, w_ret, row(b_ret), cos, sin, dmat, qd, kd, row(ret_g), row(ret_b),
      w_gdn, row(b_gdn), w_small, row(b_small), gdn_cw, pad_row(a_log), pad_row(dt_bias), row(gdn_g),
      w_lru, row(b_lru), lru_cw, row(lru_cb), w_r, row(b_r), w_i, row(b_i), row(lam))


def _first_index_of_max(x, idx, size):
    m = jnp.max(x, axis=0, keepdims=True)
    first = jnp.min(jnp.where(x == m, idx, size), axis=0, keepdims=True)
    return m, idx == first


def _route(logits_t, bias_col):
    n_exp, t = logits_t.shape
    per_group = n_exp // N_GROUPS
    scores = _sigmoid(logits_t)
    sel = scores + bias_col
    idx_g = lax.broadcasted_iota(jnp.int32, (per_group, t), 0)
    group_scores = []
    for g in range(N_GROUPS):
        x = sel[g * per_group:(g + 1) * per_group, :]
        m1, hit = _first_index_of_max(x, idx_g, per_group)
        m2 = jnp.max(jnp.where(hit, NEG_INF, x), axis=0, keepdims=True)
        group_scores.append(m1 + m2)
    gsc = jnp.concatenate(group_scores, axis=0)
    idx_n = lax.broadcasted_iota(jnp.int32, (N_GROUPS, t), 0)
    gmask = jnp.zeros((N_GROUPS, t), F32)
    for _ in range(TOPK_GROUPS):
        _, hit = _first_index_of_max(gsc, idx_n, N_GROUPS)
        gmask = jnp.where(hit, 1.0, gmask)
        gsc = jnp.where(hit, NEG_INF, gsc)
    emask = jnp.concatenate([jnp.broadcast_to(gmask[g:g + 1, :], (per_group, t)) for g in range(N_GROUPS)], axis=0)
    cand = jnp.where(emask > 0.0, sel, NEG_INF)
    idx_e = lax.broadcasted_iota(jnp.int32, (n_exp, t), 0)
    picked = jnp.zeros((n_exp, t), F32)
    hits = []
    for _ in range(TOP_K):
        _, hit = _first_index_of_max(cand, idx_e, n_exp)
        hits.append(hit)
        picked = jnp.where(hit, 1.0, picked)
        cand = jnp.where(hit, NEG_INF, cand)
    gw = jnp.where(picked > 0.0, scores, 0.0)
    return gw / jnp.sum(gw, axis=0, keepdims=True) * ROUTED_SCALE, picked, hits, idx_e


def _pack_bf16_pairs(x):
    c = x.shape[1] // 2
    hi = pltpu.bitcast(x[:, :c].astype(BF16).astype(F32), jnp.int32)
    lo = pltpu.bitcast(x[:, c:].astype(BF16).astype(F32), jnp.int32)
    return hi | lax.shift_right_logical(lo, jnp.full(lo.shape, 16, jnp.int32))


def _unpack_bf16_pairs(w):
    hi = pltpu.bitcast(w & jnp.int32(-65536), F32)
    lo = pltpu.bitcast(lax.shift_left(w, jnp.full(w.shape, 16, jnp.int32)), F32)
    return hi, lo


def _merge_kernel(h_ref, yr_ref, yg_ref, yl_ref, wmg_ref, bmg_ref, wbr_ref, wout_ref, g_ref, b_ref,
                  rwt_ref, rb_ref, eye_ref, tri_ref, h1_ref, h1b_ref, xp_ref, eidx_ref, rank_ref, gwt_ref,
                  count_ref, *, alpha):
    @pl.when(pl.program_id(0) == 0)
    def _():
        count_ref[...] = jnp.zeros_like(count_ref)

    d = h_ref.shape[1]
    sub = eye_ref.shape[0]
    parts = [slice(s, s + sub) for s in range(0, h_ref.shape[0], sub)]
    logits = []
    for rows in parts:
        h = h_ref[rows, :]
        hb = h.astype(BF16)
        gates = _sigmoid(_dot(hb, wmg_ref[...]) + bmg_ref[...])
        mixed = None
        for n, y_ref in enumerate((yr_ref, yg_ref, yl_ref)):
            term = gates[:, n * d:(n + 1) * d] * _dot(y_ref[rows, :], wbr_ref[n])
            mixed = term if mixed is None else mixed + term
        mix = _dot(mixed.astype(BF16), wout_ref[...])
        h1 = _layer_norm(alpha * h + mix, g_ref[...], b_ref[...])
        h1b = h1.astype(BF16)
        h1_ref[rows, :] = h1
        h1b_ref[rows, :] = h1b
        xp_ref[rows, :] = _pack_bf16_pairs(h1)
        logits.append(_dot_nt(rwt_ref[...], h1b))
    pick = lambda hit, vals, zero: jnp.sum(jnp.where(hit, vals, zero), axis=0, keepdims=True)
    for rows, logit in zip(parts, logits):
        combine_t, picked, hits, idx_e = _route(logit, rb_ref[...])
        rank_full = count_ref[...] + _dot(picked.astype(BF16), tri_ref[...])
        count_ref[...] += jnp.sum(picked, axis=1, keepdims=True)
        eidx_ref[:, rows] = jnp.concatenate([pick(hit, idx_e, 0) for hit in hits], axis=0)
        rank_ref[:, rows] = jnp.concatenate([pick(hit, rank_full, 0.0) for hit in hits], axis=0).astype(jnp.int32)
        gw = jnp.concatenate([pick(hit, combine_t, 0.0) for hit in hits], axis=0)
        gwt_ref[rows, :] = lax.dot_general(eye_ref[...], gw, (((1,), (1,)), ((), ())),
                                           preferred_element_type=F32, precision=lax.Precision.HIGHEST)


def _merge(h2, y_ret, y_gdn, y_lru, w_mg, b_mg, w_branch, w_out, ln_g, ln_b, router_wt, router_b, alpha):
    n, d = h2.shape
    t = min(MERGE_TILE, n)
    width = y_ret.shape[1]
    n_exp = router_wt.shape[0]
    tile = lambda cols: pl.BlockSpec((t, cols), lambda i: (i, 0))
    lane_tile = pl.BlockSpec((TOP_K, t), lambda i: (0, i))
    sub = min(MERGE_SUB, t)
    tri = jnp.triu(jnp.ones((sub, sub), BF16), 1)
    return pl.pallas_call(
        functools.partial(_merge_kernel, alpha=alpha),
        grid=(n // t,),
        in_specs=[
            tile(d), tile(width), tile(width), tile(width),
            _const_spec((d, N_BRANCHES * d)),
            _const_spec((1, N_BRANCHES * d)),
            _const_spec((N_BRANCHES, width, d)),
            _const_spec((d, d)),
            _const_spec((1, d)),
            _const_spec((1, d)),
            _const_spec((n_exp, d)),
            _const_spec((n_exp, 1)),
            _const_spec((sub, sub)),
            _const_spec((sub, sub)),
        ],
        out_specs=[tile(d), tile(d), tile(d // 2), lane_tile, lane_tile, tile(TOP_K),
                   pl.BlockSpec((n_exp, 1), lambda i: (0, 0))],
        out_shape=[jax.ShapeDtypeStruct((n, d), F32), jax.ShapeDtypeStruct((n, d), BF16),
                   jax.ShapeDtypeStruct((n, d // 2), jnp.int32),
                   jax.ShapeDtypeStruct((TOP_K, n), jnp.int32), jax.ShapeDtypeStruct((TOP_K, n), jnp.int32),
                   jax.ShapeDtypeStruct((n, TOP_K), F32), jax.ShapeDtypeStruct((n_exp, 1), F32)],
        compiler_params=_params("arbitrary"),
        name="merge_route",
    )(h2, y_ret, y_gdn, y_lru, w_mg, b_mg.reshape(1, -1), w_branch, w_out, ln_g.reshape(1, -1),
      ln_b.reshape(1, -1), router_wt, router_b.reshape(-1, 1), jnp.eye(sub, dtype=F32), tri)


def _dest_kernel(start_ref, eidx_ref, rank_ref, o_ref, *, n_exp):
    eidx = eidx_ref[...]
    dest = rank_ref[...]
    for e in range(n_exp):
        dest = dest + jnp.where(eidx == e, start_ref[e], 0)
    o_ref[...] = dest


def _dest_rows(seg_start, eidx, rank):
    k, n = eidx.shape
    t = min(DEST_TILE, n)
    blk = pl.BlockSpec((k, t), lambda i, s: (0, i))
    return pl.pallas_call(
        functools.partial(_dest_kernel, n_exp=seg_start.shape[0]),
        grid_spec=pltpu.PrefetchScalarGridSpec(num_scalar_prefetch=1, grid=(n // t,), in_specs=[blk, blk],
                                               out_specs=blk),
        out_shape=jax.ShapeDtypeStruct((k, n), jnp.int32),
        compiler_params=_params("parallel"),
        name="dest_rows",
    )(seg_start, eidx, rank)


def _sc_workers():
    info = plsc.get_sparse_core_info()
    return info.num_cores, info.num_subcores


def _sc_mesh():
    return plsc.VectorSubcoreMesh(core_axis_name="c", subcore_axis_name="s")


def _sc_scatter_rows(x, dest, n_rows):
    n, c = x.shape
    k = dest.shape[0]
    n_cores, n_sub = _sc_workers()
    per_worker = n // (n_cores * n_sub)
    steps = per_worker // SC_WINDOW

    @functools.partial(pl.kernel, out_type=jax.ShapeDtypeStruct((n_rows, c), x.dtype), mesh=_sc_mesh(),
                       scratch_types=[pltpu.VMEM((k, SC_WINDOW), jnp.int32), pltpu.VMEM((SC_WINDOW, c), x.dtype),
                                      pltpu.SemaphoreType.DMA, pltpu.SemaphoreType.DMA],
                       name="dispatch_rows")
    def scatter(x_hbm, d_hbm, o_hbm, idx_v, rows_v, sem, row_sem):
        base = (lax.axis_index("s") * n_cores + lax.axis_index("c")) * per_worker

        @pl.loop(0, steps)
        def _(j):
            off = base + j * SC_WINDOW
            loads = [pltpu.async_copy(x_hbm.at[pl.ds(off, SC_WINDOW)], rows_v, row_sem)]
            loads += [pltpu.async_copy(d_hbm.at[pl.ds(kk * n + off, SC_WINDOW)], idx_v.at[kk], sem) for kk in range(k)]
            for cp in loads:
                cp.wait()
            copies = [pltpu.async_copy(rows_v, o_hbm.at[idx_v.at[kk]], sem) for kk in range(k)]
            for cp in copies:
                cp.wait()

    return scatter(x, dest.reshape(k * n))


def _sc_gather_rows(table, idx):
    b = idx.shape[0]
    c = table.shape[1]
    n_cores, n_sub = _sc_workers()
    per_worker = b // (n_cores * n_sub)
    steps = per_worker // SC_WINDOW

    half = SC_WINDOW // 2
    half_buf = lambda dtype, *shape: pltpu.VMEM((half,) + shape, dtype)

    @functools.partial(pl.kernel, out_type=jax.ShapeDtypeStruct((b, c), table.dtype), mesh=_sc_mesh(),
                       scratch_types=[half_buf(jnp.int32), half_buf(jnp.int32), half_buf(table.dtype, c),
                                      half_buf(table.dtype, c), pltpu.SemaphoreType.DMA, pltpu.SemaphoreType.DMA,
                                      pltpu.SemaphoreType.DMA, pltpu.SemaphoreType.DMA],
                       name="collect_rows")
    def gather(t_hbm, i_hbm, o_hbm, idx_a, idx_b, rows_a, rows_b, gat_a, gat_b, put_a, put_b):
        base = (lax.axis_index("s") * n_cores + lax.axis_index("c")) * per_worker

        @pl.loop(0, steps)
        def _(j):
            off_a = base + j * SC_WINDOW
            off_b = off_a + half
            pltpu.sync_copy(i_hbm.at[pl.ds(off_a, half)], idx_a)
            in_a = pltpu.async_copy(t_hbm.at[idx_a], rows_a, gat_a)
            pltpu.sync_copy(i_hbm.at[pl.ds(off_b, half)], idx_b)
            in_b = pltpu.async_copy(t_hbm.at[idx_b], rows_b, gat_b)
            in_a.wait()
            out_a = pltpu.async_copy(rows_a, o_hbm.at[pl.ds(off_a, half)], put_a)
            in_b.wait()
            out_b = pltpu.async_copy(rows_b, o_hbm.at[pl.ds(off_b, half)], put_b)
            out_a.wait()
            out_b.wait()

    return gather(table, idx)


def _expert_block_kernel(meta_ref, xs_ref, wgu_ref, wd_ref, ys_ref, wgu_b, wd_b, *, n_blocks):
    i = pl.program_id(0)

    @pl.when((i == 0) | (meta_ref[i] != meta_ref[jnp.maximum(i - 1, 0)]))
    def _():
        wgu_b[...] = wgu_ref[0, 0].astype(BF16)
        wd_b[...] = wd_ref[0, 0].astype(BF16)

    @pl.when(i < meta_ref[n_blocks])
    def _():
        half = xs_ref.shape[1]
        ff = wd_b.shape[0]
        sub = xs_ref.shape[0] // 2
        parts = [slice(0, sub), slice(sub, 2 * sub)]
        xs = [_unpack_bf16_pairs(xs_ref[rows, :]) for rows in parts]
        gus = [_dot(hi.astype(BF16), wgu_b[:half, :]) + _dot(lo.astype(BF16), wgu_b[half:, :]) for hi, lo in xs]
        mids = [(_silu(gu[:, :ff]) * gu[:, ff:]).astype(BF16) for gu in gus]
        outs = [_dot(mid, wd_b[...]) for mid in mids]
        for rows, out in zip(parts, outs):
            ys_ref[rows, :] = _pack_bf16_pairs(out)


def _expert_blocks(meta, xs, w_gu, w_down, layer):
    rows, half = xs.shape
    _, _, d, ff2 = w_gu.shape
    n_blocks = rows // MOE_BLOCK
    row_blk = pl.BlockSpec((MOE_BLOCK, half), lambda i, meta: (jnp.minimum(i, meta[n_blocks] - 1), 0))
    return pl.pallas_call(
        functools.partial(_expert_block_kernel, n_blocks=n_blocks),
        grid_spec=pltpu.PrefetchScalarGridSpec(
            num_scalar_prefetch=1, grid=(n_blocks,),
            in_specs=[row_blk,
                      pl.BlockSpec((1, 1, d, ff2), lambda i, meta: (layer, meta[i], 0, 0)),
                      pl.BlockSpec((1, 1, ff2 // 2, d), lambda i, meta: (layer, meta[i], 0, 0))],
            out_specs=row_blk,
            scratch_shapes=[pltpu.VMEM((d, ff2), BF16), pltpu.VMEM((ff2 // 2, d), BF16)]),
        out_shape=jax.ShapeDtypeStruct((rows, half), jnp.int32),
        compiler_params=_params("arbitrary"),
        name="expert_blocks",
    )(meta, xs, w_gu, w_down)


def _routed_rows(xp, eidx, rank, counts, w_gu, w_down, layer):
    n = xp.shape[0]
    k = eidx.shape[0]
    n_exp = w_gu.shape[1]
    counts = counts.reshape(n_exp).astype(jnp.int32)
    padded = (counts + MOE_BLOCK - 1) // MOE_BLOCK * MOE_BLOCK
    seg_end = jnp.cumsum(padded)
    n_blocks = k * n // MOE_BLOCK + n_exp
    blk_start = jnp.arange(n_blocks, dtype=jnp.int32) * MOE_BLOCK
    blk_expert = jnp.minimum(jnp.sum(seg_end[None, :] <= blk_start[:, None], axis=1), n_exp - 1)
    meta = jnp.concatenate([blk_expert, seg_end[-1:] // MOE_BLOCK]).astype(jnp.int32)
    dest = _dest_rows(seg_end - padded, eidx, rank)
    xs = _sc_scatter_rows(xp, dest, n_blocks * MOE_BLOCK)
    ys = _expert_blocks(meta, xs, w_gu, w_down, layer)
    return _sc_gather_rows(ys, dest.reshape(k * n)).reshape(k, n, -1)


def _shared_ple_kernel(h1_ref, h1b_ref, p_ref, wgu_ref, wd_ref, wpg_ref, bpg_ref, wpe_ref, o_ref, *, alpha):
    xb = h1b_ref[...]
    ff = wd_ref.shape[0]
    gu = _dot(xb, wgu_ref[...])
    shared = _dot((_silu(gu[:, :ff]) * gu[:, ff:]).astype(BF16), wd_ref[...])
    ple = _sigmoid(_dot(xb, wpg_ref[...]) + bpg_ref[...]) * _dot(p_ref[0].astype(BF16), wpe_ref[...])
    o_ref[...] = alpha * h1_ref[...] + shared + ple


def _shared_ple(h1, h1b, p3, layer, part, sh_w_gu, sh_w_down, ple_w_g, ple_b_g, ple_w_e, alpha):
    n, d = h1.shape
    t = min(ROW_TILE, n)
    steps = n // t
    pdim = p3.shape[2]
    ff2 = sh_w_gu.shape[1]
    tile = lambda cols: pl.BlockSpec((t, cols), lambda i: (i, 0))
    return pl.pallas_call(
        functools.partial(_shared_ple_kernel, alpha=alpha),
        grid=(n // t,),
        in_specs=[
            tile(d), tile(d), pl.BlockSpec((1, t, pdim), lambda i: (layer, part * steps + i, 0)),
            _const_spec((d, ff2)),
            _const_spec((ff2 // 2, d)),
            _const_spec((d, d)),
            _const_spec((1, d)),
            _const_spec((pdim, d)),
        ],
        out_specs=tile(d),
        out_shape=jax.ShapeDtypeStruct((n, d), F32),
        compiler_params=_params("parallel"),
        name="shared_ple",
    )(h1, h1b, p3, sh_w_gu, sh_w_down, ple_w_g, ple_b_g.reshape(1, -1), ple_w_e)


def _combine_norm_kernel(pre_ref, yk_ref, gwt_ref, g_ref, b_ref, o_ref):
    gwt = gwt_ref[...]
    routed_hi = routed_lo = None
    for k in range(yk_ref.shape[0]):
        hi, lo = _unpack_bf16_pairs(yk_ref[k])
        wk = gwt[:, k:k + 1]
        routed_hi = hi * wk if routed_hi is None else routed_hi + hi * wk
        routed_lo = lo * wk if routed_lo is None else routed_lo + lo * wk
    routed = jnp.concatenate([routed_hi, routed_lo], axis=1)
    o_ref[...] = _layer_norm(pre_ref[...] + routed, g_ref[...], b_ref[...])


def _combine_norm(pre, yk, gwt, ln_g, ln_b):
    n, d = pre.shape
    t = min(ROW_TILE, n)
    top_k = yk.shape[0]
    tile = lambda cols: pl.BlockSpec((t, cols), lambda i: (i, 0))
    return pl.pallas_call(
        _combine_norm_kernel,
        grid=(n // t,),
        in_specs=[tile(d), pl.BlockSpec((top_k, t, d // 2), lambda i: (0, i, 0)), tile(top_k),
                  _const_spec((1, d)), _const_spec((1, d))],
        out_specs=tile(d),
        out_shape=jax.ShapeDtypeStruct((n, d), F32),
        compiler_params=_params("parallel"),
        name="combine_norm",
    )(pre, yk, gwt, ln_g.reshape(1, -1), ln_b.reshape(1, -1))


def _block_diag(w):
    g, i, j = w.shape
    eye = jnp.eye(g, dtype=w.dtype)
    return (eye[:, None, :, None] * w[:, :, None, :]).reshape(g * i, g * j)


def kernel(x, p, ln_in_g, ln_in_b, w_in, b_in, ret_norm_g, ret_norm_b, gdn_conv_w, gdn_a_log, gdn_dt_bias, gdn_norm_g, lru_conv_w, lru_conv_b, lru_w_r, lru_b_r, lru_w_i, lru_b_i, lru_lambda, w_branch, w_out, ln1_g, ln1_b, router_w, router_b, exp_w_gu, exp_w_down, sh_w_gu, sh_w_down, ple_w_e, ple_w_g, ple_b_g, ln2_g, ln2_b):
    bsz, seq, d = x.shape
    depth = w_in.shape[0]
    n = bsz * seq
    width = N_HEADS * HEAD_DIM
    alpha = (2 * depth) ** 0.25
    lanes = 128
    o_ret = 0
    o_gdn = o_ret + 4 * width
    o_small = o_gdn + 4 * width
    o_lru = o_small + 2 * N_HEADS
    o_mg = o_lru + 2 * width

    parts = range(N_PARTS if bsz % N_PARTS == 0 else 1)
    bsz_p = bsz // len(parts)
    n_p = bsz_p * seq
    p3 = p.reshape(depth, n, -1)
    bf = lambda a: a.astype(BF16)
    hs = [_entry_norm(x.reshape(n, d), ln_in_g, ln_in_b, part, n_p) for part in parts]
    for l in range(depth):
        wl, bl = w_in[l], b_in[l]
        w_small = bf(jnp.zeros((d, lanes), F32).at[:, :2 * N_HEADS].set(wl[:, o_small:o_lru]))
        b_small = jnp.zeros((lanes,), F32).at[:2 * N_HEADS].set(bl[o_small:o_lru])
        w_ret, w_gdn, w_lru, w_mg = bf(wl[:, o_ret:o_gdn]), bf(wl[:, o_gdn:o_small]), bf(wl[:, o_lru:o_mg]), bf(wl[:, o_mg:])
        w_r, w_i = bf(_block_diag(lru_w_r[l])), bf(_block_diag(lru_w_i[l]))
        w_br, w_o, r_wt = bf(w_branch[l]), bf(w_out[l]), bf(router_w[l].T)
        w_sgu, w_sd, w_pg, w_pe = bf(sh_w_gu[l]), bf(sh_w_down[l]), bf(ple_w_g[l]), bf(ple_w_e[l])
        for part in parts:
            h = hs[part]
            h3 = h.reshape(bsz_p, seq, d)
            y_ret, y_gdn, y_lru = _mixers(
                h3, (w_ret, bl[o_ret:o_gdn], ret_norm_g[l], ret_norm_b[l]),
                (w_gdn, bl[o_gdn:o_small], w_small, b_small, gdn_conv_w[l], gdn_a_log[l], gdn_dt_bias[l], gdn_norm_g[l]),
                (w_lru, bl[o_lru:o_mg], lru_conv_w[l], lru_conv_b[l], w_r, lru_b_r[l], w_i, lru_b_i[l], lru_lambda[l]))
            h1, h1b, xp, eidx, rank, gwt, counts = _merge(
                h, y_ret.reshape(n_p, width), y_gdn.reshape(n_p, width), y_lru.reshape(n_p, width), w_mg, bl[o_mg:],
                w_br, w_o, ln1_g[l], ln1_b[l], r_wt, router_b[l], alpha)
            pre = _shared_ple(h1, h1b, p3, l, part, w_sgu, w_sd, w_pg, ple_b_g[l], w_pe, alpha)
            yk = _routed_rows(xp, eidx, rank, counts, exp_w_gu, exp_w_down, l)
            hs[part] = _combine_norm(pre, yk, gwt, ln2_g[l], ln2_b[l])
    return jnp.concatenate([h.reshape(bsz_p, seq, d) for h in hs], axis=0)
```

```python
import functools

import numpy as np
import jax
import jax.numpy as jnp
from jax import lax
from jax.experimental import pallas as pl
from jax.experimental.pallas import tpu as pltpu
from jax.experimental.pallas import tpu_sc as plsc

F32 = jnp.float32
BF16 = jnp.bfloat16

HEAD_DIM = 128
N_HEADS = 4
LRU_C = 8.0
CONV_WIDTH = 4
N_BRANCHES = 3
ROPE_BASE = 10000.0
N_GROUPS = 8
TOPK_GROUPS = 4
TOP_K = 8
ROUTED_SCALE = 2.5
LN_EPS = 1e-5
GDN_CHUNK = 64
SEQ_TILE = 256
MERGE_TILE = 512
MERGE_SUB = 256
ROW_TILE = 512
MOE_BLOCK = 512
DEST_TILE = 4096
COMBINE_CHUNKS = 4
N_PARTS = 1
SC_WINDOW = 128
CARRY_ROWS = 8
SCAN_GROUP = 16
VMEM_LIMIT_BYTES = 56 * 1024 * 1024
NEG_INF = float("-inf")


def _const_spec(shape):
    nd = len(shape)
    return pl.BlockSpec(shape, lambda *_: (0,) * nd, pipeline_mode=pl.Buffered(1))


def _params(*sem):
    return pltpu.CompilerParams(dimension_semantics=sem, vmem_limit_bytes=VMEM_LIMIT_BYTES)


def _layer_norm(x, g, b):
    mu = jnp.mean(x, axis=-1, keepdims=True)
    xc = x - mu
    var = jnp.mean(xc * xc, axis=-1, keepdims=True)
    return xc * lax.rsqrt(var + LN_EPS) * g + b


def _sigmoid(x):
    return 1.0 / (1.0 + jnp.exp(-x))


def _silu(x):
    return x * _sigmoid(x)


def _softplus(x):
    return jnp.maximum(x, 0.0) + jnp.log1p(jnp.exp(-jnp.abs(x)))


def _dot(a, b):
    return jnp.dot(a, b, preferred_element_type=F32)


def _dot_nt(a, b):
    return lax.dot_general(a, b, (((1,), (1,)), ((), ())), preferred_element_type=F32)


def _dot_tn(a, b):
    return lax.dot_general(a, b, (((0,), (0,)), ((), ())), preferred_element_type=F32)


def _ln_kernel(x_ref, g_ref, b_ref, o_ref):
    o_ref[...] = _layer_norm(x_ref[...], g_ref[...], b_ref[...])


def _entry_norm(x2, g, b, part, n_part):
    d = x2.shape[1]
    t = min(1024, n_part)
    steps = n_part // t
    return pl.pallas_call(
        _ln_kernel,
        grid=(steps,),
        in_specs=[pl.BlockSpec((t, d), lambda i: (part * steps + i, 0)), _const_spec((1, d)), _const_spec((1, d))],
        out_specs=pl.BlockSpec((t, d), lambda i: (i, 0)),
        out_shape=jax.ShapeDtypeStruct((n_part, d), F32),
        compiler_params=_params("parallel"),
        name="entry_norm",
    )(x2, g.reshape(1, d), b.reshape(1, d))


def _retention_body(hb, w_ref, b_ref, cos_ref, sin_ref, dmat_ref, qd_ref, kd_ref, ng_ref, nb_ref,
                    y_ref, state_ref, *, chunk_decay):
    width = N_HEADS * HEAD_DIM
    proj = _dot(hb, w_ref[...]) + b_ref[...]
    yield
    cos = cos_ref[...]
    sin = sin_ref[...]
    heads = range(N_HEADS)
    qs, ks, vbs = [], [], []
    for hh in heads:
        lo = hh * HEAD_DIM
        q = proj[:, lo:lo + HEAD_DIM]
        k = proj[:, width + lo:width + lo + HEAD_DIM]
        qs.append(q * cos + pltpu.roll(q, HEAD_DIM // 2, axis=1) * sin)
        ks.append((k * cos + pltpu.roll(k, HEAD_DIM // 2, axis=1) * sin) * (HEAD_DIM ** -0.5))
        vbs.append(proj[:, 2 * width + lo:2 * width + lo + HEAD_DIM].astype(BF16))
        yield
    states = [state_ref[hh] for hh in heads]
    scores, inter, outs = [], [], []
    for hh in heads:
        scores.append((_dot_nt(qs[hh].astype(BF16), ks[hh].astype(BF16)) * dmat_ref[hh]).astype(BF16))
        yield
    for hh in heads:
        inter.append(_dot((qs[hh] * qd_ref[hh]).astype(BF16), states[hh].astype(BF16)))
        yield
    for hh in heads:
        state_ref[hh] = states[hh] * chunk_decay[hh] + _dot_tn((ks[hh] * kd_ref[hh]).astype(BF16), vbs[hh])
        yield
    for hh in heads:
        outs.append(_dot(scores[hh], vbs[hh]) + inter[hh])
        yield
    for hh in heads:
        lo = hh * HEAD_DIM
        o = outs[hh]
        gate = proj[:, 3 * width + lo:3 * width + lo + HEAD_DIM]
        mu = jnp.mean(o, axis=-1, keepdims=True)
        oc = o - mu
        var = jnp.mean(oc * oc, axis=-1, keepdims=True)
        on = oc * lax.rsqrt(var + LN_EPS) * ng_ref[:, lo:lo + HEAD_DIM] + nb_ref[:, lo:lo + HEAD_DIM]
        y_ref[:, lo:lo + HEAD_DIM] = (_silu(gate) * on).astype(y_ref.dtype)
        yield


def _retention_tables(seq, tile):
    half = HEAD_DIM // 2
    inv_freq = ROPE_BASE ** (-np.linspace(0.0, 1.0, half))
    ang = np.arange(seq)[:, None] * inv_freq[None, :]
    cos = np.concatenate([np.cos(ang), np.cos(ang)], axis=1)
    sin = np.concatenate([-np.sin(ang), np.sin(ang)], axis=1)
    log_gamma = np.log1p(-np.exp2(-5.0 - np.arange(N_HEADS)))
    pos = np.arange(tile)
    diff = pos[:, None] - pos[None, :]
    dmat = np.where(diff >= 0, np.exp(log_gamma[:, None, None] * np.maximum(diff, 0)), 0.0)
    qd = np.exp(log_gamma[:, None] * (pos + 1.0))[:, :, None] * np.ones((1, 1, HEAD_DIM))
    kd = np.exp(log_gamma[:, None] * (tile - 1.0 - pos))[:, :, None] * np.ones((1, 1, HEAD_DIM))
    chunk_decay = tuple(float(c) for c in np.exp(log_gamma * tile))
    as32 = lambda a: jnp.asarray(a, F32)
    return as32(cos), as32(sin), as32(dmat), as32(qd), as32(kd), chunk_decay


def _causal_conv(x, xs_ref, cw_ref):
    t = x.shape[0]
    xs_ref[CARRY_ROWS:, :] = x
    ext = xs_ref[...]
    acc = ext * cw_ref[0:1, :]
    for j in range(1, CONV_WIDTH):
        acc = pltpu.roll(acc, 1, axis=0) + ext * cw_ref[j:j + 1, :]
    xs_ref[0:CARRY_ROWS, :] = xs_ref[t:t + CARRY_ROWS, :]
    return acc[CARRY_ROWS:, :]


def _cumsum_rows(x):
    n = x.shape[0]
    row = lax.broadcasted_iota(jnp.int32, x.shape, 0)
    d = 1
    while d < n:
        x = x + jnp.where(row >= d, pltpu.roll(x, d, axis=0), 0.0)
        d *= 2
    return x


def _gdn_body(hb, w_ref, b_ref, ws_ref, bs_ref, cw_ref, alog_ref, dtb_ref, ng_ref,
              y_ref, xs_ref, state_ref, u_ref, wf_ref, w_s_ref, qd_ref, kd_ref, qk_ref):
    width = N_HEADS * HEAD_DIM
    c = GDN_CHUNK
    t = hb.shape[0]
    proj = _dot(hb, w_ref[...]) + b_ref[...]
    small = _dot(hb, ws_ref[...]) + bs_ref[...]
    yield
    qkv = _silu(_causal_conv(proj[:, :3 * width], xs_ref, cw_ref))
    yield
    beta_all = _sigmoid(small)
    la_all = -jnp.exp(alog_ref[...]) * _softplus(small + dtb_ref[...])

    ri = lax.broadcasted_iota(jnp.int32, (c, c), 0)
    ci = lax.broadcasted_iota(jnp.int32, (c, c), 1)
    lower = ri >= ci
    strict = ri > ci

    items = [(n, hh) for n in range(t // c) for hh in range(N_HEADS)]
    gcs = {}
    for n in range(t // c):
        la_c = la_all[n * c:(n + 1) * c, :]
        gc_c = _cumsum_rows(la_c)
        gcs[n] = (la_c, gc_c, jnp.exp(gc_c))
    g_last, pws, rems = {}, {}, {}
    for n, hh in items:
        r0, lo = n * c, hh * HEAD_DIM
        rows, cols = slice(r0, r0 + c), slice(lo, lo + HEAD_DIM)
        la_c, gc_c, egc_c = gcs[n]
        q = qkv[rows, lo:lo + HEAD_DIM]
        k = qkv[rows, width + lo:width + lo + HEAD_DIM]
        v = qkv[rows, 2 * width + lo:2 * width + lo + HEAD_DIM]
        q = q * lax.rsqrt(jnp.sum(q * q, axis=-1, keepdims=True) + 1e-6) * (HEAD_DIM ** -0.5)
        k = k * lax.rsqrt(jnp.sum(k * k, axis=-1, keepdims=True) + 1e-6)
        beta = beta_all[rows, hh:hh + 1]
        la = la_c[:, N_HEADS + hh:N_HEADS + hh + 1]
        gc = gc_c[:, N_HEADS + hh:N_HEADS + hh + 1]
        egc = egc_c[:, N_HEADS + hh:N_HEADS + hh + 1]
        gc_row = jnp.sum(jnp.where(ri <= ci, jnp.broadcast_to(la, (c, c)), 0.0), axis=0, keepdims=True)
        gc_last = gc_row[:, c - 1:c]
        decay = jnp.where(lower, jnp.exp(jnp.where(lower, gc - gc_row, 0.0)), 0.0)
        kb = k * beta
        kbf = k.astype(BF16)
        a_neg = jnp.where(strict, -(_dot_nt(kb.astype(BF16), kbf) * decay), 0.0)
        pws[n, hh] = a_neg
        rems[n, hh] = a_neg
        u_ref[rows, cols] = v * beta
        wf_ref[rows, cols] = kb * egc
        qk_ref[hh, rows, :] = (_dot_nt(q.astype(BF16), kbf) * decay).astype(BF16)
        qd_ref[rows, cols] = (q * egc).astype(BF16)
        kd_ref[rows, cols] = (k * jnp.exp(gc_last - gc)).astype(BF16)
        g_last[n, hh] = jnp.exp(gc_last)
        yield
    m = 2
    while m < c:
        for it in items:
            pwb = pws[it].astype(BF16)
            pws[it] = _dot(pwb, pwb)
        yield
        for it in items:
            rems[it] = rems[it] + pws[it] + _dot(rems[it].astype(BF16), pws[it].astype(BF16))
        yield
        m *= 2
    for n, hh in items:
        rows, cols = slice(n * c, (n + 1) * c), slice(hh * HEAD_DIM, (hh + 1) * HEAD_DIM)
        remb = rems[n, hh].astype(BF16)
        u_ref[rows, cols] = u_ref[rows, cols] + _dot(remb, u_ref[rows, cols].astype(BF16))
        w_s_ref[rows, cols] = (wf_ref[rows, cols] + _dot(remb, wf_ref[rows, cols].astype(BF16))).astype(BF16)
    yield

    heads = range(N_HEADS)
    for n in range(t // c):
        rows = slice(n * c, (n + 1) * c)
        cols = [slice(hh * HEAD_DIM, (hh + 1) * HEAD_DIM) for hh in heads]
        states = [state_ref[hh] for hh in heads]
        sbs = [s.astype(BF16) for s in states]
        vnbs = [(u_ref[rows, cols[hh]] - _dot(w_s_ref[rows, cols[hh]], sbs[hh])).astype(BF16) for hh in heads]
        yield
        outs = [_dot(qd_ref[rows, cols[hh]], sbs[hh]) + _dot(qk_ref[hh, rows, :], vnbs[hh]) for hh in heads]
        for hh in heads:
            state_ref[hh] = states[hh] * g_last[n, hh] + _dot_tn(kd_ref[rows, cols[hh]], vnbs[hh])
        yield
        for hh in heads:
            o = outs[hh]
            o = o * lax.rsqrt(jnp.mean(o * o, axis=-1, keepdims=True) + 1e-6) * ng_ref[...]
            og = proj[rows, 3 * width + hh * HEAD_DIM:3 * width + (hh + 1) * HEAD_DIM]
            y_ref[rows, cols[hh]] = (o * _silu(og)).astype(y_ref.dtype)
        yield


def _lru_body(hb, w_ref, b_ref, cw_ref, cb_ref, wr_ref, br_ref, wi_ref, bi_ref, lam_ref,
              y_ref, xs_ref, carry_ref):
    width = cw_ref.shape[1]
    t = hb.shape[0]
    proj = _dot(hb, w_ref[...]) + b_ref[...]
    yield
    xc = _causal_conv(proj[:, :width], xs_ref, cw_ref) + cb_ref[...]
    xcb = xc.astype(BF16)
    yield
    r = _sigmoid(_dot(xcb, wr_ref[...]) + br_ref[...])
    gi = _sigmoid(_dot(xcb, wi_ref[...]) + bi_ref[...])
    yield
    log_a = -LRU_C * r * _softplus(-lam_ref[...])
    a = jnp.exp(log_a)
    th = jnp.tanh(log_a)
    hs = jnp.sqrt(-2.0 * th / (1.0 - th)) * (gi * xc)
    row = lax.broadcasted_iota(jnp.int32, (t, width), 0) % SCAN_GROUP
    d = 1
    while d < SCAN_GROUP:
        keep = row >= d
        hs = hs + a * jnp.where(keep, pltpu.roll(hs, d, axis=0), 0.0)
        a = a * jnp.where(keep, pltpu.roll(a, d, axis=0), 1.0)
        d *= 2
        yield
    gate = jax.nn.gelu(proj[:, width:], approximate=True)
    carry = carry_ref[...]
    for g in range(t // SCAN_GROUP):
        rows = slice(g * SCAN_GROUP, (g + 1) * SCAN_GROUP)
        hg = hs[rows, :] + a[rows, :] * carry
        carry = hg[SCAN_GROUP - 1:SCAN_GROUP, :]
        y_ref[rows, :] = (gate[rows, :] * hg).astype(y_ref.dtype)
        if g % 8 == 7:
            yield
    carry_ref[...] = carry


N_RET_IN, N_GDN_IN, N_LRU_IN = 9, 8, 9
MIX_STRIDE = (1, 2, 4)


def _mixers_kernel(h_ref, *refs, chunk_decay):
    ret_in, refs = refs[:N_RET_IN], refs[N_RET_IN:]
    gdn_in, refs = refs[:N_GDN_IN], refs[N_GDN_IN:]
    lru_in, refs = refs[:N_LRU_IN], refs[N_LRU_IN:]
    yr_ref, yg_ref, yl_ref = refs[:3]
    ret_state, gdn_xs, gdn_state, u_ref, wf_ref, w_s_ref, qd_ref, kd_ref, qk_ref, lru_xs, lru_carry = refs[3:]

    @pl.when(pl.program_id(1) == 0)
    def _():
        ret_state[...] = jnp.zeros_like(ret_state)
        gdn_state[...] = jnp.zeros_like(gdn_state)
        lru_carry[...] = jnp.zeros_like(lru_carry)
        gdn_xs[0:CARRY_ROWS, :] = jnp.zeros((CARRY_ROWS, gdn_xs.shape[1]), F32)
        lru_xs[0:CARRY_ROWS, :] = jnp.zeros((CARRY_ROWS, lru_xs.shape[1]), F32)

    hb = h_ref[0].astype(BF16)
    branches = [
        (_gdn_body(hb, *gdn_in, yg_ref.at[0], gdn_xs, gdn_state, u_ref, wf_ref, w_s_ref, qd_ref, kd_ref, qk_ref),
         MIX_STRIDE[0]),
        (_retention_body(hb, *ret_in, yr_ref.at[0], ret_state, chunk_decay=chunk_decay), MIX_STRIDE[1]),
        (_lru_body(hb, *lru_in, yl_ref.at[0], lru_xs, lru_carry), MIX_STRIDE[2]),
    ]
    tick = 0
    while branches:
        for gen, stride in list(branches):
            if tick % stride == 0 and next(gen, StopIteration) is StopIteration:
                branches.remove((gen, stride))
        tick += 1


def _mixers(h, ret_args, gdn_args, lru_args):
    bsz, seq, d = h.shape
    t = min(SEQ_TILE, seq)
    width = N_HEADS * HEAD_DIM
    row = lambda vec: vec.reshape(1, -1)
    w_ret, b_ret, ret_g, ret_b = ret_args
    w_gdn, b_gdn, w_small, b_small, gdn_cw, a_log, dt_bias, gdn_g = gdn_args
    w_lru, b_lru, lru_cw, lru_cb, w_r, b_r, w_i, b_i, lam = lru_args
    lanes = w_small.shape[1]
    lru_w = lru_cw.shape[1]
    cos, sin, dmat, qd, kd, chunk_decay = _retention_tables(seq, t)
    pad_row = lambda vec: jnp.zeros((1, lanes), F32).at[0, N_HEADS:2 * N_HEADS].set(vec.astype(F32))
    seq_tile = lambda cols: pl.BlockSpec((1, t, cols), lambda i, j: (i, j, 0))
    pos_tile = pl.BlockSpec((t, HEAD_DIM), lambda i, j: (j, 0))
    ret_specs = [_const_spec((d, 4 * width)), _const_spec((1, 4 * width)), pos_tile, pos_tile,
                 _const_spec((N_HEADS, t, t)), _const_spec((N_HEADS, t, HEAD_DIM)), _const_spec((N_HEADS, t, HEAD_DIM)),
                 _const_spec((1, width)), _const_spec((1, width))]
    gdn_specs = [_const_spec((d, 4 * width)), _const_spec((1, 4 * width)), _const_spec((d, lanes)),
                 _const_spec((1, lanes)), _const_spec((CONV_WIDTH, 3 * width)), _const_spec((1, lanes)),
                 _const_spec((1, lanes)), _const_spec((1, HEAD_DIM))]
    lru_specs = [_const_spec((d, 2 * lru_w)), _const_spec((1, 2 * lru_w)), _const_spec((CONV_WIDTH, lru_w)),
                 _const_spec((1, lru_w)), _const_spec((lru_w, lru_w)), _const_spec((1, lru_w)),
                 _const_spec((lru_w, lru_w)), _const_spec((1, lru_w)), _const_spec((1, lru_w))]
    assert (len(ret_specs), len(gdn_specs), len(lru_specs)) == (N_RET_IN, N_GDN_IN, N_LRU_IN)
    out = jax.ShapeDtypeStruct((bsz, seq, width), BF16)
    return pl.pallas_call(
        functools.partial(_mixers_kernel, chunk_decay=chunk_decay),
        grid=(bsz, seq // t),
        in_specs=[seq_tile(d)] + ret_specs + gdn_specs + lru_specs,
        out_specs=[seq_tile(width), seq_tile(width), seq_tile(lru_w)],
        out_shape=[out, out, jax.ShapeDtypeStruct((bsz, seq, lru_w), BF16)],
        scratch_shapes=[pltpu.VMEM((N_HEADS, HEAD_DIM, HEAD_DIM), F32),
                        pltpu.VMEM((CARRY_ROWS + t, 3 * width), F32),
                        pltpu.VMEM((N_HEADS, HEAD_DIM, HEAD_DIM), F32),
                        pltpu.VMEM((t, width), F32),
                        pltpu.VMEM((t, width), F32),
                        pltpu.VMEM((t, width), BF16),
                        pltpu.VMEM((t, width), BF16),
                        pltpu.VMEM((t, width), BF16),
                        pltpu.VMEM((N_HEADS, t, GDN_CHUNK), BF16),
                        pltpu.VMEM((CARRY_ROWS + t, lru_w), F32),
                        pltpu.VMEM((1, lru_w), F32)],
        compiler_params=_params("parallel", "arbitrary"),
        name="token_mixers",
    )(h, w_ret, row(b_ret), cos, sin, dmat, qd, kd, row(ret_g), row(ret_b),
      w_gdn, row(b_gdn), w_small, row(b_small), gdn_cw, pad_row(a_log), pad_row(dt_bias), row(gdn_g),
      w_lru, row(b_lru), lru_cw, row(lru_cb), w_r, row(b_r), w_i, row(b_i), row(lam))


def _first_index_of_max(x, idx, size):
    m = jnp.max(x, axis=0, keepdims=True)
    first = jnp.min(jnp.where(x == m, idx, size), axis=0, keepdims=True)
    return m, idx == first


def _route(logits_t, bias_col):
    n_exp, t = logits_t.shape
    per_group = n_exp // N_GROUPS
    scores = _sigmoid(logits_t)
    sel = scores + bias_col
    idx_g = lax.broadcasted_iota(jnp.int32, (per_group, t), 0)
    group_scores = []
    for g in range(N_GROUPS):
        x = sel[g * per_group:(g + 1) * per_group, :]
        m1, hit = _first_index_of_max(x, idx_g, per_group)
        m2 = jnp.max(jnp.where(hit, NEG_INF, x), axis=0, keepdims=True)
        group_scores.append(m1 + m2)
    gsc = jnp.concatenate(group_scores, axis=0)
    idx_n = lax.broadcasted_iota(jnp.int32, (N_GROUPS, t), 0)
    gmask = jnp.zeros((N_GROUPS, t), F32)
    for _ in range(TOPK_GROUPS):
        _, hit = _first_index_of_max(gsc, idx_n, N_GROUPS)
        gmask = jnp.where(hit, 1.0, gmask)
        gsc = jnp.where(hit, NEG_INF, gsc)
    emask = jnp.concatenate([jnp.broadcast_to(gmask[g:g + 1, :], (per_group, t)) for g in range(N_GROUPS)], axis=0)
    cand = jnp.where(emask > 0.0, sel, NEG_INF)
    idx_e = lax.broadcasted_iota(jnp.int32, (n_exp, t), 0)
    picked = jnp.zeros((n_exp, t), F32)
    hits = []
    for _ in range(TOP_K):
        _, hit = _first_index_of_max(cand, idx_e, n_exp)
        hits.append(hit)
        picked = jnp.where(hit, 1.0, picked)
        cand = jnp.where(hit, NEG_INF, cand)
    gw = jnp.where(picked > 0.0, scores, 0.0)
    return gw / jnp.sum(gw, axis=0, keepdims=True) * ROUTED_SCALE, picked, hits, idx_e


def _pack_bf16_pairs(x):
    c = x.shape[1] // 2
    hi = pltpu.bitcast(x[:, :c].astype(BF16).astype(F32), jnp.int32)
    lo = pltpu.bitcast(x[:, c:].astype(BF16).astype(F32), jnp.int32)
    return hi | lax.shift_right_logical(lo, jnp.full(lo.shape, 16, jnp.int32))


def _unpack_bf16_pairs(w):
    hi = pltpu.bitcast(w & jnp.int32(-65536), F32)
    lo = pltpu.bitcast(lax.shift_left(w, jnp.full(w.shape, 16, jnp.int32)), F32)
    return hi, lo


def _merge_kernel(h_ref, yr_ref, yg_ref, yl_ref, wmg_ref, bmg_ref, wbr_ref, wout_ref, g_ref, b_ref,
                  rwt_ref, rb_ref, eye_ref, tri_ref, h1_ref, h1b_ref, xp_ref, eidx_ref, rank_ref, gwt_ref,
                  count_ref, *, alpha):
    @pl.when(pl.program_id(0) == 0)
    def _():
        count_ref[...] = jnp.zeros_like(count_ref)

    d = h_ref.shape[1]
    sub = eye_ref.shape[0]
    parts = [slice(s, s + sub) for s in range(0, h_ref.shape[0], sub)]
    logits = []
    for rows in parts:
        h = h_ref[rows, :]
        hb = h.astype(BF16)
        gates = _sigmoid(_dot(hb, wmg_ref[...]) + bmg_ref[...])
        mixed = None
        for n, y_ref in enumerate((yr_ref, yg_ref, yl_ref)):
            term = gates[:, n * d:(n + 1) * d] * _dot(y_ref[rows, :], wbr_ref[n])
            mixed = term if mixed is None else mixed + term
        mix = _dot(mixed.astype(BF16), wout_ref[...])
        h1 = _layer_norm(alpha * h + mix, g_ref[...], b_ref[...])
        h1b = h1.astype(BF16)
        h1_ref[rows, :] = h1
        h1b_ref[rows, :] = h1b
        xp_ref[rows, :] = _pack_bf16_pairs(h1)
        logits.append(_dot_nt(rwt_ref[...], h1b))
    pick = lambda hit, vals, zero: jnp.sum(jnp.where(hit, vals, zero), axis=0, keepdims=True)
    for rows, logit in zip(parts, logits):
        combine_t, picked, hits, idx_e = _route(logit, rb_ref[...])
        rank_full = count_ref[...] + _dot(picked.astype(BF16), tri_ref[...])
        count_ref[...] += jnp.sum(picked, axis=1, keepdims=True)
        eidx_ref[:, rows] = jnp.concatenate([pick(hit, idx_e, 0) for hit in hits], axis=0)
        rank_ref[:, rows] = jnp.concatenate([pick(hit, rank_full, 0.0) for hit in hits], axis=0).astype(jnp.int32)
        gw = jnp.concatenate([pick(hit, combine_t, 0.0) for hit in hits], axis=0)
        gwt_ref[rows, :] = lax.dot_general(eye_ref[...], gw, (((1,), (1,)), ((), ())),
                                           preferred_element_type=F32, precision=lax.Precision.HIGHEST)


def _merge(h2, y_ret, y_gdn, y_lru, w_mg, b_mg, w_branch, w_out, ln_g, ln_b, router_wt, router_b, alpha):
    n, d = h2.shape
    t = min(MERGE_TILE, n)
    width = y_ret.shape[1]
    n_exp = router_wt.shape[0]
    tile = lambda cols: pl.BlockSpec((t, cols), lambda i: (i, 0))
    lane_tile = pl.BlockSpec((TOP_K, t), lambda i: (0, i))
    sub = min(MERGE_SUB, t)
    tri = jnp.triu(jnp.ones((sub, sub), BF16), 1)
    return pl.pallas_call(
        functools.partial(_merge_kernel, alpha=alpha),
        grid=(n // t,),
        in_specs=[
            tile(d), tile(width), tile(width), tile(width),
            _const_spec((d, N_BRANCHES * d)),
            _const_spec((1, N_BRANCHES * d)),
            _const_spec((N_BRANCHES, width, d)),
            _const_spec((d, d)),
            _const_spec((1, d)),
            _const_spec((1, d)),
            _const_spec((n_exp, d)),
            _const_spec((n_exp, 1)),
            _const_spec((sub, sub)),
            _const_spec((sub, sub)),
        ],
        out_specs=[tile(d), tile(d), tile(d // 2), lane_tile, lane_tile, tile(TOP_K),
                   pl.BlockSpec((n_exp, 1), lambda i: (0, 0))],
        out_shape=[jax.ShapeDtypeStruct((n, d), F32), jax.ShapeDtypeStruct((n, d), BF16),
                   jax.ShapeDtypeStruct((n, d // 2), jnp.int32),
                   jax.ShapeDtypeStruct((TOP_K, n), jnp.int32), jax.ShapeDtypeStruct((TOP_K, n), jnp.int32),
                   jax.ShapeDtypeStruct((n, TOP_K), F32), jax.ShapeDtypeStruct((n_exp, 1), F32)],
        compiler_params=_params("arbitrary"),
        name="merge_route",
    )(h2, y_ret, y_gdn, y_lru, w_mg, b_mg.reshape(1, -1), w_branch, w_out, ln_g.reshape(1, -1),
      ln_b.reshape(1, -1), router_wt, router_b.reshape(-1, 1), jnp.eye(sub, dtype=F32), tri)


def _dest_kernel(start_ref, eidx_ref, rank_ref, o_ref, *, n_exp):
    eidx = eidx_ref[...]
    dest = rank_ref[...]
    for e in range(n_exp):
        dest = dest + jnp.where(eidx == e, start_ref[e], 0)
    o_ref[...] = dest


def _dest_rows(seg_start, eidx, rank):
    k, n = eidx.shape
    t = min(DEST_TILE, n)
    blk = pl.BlockSpec((k, t), lambda i, s: (0, i))
    return pl.pallas_call(
        functools.partial(_dest_kernel, n_exp=seg_start.shape[0]),
        grid_spec=pltpu.PrefetchScalarGridSpec(num_scalar_prefetch=1, grid=(n // t,), in_specs=[blk, blk],
                                               out_specs=blk),
        out_shape=jax.ShapeDtypeStruct((k, n), jnp.int32),
        compiler_params=_params("parallel"),
        name="dest_rows",
    )(seg_start, eidx, rank)


def _sc_workers():
    info = plsc.get_sparse_core_info()
    return info.num_cores, info.num_subcores


def _sc_mesh():
    return plsc.VectorSubcoreMesh(core_axis_name="c", subcore_axis_name="s")


def _sc_scatter_rows(x, dest, n_rows):
    n, c = x.shape
    k = dest.shape[0]
    n_cores, n_sub = _sc_workers()
    per_worker = n // (n_cores * n_sub)
    steps = per_worker // SC_WINDOW

    @functools.partial(pl.kernel, out_type=jax.ShapeDtypeStruct((n_rows, c), x.dtype), mesh=_sc_mesh(),
                       scratch_types=[pltpu.VMEM((k, SC_WINDOW), jnp.int32), pltpu.VMEM((SC_WINDOW, c), x.dtype),
                                      pltpu.SemaphoreType.DMA, pltpu.SemaphoreType.DMA],
                       name="dispatch_rows")
    def scatter(x_hbm, d_hbm, o_hbm, idx_v, rows_v, sem, row_sem):
        base = (lax.axis_index("s") * n_cores + lax.axis_index("c")) * per_worker

        @pl.loop(0, steps)
        def _(j):
            off = base + j * SC_WINDOW
            loads = [pltpu.async_copy(x_hbm.at[pl.ds(off, SC_WINDOW)], rows_v, row_sem)]
            loads += [pltpu.async_copy(d_hbm.at[pl.ds(kk * n + off, SC_WINDOW)], idx_v.at[kk], sem) for kk in range(k)]
            for cp in loads:
                cp.wait()
            copies = [pltpu.async_copy(rows_v, o_hbm.at[idx_v.at[kk]], sem) for kk in range(k)]
            for cp in copies:
                cp.wait()

    return scatter(x, dest.reshape(k * n))


def _sc_gather_rows(table, idx):
    b = idx.shape[0]
    c = table.shape[1]
    n_cores, n_sub = _sc_workers()
    per_worker = b // (n_cores * n_sub)
    steps = per_worker // SC_WINDOW

    half = SC_WINDOW // 2
    half_buf = lambda dtype, *shape: pltpu.VMEM((half,) + shape, dtype)

    @functools.partial(pl.kernel, out_type=jax.ShapeDtypeStruct((b, c), table.dtype), mesh=_sc_mesh(),
                       scratch_types=[half_buf(jnp.int32), half_buf(jnp.int32), half_buf(table.dtype, c),
                                      half_buf(table.dtype, c), pltpu.SemaphoreType.DMA, pltpu.SemaphoreType.DMA,
                                      pltpu.SemaphoreType.DMA, pltpu.SemaphoreType.DMA],
                       name="collect_rows")
    def gather(t_hbm, i_hbm, o_hbm, idx_a, idx_b, rows_a, rows_b, gat_a, gat_b, put_a, put_b):
        base = (lax.axis_index("s") * n_cores + lax.axis_index("c")) * per_worker

        @pl.loop(0, steps)
        def _(j):
            off_a = base + j * SC_WINDOW
            off_b = off_a + half
            pltpu.sync_copy(i_hbm.at[pl.ds(off_a, half)], idx_a)
            in_a = pltpu.async_copy(t_hbm.at[idx_a], rows_a, gat_a)
            pltpu.sync_copy(i_hbm.at[pl.ds(off_b, half)], idx_b)
            in_b = pltpu.async_copy(t_hbm.at[idx_b], rows_b, gat_b)
            in_a.wait()
            out_a = pltpu.async_copy(rows_a, o_hbm.at[pl.ds(off_a, half)], put_a)
            in_b.wait()
            out_b = pltpu.async_copy(rows_b, o_hbm.at[pl.ds(off_b, half)], put_b)
            out_a.wait()
            out_b.wait()

    return gather(table, idx)


def _expert_block_kernel(meta_ref, xs_ref, wgu_ref, wd_ref, ys_ref, wgu_b, wd_b, *, n_blocks):
    i = pl.program_id(0)

    @pl.when((i == 0) | (meta_ref[i] != meta_ref[jnp.maximum(i - 1, 0)]))
    def _():
        wgu_b[...] = wgu_ref[0, 0].astype(BF16)
        wd_b[...] = wd_ref[0, 0].astype(BF16)

    @pl.when(i < meta_ref[n_blocks])
    def _():
        half = xs_ref.shape[1]
        ff = wd_b.shape[0]
        sub = xs_ref.shape[0] // 2
        parts = [slice(0, sub), slice(sub, 2 * sub)]
        xs = [_unpack_bf16_pairs(xs_ref[rows, :]) for rows in parts]
        gus = [_dot(hi.astype(BF16), wgu_b[:half, :]) + _dot(lo.astype(BF16), wgu_b[half:, :]) for hi, lo in xs]
        mids = [(_silu(gu[:, :ff]) * gu[:, ff:]).astype(BF16) for gu in gus]
        outs = [_dot(mid, wd_b[...]) for mid in mids]
        for rows, out in zip(parts, outs):
            ys_ref[rows, :] = _pack_bf16_pairs(out)


def _expert_blocks(meta, xs, w_gu, w_down, layer):
    rows, half = xs.shape
    _, _, d, ff2 = w_gu.shape
    n_blocks = rows // MOE_BLOCK
    row_blk = pl.BlockSpec((MOE_BLOCK, half), lambda i, meta: (jnp.minimum(i, meta[n_blocks] - 1), 0))
    return pl.pallas_call(
        functools.partial(_expert_block_kernel, n_blocks=n_blocks),
        grid_spec=pltpu.PrefetchScalarGridSpec(
            num_scalar_prefetch=1, grid=(n_blocks,),
            in_specs=[row_blk,
                      pl.BlockSpec((1, 1, d, ff2), lambda i, meta: (layer, meta[i], 0, 0)),
                      pl.BlockSpec((1, 1, ff2 // 2, d), lambda i, meta: (layer, meta[i], 0, 0))],
            out_specs=row_blk,
            scratch_shapes=[pltpu.VMEM((d, ff2), BF16), pltpu.VMEM((ff2 // 2, d), BF16)]),
        out_shape=jax.ShapeDtypeStruct((rows, half), jnp.int32),
        compiler_params=_params("arbitrary"),
        name="expert_blocks",
    )(meta, xs, w_gu, w_down)


def _routed_rows(xp, eidx, rank, counts, w_gu, w_down, layer):
    n = xp.shape[0]
    k = eidx.shape[0]
    n_exp = w_gu.shape[1]
    counts = counts.reshape(n_exp).astype(jnp.int32)
    padded = (counts + MOE_BLOCK - 1) // MOE_BLOCK * MOE_BLOCK
    seg_end = jnp.cumsum(padded)
    n_blocks = k * n // MOE_BLOCK + n_exp
    blk_start = jnp.arange(n_blocks, dtype=jnp.int32) * MOE_BLOCK
    blk_expert = jnp.minimum(jnp.sum(seg_end[None, :] <= blk_start[:, None], axis=1), n_exp - 1)
    meta = jnp.concatenate([blk_expert, seg_end[-1:] // MOE_BLOCK]).astype(jnp.int32)
    dest = _dest_rows(seg_end - padded, eidx, rank)
    xs = _sc_scatter_rows(xp, dest, n_blocks * MOE_BLOCK)
    return _expert_blocks(meta, xs, w_gu, w_down, layer), dest


def _shared_ple_kernel(h1_ref, h1b_ref, p_ref, wgu_ref, wd_ref, wpg_ref, bpg_ref, wpe_ref, o_ref, *, alpha):
    xb = h1b_ref[...]
    ff = wd_ref.shape[0]
    gu = _dot(xb, wgu_ref[...])
    shared = _dot((_silu(gu[:, :ff]) * gu[:, ff:]).astype(BF16), wd_ref[...])
    ple = _sigmoid(_dot(xb, wpg_ref[...]) + bpg_ref[...]) * _dot(p_ref[0].astype(BF16), wpe_ref[...])
    o_ref[...] = alpha * h1_ref[...] + shared + ple


def _shared_ple(h1, h1b, p3, layer, part, sh_w_gu, sh_w_down, ple_w_g, ple_b_g, ple_w_e, alpha):
    n, d = h1.shape
    t = min(ROW_TILE, n)
    steps = n // t
    pdim = p3.shape[2]
    ff2 = sh_w_gu.shape[1]
    tile = lambda cols: pl.BlockSpec((t, cols), lambda i: (i, 0))
    return pl.pallas_call(
        functools.partial(_shared_ple_kernel, alpha=alpha),
        grid=(n // t,),
        in_specs=[
            tile(d), tile(d), pl.BlockSpec((1, t, pdim), lambda i: (layer, part * steps + i, 0)),
            _const_spec((d, ff2)),
            _const_spec((ff2 // 2, d)),
            _const_spec((d, d)),
            _const_spec((1, d)),
            _const_spec((pdim, d)),
        ],
        out_specs=tile(d),
        out_shape=jax.ShapeDtypeStruct((n, d), F32),
        compiler_params=_params("parallel"),
        name="shared_ple",
    )(h1, h1b, p3, sh_w_gu, sh_w_down, ple_w_g, ple_b_g.reshape(1, -1), ple_w_e)


def _combine_norm_kernel(pre_ref, yk_ref, gwt_ref, g_ref, b_ref, *rest):
    o_ref = rest[-1]
    gwt = gwt_ref[...]
    routed_hi = routed_lo = None
    for k in range(yk_ref.shape[0]):
        hi, lo = _unpack_bf16_pairs(yk_ref[k])
        wk = gwt[:, k:k + 1]
        routed_hi = hi * wk if routed_hi is None else routed_hi + hi * wk
        routed_lo = lo * wk if routed_lo is None else routed_lo + lo * wk
    routed = jnp.concatenate([routed_hi, routed_lo], axis=1)
    o_ref[...] = _layer_norm(pre_ref[...] + routed, g_ref[...], b_ref[...])


def _combine_norm(pre, ys, dest, gwt, ln_g, ln_b):
    n, d = pre.shape
    top_k = dest.shape[0]
    t = min(ROW_TILE, n)
    chunks = COMBINE_CHUNKS if n % (COMBINE_CHUNKS * t) == 0 else 1
    n_c = n // chunks
    steps = n_c // t
    out = None
    for c in range(chunks):
        rows = _sc_gather_rows(ys, dest[:, c * n_c:(c + 1) * n_c].reshape(top_k * n_c)).reshape(top_k, n_c, d // 2)
        tile = lambda cols: pl.BlockSpec((t, cols), lambda i, c=c: (c * steps + i, 0))
        in_specs = [tile(d), pl.BlockSpec((top_k, t, d // 2), lambda i: (0, i, 0)), tile(top_k),
                    _const_spec((1, d)), _const_spec((1, d))]
        args = [pre, rows, gwt, ln_g.reshape(1, -1), ln_b.reshape(1, -1)]
        if out is not None:
            in_specs.append(pl.BlockSpec(memory_space=pl.ANY))
            args.append(out)
        out = pl.pallas_call(
            _combine_norm_kernel,
            grid=(steps,),
            in_specs=in_specs,
            out_specs=tile(d),
            out_shape=jax.ShapeDtypeStruct((n, d), F32),
            input_output_aliases={} if out is None else {len(args) - 1: 0},
            compiler_params=_params("parallel"),
            name="combine_norm",
        )(*args)
    return out


def _block_diag(w):
    g, i, j = w.shape
    eye = jnp.eye(g, dtype=w.dtype)
    return (eye[:, None, :, None] * w[:, :, None, :]).reshape(g * i, g * j)


def kernel(x, p, ln_in_g, ln_in_b, w_in, b_in, ret_norm_g, ret_norm_b, gdn_conv_w, gdn_a_log, gdn_dt_bias, gdn_norm_g, lru_conv_w, lru_conv_b, lru_w_r, lru_b_r, lru_w_i, lru_b_i, lru_lambda, w_branch, w_out, ln1_g, ln1_b, router_w, router_b, exp_w_gu, exp_w_down, sh_w_gu, sh_w_down, ple_w_e, ple_w_g, ple_b_g, ln2_g, ln2_b):
    bsz, seq, d = x.shape
    depth = w_in.shape[0]
    n = bsz * seq
    width = N_HEADS * HEAD_DIM
    alpha = (2 * depth) ** 0.25
    lanes = 128
    o_ret = 0
    o_gdn = o_ret + 4 * width
    o_small = o_gdn + 4 * width
    o_lru = o_small + 2 * N_HEADS
    o_mg = o_lru + 2 * width

    parts = range(N_PARTS if bsz % N_PARTS == 0 else 1)
    bsz_p = bsz // len(parts)
    n_p = bsz_p * seq
    p3 = p.reshape(depth, n, -1)
    bf = lambda a: a.astype(BF16)
    hs = [_entry_norm(x.reshape(n, d), ln_in_g, ln_in_b, part, n_p) for part in parts]
    for l in range(depth):
        wl, bl = w_in[l], b_in[l]
        w_small = bf(jnp.zeros((d, lanes), F32).at[:, :2 * N_HEADS].set(wl[:, o_small:o_lru]))
        b_small = jnp.zeros((lanes,), F32).at[:2 * N_HEADS].set(bl[o_small:o_lru])
        w_ret, w_gdn, w_lru, w_mg = bf(wl[:, o_ret:o_gdn]), bf(wl[:, o_gdn:o_small]), bf(wl[:, o_lru:o_mg]), bf(wl[:, o_mg:])
        w_r, w_i = bf(_block_diag(lru_w_r[l])), bf(_block_diag(lru_w_i[l]))
        w_br, w_o, r_wt = bf(w_branch[l]), bf(w_out[l]), bf(router_w[l].T)
        w_sgu, w_sd, w_pg, w_pe = bf(sh_w_gu[l]), bf(sh_w_down[l]), bf(ple_w_g[l]), bf(ple_w_e[l])
        for part in parts:
            h = hs[part]
            h3 = h.reshape(bsz_p, seq, d)
            y_ret, y_gdn, y_lru = _mixers(
                h3, (w_ret, bl[o_ret:o_gdn], ret_norm_g[l], ret_norm_b[l]),
                (w_gdn, bl[o_gdn:o_small], w_small, b_small, gdn_conv_w[l], gdn_a_log[l], gdn_dt_bias[l], gdn_norm_g[l]),
                (w_lru, bl[o_lru:o_mg], lru_conv_w[l], lru_conv_b[l], w_r, lru_b_r[l], w_i, lru_b_i[l], lru_lambda[l]))
            h1, h1b, xp, eidx, rank, gwt, counts = _merge(
                h, y_ret.reshape(n_p, width), y_gdn.reshape(n_p, width), y_lru.reshape(n_p, width), w_mg, bl[o_mg:],
                w_br, w_o, ln1_g[l], ln1_b[l], r_wt, router_b[l], alpha)
            pre = _shared_ple(h1, h1b, p3, l, part, w_sgu, w_sd, w_pg, ple_b_g[l], w_pe, alpha)
            ys, dest = _routed_rows(xp, eidx, rank, counts, exp_w_gu, exp_w_down, l)
            hs[part] = _combine_norm(pre, ys, dest, gwt, ln2_g[l], ln2_b[l])
    return jnp.concatenate([h.reshape(bsz_p, seq, d) for h in hs], axis=0)
```

```python
import functools

import numpy as np
import jax
import jax.numpy as jnp
from jax import lax
from jax.experimental import pallas as pl
from jax.experimental.pallas import tpu as pltpu
from jax.experimental.pallas import tpu_sc as plsc

F32 = jnp.float32
BF16 = jnp.bfloat16

HEAD_DIM = 128
N_HEADS = 4
LRU_C = 8.0
CONV_WIDTH = 4
N_BRANCHES = 3
ROPE_BASE = 10000.0
N_GROUPS = 8
TOPK_GROUPS = 4
TOP_K = 8
ROUTED_SCALE = 2.5
LN_EPS = 1e-5
GDN_CHUNK = 64
SEQ_TILE = 256
MERGE_TILE = 512
MERGE_SUB = 256
ROW_TILE = 512
MOE_BLOCK = 1024
DEST_TILE = 4096
COMBINE_CHUNKS = 4
N_PARTS = 1
SC_WINDOW = 128
CARRY_ROWS = 8
SCAN_GROUP = 16
VMEM_LIMIT_BYTES = 56 * 1024 * 1024
NEG_INF = float("-inf")


def _const_spec(shape):
    nd = len(shape)
    return pl.BlockSpec(shape, lambda *_: (0,) * nd, pipeline_mode=pl.Buffered(1))


def _params(*sem):
    return pltpu.CompilerParams(dimension_semantics=sem, vmem_limit_bytes=VMEM_LIMIT_BYTES)


def _layer_norm(x, g, b):
    mu = jnp.mean(x, axis=-1, keepdims=True)
    xc = x - mu
    var = jnp.mean(xc * xc, axis=-1, keepdims=True)
    return xc * lax.rsqrt(var + LN_EPS) * g + b


def _sigmoid(x):
    return 1.0 / (1.0 + jnp.exp(-x))


def _silu(x):
    return x * _sigmoid(x)


def _softplus(x):
    return jnp.maximum(x, 0.0) + jnp.log1p(jnp.exp(-jnp.abs(x)))


def _dot(a, b):
    return jnp.dot(a, b, preferred_element_type=F32)


def _dot_nt(a, b):
    return lax.dot_general(a, b, (((1,), (1,)), ((), ())), preferred_element_type=F32)


def _dot_tn(a, b):
    return lax.dot_general(a, b, (((0,), (0,)), ((), ())), preferred_element_type=F32)


def _ln_kernel(x_ref, g_ref, b_ref, o_ref):
    o_ref[...] = _layer_norm(x_ref[...], g_ref[...], b_ref[...])


def _entry_norm(x2, g, b, part, n_part):
    d = x2.shape[1]
    t = min(1024, n_part)
    steps = n_part // t
    return pl.pallas_call(
        _ln_kernel,
        grid=(steps,),
        in_specs=[pl.BlockSpec((t, d), lambda i: (part * steps + i, 0)), _const_spec((1, d)), _const_spec((1, d))],
        out_specs=pl.BlockSpec((t, d), lambda i: (i, 0)),
        out_shape=jax.ShapeDtypeStruct((n_part, d), F32),
        compiler_params=_params("parallel"),
        name="entry_norm",
    )(x2, g.reshape(1, d), b.reshape(1, d))


def _retention_body(hb, w_ref, b_ref, cos_ref, sin_ref, dmat_ref, qd_ref, kd_ref, ng_ref, nb_ref,
                    y_ref, state_ref, *, chunk_decay):
    width = N_HEADS * HEAD_DIM
    proj = _dot(hb, w_ref[...]) + b_ref[...]
    yield
    cos = cos_ref[...]
    sin = sin_ref[...]
    heads = range(N_HEADS)
    qs, ks, vbs = [], [], []
    for hh in heads:
        lo = hh * HEAD_DIM
        q = proj[:, lo:lo + HEAD_DIM]
        k = proj[:, width + lo:width + lo + HEAD_DIM]
        qs.append(q * cos + pltpu.roll(q, HEAD_DIM // 2, axis=1) * sin)
        ks.append((k * cos + pltpu.roll(k, HEAD_DIM // 2, axis=1) * sin) * (HEAD_DIM ** -0.5))
        vbs.append(proj[:, 2 * width + lo:2 * width + lo + HEAD_DIM].astype(BF16))
        yield
    states = [state_ref[hh] for hh in heads]
    scores, inter, outs = [], [], []
    for hh in heads:
        scores.append((_dot_nt(qs[hh].astype(BF16), ks[hh].astype(BF16)) * dmat_ref[hh]).astype(BF16))
        yield
    for hh in heads:
        inter.append(_dot((qs[hh] * qd_ref[hh]).astype(BF16), states[hh].astype(BF16)))
        yield
    for hh in heads:
        state_ref[hh] = states[hh] * chunk_decay[hh] + _dot_tn((ks[hh] * kd_ref[hh]).astype(BF16), vbs[hh])
        yield
    for hh in heads:
        outs.append(_dot(scores[hh], vbs[hh]) + inter[hh])
        yield
    for hh in heads:
        lo = hh * HEAD_DIM
        o = outs[hh]
        gate = proj[:, 3 * width + lo:3 * width + lo + HEAD_DIM]
        mu = jnp.mean(o, axis=-1, keepdims=True)
        oc = o - mu
        var = jnp.mean(oc * oc, axis=-1, keepdims=True)
        on = oc * lax.rsqrt(var + LN_EPS) * ng_ref[:, lo:lo + HEAD_DIM] + nb_ref[:, lo:lo + HEAD_DIM]
        y_ref[:, lo:lo + HEAD_DIM] = (_silu(gate) * on).astype(y_ref.dtype)
        yield


def _retention_tables(seq, tile):
    half = HEAD_DIM // 2
    inv_freq = ROPE_BASE ** (-np.linspace(0.0, 1.0, half))
    ang = np.arange(seq)[:, None] * inv_freq[None, :]
    cos = np.concatenate([np.cos(ang), np.cos(ang)], axis=1)
    sin = np.concatenate([-np.sin(ang), np.sin(ang)], axis=1)
    log_gamma = np.log1p(-np.exp2(-5.0 - np.arange(N_HEADS)))
    pos = np.arange(tile)
    diff = pos[:, None] - pos[None, :]
    dmat = np.where(diff >= 0, np.exp(log_gamma[:, None, None] * np.maximum(diff, 0)), 0.0)
    qd = np.exp(log_gamma[:, None] * (pos + 1.0))[:, :, None] * np.ones((1, 1, HEAD_DIM))
    kd = np.exp(log_gamma[:, None] * (tile - 1.0 - pos))[:, :, None] * np.ones((1, 1, HEAD_DIM))
    chunk_decay = tuple(float(c) for c in np.exp(log_gamma * tile))
    as32 = lambda a: jnp.asarray(a, F32)
    return as32(cos), as32(sin), as32(dmat), as32(qd), as32(kd), chunk_decay


def _causal_conv(x, xs_ref, cw_ref):
    t = x.shape[0]
    xs_ref[CARRY_ROWS:, :] = x
    ext = xs_ref[...]
    acc = ext * cw_ref[0:1, :]
    for j in range(1, CONV_WIDTH):
        acc = pltpu.roll(acc, 1, axis=0) + ext * cw_ref[j:j + 1, :]
    xs_ref[0:CARRY_ROWS, :] = xs_ref[t:t + CARRY_ROWS, :]
    return acc[CARRY_ROWS:, :]


def _cumsum_rows(x):
    n = x.shape[0]
    row = lax.broadcasted_iota(jnp.int32, x.shape, 0)
    d = 1
    while d < n:
        x = x + jnp.where(row >= d, pltpu.roll(x, d, axis=0), 0.0)
        d *= 2
    return x


def _gdn_body(hb, w_ref, b_ref, ws_ref, bs_ref, cw_ref, alog_ref, dtb_ref, ng_ref,
              y_ref, xs_ref, state_ref, u_ref, wf_ref, w_s_ref, qd_ref, kd_ref, qk_ref):
    width = N_HEADS * HEAD_DIM
    c = GDN_CHUNK
    t = hb.shape[0]
    proj = _dot(hb, w_ref[...]) + b_ref[...]
    small = _dot(hb, ws_ref[...]) + bs_ref[...]
    yield
    qkv = _silu(_causal_conv(proj[:, :3 * width], xs_ref, cw_ref))
    yield
    beta_all = _sigmoid(small)
    la_all = -jnp.exp(alog_ref[...]) * _softplus(small + dtb_ref[...])

    ri = lax.broadcasted_iota(jnp.int32, (c, c), 0)
    ci = lax.broadcasted_iota(jnp.int32, (c, c), 1)
    lower = ri >= ci
    strict = ri > ci

    items = [(n, hh) for n in range(t // c) for hh in range(N_HEADS)]
    gcs = {}
    for n in range(t // c):
        la_c = la_all[n * c:(n + 1) * c, :]
        gc_c = _cumsum_rows(la_c)
        gcs[n] = (la_c, gc_c, jnp.exp(gc_c))
    g_last, pws, rems = {}, {}, {}
    for n, hh in items:
        r0, lo = n * c, hh * HEAD_DIM
        rows, cols = slice(r0, r0 + c), slice(lo, lo + HEAD_DIM)
        la_c, gc_c, egc_c = gcs[n]
        q = qkv[rows, lo:lo + HEAD_DIM]
        k = qkv[rows, width + lo:width + lo + HEAD_DIM]
        v = qkv[rows, 2 * width + lo:2 * width + lo + HEAD_DIM]
        q = q * lax.rsqrt(jnp.sum(q * q, axis=-1, keepdims=True) + 1e-6) * (HEAD_DIM ** -0.5)
        k = k * lax.rsqrt(jnp.sum(k * k, axis=-1, keepdims=True) + 1e-6)
        beta = beta_all[rows, hh:hh + 1]
        la = la_c[:, N_HEADS + hh:N_HEADS + hh + 1]
        gc = gc_c[:, N_HEADS + hh:N_HEADS + hh + 1]
        egc = egc_c[:, N_HEADS + hh:N_HEADS + hh + 1]
        gc_row = jnp.sum(jnp.where(ri <= ci, jnp.broadcast_to(la, (c, c)), 0.0), axis=0, keepdims=True)
        gc_last = gc_row[:, c - 1:c]
        decay = jnp.where(lower, jnp.exp(jnp.where(lower, gc - gc_row, 0.0)), 0.0)
        kb = k * beta
        kbf = k.astype(BF16)
        a_neg = jnp.where(strict, -(_dot_nt(kb.astype(BF16), kbf) * decay), 0.0)
        pws[n, hh] = a_neg
        rems[n, hh] = a_neg
        u_ref[rows, cols] = v * beta
        wf_ref[rows, cols] = kb * egc
        qk_ref[hh, rows, :] = (_dot_nt(q.astype(BF16), kbf) * decay).astype(BF16)
        qd_ref[rows, cols] = (q * egc).astype(BF16)
        kd_ref[rows, cols] = (k * jnp.exp(gc_last - gc)).astype(BF16)
        g_last[n, hh] = jnp.exp(gc_last)
        yield
    m = 2
    while m < c:
        for it in items:
            pwb = pws[it].astype(BF16)
            pws[it] = _dot(pwb, pwb)
        yield
        for it in items:
            rems[it] = rems[it] + pws[it] + _dot(rems[it].astype(BF16), pws[it].astype(BF16))
        yield
        m *= 2
    for n, hh in items:
        rows, cols = slice(n * c, (n + 1) * c), slice(hh * HEAD_DIM, (hh + 1) * HEAD_DIM)
        remb = rems[n, hh].astype(BF16)
        u_ref[rows, cols] = u_ref[rows, cols] + _dot(remb, u_ref[rows, cols].astype(BF16))
        w_s_ref[rows, cols] = (wf_ref[rows, cols] + _dot(remb, wf_ref[rows, cols].astype(BF16))).astype(BF16)
    yield

    heads = range(N_HEADS)
    for n in range(t // c):
        rows = slice(n * c, (n + 1) * c)
        cols = [slice(hh * HEAD_DIM, (hh + 1) * HEAD_DIM) for hh in heads]
        states = [state_ref[hh] for hh in heads]
        sbs = [s.astype(BF16) for s in states]
        vnbs = [(u_ref[rows, cols[hh]] - _dot(w_s_ref[rows, cols[hh]], sbs[hh])).astype(BF16) for hh in heads]
        yield
        outs = [_dot(qd_ref[rows, cols[hh]], sbs[hh]) + _dot(qk_ref[hh, rows, :], vnbs[hh]) for hh in heads]
        for hh in heads:
            state_ref[hh] = states[hh] * g_last[n, hh] + _dot_tn(kd_ref[rows, cols[hh]], vnbs[hh])
        yield
        for hh in heads:
            o = outs[hh]
            o = o * lax.rsqrt(jnp.mean(o * o, axis=-1, keepdims=True) + 1e-6) * ng_ref[...]
            og = proj[rows, 3 * width + hh * HEAD_DIM:3 * width + (hh + 1) * HEAD_DIM]
            y_ref[rows, cols[hh]] = (o * _silu(og)).astype(y_ref.dtype)
        yield


def _lru_body(hb, w_ref, b_ref, cw_ref, cb_ref, wr_ref, br_ref, wi_ref, bi_ref, lam_ref,
              y_ref, xs_ref, carry_ref):
    width = cw_ref.shape[1]
    t = hb.shape[0]
    proj = _dot(hb, w_ref[...]) + b_ref[...]
    yield
    xc = _causal_conv(proj[:, :width], xs_ref, cw_ref) + cb_ref[...]
    xcb = xc.astype(BF16)
    yield
    r = _sigmoid(_dot(xcb, wr_ref[...]) + br_ref[...])
    gi = _sigmoid(_dot(xcb, wi_ref[...]) + bi_ref[...])
    yield
    log_a = -LRU_C * r * _softplus(-lam_ref[...])
    a = jnp.exp(log_a)
    th = jnp.tanh(log_a)
    hs = jnp.sqrt(-2.0 * th / (1.0 - th)) * (gi * xc)
    row = lax.broadcasted_iota(jnp.int32, (t, width), 0) % SCAN_GROUP
    d = 1
    while d < SCAN_GROUP:
        keep = row >= d
        hs = hs + a * jnp.where(keep, pltpu.roll(hs, d, axis=0), 0.0)
        a = a * jnp.where(keep, pltpu.roll(a, d, axis=0), 1.0)
        d *= 2
        yield
    gate = jax.nn.gelu(proj[:, width:], approximate=True)
    carry = carry_ref[...]
    for g in range(t // SCAN_GROUP):
        rows = slice(g * SCAN_GROUP, (g + 1) * SCAN_GROUP)
        hg = hs[rows, :] + a[rows, :] * carry
        carry = hg[SCAN_GROUP - 1:SCAN_GROUP, :]
        y_ref[rows, :] = (gate[rows, :] * hg).astype(y_ref.dtype)
        if g % 8 == 7:
            yield
    carry_ref[...] = carry


N_RET_IN, N_GDN_IN, N_LRU_IN = 9, 8, 9
MIX_STRIDE = (1, 2, 2)


def _mixers_kernel(h_ref, *refs, chunk_decay):
    ret_in, refs = refs[:N_RET_IN], refs[N_RET_IN:]
    gdn_in, refs = refs[:N_GDN_IN], refs[N_GDN_IN:]
    lru_in, refs = refs[:N_LRU_IN], refs[N_LRU_IN:]
    yr_ref, yg_ref, yl_ref = refs[:3]
    ret_state, gdn_xs, gdn_state, u_ref, wf_ref, w_s_ref, qd_ref, kd_ref, qk_ref, lru_xs, lru_carry = refs[3:]

    @pl.when(pl.program_id(1) == 0)
    def _():
        ret_state[...] = jnp.zeros_like(ret_state)
        gdn_state[...] = jnp.zeros_like(gdn_state)
        lru_carry[...] = jnp.zeros_like(lru_carry)
        gdn_xs[0:CARRY_ROWS, :] = jnp.zeros((CARRY_ROWS, gdn_xs.shape[1]), F32)
        lru_xs[0:CARRY_ROWS, :] = jnp.zeros((CARRY_ROWS, lru_xs.shape[1]), F32)

    hb = h_ref[0].astype(BF16)
    branches = [
        (_gdn_body(hb, *gdn_in, yg_ref.at[0], gdn_xs, gdn_state, u_ref, wf_ref, w_s_ref, qd_ref, kd_ref, qk_ref),
         MIX_STRIDE[0]),
        (_retention_body(hb, *ret_in, yr_ref.at[0], ret_state, chunk_decay=chunk_decay), MIX_STRIDE[1]),
        (_lru_body(hb, *lru_in, yl_ref.at[0], lru_xs, lru_carry), MIX_STRIDE[2]),
    ]
    tick = 0
    while branches:
        for gen, stride in list(branches):
            if tick % stride == 0 and next(gen, StopIteration) is StopIteration:
                branches.remove((gen, stride))
        tick += 1


def _mixers(h, ret_args, gdn_args, lru_args):
    bsz, seq, d = h.shape
    t = min(SEQ_TILE, seq)
    width = N_HEADS * HEAD_DIM
    row = lambda vec: vec.reshape(1, -1)
    w_ret, b_ret, ret_g, ret_b = ret_args
    w_gdn, b_gdn, w_small, b_small, gdn_cw, a_log, dt_bias, gdn_g = gdn_args
    w_lru, b_lru, lru_cw, lru_cb, w_r, b_r, w_i, b_i, lam = lru_args
    lanes = w_small.shape[1]
    lru_w = lru_cw.shape[1]
    cos, sin, dmat, qd, kd, chunk_decay = _retention_tables(seq, t)
    pad_row = lambda vec: jnp.zeros((1, lanes), F32).at[0, N_HEADS:2 * N_HEADS].set(vec.astype(F32))
    seq_tile = lambda cols: pl.BlockSpec((1, t, cols), lambda i, j: (i, j, 0))
    pos_tile = pl.BlockSpec((t, HEAD_DIM), lambda i, j: (j, 0))
    ret_specs = [_const_spec((d, 4 * width)), _const_spec((1, 4 * width)), pos_tile, pos_tile,
                 _const_spec((N_HEADS, t, t)), _const_spec((N_HEADS, t, HEAD_DIM)), _const_spec((N_HEADS, t, HEAD_DIM)),
                 _const_spec((1, width)), _const_spec((1, width))]
    gdn_specs = [_const_spec((d, 4 * width)), _const_spec((1, 4 * width)), _const_spec((d, lanes)),
                 _const_spec((1, lanes)), _const_spec((CONV_WIDTH, 3 * width)), _const_spec((1, lanes)),
                 _const_spec((1, lanes)), _const_spec((1, HEAD_DIM))]
    lru_specs = [_const_spec((d, 2 * lru_w)), _const_spec((1, 2 * lru_w)), _const_spec((CONV_WIDTH, lru_w)),
                 _const_spec((1, lru_w)), _const_spec((lru_w, lru_w)), _const_spec((1, lru_w)),
                 _const_spec((lru_w, lru_w)), _const_spec((1, lru_w)), _const_spec((1, lru_w))]
    assert (len(ret_specs), len(gdn_specs), len(lru_specs)) == (N_RET_IN, N_GDN_IN, N_LRU_IN)
    out = jax.ShapeDtypeStruct((bsz, seq, width), BF16)
    return pl.pallas_call(
        functools.partial(_mixers_kernel, chunk_decay=chunk_decay),
        grid=(bsz, seq // t),
        in_specs=[seq_tile(d)] + ret_specs + gdn_specs + lru_specs,
        out_specs=[seq_tile(width), seq_tile(width), seq_tile(lru_w)],
        out_shape=[out, out, jax.ShapeDtypeStruct((bsz, seq, lru_w), BF16)],
        scratch_shapes=[pltpu.VMEM((N_HEADS, HEAD_DIM, HEAD_DIM), F32),
                        pltpu.VMEM((CARRY_ROWS + t, 3 * width), F32),
                        pltpu.VMEM((N_HEADS, HEAD_DIM, HEAD_DIM), F32),
                        pltpu.VMEM((t, width), F32),
                        pltpu.VMEM((t, width), F32),
                        pltpu.VMEM((t, width), BF16),
                        pltpu.VMEM((t, width), BF16),
                        pltpu.VMEM((t, width), BF16),
                        pltpu.VMEM((N_HEADS, t, GDN_CHUNK), BF16),
                        pltpu.VMEM((CARRY_ROWS + t, lru_w), F32),
                        pltpu.VMEM((1, lru_w), F32)],
        compiler_params=_params("parallel", "arbitrary"),
        name="token_mixers",
    )(h, w_ret, row(b_ret), cos, sin, dmat, qd, kd, row(ret_g), row(ret_b),
      w_gdn, row(b_gdn), w_small, row(b_small), gdn_cw, pad_row(a_log), pad_row(dt_bias), row(gdn_g),
      w_lru, row(b_lru), lru_cw, row(lru_cb), w_r, row(b_r), w_i, row(b_i), row(lam))


def _first_index_of_max(x, idx, size):
    m = jnp.max(x, axis=0, keepdims=True)
    first = jnp.min(jnp.where(x == m, idx, size), axis=0, keepdims=True)
    return m, idx == first


def _route(logits_t, bias_col, out):
    n_exp, t = logits_t.shape
    per_group = n_exp // N_GROUPS
    scores = _sigmoid(logits_t)
    sel = scores + bias_col
    idx_g = lax.broadcasted_iota(jnp.int32, (per_group, t), 0)
    group_scores = []
    for g in range(N_GROUPS):
        x = sel[g * per_group:(g + 1) * per_group, :]
        m1, hit = _first_index_of_max(x, idx_g, per_group)
        m2 = jnp.max(jnp.where(hit, NEG_INF, x), axis=0, keepdims=True)
        group_scores.append(m1 + m2)
        if g % 2 == 1:
            yield
    gsc = jnp.concatenate(group_scores, axis=0)
    idx_n = lax.broadcasted_iota(jnp.int32, (N_GROUPS, t), 0)
    gmask = jnp.zeros((N_GROUPS, t), F32)
    for _ in range(TOPK_GROUPS):
        _, hit = _first_index_of_max(gsc, idx_n, N_GROUPS)
        gmask = jnp.where(hit, 1.0, gmask)
        gsc = jnp.where(hit, NEG_INF, gsc)
    yield
    emask = jnp.concatenate([jnp.broadcast_to(gmask[g:g + 1, :], (per_group, t)) for g in range(N_GROUPS)], axis=0)
    cand = jnp.where(emask > 0.0, sel, NEG_INF)
    idx_e = lax.broadcasted_iota(jnp.int32, (n_exp, t), 0)
    picked = jnp.zeros((n_exp, t), F32)
    hits = []
    for _ in range(TOP_K):
        _, hit = _first_index_of_max(cand, idx_e, n_exp)
        hits.append(hit)
        picked = jnp.where(hit, 1.0, picked)
        cand = jnp.where(hit, NEG_INF, cand)
        yield
    gw = jnp.where(picked > 0.0, scores, 0.0)
    out.update(combine=gw / jnp.sum(gw, axis=0, keepdims=True) * ROUTED_SCALE, picked=picked, hits=hits, idx_e=idx_e)


def _pack_bf16_pairs(x):
    c = x.shape[1] // 2
    hi = pltpu.bitcast(x[:, :c].astype(BF16).astype(F32), jnp.int32)
    lo = pltpu.bitcast(x[:, c:].astype(BF16).astype(F32), jnp.int32)
    return hi | lax.shift_right_logical(lo, jnp.full(lo.shape, 16, jnp.int32))


def _unpack_bf16_pairs(w):
    hi = pltpu.bitcast(w & jnp.int32(-65536), F32)
    lo = pltpu.bitcast(lax.shift_left(w, jnp.full(w.shape, 16, jnp.int32)), F32)
    return hi, lo


def _merge_kernel(h_ref, yr_ref, yg_ref, yl_ref, wmg_ref, bmg_ref, wbr_ref, wout_ref, g_ref, b_ref,
                  rwt_ref, rb_ref, eye_ref, tri_ref, h1_ref, h1b_ref, xp_ref, eidx_ref, rank_ref, gwt_ref,
                  count_ref, *, alpha):
    @pl.when(pl.program_id(0) == 0)
    def _():
        count_ref[...] = jnp.zeros_like(count_ref)

    d = h_ref.shape[1]
    sub = eye_ref.shape[0]
    parts = [slice(s, s + sub) for s in range(0, h_ref.shape[0], sub)]
    pick = lambda hit, vals, zero: jnp.sum(jnp.where(hit, vals, zero), axis=0, keepdims=True)

    def dense(rows, logits):
        h = h_ref[rows, :]
        hb = h.astype(BF16)
        gates = _sigmoid(_dot(hb, wmg_ref[...]) + bmg_ref[...])
        yield
        mixed = None
        for n, y_ref in enumerate((yr_ref, yg_ref, yl_ref)):
            term = gates[:, n * d:(n + 1) * d] * _dot(y_ref[rows, :], wbr_ref[n])
            mixed = term if mixed is None else mixed + term
            yield
        mix = _dot(mixed.astype(BF16), wout_ref[...])
        yield
        h1 = _layer_norm(alpha * h + mix, g_ref[...], b_ref[...])
        h1b = h1.astype(BF16)
        h1_ref[rows, :] = h1
        h1b_ref[rows, :] = h1b
        xp_ref[rows, :] = _pack_bf16_pairs(h1)
        logits.append(_dot_nt(rwt_ref[...], h1b))
        yield

    def routing(rows, logit):
        res = {}
        yield from _route(logit, rb_ref[...], res)
        combine_t, picked, hits, idx_e = res["combine"], res["picked"], res["hits"], res["idx_e"]
        rank_full = count_ref[...] + _dot(picked.astype(BF16), tri_ref[...])
        count_ref[...] += jnp.sum(picked, axis=1, keepdims=True)
        eidx_ref[:, rows] = jnp.concatenate([pick(hit, idx_e, 0) for hit in hits], axis=0)
        yield
        rank_ref[:, rows] = jnp.concatenate([pick(hit, rank_full, 0.0) for hit in hits], axis=0).astype(jnp.int32)
        gw = jnp.concatenate([pick(hit, combine_t, 0.0) for hit in hits], axis=0)
        gwt_ref[rows, :] = lax.dot_general(eye_ref[...], gw, (((1,), (1,)), ((), ())),
                                           preferred_element_type=F32, precision=lax.Precision.HIGHEST)
        yield

    def interleave(gens):
        while gens:
            gens = [g for g in gens if next(g, StopIteration) is not StopIteration]

    logits = [[] for _ in parts]
    interleave([dense(rows, logits[s]) for s, rows in enumerate(parts)])
    interleave([routing(rows, logits[s][0]) for s, rows in enumerate(parts)])


def _merge(h2, y_ret, y_gdn, y_lru, w_mg, b_mg, w_branch, w_out, ln_g, ln_b, router_wt, router_b, alpha):
    n, d = h2.shape
    t = min(MERGE_TILE, n)
    width = y_ret.shape[1]
    n_exp = router_wt.shape[0]
    tile = lambda cols: pl.BlockSpec((t, cols), lambda i: (i, 0))
    lane_tile = pl.BlockSpec((TOP_K, t), lambda i: (0, i))
    sub = min(MERGE_SUB, t)
    tri = jnp.triu(jnp.ones((sub, sub), BF16), 1)
    return pl.pallas_call(
        functools.partial(_merge_kernel, alpha=alpha),
        grid=(n // t,),
        in_specs=[
            tile(d), tile(width), tile(width), tile(width),
            _const_spec((d, N_BRANCHES * d)),
            _const_spec((1, N_BRANCHES * d)),
            _const_spec((N_BRANCHES, width, d)),
            _const_spec((d, d)),
            _const_spec((1, d)),
            _const_spec((1, d)),
            _const_spec((n_exp, d)),
            _const_spec((n_exp, 1)),
            _const_spec((sub, sub)),
            _const_spec((sub, sub)),
        ],
        out_specs=[tile(d), tile(d), tile(d // 2), lane_tile, lane_tile, tile(TOP_K),
                   pl.BlockSpec((n_exp, 1), lambda i: (0, 0))],
        out_shape=[jax.ShapeDtypeStruct((n, d), F32), jax.ShapeDtypeStruct((n, d), BF16),
                   jax.ShapeDtypeStruct((n, d // 2), jnp.int32),
                   jax.ShapeDtypeStruct((TOP_K, n), jnp.int32), jax.ShapeDtypeStruct((TOP_K, n), jnp.int32),
                   jax.ShapeDtypeStruct((n, TOP_K), F32), jax.ShapeDtypeStruct((n_exp, 1), F32)],
        compiler_params=_params("arbitrary"),
        name="merge_route",
    )(h2, y_ret, y_gdn, y_lru, w_mg, b_mg.reshape(1, -1), w_branch, w_out, ln_g.reshape(1, -1),
      ln_b.reshape(1, -1), router_wt, router_b.reshape(-1, 1), jnp.eye(sub, dtype=F32), tri)


def _dest_kernel(start_ref, eidx_ref, rank_ref, o_ref, *, n_exp):
    eidx = eidx_ref[...]
    dest = rank_ref[...]
    for e in range(n_exp):
        dest = dest + jnp.where(eidx == e, start_ref[e], 0)
    o_ref[...] = dest


def _dest_rows(seg_start, eidx, rank):
    k, n = eidx.shape
    t = min(DEST_TILE, n)
    blk = pl.BlockSpec((k, t), lambda i, s: (0, i))
    return pl.pallas_call(
        functools.partial(_dest_kernel, n_exp=seg_start.shape[0]),
        grid_spec=pltpu.PrefetchScalarGridSpec(num_scalar_prefetch=1, grid=(n // t,), in_specs=[blk, blk],
                                               out_specs=blk),
        out_shape=jax.ShapeDtypeStruct((k, n), jnp.int32),
        compiler_params=_params("parallel"),
        name="dest_rows",
    )(seg_start, eidx, rank)


def _sc_workers():
    info = plsc.get_sparse_core_info()
    return info.num_cores, info.num_subcores


def _sc_mesh():
    return plsc.VectorSubcoreMesh(core_axis_name="c", subcore_axis_name="s")


def _sc_scatter_rows(x, dest, n_rows):
    n, c = x.shape
    k = dest.shape[0]
    n_cores, n_sub = _sc_workers()
    per_worker = n // (n_cores * n_sub)
    steps = per_worker // SC_WINDOW

    @functools.partial(pl.kernel, out_type=jax.ShapeDtypeStruct((n_rows, c), x.dtype), mesh=_sc_mesh(),
                       scratch_types=[pltpu.VMEM((k, SC_WINDOW), jnp.int32), pltpu.VMEM((SC_WINDOW, c), x.dtype),
                                      pltpu.SemaphoreType.DMA, pltpu.SemaphoreType.DMA],
                       name="dispatch_rows")
    def scatter(x_hbm, d_hbm, o_hbm, idx_v, rows_v, sem, row_sem):
        base = (lax.axis_index("s") * n_cores + lax.axis_index("c")) * per_worker

        @pl.loop(0, steps)
        def _(j):
            off = base + j * SC_WINDOW
            loads = [pltpu.async_copy(x_hbm.at[pl.ds(off, SC_WINDOW)], rows_v, row_sem)]
            loads += [pltpu.async_copy(d_hbm.at[pl.ds(kk * n + off, SC_WINDOW)], idx_v.at[kk], sem) for kk in range(k)]
            for cp in loads:
                cp.wait()
            copies = [pltpu.async_copy(rows_v, o_hbm.at[idx_v.at[kk]], sem) for kk in range(k)]
            for cp in copies:
                cp.wait()

    return scatter(x, dest.reshape(k * n))


def _sc_gather_rows(table, idx):
    b = idx.shape[0]
    c = table.shape[1]
    n_cores, n_sub = _sc_workers()
    per_worker = b // (n_cores * n_sub)
    steps = per_worker // SC_WINDOW

    half = SC_WINDOW // 2
    half_buf = lambda dtype, *shape: pltpu.VMEM((half,) + shape, dtype)

    @functools.partial(pl.kernel, out_type=jax.ShapeDtypeStruct((b, c), table.dtype), mesh=_sc_mesh(),
                       scratch_types=[half_buf(jnp.int32), half_buf(jnp.int32), half_buf(table.dtype, c),
                                      half_buf(table.dtype, c), pltpu.SemaphoreType.DMA, pltpu.SemaphoreType.DMA,
                                      pltpu.SemaphoreType.DMA, pltpu.SemaphoreType.DMA],
                       name="collect_rows")
    def gather(t_hbm, i_hbm, o_hbm, idx_a, idx_b, rows_a, rows_b, gat_a, gat_b, put_a, put_b):
        base = (lax.axis_index("s") * n_cores + lax.axis_index("c")) * per_worker

        @pl.loop(0, steps)
        def _(j):
            off_a = base + j * SC_WINDOW
            off_b = off_a + half
            pltpu.sync_copy(i_hbm.at[pl.ds(off_a, half)], idx_a)
            in_a = pltpu.async_copy(t_hbm.at[idx_a], rows_a, gat_a)
            pltpu.sync_copy(i_hbm.at[pl.ds(off_b, half)], idx_b)
            in_b = pltpu.async_copy(t_hbm.at[idx_b], rows_b, gat_b)
            in_a.wait()
            out_a = pltpu.async_copy(rows_a, o_hbm.at[pl.ds(off_a, half)], put_a)
            in_b.wait()
            out_b = pltpu.async_copy(rows_b, o_hbm.at[pl.ds(off_b, half)], put_b)
            out_a.wait()
            out_b.wait()

    return gather(table, idx)


def _expert_block_kernel(meta_ref, xs_ref, wgu_ref, wd_ref, ys_ref, wgu_b, wd_b, *, n_blocks):
    i = pl.program_id(0)

    @pl.when((i == 0) | (meta_ref[i] != meta_ref[jnp.maximum(i - 1, 0)]))
    def _():
        wgu_b[...] = wgu_ref[0, 0].astype(BF16)
        wd_b[...] = wd_ref[0, 0].astype(BF16)

    @pl.when(i < meta_ref[n_blocks])
    def _():
        half = xs_ref.shape[1]
        ff = wd_b.shape[0]
        sub = xs_ref.shape[0] // 2
        parts = [slice(0, sub), slice(sub, 2 * sub)]
        xs = [_unpack_bf16_pairs(xs_ref[rows, :]) for rows in parts]
        gus = [_dot(hi.astype(BF16), wgu_b[:half, :]) + _dot(lo.astype(BF16), wgu_b[half:, :]) for hi, lo in xs]
        mids = [(_silu(gu[:, :ff]) * gu[:, ff:]).astype(BF16) for gu in gus]
        outs = [_dot(mid, wd_b[...]) for mid in mids]
        for rows, out in zip(parts, outs):
            ys_ref[rows, :] = _pack_bf16_pairs(out)


def _expert_blocks(meta, xs, w_gu, w_down, layer):
    rows, half = xs.shape
    _, _, d, ff2 = w_gu.shape
    n_blocks = rows // MOE_BLOCK
    row_blk = pl.BlockSpec((MOE_BLOCK, half), lambda i, meta: (jnp.minimum(i, meta[n_blocks] - 1), 0))
    return pl.pallas_call(
        functools.partial(_expert_block_kernel, n_blocks=n_blocks),
        grid_spec=pltpu.PrefetchScalarGridSpec(
            num_scalar_prefetch=1, grid=(n_blocks,),
            in_specs=[row_blk,
                      pl.BlockSpec((1, 1, d, ff2), lambda i, meta: (layer, meta[i], 0, 0)),
                      pl.BlockSpec((1, 1, ff2 // 2, d), lambda i, meta: (layer, meta[i], 0, 0))],
            out_specs=row_blk,
            scratch_shapes=[pltpu.VMEM((d, ff2), BF16), pltpu.VMEM((ff2 // 2, d), BF16)]),
        out_shape=jax.ShapeDtypeStruct((rows, half), jnp.int32),
        compiler_params=_params("arbitrary"),
        name="expert_blocks",
    )(meta, xs, w_gu, w_down)


def _routed_rows(xp, eidx, rank, counts, w_gu, w_down, layer):
    n = xp.shape[0]
    k = eidx.shape[0]
    n_exp = w_gu.shape[1]
    counts = counts.reshape(n_exp).astype(jnp.int32)
    padded = (counts + MOE_BLOCK - 1) // MOE_BLOCK * MOE_BLOCK
    seg_end = jnp.cumsum(padded)
    n_blocks = k * n // MOE_BLOCK + n_exp
    blk_start = jnp.arange(n_blocks, dtype=jnp.int32) * MOE_BLOCK
    blk_expert = jnp.minimum(jnp.sum(seg_end[None, :] <= blk_start[:, None], axis=1), n_exp - 1)
    meta = jnp.concatenate([blk_expert, seg_end[-1:] // MOE_BLOCK]).astype(jnp.int32)
    dest = _dest_rows(seg_end - padded, eidx, rank)
    xs = _sc_scatter_rows(xp, dest, n_blocks * MOE_BLOCK)
    return _expert_blocks(meta, xs, w_gu, w_down, layer), dest


def _shared_ple_kernel(h1_ref, h1b_ref, p_ref, wgu_ref, wd_ref, wpg_ref, bpg_ref, wpe_ref, o_ref, *, alpha):
    xb = h1b_ref[...]
    ff = wd_ref.shape[0]
    gu = _dot(xb, wgu_ref[...])
    shared = _dot((_silu(gu[:, :ff]) * gu[:, ff:]).astype(BF16), wd_ref[...])
    ple = _sigmoid(_dot(xb, wpg_ref[...]) + bpg_ref[...]) * _dot(p_ref[0].astype(BF16), wpe_ref[...])
    o_ref[...] = alpha * h1_ref[...] + shared + ple


def _shared_ple(h1, h1b, p3, layer, part, sh_w_gu, sh_w_down, ple_w_g, ple_b_g, ple_w_e, alpha):
    n, d = h1.shape
    t = min(ROW_TILE, n)
    steps = n // t
    pdim = p3.shape[2]
    ff2 = sh_w_gu.shape[1]
    tile = lambda cols: pl.BlockSpec((t, cols), lambda i: (i, 0))
    return pl.pallas_call(
        functools.partial(_shared_ple_kernel, alpha=alpha),
        grid=(n // t,),
        in_specs=[
            tile(d), tile(d), pl.BlockSpec((1, t, pdim), lambda i: (layer, part * steps + i, 0)),
            _const_spec((d, ff2)),
            _const_spec((ff2 // 2, d)),
            _const_spec((d, d)),
            _const_spec((1, d)),
            _const_spec((pdim, d)),
        ],
        out_specs=tile(d),
        out_shape=jax.ShapeDtypeStruct((n, d), F32),
        compiler_params=_params("parallel"),
        name="shared_ple",
    )(h1, h1b, p3, sh_w_gu, sh_w_down, ple_w_g, ple_b_g.reshape(1, -1), ple_w_e)


def _combine_norm_kernel(pre_ref, yk_ref, gwt_ref, g_ref, b_ref, *rest):
    o_ref = rest[-1]
    gwt = gwt_ref[...]
    routed_hi = routed_lo = None
    for k in range(yk_ref.shape[0]):
        hi, lo = _unpack_bf16_pairs(yk_ref[k])
        wk = gwt[:, k:k + 1]
        routed_hi = hi * wk if routed_hi is None else routed_hi + hi * wk
        routed_lo = lo * wk if routed_lo is None else routed_lo + lo * wk
    routed = jnp.concatenate([routed_hi, routed_lo], axis=1)
    o_ref[...] = _layer_norm(pre_ref[...] + routed, g_ref[...], b_ref[...])


def _combine_norm(pre, ys, dest, gwt, ln_g, ln_b):
    n, d = pre.shape
    top_k = dest.shape[0]
    t = min(ROW_TILE, n)
    chunks = COMBINE_CHUNKS if n % (COMBINE_CHUNKS * t) == 0 else 1
    n_c = n // chunks
    steps = n_c // t
    out = None
    for c in range(chunks):
        rows = _sc_gather_rows(ys, dest[:, c * n_c:(c + 1) * n_c].reshape(top_k * n_c)).reshape(top_k, n_c, d // 2)
        tile = lambda cols: pl.BlockSpec((t, cols), lambda i, c=c: (c * steps + i, 0))
        in_specs = [tile(d), pl.BlockSpec((top_k, t, d // 2), lambda i: (0, i, 0)), tile(top_k),
                    _const_spec((1, d)), _const_spec((1, d))]
        args = [pre, rows, gwt, ln_g.reshape(1, -1), ln_b.reshape(1, -1)]
        if out is not None:
            in_specs.append(pl.BlockSpec(memory_space=pl.ANY))
            args.append(out)
        out = pl.pallas_call(
            _combine_norm_kernel,
            grid=(steps,),
            in_specs=in_specs,
            out_specs=tile(d),
            out_shape=jax.ShapeDtypeStruct((n, d), F32),
            input_output_aliases={} if out is None else {len(args) - 1: 0},
            compiler_params=_params("parallel"),
            name="combine_norm",
        )(*args)
    return out


def _block_diag(w):
    g, i, j = w.shape
    eye = jnp.eye(g, dtype=w.dtype)
    return (eye[:, None, :, None] * w[:, :, None, :]).reshape(g * i, g * j)


def kernel(x, p, ln_in_g, ln_in_b, w_in, b_in, ret_norm_g, ret_norm_b, gdn_conv_w, gdn_a_log, gdn_dt_bias, gdn_norm_g, lru_conv_w, lru_conv_b, lru_w_r, lru_b_r, lru_w_i, lru_b_i, lru_lambda, w_branch, w_out, ln1_g, ln1_b, router_w, router_b, exp_w_gu, exp_w_down, sh_w_gu, sh_w_down, ple_w_e, ple_w_g, ple_b_g, ln2_g, ln2_b):
    bsz, seq, d = x.shape
    depth = w_in.shape[0]
    n = bsz * seq
    width = N_HEADS * HEAD_DIM
    alpha = (2 * depth) ** 0.25
    lanes = 128
    o_ret = 0
    o_gdn = o_ret + 4 * width
    o_small = o_gdn + 4 * width
    o_lru = o_small + 2 * N_HEADS
    o_mg = o_lru + 2 * width

    parts = range(N_PARTS if bsz % N_PARTS == 0 else 1)
    bsz_p = bsz // len(parts)
    n_p = bsz_p * seq
    p3 = p.reshape(depth, n, -1)
    bf = lambda a: a.astype(BF16)
    hs = [_entry_norm(x.reshape(n, d), ln_in_g, ln_in_b, part, n_p) for part in parts]
    for l in range(depth):
        wl, bl = w_in[l], b_in[l]
        w_small = bf(jnp.zeros((d, lanes), F32).at[:, :2 * N_HEADS].set(wl[:, o_small:o_lru]))
        b_small = jnp.zeros((lanes,), F32).at[:2 * N_HEADS].set(bl[o_small:o_lru])
        w_ret, w_gdn, w_lru, w_mg = bf(wl[:, o_ret:o_gdn]), bf(wl[:, o_gdn:o_small]), bf(wl[:, o_lru:o_mg]), bf(wl[:, o_mg:])
        w_r, w_i = bf(_block_diag(lru_w_r[l])), bf(_block_diag(lru_w_i[l]))
        w_br, w_o, r_wt = bf(w_branch[l]), bf(w_out[l]), bf(router_w[l].T)
        w_sgu, w_sd, w_pg, w_pe = bf(sh_w_gu[l]), bf(sh_w_down[l]), bf(ple_w_g[l]), bf(ple_w_e[l])
        for part in parts:
            h = hs[part]
            h3 = h.reshape(bsz_p, seq, d)
            y_ret, y_gdn, y_lru = _mixers(
                h3, (w_ret, bl[o_ret:o_gdn], ret_norm_g[l], ret_norm_b[l]),
                (w_gdn, bl[o_gdn:o_small], w_small, b_small, gdn_conv_w[l], gdn_a_log[l], gdn_dt_bias[l], gdn_norm_g[l]),
                (w_lru, bl[o_lru:o_mg], lru_conv_w[l], lru_conv_b[l], w_r, lru_b_r[l], w_i, lru_b_i[l], lru_lambda[l]))
            h1, h1b, xp, eidx, rank, gwt, counts = _merge(
                h, y_ret.reshape(n_p, width), y_gdn.reshape(n_p, width), y_lru.reshape(n_p, width), w_mg, bl[o_mg:],
                w_br, w_o, ln1_g[l], ln1_b[l], r_wt, router_b[l], alpha)
            pre = _shared_ple(h1, h1b, p3, l, part, w_sgu, w_sd, w_pg, ple_b_g[l], w_pe, alpha)
            ys, dest = _routed_rows(xp, eidx, rank, counts, exp_w_gu, exp_w_down, l)
            hs[part] = _combine_norm(pre, ys, dest, gwt, ln2_g[l], ln2_b[l])
    return jnp.concatenate([h.reshape(bsz_p, seq, d) for h in hs], axis=0)
```

```python
import functools

import numpy as np
import jax
import jax.numpy as jnp
from jax import lax
from jax.experimental import pallas as pl
from jax.experimental.pallas import tpu as pltpu
from jax.experimental.pallas import tpu_sc as plsc

F32 = jnp.float32
BF16 = jnp.bfloat16

HEAD_DIM = 128
N_HEADS = 4
LRU_C = 8.0
CONV_WIDTH = 4
N_BRANCHES = 3
ROPE_BASE = 10000.0
N_GROUPS = 8
TOPK_GROUPS = 4
TOP_K = 8
ROUTED_SCALE = 2.5
LN_EPS = 1e-5
GDN_CHUNK = 64
SEQ_TILE = 256
MERGE_TILE = 512
MERGE_SUB = 128
ROW_TILE = 512
MOE_BLOCK = 1024
EXPERT_SUB = 256
DEST_TILE = 4096
COMBINE_CHUNKS = 4
N_PARTS = 1
SC_WINDOW = 128
CARRY_ROWS = 8
SCAN_GROUP = 16
VMEM_LIMIT_BYTES = 56 * 1024 * 1024
NEG_INF = float("-inf")


def _const_spec(shape):
    nd = len(shape)
    return pl.BlockSpec(shape, lambda *_: (0,) * nd, pipeline_mode=pl.Buffered(1))


def _params(*sem):
    return pltpu.CompilerParams(dimension_semantics=sem, vmem_limit_bytes=VMEM_LIMIT_BYTES)


def _layer_norm(x, g, b):
    mu = jnp.mean(x, axis=-1, keepdims=True)
    xc = x - mu
    var = jnp.mean(xc * xc, axis=-1, keepdims=True)
    return xc * lax.rsqrt(var + LN_EPS) * g + b


def _sigmoid(x):
    return 1.0 / (1.0 + jnp.exp(-x))


def _silu(x):
    return x * _sigmoid(x)


def _softplus(x):
    return jnp.maximum(x, 0.0) + jnp.log1p(jnp.exp(-jnp.abs(x)))


def _dot(a, b):
    return jnp.dot(a, b, preferred_element_type=F32)


def _dot_nt(a, b):
    return lax.dot_general(a, b, (((1,), (1,)), ((), ())), preferred_element_type=F32)


def _dot_tn(a, b):
    return lax.dot_general(a, b, (((0,), (0,)), ((), ())), preferred_element_type=F32)


def _ln_kernel(x_ref, g_ref, b_ref, o_ref):
    o_ref[...] = _layer_norm(x_ref[...], g_ref[...], b_ref[...])


def _entry_norm(x2, g, b, part, n_part):
    d = x2.shape[1]
    t = min(1024, n_part)
    steps = n_part // t
    return pl.pallas_call(
        _ln_kernel,
        grid=(steps,),
        in_specs=[pl.BlockSpec((t, d), lambda i: (part * steps + i, 0)), _const_spec((1, d)), _const_spec((1, d))],
        out_specs=pl.BlockSpec((t, d), lambda i: (i, 0)),
        out_shape=jax.ShapeDtypeStruct((n_part, d), F32),
        compiler_params=_params("parallel"),
        name="entry_norm",
    )(x2, g.reshape(1, d), b.reshape(1, d))


def _retention_body(hb, w_ref, b_ref, cos_ref, sin_ref, dmat_ref, qd_ref, kd_ref, ng_ref, nb_ref,
                    y_ref, state_ref, *, chunk_decay):
    width = N_HEADS * HEAD_DIM
    proj = _dot(hb, w_ref[...]) + b_ref[...]
    yield
    cos = cos_ref[...]
    sin = sin_ref[...]
    heads = range(N_HEADS)
    qs, ks, vbs = [], [], []
    for hh in heads:
        lo = hh * HEAD_DIM
        q = proj[:, lo:lo + HEAD_DIM]
        k = proj[:, width + lo:width + lo + HEAD_DIM]
        qs.append(q * cos + pltpu.roll(q, HEAD_DIM // 2, axis=1) * sin)
        ks.append((k * cos + pltpu.roll(k, HEAD_DIM // 2, axis=1) * sin) * (HEAD_DIM ** -0.5))
        vbs.append(proj[:, 2 * width + lo:2 * width + lo + HEAD_DIM].astype(BF16))
        yield
    states = [state_ref[hh] for hh in heads]
    scores, inter, outs = [], [], []
    for hh in heads:
        scores.append((_dot_nt(qs[hh].astype(BF16), ks[hh].astype(BF16)) * dmat_ref[hh]).astype(BF16))
        yield
    for hh in heads:
        inter.append(_dot((qs[hh] * qd_ref[hh]).astype(BF16), states[hh].astype(BF16)))
        yield
    for hh in heads:
        state_ref[hh] = states[hh] * chunk_decay[hh] + _dot_tn((ks[hh] * kd_ref[hh]).astype(BF16), vbs[hh])
        yield
    for hh in heads:
        outs.append(_dot(scores[hh], vbs[hh]) + inter[hh])
        yield
    for hh in heads:
        lo = hh * HEAD_DIM
        o = outs[hh]
        gate = proj[:, 3 * width + lo:3 * width + lo + HEAD_DIM]
        mu = jnp.mean(o, axis=-1, keepdims=True)
        oc = o - mu
        var = jnp.mean(oc * oc, axis=-1, keepdims=True)
        on = oc * lax.rsqrt(var + LN_EPS) * ng_ref[:, lo:lo + HEAD_DIM] + nb_ref[:, lo:lo + HEAD_DIM]
        y_ref[:, lo:lo + HEAD_DIM] = (_silu(gate) * on).astype(y_ref.dtype)
        yield


def _retention_tables(seq, tile):
    half = HEAD_DIM // 2
    inv_freq = ROPE_BASE ** (-np.linspace(0.0, 1.0, half))
    ang = np.arange(seq)[:, None] * inv_freq[None, :]
    cos = np.concatenate([np.cos(ang), np.cos(ang)], axis=1)
    sin = np.concatenate([-np.sin(ang), np.sin(ang)], axis=1)
    log_gamma = np.log1p(-np.exp2(-5.0 - np.arange(N_HEADS)))
    pos = np.arange(tile)
    diff = pos[:, None] - pos[None, :]
    dmat = np.where(diff >= 0, np.exp(log_gamma[:, None, None] * np.maximum(diff, 0)), 0.0)
    qd = np.exp(log_gamma[:, None] * (pos + 1.0))[:, :, None] * np.ones((1, 1, HEAD_DIM))
    kd = np.exp(log_gamma[:, None] * (tile - 1.0 - pos))[:, :, None] * np.ones((1, 1, HEAD_DIM))
    chunk_decay = tuple(float(c) for c in np.exp(log_gamma * tile))
    as32 = lambda a: jnp.asarray(a, F32)
    return as32(cos), as32(sin), as32(dmat), as32(qd), as32(kd), chunk_decay


def _causal_conv(x, xs_ref, cw_ref):
    t = x.shape[0]
    xs_ref[CARRY_ROWS:, :] = x
    ext = xs_ref[...]
    acc = ext * cw_ref[0:1, :]
    for j in range(1, CONV_WIDTH):
        acc = pltpu.roll(acc, 1, axis=0) + ext * cw_ref[j:j + 1, :]
    xs_ref[0:CARRY_ROWS, :] = xs_ref[t:t + CARRY_ROWS, :]
    return acc[CARRY_ROWS:, :]


def _cumsum_rows(x):
    n = x.shape[0]
    row = lax.broadcasted_iota(jnp.int32, x.shape, 0)
    d = 1
    while d < n:
        x = x + jnp.where(row >= d, pltpu.roll(x, d, axis=0), 0.0)
        d *= 2
    return x


def _gdn_body(hb, w_ref, b_ref, ws_ref, bs_ref, cw_ref, alog_ref, dtb_ref, ng_ref,
              y_ref, xs_ref, state_ref, u_ref, wf_ref, w_s_ref, qd_ref, kd_ref, qk_ref):
    width = N_HEADS * HEAD_DIM
    c = GDN_CHUNK
    t = hb.shape[0]
    proj = _dot(hb, w_ref[...]) + b_ref[...]
    small = _dot(hb, ws_ref[...]) + bs_ref[...]
    yield
    qkv = _silu(_causal_conv(proj[:, :3 * width], xs_ref, cw_ref))
    yield
    beta_all = _sigmoid(small)
    la_all = -jnp.exp(alog_ref[...]) * _softplus(small + dtb_ref[...])

    ri = lax.broadcasted_iota(jnp.int32, (c, c), 0)
    ci = lax.broadcasted_iota(jnp.int32, (c, c), 1)
    lower = ri >= ci
    strict = ri > ci

    items = [(n, hh) for n in range(t // c) for hh in range(N_HEADS)]
    gcs = {}
    for n in range(t // c):
        la_c = la_all[n * c:(n + 1) * c, :]
        gc_c = _cumsum_rows(la_c)
        gcs[n] = (la_c, gc_c, jnp.exp(gc_c))
    g_last, pws, rems = {}, {}, {}
    for n, hh in items:
        r0, lo = n * c, hh * HEAD_DIM
        rows, cols = slice(r0, r0 + c), slice(lo, lo + HEAD_DIM)
        la_c, gc_c, egc_c = gcs[n]
        q = qkv[rows, lo:lo + HEAD_DIM]
        k = qkv[rows, width + lo:width + lo + HEAD_DIM]
        v = qkv[rows, 2 * width + lo:2 * width + lo + HEAD_DIM]
        q = q * lax.rsqrt(jnp.sum(q * q, axis=-1, keepdims=True) + 1e-6) * (HEAD_DIM ** -0.5)
        k = k * lax.rsqrt(jnp.sum(k * k, axis=-1, keepdims=True) + 1e-6)
        beta = beta_all[rows, hh:hh + 1]
        la = la_c[:, N_HEADS + hh:N_HEADS + hh + 1]
        gc = gc_c[:, N_HEADS + hh:N_HEADS + hh + 1]
        egc = egc_c[:, N_HEADS + hh:N_HEADS + hh + 1]
        gc_row = jnp.sum(jnp.where(ri <= ci, jnp.broadcast_to(la, (c, c)), 0.0), axis=0, keepdims=True)
        gc_last = gc_row[:, c - 1:c]
        decay = jnp.where(lower, jnp.exp(jnp.where(lower, gc - gc_row, 0.0)), 0.0)
        kb = k * beta
        kbf = k.astype(BF16)
        a_neg = jnp.where(strict, -(_dot_nt(kb.astype(BF16), kbf) * decay), 0.0)
        pws[n, hh] = a_neg
        rems[n, hh] = a_neg
        u_ref[rows, cols] = v * beta
        wf_ref[rows, cols] = kb * egc
        qk_ref[hh, rows, :] = (_dot_nt(q.astype(BF16), kbf) * decay).astype(BF16)
        qd_ref[rows, cols] = (q * egc).astype(BF16)
        kd_ref[rows, cols] = (k * jnp.exp(gc_last - gc)).astype(BF16)
        g_last[n, hh] = jnp.exp(gc_last)
        yield
    m = 2
    while m < c:
        for it in items:
            pwb = pws[it].astype(BF16)
            pws[it] = _dot(pwb, pwb)
        yield
        for it in items:
            rems[it] = rems[it] + pws[it] + _dot(rems[it].astype(BF16), pws[it].astype(BF16))
        yield
        m *= 2
    for n, hh in items:
        rows, cols = slice(n * c, (n + 1) * c), slice(hh * HEAD_DIM, (hh + 1) * HEAD_DIM)
        remb = rems[n, hh].astype(BF16)
        u_ref[rows, cols] = u_ref[rows, cols] + _dot(remb, u_ref[rows, cols].astype(BF16))
        w_s_ref[rows, cols] = (wf_ref[rows, cols] + _dot(remb, wf_ref[rows, cols].astype(BF16))).astype(BF16)
    yield

    heads = range(N_HEADS)
    for n in range(t // c):
        rows = slice(n * c, (n + 1) * c)
        cols = [slice(hh * HEAD_DIM, (hh + 1) * HEAD_DIM) for hh in heads]
        states = [state_ref[hh] for hh in heads]
        sbs = [s.astype(BF16) for s in states]
        vnbs = [(u_ref[rows, cols[hh]] - _dot(w_s_ref[rows, cols[hh]], sbs[hh])).astype(BF16) for hh in heads]
        yield
        outs = [_dot(qd_ref[rows, cols[hh]], sbs[hh]) + _dot(qk_ref[hh, rows, :], vnbs[hh]) for hh in heads]
        for hh in heads:
            state_ref[hh] = states[hh] * g_last[n, hh] + _dot_tn(kd_ref[rows, cols[hh]], vnbs[hh])
        yield
        for hh in heads:
            o = outs[hh]
            o = o * lax.rsqrt(jnp.mean(o * o, axis=-1, keepdims=True) + 1e-6) * ng_ref[...]
            og = proj[rows, 3 * width + hh * HEAD_DIM:3 * width + (hh + 1) * HEAD_DIM]
            y_ref[rows, cols[hh]] = (o * _silu(og)).astype(y_ref.dtype)
        yield


def _lru_body(hb, w_ref, b_ref, cw_ref, cb_ref, wr_ref, br_ref, wi_ref, bi_ref, lam_ref,
              y_ref, xs_ref, carry_ref):
    width = cw_ref.shape[1]
    t = hb.shape[0]
    proj = _dot(hb, w_ref[...]) + b_ref[...]
    yield
    xc = _causal_conv(proj[:, :width], xs_ref, cw_ref) + cb_ref[...]
    xcb = xc.astype(BF16)
    yield
    r = _sigmoid(_dot(xcb, wr_ref[...]) + br_ref[...])
    gi = _sigmoid(_dot(xcb, wi_ref[...]) + bi_ref[...])
    yield
    log_a = -LRU_C * r * _softplus(-lam_ref[...])
    a = jnp.exp(log_a)
    th = jnp.tanh(log_a)
    hs = jnp.sqrt(-2.0 * th / (1.0 - th)) * (gi * xc)
    row = lax.broadcasted_iota(jnp.int32, (t, width), 0) % SCAN_GROUP
    d = 1
    while d < SCAN_GROUP:
        keep = row >= d
        hs = hs + a * jnp.where(keep, pltpu.roll(hs, d, axis=0), 0.0)
        a = a * jnp.where(keep, pltpu.roll(a, d, axis=0), 1.0)
        d *= 2
        yield
    gate = jax.nn.gelu(proj[:, width:], approximate=True)
    carry = carry_ref[...]
    for g in range(t // SCAN_GROUP):
        rows = slice(g * SCAN_GROUP, (g + 1) * SCAN_GROUP)
        hg = hs[rows, :] + a[rows, :] * carry
        carry = hg[SCAN_GROUP - 1:SCAN_GROUP, :]
        y_ref[rows, :] = (gate[rows, :] * hg).astype(y_ref.dtype)
        if g % 8 == 7:
            yield
    carry_ref[...] = carry


N_RET_IN, N_GDN_IN, N_LRU_IN = 9, 8, 9
MIX_STRIDE = (1, 2, 2)


def _mixers_kernel(h_ref, *refs, chunk_decay):
    ret_in, refs = refs[:N_RET_IN], refs[N_RET_IN:]
    gdn_in, refs = refs[:N_GDN_IN], refs[N_GDN_IN:]
    lru_in, refs = refs[:N_LRU_IN], refs[N_LRU_IN:]
    yr_ref, yg_ref, yl_ref = refs[:3]
    ret_state, gdn_xs, gdn_state, u_ref, wf_ref, w_s_ref, qd_ref, kd_ref, qk_ref, lru_xs, lru_carry = refs[3:]

    @pl.when(pl.program_id(1) == 0)
    def _():
        ret_state[...] = jnp.zeros_like(ret_state)
        gdn_state[...] = jnp.zeros_like(gdn_state)
        lru_carry[...] = jnp.zeros_like(lru_carry)
        gdn_xs[0:CARRY_ROWS, :] = jnp.zeros((CARRY_ROWS, gdn_xs.shape[1]), F32)
        lru_xs[0:CARRY_ROWS, :] = jnp.zeros((CARRY_ROWS, lru_xs.shape[1]), F32)

    hb = h_ref[0].astype(BF16)
    branches = [
        (_gdn_body(hb, *gdn_in, yg_ref.at[0], gdn_xs, gdn_state, u_ref, wf_ref, w_s_ref, qd_ref, kd_ref, qk_ref),
         MIX_STRIDE[0]),
        (_retention_body(hb, *ret_in, yr_ref.at[0], ret_state, chunk_decay=chunk_decay), MIX_STRIDE[1]),
        (_lru_body(hb, *lru_in, yl_ref.at[0], lru_xs, lru_carry), MIX_STRIDE[2]),
    ]
    tick = 0
    while branches:
        for gen, stride in list(branches):
            if tick % stride == 0 and next(gen, StopIteration) is StopIteration:
                branches.remove((gen, stride))
        tick += 1


def _mixers(h, ret_args, gdn_args, lru_args):
    bsz, seq, d = h.shape
    t = min(SEQ_TILE, seq)
    width = N_HEADS * HEAD_DIM
    row = lambda vec: vec.reshape(1, -1)
    w_ret, b_ret, ret_g, ret_b = ret_args
    w_gdn, b_gdn, w_small, b_small, gdn_cw, a_log, dt_bias, gdn_g = gdn_args
    w_lru, b_lru, lru_cw, lru_cb, w_r, b_r, w_i, b_i, lam = lru_args
    lanes = w_small.shape[1]
    lru_w = lru_cw.shape[1]
    cos, sin, dmat, qd, kd, chunk_decay = _retention_tables(seq, t)
    pad_row = lambda vec: jnp.zeros((1, lanes), F32).at[0, N_HEADS:2 * N_HEADS].set(vec.astype(F32))
    seq_tile = lambda cols: pl.BlockSpec((1, t, cols), lambda i, j: (i, j, 0))
    pos_tile = pl.BlockSpec((t, HEAD_DIM), lambda i, j: (j, 0))
    ret_specs = [_const_spec((d, 4 * width)), _const_spec((1, 4 * width)), pos_tile, pos_tile,
                 _const_spec((N_HEADS, t, t)), _const_spec((N_HEADS, t, HEAD_DIM)), _const_spec((N_HEADS, t, HEAD_DIM)),
                 _const_spec((1, width)), _const_spec((1, width))]
    gdn_specs = [_const_spec((d, 4 * width)), _const_spec((1, 4 * width)), _const_spec((d, lanes)),
                 _const_spec((1, lanes)), _const_spec((CONV_WIDTH, 3 * width)), _const_spec((1, lanes)),
                 _const_spec((1, lanes)), _const_spec((1, HEAD_DIM))]
    lru_specs = [_const_spec((d, 2 * lru_w)), _const_spec((1, 2 * lru_w)), _const_spec((CONV_WIDTH, lru_w)),
                 _const_spec((1, lru_w)), _const_spec((lru_w, lru_w)), _const_spec((1, lru_w)),
                 _const_spec((lru_w, lru_w)), _const_spec((1, lru_w)), _const_spec((1, lru_w))]
    assert (len(ret_specs), len(gdn_specs), len(lru_specs)) == (N_RET_IN, N_GDN_IN, N_LRU_IN)
    out = jax.ShapeDtypeStruct((bsz, seq, width), BF16)
    return pl.pallas_call(
        functools.partial(_mixers_kernel, chunk_decay=chunk_decay),
        grid=(bsz, seq // t),
        in_specs=[seq_tile(d)] + ret_specs + gdn_specs + lru_specs,
        out_specs=[seq_tile(width), seq_tile(width), seq_tile(lru_w)],
        out_shape=[out, out, jax.ShapeDtypeStruct((bsz, seq, lru_w), BF16)],
        scratch_shapes=[pltpu.VMEM((N_HEADS, HEAD_DIM, HEAD_DIM), F32),
                        pltpu.VMEM((CARRY_ROWS + t, 3 * width), F32),
                        pltpu.VMEM((N_HEADS, HEAD_DIM, HEAD_DIM), F32),
                        pltpu.VMEM((t, width), F32),
                        pltpu.VMEM((t, width), F32),
                        pltpu.VMEM((t, width), BF16),
                        pltpu.VMEM((t, width), BF16),
                        pltpu.VMEM((t, width), BF16),
                        pltpu.VMEM((N_HEADS, t, GDN_CHUNK), BF16),
                        pltpu.VMEM((CARRY_ROWS + t, lru_w), F32),
                        pltpu.VMEM((1, lru_w), F32)],
        compiler_params=_params("parallel", "arbitrary"),
        name="token_mixers",
    )(h, w_ret, row(b_ret), cos, sin, dmat, qd, kd, row(ret_g), row(ret_b),
      w_gdn, row(b_gdn), w_small, row(b_small), gdn_cw, pad_row(a_log), pad_row(dt_bias), row(gdn_g),
      w_lru, row(b_lru), lru_cw, row(lru_cb), w_r, row(b_r), w_i, row(b_i), row(lam))


def _first_index_of_max(x, idx, size):
    m = jnp.max(x, axis=0, keepdims=True)
    first = jnp.min(jnp.where(x == m, idx, size), axis=0, keepdims=True)
    return m, idx == first


def _route(logits_t, bias_col, out):
    n_exp, t = logits_t.shape
    per_group = n_exp // N_GROUPS
    scores = _sigmoid(logits_t)
    sel = scores + bias_col
    idx_g = lax.broadcasted_iota(jnp.int32, (per_group, t), 0)
    group_scores = []
    for g in range(N_GROUPS):
        x = sel[g * per_group:(g + 1) * per_group, :]
        m1, hit = _first_index_of_max(x, idx_g, per_group)
        m2 = jnp.max(jnp.where(hit, NEG_INF, x), axis=0, keepdims=True)
        group_scores.append(m1 + m2)
        if g % 2 == 1:
            yield
    gsc = jnp.concatenate(group_scores, axis=0)
    idx_n = lax.broadcasted_iota(jnp.int32, (N_GROUPS, t), 0)
    gmask = jnp.zeros((N_GROUPS, t), F32)
    for _ in range(TOPK_GROUPS):
        _, hit = _first_index_of_max(gsc, idx_n, N_GROUPS)
        gmask = jnp.where(hit, 1.0, gmask)
        gsc = jnp.where(hit, NEG_INF, gsc)
    yield
    emask = jnp.concatenate([jnp.broadcast_to(gmask[g:g + 1, :], (per_group, t)) for g in range(N_GROUPS)], axis=0)
    cand = jnp.where(emask > 0.0, sel, NEG_INF)
    idx_e = lax.broadcasted_iota(jnp.int32, (n_exp, t), 0)
    picked = jnp.zeros((n_exp, t), F32)
    hits = []
    for _ in range(TOP_K):
        _, hit = _first_index_of_max(cand, idx_e, n_exp)
        hits.append(hit)
        picked = jnp.where(hit, 1.0, picked)
        cand = jnp.where(hit, NEG_INF, cand)
        yield
    gw = jnp.where(picked > 0.0, scores, 0.0)
    out.update(combine=gw / jnp.sum(gw, axis=0, keepdims=True) * ROUTED_SCALE, picked=picked, hits=hits, idx_e=idx_e)


def _pack_bf16_pairs(x):
    c = x.shape[1] // 2
    hi = pltpu.bitcast(x[:, :c].astype(BF16).astype(F32), jnp.int32)
    lo = pltpu.bitcast(x[:, c:].astype(BF16).astype(F32), jnp.int32)
    return hi | lax.shift_right_logical(lo, jnp.full(lo.shape, 16, jnp.int32))


def _unpack_bf16_pairs(w):
    hi = pltpu.bitcast(w & jnp.int32(-65536), F32)
    lo = pltpu.bitcast(lax.shift_left(w, jnp.full(w.shape, 16, jnp.int32)), F32)
    return hi, lo


def _merge_kernel(h_ref, yr_ref, yg_ref, yl_ref, wmg_ref, bmg_ref, wbr_ref, wout_ref, g_ref, b_ref,
                  rwt_ref, rb_ref, eye_ref, tri_ref, h1_ref, h1b_ref, xp_ref, eidx_ref, rank_ref, gwt_ref,
                  count_ref, *, alpha):
    @pl.when(pl.program_id(0) == 0)
    def _():
        count_ref[...] = jnp.zeros_like(count_ref)

    d = h_ref.shape[1]
    sub = eye_ref.shape[0]
    parts = [slice(s, s + sub) for s in range(0, h_ref.shape[0], sub)]
    pick = lambda hit, vals, zero: jnp.sum(jnp.where(hit, vals, zero), axis=0, keepdims=True)

    def dense(rows, logits):
        h = h_ref[rows, :]
        hb = h.astype(BF16)
        gates = _sigmoid(_dot(hb, wmg_ref[...]) + bmg_ref[...])
        yield
        mixed = None
        for n, y_ref in enumerate((yr_ref, yg_ref, yl_ref)):
            term = gates[:, n * d:(n + 1) * d] * _dot(y_ref[rows, :], wbr_ref[n])
            mixed = term if mixed is None else mixed + term
            yield
        mix = _dot(mixed.astype(BF16), wout_ref[...])
        yield
        h1 = _layer_norm(alpha * h + mix, g_ref[...], b_ref[...])
        h1b = h1.astype(BF16)
        h1_ref[rows, :] = h1
        h1b_ref[rows, :] = h1b
        xp_ref[rows, :] = _pack_bf16_pairs(h1)
        logits.append(_dot_nt(rwt_ref[...], h1b))
        yield

    def routing(rows, logit):
        res = {}
        yield from _route(logit, rb_ref[...], res)
        combine_t, picked, hits, idx_e = res["combine"], res["picked"], res["hits"], res["idx_e"]
        rank_full = count_ref[...] + _dot(picked.astype(BF16), tri_ref[...])
        count_ref[...] += jnp.sum(picked, axis=1, keepdims=True)
        eidx_ref[:, rows] = jnp.concatenate([pick(hit, idx_e, 0) for hit in hits], axis=0)
        yield
        rank_ref[:, rows] = jnp.concatenate([pick(hit, rank_full, 0.0) for hit in hits], axis=0).astype(jnp.int32)
        gw = jnp.concatenate([pick(hit, combine_t, 0.0) for hit in hits], axis=0)
        gwt_ref[rows, :] = lax.dot_general(eye_ref[...], gw, (((1,), (1,)), ((), ())),
                                           preferred_element_type=F32, precision=lax.Precision.HIGHEST)
        yield

    def interleave(gens):
        while gens:
            gens = [g for g in gens if next(g, StopIteration) is not StopIteration]

    logits = [[] for _ in parts]
    interleave([dense(rows, logits[s]) for s, rows in enumerate(parts)])
    interleave([routing(rows, logits[s][0]) for s, rows in enumerate(parts)])


def _merge(h2, y_ret, y_gdn, y_lru, w_mg, b_mg, w_branch, w_out, ln_g, ln_b, router_wt, router_b, alpha):
    n, d = h2.shape
    t = min(MERGE_TILE, n)
    width = y_ret.shape[1]
    n_exp = router_wt.shape[0]
    tile = lambda cols: pl.BlockSpec((t, cols), lambda i: (i, 0))
    lane_tile = pl.BlockSpec((TOP_K, t), lambda i: (0, i))
    sub = min(MERGE_SUB, t)
    tri = jnp.triu(jnp.ones((sub, sub), BF16), 1)
    return pl.pallas_call(
        functools.partial(_merge_kernel, alpha=alpha),
        grid=(n // t,),
        in_specs=[
            tile(d), tile(width), tile(width), tile(width),
            _const_spec((d, N_BRANCHES * d)),
            _const_spec((1, N_BRANCHES * d)),
            _const_spec((N_BRANCHES, width, d)),
            _const_spec((d, d)),
            _const_spec((1, d)),
            _const_spec((1, d)),
            _const_spec((n_exp, d)),
            _const_spec((n_exp, 1)),
            _const_spec((sub, sub)),
            _const_spec((sub, sub)),
        ],
        out_specs=[tile(d), tile(d), tile(d // 2), lane_tile, lane_tile, tile(TOP_K),
                   pl.BlockSpec((n_exp, 1), lambda i: (0, 0))],
        out_shape=[jax.ShapeDtypeStruct((n, d), F32), jax.ShapeDtypeStruct((n, d), BF16),
                   jax.ShapeDtypeStruct((n, d // 2), jnp.int32),
                   jax.ShapeDtypeStruct((TOP_K, n), jnp.int32), jax.ShapeDtypeStruct((TOP_K, n), jnp.int32),
                   jax.ShapeDtypeStruct((n, TOP_K), F32), jax.ShapeDtypeStruct((n_exp, 1), F32)],
        compiler_params=_params("arbitrary"),
        name="merge_route",
    )(h2, y_ret, y_gdn, y_lru, w_mg, b_mg.reshape(1, -1), w_branch, w_out, ln_g.reshape(1, -1),
      ln_b.reshape(1, -1), router_wt, router_b.reshape(-1, 1), jnp.eye(sub, dtype=F32), tri)


def _dest_kernel(start_ref, eidx_ref, rank_ref, o_ref, *, n_exp):
    eidx = eidx_ref[...]
    dest = rank_ref[...]
    for e in range(n_exp):
        dest = dest + jnp.where(eidx == e, start_ref[e], 0)
    o_ref[...] = dest


def _dest_rows(seg_start, eidx, rank):
    k, n = eidx.shape
    t = min(DEST_TILE, n)
    blk = pl.BlockSpec((k, t), lambda i, s: (0, i))
    return pl.pallas_call(
        functools.partial(_dest_kernel, n_exp=seg_start.shape[0]),
        grid_spec=pltpu.PrefetchScalarGridSpec(num_scalar_prefetch=1, grid=(n // t,), in_specs=[blk, blk],
                                               out_specs=blk),
        out_shape=jax.ShapeDtypeStruct((k, n), jnp.int32),
        compiler_params=_params("parallel"),
        name="dest_rows",
    )(seg_start, eidx, rank)


def _sc_workers():
    info = plsc.get_sparse_core_info()
    return info.num_cores, info.num_subcores


def _sc_mesh():
    return plsc.VectorSubcoreMesh(core_axis_name="c", subcore_axis_name="s")


def _sc_scatter_rows(x, dest, n_rows):
    n, c = x.shape
    k = dest.shape[0]
    n_cores, n_sub = _sc_workers()
    per_worker = n // (n_cores * n_sub)
    steps = per_worker // SC_WINDOW

    @functools.partial(pl.kernel, out_type=jax.ShapeDtypeStruct((n_rows, c), x.dtype), mesh=_sc_mesh(),
                       scratch_types=[pltpu.VMEM((k, SC_WINDOW), jnp.int32), pltpu.VMEM((SC_WINDOW, c), x.dtype),
                                      pltpu.SemaphoreType.DMA, pltpu.SemaphoreType.DMA],
                       name="dispatch_rows")
    def scatter(x_hbm, d_hbm, o_hbm, idx_v, rows_v, sem, row_sem):
        base = (lax.axis_index("s") * n_cores + lax.axis_index("c")) * per_worker

        @pl.loop(0, steps)
        def _(j):
            off = base + j * SC_WINDOW
            loads = [pltpu.async_copy(x_hbm.at[pl.ds(off, SC_WINDOW)], rows_v, row_sem)]
            loads += [pltpu.async_copy(d_hbm.at[pl.ds(kk * n + off, SC_WINDOW)], idx_v.at[kk], sem) for kk in range(k)]
            for cp in loads:
                cp.wait()
            copies = [pltpu.async_copy(rows_v, o_hbm.at[idx_v.at[kk]], sem) for kk in range(k)]
            for cp in copies:
                cp.wait()

    return scatter(x, dest.reshape(k * n))


def _sc_gather_rows(table, idx):
    b = idx.shape[0]
    c = table.shape[1]
    n_cores, n_sub = _sc_workers()
    per_worker = b // (n_cores * n_sub)
    steps = per_worker // SC_WINDOW

    half = SC_WINDOW // 2
    half_buf = lambda dtype, *shape: pltpu.VMEM((half,) + shape, dtype)

    @functools.partial(pl.kernel, out_type=jax.ShapeDtypeStruct((b, c), table.dtype), mesh=_sc_mesh(),
                       scratch_types=[half_buf(jnp.int32), half_buf(jnp.int32), half_buf(table.dtype, c),
                                      half_buf(table.dtype, c), pltpu.SemaphoreType.DMA, pltpu.SemaphoreType.DMA,
                                      pltpu.SemaphoreType.DMA, pltpu.SemaphoreType.DMA],
                       name="collect_rows")
    def gather(t_hbm, i_hbm, o_hbm, idx_a, idx_b, rows_a, rows_b, gat_a, gat_b, put_a, put_b):
        base = (lax.axis_index("s") * n_cores + lax.axis_index("c")) * per_worker

        @pl.loop(0, steps)
        def _(j):
            off_a = base + j * SC_WINDOW
            off_b = off_a + half
            pltpu.sync_copy(i_hbm.at[pl.ds(off_a, half)], idx_a)
            in_a = pltpu.async_copy(t_hbm.at[idx_a], rows_a, gat_a)
            pltpu.sync_copy(i_hbm.at[pl.ds(off_b, half)], idx_b)
            in_b = pltpu.async_copy(t_hbm.at[idx_b], rows_b, gat_b)
            in_a.wait()
            out_a = pltpu.async_copy(rows_a, o_hbm.at[pl.ds(off_a, half)], put_a)
            in_b.wait()
            out_b = pltpu.async_copy(rows_b, o_hbm.at[pl.ds(off_b, half)], put_b)
            out_a.wait()
            out_b.wait()

    return gather(table, idx)


def _expert_block_kernel(meta_ref, xs_ref, wgu_ref, wd_ref, ys_ref, wgu_b, wd_b, *, n_blocks):
    i = pl.program_id(0)

    @pl.when((i == 0) | (meta_ref[i] != meta_ref[jnp.maximum(i - 1, 0)]))
    def _():
        wgu_b[...] = wgu_ref[0, 0].astype(BF16)
        wd_b[...] = wd_ref[0, 0].astype(BF16)

    @pl.when(i < meta_ref[n_blocks])
    def _():
        half = xs_ref.shape[1]
        ff = wd_b.shape[0]

        def sub_block(rows):
            hi, lo = _unpack_bf16_pairs(xs_ref[rows, :])
            hib, lob = hi.astype(BF16), lo.astype(BF16)
            yield
            gu = _dot(hib, wgu_b[:half, :]) + _dot(lob, wgu_b[half:, :])
            yield
            mid = (_silu(gu[:, :ff]) * gu[:, ff:]).astype(BF16)
            yield
            out = _dot(mid, wd_b[...])
            yield
            ys_ref[rows, :] = _pack_bf16_pairs(out)
            yield

        waiting = [sub_block(slice(r, r + EXPERT_SUB)) for r in range(0, xs_ref.shape[0], EXPERT_SUB)]
        active = []
        while waiting or active:
            if waiting:
                active.append(waiting.pop(0))
            active = [g for g in active if next(g, StopIteration) is not StopIteration]


def _expert_blocks(meta, xs, w_gu, w_down, layer):
    rows, half = xs.shape
    _, _, d, ff2 = w_gu.shape
    n_blocks = rows // MOE_BLOCK
    row_blk = pl.BlockSpec((MOE_BLOCK, half), lambda i, meta: (jnp.minimum(i, meta[n_blocks] - 1), 0))
    return pl.pallas_call(
        functools.partial(_expert_block_kernel, n_blocks=n_blocks),
        grid_spec=pltpu.PrefetchScalarGridSpec(
            num_scalar_prefetch=1, grid=(n_blocks,),
            in_specs=[row_blk,
                      pl.BlockSpec((1, 1, d, ff2), lambda i, meta: (layer, meta[i], 0, 0)),
                      pl.BlockSpec((1, 1, ff2 // 2, d), lambda i, meta: (layer, meta[i], 0, 0))],
            out_specs=row_blk,
            scratch_shapes=[pltpu.VMEM((d, ff2), BF16), pltpu.VMEM((ff2 // 2, d), BF16)]),
        out_shape=jax.ShapeDtypeStruct((rows, half), jnp.int32),
        compiler_params=_params("arbitrary"),
        name="expert_blocks",
    )(meta, xs, w_gu, w_down)


def _routed_rows(xp, eidx, rank, counts, w_gu, w_down, layer):
    n = xp.shape[0]
    k = eidx.shape[0]
    n_exp = w_gu.shape[1]
    counts = counts.reshape(n_exp).astype(jnp.int32)
    padded = (counts + MOE_BLOCK - 1) // MOE_BLOCK * MOE_BLOCK
    seg_end = jnp.cumsum(padded)
    n_blocks = k * n // MOE_BLOCK + n_exp
    blk_start = jnp.arange(n_blocks, dtype=jnp.int32) * MOE_BLOCK
    blk_expert = jnp.minimum(jnp.sum(seg_end[None, :] <= blk_start[:, None], axis=1), n_exp - 1)
    meta = jnp.concatenate([blk_expert, seg_end[-1:] // MOE_BLOCK]).astype(jnp.int32)
    dest = _dest_rows(seg_end - padded, eidx, rank)
    xs = _sc_scatter_rows(xp, dest, n_blocks * MOE_BLOCK)
    return _expert_blocks(meta, xs, w_gu, w_down, layer), dest


def _shared_ple_kernel(h1_ref, h1b_ref, p_ref, wgu_ref, wd_ref, wpg_ref, bpg_ref, wpe_ref, o_ref, *, alpha):
    xb = h1b_ref[...]
    ff = wd_ref.shape[0]
    gu = _dot(xb, wgu_ref[...])
    shared = _dot((_silu(gu[:, :ff]) * gu[:, ff:]).astype(BF16), wd_ref[...])
    ple = _sigmoid(_dot(xb, wpg_ref[...]) + bpg_ref[...]) * _dot(p_ref[0].astype(BF16), wpe_ref[...])
    o_ref[...] = alpha * h1_ref[...] + shared + ple


def _shared_ple(h1, h1b, p3, layer, part, sh_w_gu, sh_w_down, ple_w_g, ple_b_g, ple_w_e, alpha):
    n, d = h1.shape
    t = min(ROW_TILE, n)
    steps = n // t
    pdim = p3.shape[2]
    ff2 = sh_w_gu.shape[1]
    tile = lambda cols: pl.BlockSpec((t, cols), lambda i: (i, 0))
    return pl.pallas_call(
        functools.partial(_shared_ple_kernel, alpha=alpha),
        grid=(n // t,),
        in_specs=[
            tile(d), tile(d), pl.BlockSpec((1, t, pdim), lambda i: (layer, part * steps + i, 0)),
            _const_spec((d, ff2)),
            _const_spec((ff2 // 2, d)),
            _const_spec((d, d)),
            _const_spec((1, d)),
            _const_spec((pdim, d)),
        ],
        out_specs=tile(d),
        out_shape=jax.ShapeDtypeStruct((n, d), F32),
        compiler_params=_params("parallel"),
        name="shared_ple",
    )(h1, h1b, p3, sh_w_gu, sh_w_down, ple_w_g, ple_b_g.reshape(1, -1), ple_w_e)


def _combine_norm_kernel(pre_ref, yk_ref, gwt_ref, g_ref, b_ref, *rest):
    o_ref = rest[-1]
    gwt = gwt_ref[...]
    routed_hi = routed_lo = None
    for k in range(yk_ref.shape[0]):
        hi, lo = _unpack_bf16_pairs(yk_ref[k])
        wk = gwt[:, k:k + 1]
        routed_hi = hi * wk if routed_hi is None else routed_hi + hi * wk
        routed_lo = lo * wk if routed_lo is None else routed_lo + lo * wk
    routed = jnp.concatenate([routed_hi, routed_lo], axis=1)
    o_ref[...] = _layer_norm(pre_ref[...] + routed, g_ref[...], b_ref[...])


def _combine_norm(pre, ys, dest, gwt, ln_g, ln_b):
    n, d = pre.shape
    top_k = dest.shape[0]
    t = min(ROW_TILE, n)
    chunks = COMBINE_CHUNKS if n % (COMBINE_CHUNKS * t) == 0 else 1
    n_c = n // chunks
    steps = n_c // t
    out = None
    for c in range(chunks):
        rows = _sc_gather_rows(ys, dest[:, c * n_c:(c + 1) * n_c].reshape(top_k * n_c)).reshape(top_k, n_c, d // 2)
        tile = lambda cols: pl.BlockSpec((t, cols), lambda i, c=c: (c * steps + i, 0))
        in_specs = [tile(d), pl.BlockSpec((top_k, t, d // 2), lambda i: (0, i, 0)), tile(top_k),
                    _const_spec((1, d)), _const_spec((1, d))]
        args = [pre, rows, gwt, ln_g.reshape(1, -1), ln_b.reshape(1, -1)]
        if out is not None:
            in_specs.append(pl.BlockSpec(memory_space=pl.ANY))
            args.append(out)
        out = pl.pallas_call(
            _combine_norm_kernel,
            grid=(steps,),
            in_specs=in_specs,
            out_specs=tile(d),
            out_shape=jax.ShapeDtypeStruct((n, d), F32),
            input_output_aliases={} if out is None else {len(args) - 1: 0},
            compiler_params=_params("parallel"),
            name="combine_norm",
        )(*args)
    return out


def _block_diag(w):
    g, i, j = w.shape
    eye = jnp.eye(g, dtype=w.dtype)
    return (eye[:, None, :, None] * w[:, :, None, :]).reshape(g * i, g * j)


def kernel(x, p, ln_in_g, ln_in_b, w_in, b_in, ret_norm_g, ret_norm_b, gdn_conv_w, gdn_a_log, gdn_dt_bias, gdn_norm_g, lru_conv_w, lru_conv_b, lru_w_r, lru_b_r, lru_w_i, lru_b_i, lru_lambda, w_branch, w_out, ln1_g, ln1_b, router_w, router_b, exp_w_gu, exp_w_down, sh_w_gu, sh_w_down, ple_w_e, ple_w_g, ple_b_g, ln2_g, ln2_b):
    bsz, seq, d = x.shape
    depth = w_in.shape[0]
    n = bsz * seq
    width = N_HEADS * HEAD_DIM
    alpha = (2 * depth) ** 0.25
    lanes = 128
    o_ret = 0
    o_gdn = o_ret + 4 * width
    o_small = o_gdn + 4 * width
    o_lru = o_small + 2 * N_HEADS
    o_mg = o_lru + 2 * width

    parts = range(N_PARTS if bsz % N_PARTS == 0 else 1)
    bsz_p = bsz // len(parts)
    n_p = bsz_p * seq
    p3 = p.reshape(depth, n, -1)
    bf = lambda a: a.astype(BF16)
    hs = [_entry_norm(x.reshape(n, d), ln_in_g, ln_in_b, part, n_p) for part in parts]
    for l in range(depth):
        wl, bl = w_in[l], b_in[l]
        w_small = bf(jnp.zeros((d, lanes), F32).at[:, :2 * N_HEADS].set(wl[:, o_small:o_lru]))
        b_small = jnp.zeros((lanes,), F32).at[:2 * N_HEADS].set(bl[o_small:o_lru])
        w_ret, w_gdn, w_lru, w_mg = bf(wl[:, o_ret:o_gdn]), bf(wl[:, o_gdn:o_small]), bf(wl[:, o_lru:o_mg]), bf(wl[:, o_mg:])
        w_r, w_i = bf(_block_diag(lru_w_r[l])), bf(_block_diag(lru_w_i[l]))
        w_br, w_o, r_wt = bf(w_branch[l]), bf(w_out[l]), bf(router_w[l].T)
        w_sgu, w_sd, w_pg, w_pe = bf(sh_w_gu[l]), bf(sh_w_down[l]), bf(ple_w_g[l]), bf(ple_w_e[l])
        for part in parts:
            h = hs[part]
            h3 = h.reshape(bsz_p, seq, d)
            y_ret, y_gdn, y_lru = _mixers(
                h3, (w_ret, bl[o_ret:o_gdn], ret_norm_g[l], ret_norm_b[l]),
                (w_gdn, bl[o_gdn:o_small], w_small, b_small, gdn_conv_w[l], gdn_a_log[l], gdn_dt_bias[l], gdn_norm_g[l]),
                (w_lru, bl[o_lru:o_mg], lru_conv_w[l], lru_conv_b[l], w_r, lru_b_r[l], w_i, lru_b_i[l], lru_lambda[l]))
            h1, h1b, xp, eidx, rank, gwt, counts = _merge(
                h, y_ret.reshape(n_p, width), y_gdn.reshape(n_p, width), y_lru.reshape(n_p, width), w_mg, bl[o_mg:],
                w_br, w_o, ln1_g[l], ln1_b[l], r_wt, router_b[l], alpha)
            pre = _shared_ple(h1, h1b, p3, l, part, w_sgu, w_sd, w_pg, ple_b_g[l], w_pe, alpha)
            ys, dest = _routed_rows(xp, eidx, rank, counts, exp_w_gu, exp_w_down, l)
            hs[part] = _combine_norm(pre, ys, dest, gwt, ln2_g[l], ln2_b[l])
    return jnp.concatenate([h.reshape(bsz_p, seq, d) for h in hs], axis=0)
```

```python
import functools

import numpy as np
import jax
import jax.numpy as jnp
from jax import lax
from jax.experimental import pallas as pl
from jax.experimental.pallas import tpu as pltpu
from jax.experimental.pallas import tpu_sc as plsc

F32 = jnp.float32
BF16 = jnp.bfloat16

HEAD_DIM = 128
N_HEADS = 4
LRU_C = 8.0
CONV_WIDTH = 4
N_BRANCHES = 3
ROPE_BASE = 10000.0
N_GROUPS = 8
TOPK_GROUPS = 4
TOP_K = 8
ROUTED_SCALE = 2.5
LN_EPS = 1e-5
GDN_CHUNK = 128
SEQ_TILE = 256
MERGE_TILE = 512
MERGE_SUB = 128
ROW_TILE = 512
MOE_BLOCK = 1024
EXPERT_SUB = 256
DEST_TILE = 4096
SC_WINDOW = 128
CARRY_ROWS = 8
SCAN_GROUP = 16
VMEM_LIMIT_BYTES = 56 * 1024 * 1024
NEG_INF = float("-inf")


def _const_spec(shape):
    nd = len(shape)
    return pl.BlockSpec(shape, lambda *_: (0,) * nd, pipeline_mode=pl.Buffered(1))


def _params(*sem):
    return pltpu.CompilerParams(dimension_semantics=sem, vmem_limit_bytes=VMEM_LIMIT_BYTES)


def _layer_norm(x, g, b):
    mu = jnp.mean(x, axis=-1, keepdims=True)
    xc = x - mu
    var = jnp.mean(xc * xc, axis=-1, keepdims=True)
    return xc * lax.rsqrt(var + LN_EPS) * g + b


def _sigmoid(x):
    return 1.0 / (1.0 + jnp.exp(-x))


def _silu(x):
    return x * _sigmoid(x)


def _softplus(x):
    return jnp.maximum(x, 0.0) + jnp.log1p(jnp.exp(-jnp.abs(x)))


def _dot(a, b):
    return jnp.dot(a, b, preferred_element_type=F32)


def _dot_nt(a, b):
    return lax.dot_general(a, b, (((1,), (1,)), ((), ())), preferred_element_type=F32)


def _dot_tn(a, b):
    return lax.dot_general(a, b, (((0,), (0,)), ((), ())), preferred_element_type=F32)


def _ln_kernel(x_ref, g_ref, b_ref, o_ref):
    o_ref[...] = _layer_norm(x_ref[...], g_ref[...], b_ref[...])


def _entry_norm(x2, g, b):
    n, d = x2.shape
    t = min(1024, n)
    return pl.pallas_call(
        _ln_kernel,
        grid=(n // t,),
        in_specs=[pl.BlockSpec((t, d), lambda i: (i, 0)), _const_spec((1, d)), _const_spec((1, d))],
        out_specs=pl.BlockSpec((t, d), lambda i: (i, 0)),
        out_shape=jax.ShapeDtypeStruct((n, d), F32),
        compiler_params=_params("parallel"),
        name="entry_norm",
    )(x2, g.reshape(1, d), b.reshape(1, d))


def _retention_body(hb, w_ref, b_ref, cos_ref, sin_ref, dmat_ref, qd_ref, kd_ref, ng_ref, nb_ref,
                    y_ref, state_ref, *, chunk_decay):
    width = N_HEADS * HEAD_DIM
    proj = _dot(hb, w_ref[...]) + b_ref[...]
    yield
    cos = cos_ref[...]
    sin = sin_ref[...]
    heads = range(N_HEADS)
    qs, ks, vbs = [], [], []
    for hh in heads:
        lo = hh * HEAD_DIM
        q = proj[:, lo:lo + HEAD_DIM]
        k = proj[:, width + lo:width + lo + HEAD_DIM]
        qs.append(q * cos + pltpu.roll(q, HEAD_DIM // 2, axis=1) * sin)
        ks.append((k * cos + pltpu.roll(k, HEAD_DIM // 2, axis=1) * sin) * (HEAD_DIM ** -0.5))
        vbs.append(proj[:, 2 * width + lo:2 * width + lo + HEAD_DIM].astype(BF16))
        yield
    states = [state_ref[hh] for hh in heads]
    scores, inter, outs = [], [], []
    for hh in heads:
        scores.append((_dot_nt(qs[hh].astype(BF16), ks[hh].astype(BF16)) * dmat_ref[hh]).astype(BF16))
        yield
    for hh in heads:
        inter.append(_dot((qs[hh] * qd_ref[hh]).astype(BF16), states[hh].astype(BF16)))
        yield
    for hh in heads:
        state_ref[hh] = states[hh] * chunk_decay[hh] + _dot_tn((ks[hh] * kd_ref[hh]).astype(BF16), vbs[hh])
        yield
    for hh in heads:
        outs.append(_dot(scores[hh], vbs[hh]) + inter[hh])
        yield
    for hh in heads:
        lo = hh * HEAD_DIM
        o = outs[hh]
        gate = proj[:, 3 * width + lo:3 * width + lo + HEAD_DIM]
        mu = jnp.mean(o, axis=-1, keepdims=True)
        oc = o - mu
        var = jnp.mean(oc * oc, axis=-1, keepdims=True)
        on = oc * lax.rsqrt(var + LN_EPS) * ng_ref[:, lo:lo + HEAD_DIM] + nb_ref[:, lo:lo + HEAD_DIM]
        y_ref[:, lo:lo + HEAD_DIM] = (_silu(gate) * on).astype(y_ref.dtype)
        yield


def _retention_tables(seq, tile):
    half = HEAD_DIM // 2
    inv_freq = ROPE_BASE ** (-np.linspace(0.0, 1.0, half))
    ang = np.arange(seq)[:, None] * inv_freq[None, :]
    cos = np.concatenate([np.cos(ang), np.cos(ang)], axis=1)
    sin = np.concatenate([-np.sin(ang), np.sin(ang)], axis=1)
    log_gamma = np.log1p(-np.exp2(-5.0 - np.arange(N_HEADS)))
    pos = np.arange(tile)
    diff = pos[:, None] - pos[None, :]
    dmat = np.where(diff >= 0, np.exp(log_gamma[:, None, None] * np.maximum(diff, 0)), 0.0)
    qd = np.exp(log_gamma[:, None] * (pos + 1.0))[:, :, None] * np.ones((1, 1, HEAD_DIM))
    kd = np.exp(log_gamma[:, None] * (tile - 1.0 - pos))[:, :, None] * np.ones((1, 1, HEAD_DIM))
    chunk_decay = tuple(float(c) for c in np.exp(log_gamma * tile))
    as32 = lambda a: jnp.asarray(a, F32)
    return as32(cos), as32(sin), as32(dmat), as32(qd), as32(kd), chunk_decay


def _causal_conv(x, xs_ref, cw_ref):
    t = x.shape[0]
    xs_ref[CARRY_ROWS:, :] = x
    ext = xs_ref[...]
    acc = ext * cw_ref[0:1, :]
    for j in range(1, CONV_WIDTH):
        acc = pltpu.roll(acc, 1, axis=0) + ext * cw_ref[j:j + 1, :]
    xs_ref[0:CARRY_ROWS, :] = xs_ref[t:t + CARRY_ROWS, :]
    return acc[CARRY_ROWS:, :]


def _cumsum_rows(x):
    n = x.shape[0]
    row = lax.broadcasted_iota(jnp.int32, x.shape, 0)
    d = 1
    while d < n:
        x = x + jnp.where(row >= d, pltpu.roll(x, d, axis=0), 0.0)
        d *= 2
    return x


def _gdn_body(hb, w_ref, b_ref, ws_ref, bs_ref, cw_ref, alog_ref, dtb_ref, ng_ref,
              y_ref, xs_ref, state_ref, u_ref, wf_ref, w_s_ref, qd_ref, kd_ref, qk_ref):
    width = N_HEADS * HEAD_DIM
    c = GDN_CHUNK
    t = hb.shape[0]
    proj = _dot(hb, w_ref[...]) + b_ref[...]
    small = _dot(hb, ws_ref[...]) + bs_ref[...]
    yield
    qkv = _silu(_causal_conv(proj[:, :3 * width], xs_ref, cw_ref))
    yield
    beta_all = _sigmoid(small)
    la_all = -jnp.exp(alog_ref[...]) * _softplus(small + dtb_ref[...])

    ri = lax.broadcasted_iota(jnp.int32, (c, c), 0)
    ci = lax.broadcasted_iota(jnp.int32, (c, c), 1)
    lower = ri >= ci
    strict = ri > ci

    items = [(n, hh) for n in range(t // c) for hh in range(N_HEADS)]
    gcs = {}
    for n in range(t // c):
        la_c = la_all[n * c:(n + 1) * c, :]
        gc_c = _cumsum_rows(la_c)
        gcs[n] = (la_c, gc_c, jnp.exp(gc_c))
    g_last, pws, rems = {}, {}, {}
    for n, hh in items:
        r0, lo = n * c, hh * HEAD_DIM
        rows, cols = slice(r0, r0 + c), slice(lo, lo + HEAD_DIM)
        la_c, gc_c, egc_c = gcs[n]
        q = qkv[rows, lo:lo + HEAD_DIM]
        k = qkv[rows, width + lo:width + lo + HEAD_DIM]
        v = qkv[rows, 2 * width + lo:2 * width + lo + HEAD_DIM]
        q = q * lax.rsqrt(jnp.sum(q * q, axis=-1, keepdims=True) + 1e-6) * (HEAD_DIM ** -0.5)
        k = k * lax.rsqrt(jnp.sum(k * k, axis=-1, keepdims=True) + 1e-6)
        beta = beta_all[rows, hh:hh + 1]
        la = la_c[:, N_HEADS + hh:N_HEADS + hh + 1]
        gc = gc_c[:, N_HEADS + hh:N_HEADS + hh + 1]
        egc = egc_c[:, N_HEADS + hh:N_HEADS + hh + 1]
        gc_row = jnp.sum(jnp.where(ri <= ci, jnp.broadcast_to(la, (c, c)), 0.0), axis=0, keepdims=True)
        gc_last = gc_row[:, c - 1:c]
        decay = jnp.where(lower, jnp.exp(jnp.where(lower, gc - gc_row, 0.0)), 0.0)
        kb = k * beta
        kbf = k.astype(BF16)
        a_neg = jnp.where(strict, -(_dot_nt(kb.astype(BF16), kbf) * decay), 0.0)
        pws[n, hh] = a_neg
        rems[n, hh] = a_neg
        u_ref[rows, cols] = v * beta
        wf_ref[rows, cols] = kb * egc
        qk_ref[hh, rows, :] = (_dot_nt(q.astype(BF16), kbf) * decay).astype(BF16)
        qd_ref[rows, cols] = (q * egc).astype(BF16)
        kd_ref[rows, cols] = (k * jnp.exp(gc_last - gc)).astype(BF16)
        g_last[n, hh] = jnp.exp(gc_last)
        yield
    m = 2
    while m < c:
        for it in items:
            pwb = pws[it].astype(BF16)
            pws[it] = _dot(pwb, pwb)
        yield
        for it in items:
            rems[it] = rems[it] + pws[it] + _dot(rems[it].astype(BF16), pws[it].astype(BF16))
        yield
        m *= 2
    for n, hh in items:
        rows, cols = slice(n * c, (n + 1) * c), slice(hh * HEAD_DIM, (hh + 1) * HEAD_DIM)
        remb = rems[n, hh].astype(BF16)
        u_ref[rows, cols] = u_ref[rows, cols] + _dot(remb, u_ref[rows, cols].astype(BF16))
        w_s_ref[rows, cols] = (wf_ref[rows, cols] + _dot(remb, wf_ref[rows, cols].astype(BF16))).astype(BF16)
    yield

    heads = range(N_HEADS)
    for n in range(t // c):
        rows = slice(n * c, (n + 1) * c)
        cols = [slice(hh * HEAD_DIM, (hh + 1) * HEAD_DIM) for hh in heads]
        states = [state_ref[hh] for hh in heads]
        sbs = [s.astype(BF16) for s in states]
        vnbs = [(u_ref[rows, cols[hh]] - _dot(w_s_ref[rows, cols[hh]], sbs[hh])).astype(BF16) for hh in heads]
        yield
        outs = [_dot(qd_ref[rows, cols[hh]], sbs[hh]) + _dot(qk_ref[hh, rows, :], vnbs[hh]) for hh in heads]
        for hh in heads:
            state_ref[hh] = states[hh] * g_last[n, hh] + _dot_tn(kd_ref[rows, cols[hh]], vnbs[hh])
        yield
        for hh in heads:
            o = outs[hh]
            o = o * lax.rsqrt(jnp.mean(o * o, axis=-1, keepdims=True) + 1e-6) * ng_ref[...]
            og = proj[rows, 3 * width + hh * HEAD_DIM:3 * width + (hh + 1) * HEAD_DIM]
            y_ref[rows, cols[hh]] = (o * _silu(og)).astype(y_ref.dtype)
        yield


def _lru_body(hb, w_ref, b_ref, cw_ref, cb_ref, wr_ref, br_ref, wi_ref, bi_ref, lam_ref,
              y_ref, xs_ref, carry_ref):
    width = cw_ref.shape[1]
    t = hb.shape[0]
    proj = _dot(hb, w_ref[...]) + b_ref[...]
    yield
    xc = _causal_conv(proj[:, :width], xs_ref, cw_ref) + cb_ref[...]
    xcb = xc.astype(BF16)
    yield
    r = _sigmoid(_dot(xcb, wr_ref[...]) + br_ref[...])
    gi = _sigmoid(_dot(xcb, wi_ref[...]) + bi_ref[...])
    yield
    log_a = -LRU_C * r * _softplus(-lam_ref[...])
    a = jnp.exp(log_a)
    th = jnp.tanh(log_a)
    hs = jnp.sqrt(-2.0 * th / (1.0 - th)) * (gi * xc)
    row = lax.broadcasted_iota(jnp.int32, (t, width), 0) % SCAN_GROUP
    d = 1
    while d < SCAN_GROUP:
        keep = row >= d
        hs = hs + a * jnp.where(keep, pltpu.roll(hs, d, axis=0), 0.0)
        a = a * jnp.where(keep, pltpu.roll(a, d, axis=0), 1.0)
        d *= 2
        yield
    gate = jax.nn.gelu(proj[:, width:], approximate=True)
    carry = carry_ref[...]
    for g in range(t // SCAN_GROUP):
        rows = slice(g * SCAN_GROUP, (g + 1) * SCAN_GROUP)
        hg = hs[rows, :] + a[rows, :] * carry
        carry = hg[SCAN_GROUP - 1:SCAN_GROUP, :]
        y_ref[rows, :] = (gate[rows, :] * hg).astype(y_ref.dtype)
        if g % 8 == 7:
            yield
    carry_ref[...] = carry


N_RET_IN, N_GDN_IN, N_LRU_IN = 9, 8, 9
MIX_STRIDE = (1, 2, 2)


def _mixers_kernel(h_ref, *refs, chunk_decay):
    ret_in, refs = refs[:N_RET_IN], refs[N_RET_IN:]
    gdn_in, refs = refs[:N_GDN_IN], refs[N_GDN_IN:]
    lru_in, refs = refs[:N_LRU_IN], refs[N_LRU_IN:]
    yr_ref, yg_ref, yl_ref = refs[:3]
    ret_state, gdn_xs, gdn_state, u_ref, wf_ref, w_s_ref, qd_ref, kd_ref, qk_ref, lru_xs, lru_carry = refs[3:]

    @pl.when(pl.program_id(1) == 0)
    def _():
        ret_state[...] = jnp.zeros_like(ret_state)
        gdn_state[...] = jnp.zeros_like(gdn_state)
        lru_carry[...] = jnp.zeros_like(lru_carry)
        gdn_xs[0:CARRY_ROWS, :] = jnp.zeros((CARRY_ROWS, gdn_xs.shape[1]), F32)
        lru_xs[0:CARRY_ROWS, :] = jnp.zeros((CARRY_ROWS, lru_xs.shape[1]), F32)

    hb = h_ref[0].astype(BF16)
    branches = [
        (_gdn_body(hb, *gdn_in, yg_ref.at[0], gdn_xs, gdn_state, u_ref, wf_ref, w_s_ref, qd_ref, kd_ref, qk_ref),
         MIX_STRIDE[0]),
        (_retention_body(hb, *ret_in, yr_ref.at[0], ret_state, chunk_decay=chunk_decay), MIX_STRIDE[1]),
        (_lru_body(hb, *lru_in, yl_ref.at[0], lru_xs, lru_carry), MIX_STRIDE[2]),
    ]
    tick = 0
    while branches:
        for gen, stride in list(branches):
            if tick % stride == 0 and next(gen, StopIteration) is StopIteration:
                branches.remove((gen, stride))
        tick += 1


def _mixers(h, ret_args, gdn_args, lru_args):
    bsz, seq, d = h.shape
    t = min(SEQ_TILE, seq)
    width = N_HEADS * HEAD_DIM
    row = lambda vec: vec.reshape(1, -1)
    w_ret, b_ret, ret_g, ret_b = ret_args
    w_gdn, b_gdn, w_small, b_small, gdn_cw, a_log, dt_bias, gdn_g = gdn_args
    w_lru, b_lru, lru_cw, lru_cb, w_r, b_r, w_i, b_i, lam = lru_args
    lanes = w_small.shape[1]
    lru_w = lru_cw.shape[1]
    cos, sin, dmat, qd, kd, chunk_decay = _retention_tables(seq, t)
    pad_row = lambda vec: jnp.zeros((1, lanes), F32).at[0, N_HEADS:2 * N_HEADS].set(vec.astype(F32))
    seq_tile = lambda cols: pl.BlockSpec((1, t, cols), lambda i, j: (i, j, 0))
    pos_tile = pl.BlockSpec((t, HEAD_DIM), lambda i, j: (j, 0))
    ret_specs = [_const_spec((d, 4 * width)), _const_spec((1, 4 * width)), pos_tile, pos_tile,
                 _const_spec((N_HEADS, t, t)), _const_spec((N_HEADS, t, HEAD_DIM)), _const_spec((N_HEADS, t, HEAD_DIM)),
                 _const_spec((1, width)), _const_spec((1, width))]
    gdn_specs = [_const_spec((d, 4 * width)), _const_spec((1, 4 * width)), _const_spec((d, lanes)),
                 _const_spec((1, lanes)), _const_spec((CONV_WIDTH, 3 * width)), _const_spec((1, lanes)),
                 _const_spec((1, lanes)), _const_spec((1, HEAD_DIM))]
    lru_specs = [_const_spec((d, 2 * lru_w)), _const_spec((1, 2 * lru_w)), _const_spec((CONV_WIDTH, lru_w)),
                 _const_spec((1, lru_w)), _const_spec((lru_w, lru_w)), _const_spec((1, lru_w)),
                 _const_spec((lru_w, lru_w)), _const_spec((1, lru_w)), _const_spec((1, lru_w))]
    assert (len(ret_specs), len(gdn_specs), len(lru_specs)) == (N_RET_IN, N_GDN_IN, N_LRU_IN)
    out = jax.ShapeDtypeStruct((bsz, seq, width), BF16)
    return pl.pallas_call(
        functools.partial(_mixers_kernel, chunk_decay=chunk_decay),
        grid=(bsz, seq // t),
        in_specs=[seq_tile(d)] + ret_specs + gdn_specs + lru_specs,
        out_specs=[seq_tile(width), seq_tile(width), seq_tile(lru_w)],
        out_shape=[out, out, jax.ShapeDtypeStruct((bsz, seq, lru_w), BF16)],
        scratch_shapes=[pltpu.VMEM((N_HEADS, HEAD_DIM, HEAD_DIM), F32),
                        pltpu.VMEM((CARRY_ROWS + t, 3 * width), F32),
                        pltpu.VMEM((N_HEADS, HEAD_DIM, HEAD_DIM), F32),
                        pltpu.VMEM((t, width), F32),
                        pltpu.VMEM((t, width), F32),
                        pltpu.VMEM((t, width), BF16),
                        pltpu.VMEM((t, width), BF16),
                        pltpu.VMEM((t, width), BF16),
                        pltpu.VMEM((N_HEADS, t, GDN_CHUNK), BF16),
                        pltpu.VMEM((CARRY_ROWS + t, lru_w), F32),
                        pltpu.VMEM((1, lru_w), F32)],
        compiler_params=_params("parallel", "arbitrary"),
        name="token_mixers",
    )(h, w_ret, row(b_ret), cos, sin, dmat, qd, kd, row(ret_g), row(ret_b),
      w_gdn, row(b_gdn), w_small, row(b_small), gdn_cw, pad_row(a_log), pad_row(dt_bias), row(gdn_g),
      w_lru, row(b_lru), lru_cw, row(lru_cb), w_r, row(b_r), w_i, row(b_i), row(lam))


def _first_index_of_max(x, idx, size):
    m = jnp.max(x, axis=0, keepdims=True)
    first = jnp.min(jnp.where(x == m, idx, size), axis=0, keepdims=True)
    return m, idx == first


def _route(logits_t, bias_col, out):
    n_exp, t = logits_t.shape
    per_group = n_exp // N_GROUPS
    scores = _sigmoid(logits_t)
    sel = scores + bias_col
    idx_g = lax.broadcasted_iota(jnp.int32, (per_group, t), 0)
    group_scores = []
    for g in range(N_GROUPS):
        x = sel[g * per_group:(g + 1) * per_group, :]
        m1, hit = _first_index_of_max(x, idx_g, per_group)
        m2 = jnp.max(jnp.where(hit, NEG_INF, x), axis=0, keepdims=True)
        group_scores.append(m1 + m2)
        if g % 2 == 1:
            yield
    gsc = jnp.concatenate(group_scores, axis=0)
    idx_n = lax.broadcasted_iota(jnp.int32, (N_GROUPS, t), 0)
    gmask = jnp.zeros((N_GROUPS, t), F32)
    for _ in range(TOPK_GROUPS):
        _, hit = _first_index_of_max(gsc, idx_n, N_GROUPS)
        gmask = jnp.where(hit, 1.0, gmask)
        gsc = jnp.where(hit, NEG_INF, gsc)
    yield
    emask = jnp.concatenate([jnp.broadcast_to(gmask[g:g + 1, :], (per_group, t)) for g in range(N_GROUPS)], axis=0)
    cand = jnp.where(emask > 0.0, sel, NEG_INF)
    idx_e = lax.broadcasted_iota(jnp.int32, (n_exp, t), 0)
    picked = jnp.zeros((n_exp, t), F32)
    hits = []
    for _ in range(TOP_K):
        _, hit = _first_index_of_max(cand, idx_e, n_exp)
        hits.append(hit)
        picked = jnp.where(hit, 1.0, picked)
        cand = jnp.where(hit, NEG_INF, cand)
        yield
    gw = jnp.where(picked > 0.0, scores, 0.0)
    out.update(combine=gw / jnp.sum(gw, axis=0, keepdims=True) * ROUTED_SCALE, picked=picked, hits=hits, idx_e=idx_e)


def _pack_bf16_pairs(x):
    c = x.shape[1] // 2
    hi = pltpu.bitcast(x[:, :c].astype(BF16).astype(F32), jnp.int32)
    lo = pltpu.bitcast(x[:, c:].astype(BF16).astype(F32), jnp.int32)
    return hi | lax.shift_right_logical(lo, jnp.full(lo.shape, 16, jnp.int32))


def _unpack_bf16_pairs(w):
    hi = pltpu.bitcast(w & jnp.int32(-65536), F32)
    lo = pltpu.bitcast(lax.shift_left(w, jnp.full(w.shape, 16, jnp.int32)), F32)
    return hi, lo


def _merge_kernel(h_ref, yr_ref, yg_ref, yl_ref, wmg_ref, bmg_ref, wbr_ref, wout_ref, g_ref, b_ref,
                  rwt_ref, rb_ref, eye_ref, tri_ref, h1_ref, h1b_ref, xp_ref, eidx_ref, rank_ref, gwt_ref,
                  count_ref, *, alpha):
    @pl.when(pl.program_id(0) == 0)
    def _():
        count_ref[...] = jnp.zeros_like(count_ref)

    d = h_ref.shape[1]
    sub = eye_ref.shape[0]
    parts = [slice(s, s + sub) for s in range(0, h_ref.shape[0], sub)]
    pick = lambda hit, vals, zero: jnp.sum(jnp.where(hit, vals, zero), axis=0, keepdims=True)

    def dense(rows, logits):
        h = h_ref[rows, :]
        hb = h.astype(BF16)
        gates = _sigmoid(_dot(hb, wmg_ref[...]) + bmg_ref[...])
        yield
        mixed = None
        for n, y_ref in enumerate((yr_ref, yg_ref, yl_ref)):
            term = gates[:, n * d:(n + 1) * d] * _dot(y_ref[rows, :], wbr_ref[n])
            mixed = term if mixed is None else mixed + term
            yield
        mix = _dot(mixed.astype(BF16), wout_ref[...])
        yield
        h1 = _layer_norm(alpha * h + mix, g_ref[...], b_ref[...])
        h1b = h1.astype(BF16)
        h1_ref[rows, :] = h1
        h1b_ref[rows, :] = h1b
        xp_ref[rows, :] = _pack_bf16_pairs(h1)
        logits.append(_dot_nt(rwt_ref[...], h1b))
        yield

    def routing(rows, logit):
        res = {}
        yield from _route(logit, rb_ref[...], res)
        combine_t, picked, hits, idx_e = res["combine"], res["picked"], res["hits"], res["idx_e"]
        rank_full = count_ref[...] + _dot(picked.astype(BF16), tri_ref[...])
        count_ref[...] += jnp.sum(picked, axis=1, keepdims=True)
        eidx_ref[:, rows] = jnp.concatenate([pick(hit, idx_e, 0) for hit in hits], axis=0)
        yield
        rank_ref[:, rows] = jnp.concatenate([pick(hit, rank_full, 0.0) for hit in hits], axis=0).astype(jnp.int32)
        gw = jnp.concatenate([pick(hit, combine_t, 0.0) for hit in hits], axis=0)
        gwt_ref[rows, :] = lax.dot_general(eye_ref[...], gw, (((1,), (1,)), ((), ())),
                                           preferred_element_type=F32, precision=lax.Precision.HIGHEST)
        yield

    def interleave(gens):
        while gens:
            gens = [g for g in gens if next(g, StopIteration) is not StopIteration]

    logits = [[] for _ in parts]
    interleave([dense(rows, logits[s]) for s, rows in enumerate(parts)])
    interleave([routing(rows, logits[s][0]) for s, rows in enumerate(parts)])


def _merge(h2, y_ret, y_gdn, y_lru, w_mg, b_mg, w_branch, w_out, ln_g, ln_b, router_wt, router_b, alpha):
    n, d = h2.shape
    t = min(MERGE_TILE, n)
    width = y_ret.shape[1]
    n_exp = router_wt.shape[0]
    tile = lambda cols: pl.BlockSpec((t, cols), lambda i: (i, 0))
    lane_tile = pl.BlockSpec((TOP_K, t), lambda i: (0, i))
    sub = min(MERGE_SUB, t)
    tri = jnp.triu(jnp.ones((sub, sub), BF16), 1)
    return pl.pallas_call(
        functools.partial(_merge_kernel, alpha=alpha),
        grid=(n // t,),
        in_specs=[
            tile(d), tile(width), tile(width), tile(width),
            _const_spec((d, N_BRANCHES * d)),
            _const_spec((1, N_BRANCHES * d)),
            _const_spec((N_BRANCHES, width, d)),
            _const_spec((d, d)),
            _const_spec((1, d)),
            _const_spec((1, d)),
            _const_spec((n_exp, d)),
            _const_spec((n_exp, 1)),
            _const_spec((sub, sub)),
            _const_spec((sub, sub)),
        ],
        out_specs=[tile(d), tile(d), tile(d // 2), lane_tile, lane_tile, tile(TOP_K),
                   pl.BlockSpec((n_exp, 1), lambda i: (0, 0))],
        out_shape=[jax.ShapeDtypeStruct((n, d), F32), jax.ShapeDtypeStruct((n, d), BF16),
                   jax.ShapeDtypeStruct((n, d // 2), jnp.int32),
                   jax.ShapeDtypeStruct((TOP_K, n), jnp.int32), jax.ShapeDtypeStruct((TOP_K, n), jnp.int32),
                   jax.ShapeDtypeStruct((n, TOP_K), F32), jax.ShapeDtypeStruct((n_exp, 1), F32)],
        compiler_params=_params("arbitrary"),
        name="merge_route",
    )(h2, y_ret, y_gdn, y_lru, w_mg, b_mg.reshape(1, -1), w_branch, w_out, ln_g.reshape(1, -1),
      ln_b.reshape(1, -1), router_wt, router_b.reshape(-1, 1), jnp.eye(sub, dtype=F32), tri)


def _dest_kernel(start_ref, eidx_ref, rank_ref, o_ref, *, n_exp):
    eidx = eidx_ref[...]
    dest = rank_ref[...]
    for e in range(n_exp):
        dest = dest + jnp.where(eidx == e, start_ref[e], 0)
    o_ref[...] = dest


def _dest_rows(seg_start, eidx, rank):
    k, n = eidx.shape
    t = min(DEST_TILE, n)
    blk = pl.BlockSpec((k, t), lambda i, s: (0, i))
    return pl.pallas_call(
        functools.partial(_dest_kernel, n_exp=seg_start.shape[0]),
        grid_spec=pltpu.PrefetchScalarGridSpec(num_scalar_prefetch=1, grid=(n // t,), in_specs=[blk, blk],
                                               out_specs=blk),
        out_shape=jax.ShapeDtypeStruct((k, n), jnp.int32),
        compiler_params=_params("parallel"),
        name="dest_rows",
    )(seg_start, eidx, rank)


def _sc_workers():
    info = plsc.get_sparse_core_info()
    return info.num_cores, info.num_subcores


def _sc_mesh():
    return plsc.VectorSubcoreMesh(core_axis_name="c", subcore_axis_name="s")


def _sc_scatter_rows(x, dest, n_rows):
    n, c = x.shape
    k = dest.shape[0]
    n_cores, n_sub = _sc_workers()
    per_worker = n // (n_cores * n_sub)
    steps = per_worker // SC_WINDOW

    @functools.partial(pl.kernel, out_type=jax.ShapeDtypeStruct((n_rows, c), x.dtype), mesh=_sc_mesh(),
                       scratch_types=[pltpu.VMEM((k, SC_WINDOW), jnp.int32), pltpu.VMEM((SC_WINDOW, c), x.dtype),
                                      pltpu.SemaphoreType.DMA, pltpu.SemaphoreType.DMA],
                       name="dispatch_rows")
    def scatter(x_hbm, d_hbm, o_hbm, idx_v, rows_v, sem, row_sem):
        base = (lax.axis_index("s") * n_cores + lax.axis_index("c")) * per_worker

        @pl.loop(0, steps)
        def _(j):
            off = base + j * SC_WINDOW
            loads = [pltpu.async_copy(x_hbm.at[pl.ds(off, SC_WINDOW)], rows_v, row_sem)]
            loads += [pltpu.async_copy(d_hbm.at[pl.ds(kk * n + off, SC_WINDOW)], idx_v.at[kk], sem) for kk in range(k)]
            for cp in loads:
                cp.wait()
            copies = [pltpu.async_copy(rows_v, o_hbm.at[idx_v.at[kk]], sem) for kk in range(k)]
            for cp in copies:
                cp.wait()

    return scatter(x, dest.reshape(k * n))


def _sc_gather_rows(table, idx):
    b = idx.shape[0]
    c = table.shape[1]
    n_cores, n_sub = _sc_workers()
    per_worker = b // (n_cores * n_sub)
    steps = per_worker // SC_WINDOW

    half = SC_WINDOW // 2
    half_buf = lambda dtype, *shape: pltpu.VMEM((half,) + shape, dtype)

    @functools.partial(pl.kernel, out_type=jax.ShapeDtypeStruct((b, c), table.dtype), mesh=_sc_mesh(),
                       scratch_types=[half_buf(jnp.int32), half_buf(jnp.int32), half_buf(table.dtype, c),
                                      half_buf(table.dtype, c), pltpu.SemaphoreType.DMA, pltpu.SemaphoreType.DMA,
                                      pltpu.SemaphoreType.DMA, pltpu.SemaphoreType.DMA],
                       name="collect_rows")
    def gather(t_hbm, i_hbm, o_hbm, idx_a, idx_b, rows_a, rows_b, gat_a, gat_b, put_a, put_b):
        base = (lax.axis_index("s") * n_cores + lax.axis_index("c")) * per_worker

        @pl.loop(0, steps)
        def _(j):
            off_a = base + j * SC_WINDOW
            off_b = off_a + half
            pltpu.sync_copy(i_hbm.at[pl.ds(off_a, half)], idx_a)
            in_a = pltpu.async_copy(t_hbm.at[idx_a], rows_a, gat_a)
            pltpu.sync_copy(i_hbm.at[pl.ds(off_b, half)], idx_b)
            in_b = pltpu.async_copy(t_hbm.at[idx_b], rows_b, gat_b)
            in_a.wait()
            out_a = pltpu.async_copy(rows_a, o_hbm.at[pl.ds(off_a, half)], put_a)
            in_b.wait()
            out_b = pltpu.async_copy(rows_b, o_hbm.at[pl.ds(off_b, half)], put_b)
            out_a.wait()
            out_b.wait()

    return gather(table, idx)


def _expert_block_kernel(meta_ref, xs_ref, wgu_ref, wd_ref, ys_ref, wgu_b, wd_b, *, n_blocks):
    i = pl.program_id(0)

    @pl.when((i == 0) | (meta_ref[i] != meta_ref[jnp.maximum(i - 1, 0)]))
    def _():
        wgu_b[...] = wgu_ref[0, 0].astype(BF16)
        wd_b[...] = wd_ref[0, 0].astype(BF16)

    @pl.when(i < meta_ref[n_blocks])
    def _():
        half = xs_ref.shape[1]
        ff = wd_b.shape[0]

        def sub_block(rows):
            hi, lo = _unpack_bf16_pairs(xs_ref[rows, :])
            hib, lob = hi.astype(BF16), lo.astype(BF16)
            yield
            gu = _dot(hib, wgu_b[:half, :]) + _dot(lob, wgu_b[half:, :])
            yield
            mid = (_silu(gu[:, :ff]) * gu[:, ff:]).astype(BF16)
            yield
            out = _dot(mid, wd_b[...])
            yield
            ys_ref[rows, :] = _pack_bf16_pairs(out)
            yield

        waiting = [sub_block(slice(r, r + EXPERT_SUB)) for r in range(0, xs_ref.shape[0], EXPERT_SUB)]
        active = []
        while waiting or active:
            if waiting:
                active.append(waiting.pop(0))
            active = [g for g in active if next(g, StopIteration) is not StopIteration]


def _expert_blocks(meta, xs, w_gu, w_down, layer):
    rows, half = xs.shape
    _, _, d, ff2 = w_gu.shape
    n_blocks = rows // MOE_BLOCK
    row_blk = pl.BlockSpec((MOE_BLOCK, half), lambda i, meta: (jnp.minimum(i, meta[n_blocks] - 1), 0))
    return pl.pallas_call(
        functools.partial(_expert_block_kernel, n_blocks=n_blocks),
        grid_spec=pltpu.PrefetchScalarGridSpec(
            num_scalar_prefetch=1, grid=(n_blocks,),
            in_specs=[row_blk,
                      pl.BlockSpec((1, 1, d, ff2), lambda i, meta: (layer, meta[i], 0, 0)),
                      pl.BlockSpec((1, 1, ff2 // 2, d), lambda i, meta: (layer, meta[i], 0, 0))],
            out_specs=row_blk,
            scratch_shapes=[pltpu.VMEM((d, ff2), BF16), pltpu.VMEM((ff2 // 2, d), BF16)]),
        out_shape=jax.ShapeDtypeStruct((rows, half), jnp.int32),
        compiler_params=_params("arbitrary"),
        name="expert_blocks",
    )(meta, xs, w_gu, w_down)


def _routed_rows(xp, eidx, rank, counts, w_gu, w_down, layer):
    n = xp.shape[0]
    k = eidx.shape[0]
    n_exp = w_gu.shape[1]
    counts = counts.reshape(n_exp).astype(jnp.int32)
    padded = (counts + MOE_BLOCK - 1) // MOE_BLOCK * MOE_BLOCK
    seg_end = jnp.cumsum(padded)
    n_blocks = k * n // MOE_BLOCK + n_exp
    blk_start = jnp.arange(n_blocks, dtype=jnp.int32) * MOE_BLOCK
    blk_expert = jnp.minimum(jnp.sum(seg_end[None, :] <= blk_start[:, None], axis=1), n_exp - 1)
    meta = jnp.concatenate([blk_expert, seg_end[-1:] // MOE_BLOCK]).astype(jnp.int32)
    dest = _dest_rows(seg_end - padded, eidx, rank)
    xs = _sc_scatter_rows(xp, dest, n_blocks * MOE_BLOCK)
    return _expert_blocks(meta, xs, w_gu, w_down, layer), dest


def _shared_ple_kernel(h1_ref, h1b_ref, p_ref, wgu_ref, wd_ref, wpg_ref, bpg_ref, wpe_ref, o_ref, *, alpha):
    xb = h1b_ref[...]
    ff = wd_ref.shape[0]
    gu = _dot(xb, wgu_ref[...])
    shared = _dot((_silu(gu[:, :ff]) * gu[:, ff:]).astype(BF16), wd_ref[...])
    ple = _sigmoid(_dot(xb, wpg_ref[...]) + bpg_ref[...]) * _dot(p_ref[0].astype(BF16), wpe_ref[...])
    o_ref[...] = alpha * h1_ref[...] + shared + ple


def _shared_ple(h1, h1b, p3, layer, sh_w_gu, sh_w_down, ple_w_g, ple_b_g, ple_w_e, alpha):
    n, d = h1.shape
    t = min(ROW_TILE, n)
    pdim = p3.shape[2]
    ff2 = sh_w_gu.shape[1]
    tile = lambda cols: pl.BlockSpec((t, cols), lambda i: (i, 0))
    return pl.pallas_call(
        functools.partial(_shared_ple_kernel, alpha=alpha),
        grid=(n // t,),
        in_specs=[
            tile(d), tile(d), pl.BlockSpec((1, t, pdim), lambda i: (layer, i, 0)),
            _const_spec((d, ff2)),
            _const_spec((ff2 // 2, d)),
            _const_spec((d, d)),
            _const_spec((1, d)),
            _const_spec((pdim, d)),
        ],
        out_specs=tile(d),
        out_shape=jax.ShapeDtypeStruct((n, d), F32),
        compiler_params=_params("parallel"),
        name="shared_ple",
    )(h1, h1b, p3, sh_w_gu, sh_w_down, ple_w_g, ple_b_g.reshape(1, -1), ple_w_e)


def _combine_norm_kernel(pre_ref, yk_ref, gwt_ref, g_ref, b_ref, o_ref):
    gwt = gwt_ref[...]
    routed_hi = routed_lo = None
    for k in range(yk_ref.shape[0]):
        hi, lo = _unpack_bf16_pairs(yk_ref[k])
        wk = gwt[:, k:k + 1]
        routed_hi = hi * wk if routed_hi is None else routed_hi + hi * wk
        routed_lo = lo * wk if routed_lo is None else routed_lo + lo * wk
    routed = jnp.concatenate([routed_hi, routed_lo], axis=1)
    o_ref[...] = _layer_norm(pre_ref[...] + routed, g_ref[...], b_ref[...])


def _combine_norm(pre, ys, dest, gwt, ln_g, ln_b):
    n, d = pre.shape
    top_k = dest.shape[0]
    t = min(ROW_TILE, n)
    yk = _sc_gather_rows(ys, dest.reshape(top_k * n)).reshape(top_k, n, d // 2)
    tile = lambda cols: pl.BlockSpec((t, cols), lambda i: (i, 0))
    return pl.pallas_call(
        _combine_norm_kernel,
        grid=(n // t,),
        in_specs=[tile(d), pl.BlockSpec((top_k, t, d // 2), lambda i: (0, i, 0)), tile(top_k),
                  _const_spec((1, d)), _const_spec((1, d))],
        out_specs=tile(d),
        out_shape=jax.ShapeDtypeStruct((n, d), F32),
        compiler_params=_params("parallel"),
        name="combine_norm",
    )(pre, yk, gwt, ln_g.reshape(1, -1), ln_b.reshape(1, -1))


def _block_diag(w):
    g, i, j = w.shape
    eye = jnp.eye(g, dtype=w.dtype)
    return (eye[:, None, :, None] * w[:, :, None, :]).reshape(g * i, g * j)


def kernel(x, p, ln_in_g, ln_in_b, w_in, b_in, ret_norm_g, ret_norm_b, gdn_conv_w, gdn_a_log, gdn_dt_bias, gdn_norm_g, lru_conv_w, lru_conv_b, lru_w_r, lru_b_r, lru_w_i, lru_b_i, lru_lambda, w_branch, w_out, ln1_g, ln1_b, router_w, router_b, exp_w_gu, exp_w_down, sh_w_gu, sh_w_down, ple_w_e, ple_w_g, ple_b_g, ln2_g, ln2_b):
    bsz, seq, d = x.shape
    depth = w_in.shape[0]
    n = bsz * seq
    width = N_HEADS * HEAD_DIM
    alpha = (2 * depth) ** 0.25
    lanes = 128
    o_ret = 0
    o_gdn = o_ret + 4 * width
    o_small = o_gdn + 4 * width
    o_lru = o_small + 2 * N_HEADS
    o_mg = o_lru + 2 * width

    p3 = p.reshape(depth, n, -1)
    bf = lambda a: a.astype(BF16)
    h = _entry_norm(x.reshape(n, d), ln_in_g, ln_in_b)
    for l in range(depth):
        wl, bl = w_in[l], b_in[l]
        w_small = bf(jnp.zeros((d, lanes), F32).at[:, :2 * N_HEADS].set(wl[:, o_small:o_lru]))
        b_small = jnp.zeros((lanes,), F32).at[:2 * N_HEADS].set(bl[o_small:o_lru])
        w_ret, w_gdn, w_lru, w_mg = bf(wl[:, o_ret:o_gdn]), bf(wl[:, o_gdn:o_small]), bf(wl[:, o_lru:o_mg]), bf(wl[:, o_mg:])
        w_r, w_i = bf(_block_diag(lru_w_r[l])), bf(_block_diag(lru_w_i[l]))
        w_br, w_o, r_wt = bf(w_branch[l]), bf(w_out[l]), bf(router_w[l].T)
        w_sgu, w_sd, w_pg, w_pe = bf(sh_w_gu[l]), bf(sh_w_down[l]), bf(ple_w_g[l]), bf(ple_w_e[l])
        y_ret, y_gdn, y_lru = _mixers(
            h.reshape(bsz, seq, d), (w_ret, bl[o_ret:o_gdn], ret_norm_g[l], ret_norm_b[l]),
            (w_gdn, bl[o_gdn:o_small], w_small, b_small, gdn_conv_w[l], gdn_a_log[l], gdn_dt_bias[l], gdn_norm_g[l]),
            (w_lru, bl[o_lru:o_mg], lru_conv_w[l], lru_conv_b[l], w_r, lru_b_r[l], w_i, lru_b_i[l], lru_lambda[l]))
        h1, h1b, xp, eidx, rank, gwt, counts = _merge(
            h, y_ret.reshape(n, width), y_gdn.reshape(n, width), y_lru.reshape(n, width), w_mg, bl[o_mg:],
            w_br, w_o, ln1_g[l], ln1_b[l], r_wt, router_b[l], alpha)
        pre = _shared_ple(h1, h1b, p3, l, w_sgu, w_sd, w_pg, ple_b_g[l], w_pe, alpha)
        ys, dest = _routed_rows(xp, eidx, rank, counts, exp_w_gu, exp_w_down, l)
        h = _combine_norm(pre, ys, dest, gwt, ln2_g[l], ln2_b[l])
    return h.reshape(bsz, seq, d)
```

```python
import functools

import numpy as np
import jax
import jax.numpy as jnp
from jax import lax
from jax.experimental import pallas as pl
from jax.experimental.pallas import tpu as pltpu
from jax.experimental.pallas import tpu_sc as plsc

F32 = jnp.float32
BF16 = jnp.bfloat16

HEAD_DIM = 128
N_HEADS = 4
LRU_C = 8.0
CONV_WIDTH = 4
N_BRANCHES = 3
ROPE_BASE = 10000.0
N_GROUPS = 8
TOPK_GROUPS = 4
TOP_K = 8
ROUTED_SCALE = 2.5
LN_EPS = 1e-5
GDN_CHUNK = 128
GDN_BASE = 16
SEQ_TILE = 512
MERGE_TILE = 512
MERGE_SUB = 128
ROW_TILE = 512
MOE_BLOCK = 1024
EXPERT_SUB = 256
DEST_TILE = 4096
SC_WINDOW = 128
CARRY_ROWS = 8
SCAN_GROUP = 16
VMEM_LIMIT_BYTES = 56 * 1024 * 1024
NEG_INF = float("-inf")


def _const_spec(shape):
    nd = len(shape)
    return pl.BlockSpec(shape, lambda *_: (0,) * nd, pipeline_mode=pl.Buffered(1))


def _params(*sem):
    return pltpu.CompilerParams(dimension_semantics=sem, vmem_limit_bytes=VMEM_LIMIT_BYTES)


def _layer_norm(x, g, b):
    mu = jnp.mean(x, axis=-1, keepdims=True)
    xc = x - mu
    var = jnp.mean(xc * xc, axis=-1, keepdims=True)
    return xc * lax.rsqrt(var + LN_EPS) * g + b


def _sigmoid(x):
    return 1.0 / (1.0 + jnp.exp(-x))


def _silu(x):
    return x * _sigmoid(x)


def _softplus(x):
    return jnp.maximum(x, 0.0) + jnp.log1p(jnp.exp(-jnp.abs(x)))


def _dot(a, b):
    return jnp.dot(a, b, preferred_element_type=F32)


def _dot_nt(a, b):
    return lax.dot_general(a, b, (((1,), (1,)), ((), ())), preferred_element_type=F32)


def _dot_tn(a, b):
    return lax.dot_general(a, b, (((0,), (0,)), ((), ())), preferred_element_type=F32)


def _ln_kernel(x_ref, g_ref, b_ref, o_ref):
    o_ref[...] = _layer_norm(x_ref[...], g_ref[...], b_ref[...])


def _entry_norm(x2, g, b):
    n, d = x2.shape
    t = min(1024, n)
    return pl.pallas_call(
        _ln_kernel,
        grid=(n // t,),
        in_specs=[pl.BlockSpec((t, d), lambda i: (i, 0)), _const_spec((1, d)), _const_spec((1, d))],
        out_specs=pl.BlockSpec((t, d), lambda i: (i, 0)),
        out_shape=jax.ShapeDtypeStruct((n, d), F32),
        compiler_params=_params("parallel"),
        name="entry_norm",
    )(x2, g.reshape(1, d), b.reshape(1, d))


def _retention_body(hb, w_ref, b_ref, cos_ref, sin_ref, dmat_ref, qd_ref, kd_ref, ng_ref, nb_ref,
                    y_ref, state_ref, *, chunk_decay):
    width = N_HEADS * HEAD_DIM
    proj = _dot(hb, w_ref[...]) + b_ref[...]
    yield
    cos = cos_ref[...]
    sin = sin_ref[...]
    heads = range(N_HEADS)
    qs, ks, vbs = [], [], []
    for hh in heads:
        lo = hh * HEAD_DIM
        q = proj[:, lo:lo + HEAD_DIM]
        k = proj[:, width + lo:width + lo + HEAD_DIM]
        qs.append(q * cos + pltpu.roll(q, HEAD_DIM // 2, axis=1) * sin)
        ks.append((k * cos + pltpu.roll(k, HEAD_DIM // 2, axis=1) * sin) * (HEAD_DIM ** -0.5))
        vbs.append(proj[:, 2 * width + lo:2 * width + lo + HEAD_DIM].astype(BF16))
        yield
    states = [state_ref[hh] for hh in heads]
    scores, inter, outs = [], [], []
    for hh in heads:
        scores.append((_dot_nt(qs[hh].astype(BF16), ks[hh].astype(BF16)) * dmat_ref[hh]).astype(BF16))
        yield
    for hh in heads:
        inter.append(_dot((qs[hh] * qd_ref[hh]).astype(BF16), states[hh].astype(BF16)))
        yield
    for hh in heads:
        state_ref[hh] = states[hh] * chunk_decay[hh] + _dot_tn((ks[hh] * kd_ref[hh]).astype(BF16), vbs[hh])
        yield
    for hh in heads:
        outs.append(_dot(scores[hh], vbs[hh]) + inter[hh])
        yield
    for hh in heads:
        lo = hh * HEAD_DIM
        o = outs[hh]
        gate = proj[:, 3 * width + lo:3 * width + lo + HEAD_DIM]
        mu = jnp.mean(o, axis=-1, keepdims=True)
        oc = o - mu
        var = jnp.mean(oc * oc, axis=-1, keepdims=True)
        on = oc * lax.rsqrt(var + LN_EPS) * ng_ref[:, lo:lo + HEAD_DIM] + nb_ref[:, lo:lo + HEAD_DIM]
        y_ref[:, lo:lo + HEAD_DIM] = (_silu(gate) * on).astype(y_ref.dtype)
        yield


def _retention_tables(seq, tile):
    half = HEAD_DIM // 2
    inv_freq = ROPE_BASE ** (-np.linspace(0.0, 1.0, half))
    ang = np.arange(seq)[:, None] * inv_freq[None, :]
    cos = np.concatenate([np.cos(ang), np.cos(ang)], axis=1)
    sin = np.concatenate([-np.sin(ang), np.sin(ang)], axis=1)
    log_gamma = np.log1p(-np.exp2(-5.0 - np.arange(N_HEADS)))
    pos = np.arange(tile)
    diff = pos[:, None] - pos[None, :]
    dmat = np.where(diff >= 0, np.exp(log_gamma[:, None, None] * np.maximum(diff, 0)), 0.0)
    qd = np.exp(log_gamma[:, None] * (pos + 1.0))[:, :, None] * np.ones((1, 1, HEAD_DIM))
    kd = np.exp(log_gamma[:, None] * (tile - 1.0 - pos))[:, :, None] * np.ones((1, 1, HEAD_DIM))
    chunk_decay = tuple(float(c) for c in np.exp(log_gamma * tile))
    as32 = lambda a: jnp.asarray(a, F32)
    return as32(cos), as32(sin), as32(dmat), as32(qd), as32(kd), chunk_decay


def _causal_conv(x, xs_ref, cw_ref):
    t = x.shape[0]
    xs_ref[CARRY_ROWS:, :] = x
    ext = xs_ref[...]
    acc = ext * cw_ref[0:1, :]
    for j in range(1, CONV_WIDTH):
        acc = pltpu.roll(acc, 1, axis=0) + ext * cw_ref[j:j + 1, :]
    xs_ref[0:CARRY_ROWS, :] = xs_ref[t:t + CARRY_ROWS, :]
    return acc[CARRY_ROWS:, :]


def _cumsum_rows(x):
    n = x.shape[0]
    row = lax.broadcasted_iota(jnp.int32, x.shape, 0)
    d = 1
    while d < n:
        x = x + jnp.where(row >= d, pltpu.roll(x, d, axis=0), 0.0)
        d *= 2
    return x


def _gdn_body(hb, w_ref, b_ref, ws_ref, bs_ref, cw_ref, alog_ref, dtb_ref, ng_ref,
              y_ref, xs_ref, state_ref, u_ref, wf_ref, w_s_ref, qd_ref, kd_ref, qk_ref):
    width = N_HEADS * HEAD_DIM
    c = GDN_CHUNK
    t = hb.shape[0]
    proj = _dot(hb, w_ref[...]) + b_ref[...]
    small = _dot(hb, ws_ref[...]) + bs_ref[...]
    yield
    qkv = _silu(_causal_conv(proj[:, :3 * width], xs_ref, cw_ref))
    yield
    beta_all = _sigmoid(small)
    la_all = -jnp.exp(alog_ref[...]) * _softplus(small + dtb_ref[...])

    ri = lax.broadcasted_iota(jnp.int32, (c, c), 0)
    ci = lax.broadcasted_iota(jnp.int32, (c, c), 1)
    lower = ri >= ci
    strict = ri > ci
    base_blocks = (ri // GDN_BASE) == (ci // GDN_BASE)

    items = [(n, hh) for n in range(t // c) for hh in range(N_HEADS)]
    gcs = {}
    for n in range(t // c):
        la_c = la_all[n * c:(n + 1) * c, :]
        gc_c = _cumsum_rows(la_c)
        gcs[n] = (la_c, gc_c, jnp.exp(gc_c))
    g_last, negs, pws, rems = {}, {}, {}, {}
    for n, hh in items:
        r0, lo = n * c, hh * HEAD_DIM
        rows, cols = slice(r0, r0 + c), slice(lo, lo + HEAD_DIM)
        la_c, gc_c, egc_c = gcs[n]
        q = qkv[rows, lo:lo + HEAD_DIM]
        k = qkv[rows, width + lo:width + lo + HEAD_DIM]
        v = qkv[rows, 2 * width + lo:2 * width + lo + HEAD_DIM]
        q = q * lax.rsqrt(jnp.sum(q * q, axis=-1, keepdims=True) + 1e-6) * (HEAD_DIM ** -0.5)
        k = k * lax.rsqrt(jnp.sum(k * k, axis=-1, keepdims=True) + 1e-6)
        beta = beta_all[rows, hh:hh + 1]
        la = la_c[:, N_HEADS + hh:N_HEADS + hh + 1]
        gc = gc_c[:, N_HEADS + hh:N_HEADS + hh + 1]
        egc = egc_c[:, N_HEADS + hh:N_HEADS + hh + 1]
        gc_row = jnp.sum(jnp.where(ri <= ci, jnp.broadcast_to(la, (c, c)), 0.0), axis=0, keepdims=True)
        gc_last = gc_row[:, c - 1:c]
        decay = jnp.where(lower, jnp.exp(jnp.where(lower, gc - gc_row, 0.0)), 0.0)
        kb = k * beta
        kbf = k.astype(BF16)
        a_neg = jnp.where(strict, -(_dot_nt(kb.astype(BF16), kbf) * decay), 0.0)
        negs[n, hh] = a_neg
        pws[n, hh] = rems[n, hh] = jnp.where(base_blocks, a_neg, 0.0)
        u_ref[rows, cols] = v * beta
        wf_ref[rows, cols] = kb * egc
        qk_ref[hh, rows, :] = (_dot_nt(q.astype(BF16), kbf) * decay).astype(BF16)
        qd_ref[rows, cols] = (q * egc).astype(BF16)
        kd_ref[rows, cols] = (k * jnp.exp(gc_last - gc)).astype(BF16)
        g_last[n, hh] = jnp.exp(gc_last)
        yield
    m = 2
    while m < GDN_BASE:
        for it in items:
            pwb = pws[it].astype(BF16)
            pws[it] = _dot(pwb, pwb)
        yield
        for it in items:
            rems[it] = rems[it] + pws[it] + _dot(rems[it].astype(BF16), pws[it].astype(BF16))
        yield
        m *= 2
    size = GDN_BASE
    while size < c:
        pair = ((ri // (2 * size)) == (ci // (2 * size))) & ((ri // size) != (ci // size))
        mids = {}
        for it in items:
            link = jnp.where(pair, negs[it], 0.0)
            mids[it] = link + _dot(rems[it].astype(BF16), link.astype(BF16))
        yield
        for it in items:
            rems[it] = rems[it] + mids[it] + _dot(mids[it].astype(BF16), rems[it].astype(BF16))
        yield
        size *= 2
    for n, hh in items:
        rows, cols = slice(n * c, (n + 1) * c), slice(hh * HEAD_DIM, (hh + 1) * HEAD_DIM)
        remb = rems[n, hh].astype(BF16)
        u_ref[rows, cols] = u_ref[rows, cols] + _dot(remb, u_ref[rows, cols].astype(BF16))
        w_s_ref[rows, cols] = (wf_ref[rows, cols] + _dot(remb, wf_ref[rows, cols].astype(BF16))).astype(BF16)
    yield

    heads = range(N_HEADS)
    for n in range(t // c):
        rows = slice(n * c, (n + 1) * c)
        cols = [slice(hh * HEAD_DIM, (hh + 1) * HEAD_DIM) for hh in heads]
        states = [state_ref[hh] for hh in heads]
        sbs = [s.astype(BF16) for s in states]
        vnbs = [(u_ref[rows, cols[hh]] - _dot(w_s_ref[rows, cols[hh]], sbs[hh])).astype(BF16) for hh in heads]
        yield
        outs = [_dot(qd_ref[rows, cols[hh]], sbs[hh]) + _dot(qk_ref[hh, rows, :], vnbs[hh]) for hh in heads]
        for hh in heads:
            state_ref[hh] = states[hh] * g_last[n, hh] + _dot_tn(kd_ref[rows, cols[hh]], vnbs[hh])
        yield
        for hh in heads:
            o = outs[hh]
            o = o * lax.rsqrt(jnp.mean(o * o, axis=-1, keepdims=True) + 1e-6) * ng_ref[...]
            og = proj[rows, 3 * width + hh * HEAD_DIM:3 * width + (hh + 1) * HEAD_DIM]
            y_ref[rows, cols[hh]] = (o * _silu(og)).astype(y_ref.dtype)
        yield


def _lru_body(hb, w_ref, b_ref, cw_ref, cb_ref, wr_ref, br_ref, wi_ref, bi_ref, lam_ref,
              y_ref, xs_ref, carry_ref):
    width = cw_ref.shape[1]
    t = hb.shape[0]
    proj = _dot(hb, w_ref[...]) + b_ref[...]
    yield
    xc = _causal_conv(proj[:, :width], xs_ref, cw_ref) + cb_ref[...]
    xcb = xc.astype(BF16)
    yield
    r = _sigmoid(_dot(xcb, wr_ref[...]) + br_ref[...])
    gi = _sigmoid(_dot(xcb, wi_ref[...]) + bi_ref[...])
    yield
    log_a = -LRU_C * r * _softplus(-lam_ref[...])
    a = jnp.exp(log_a)
    th = jnp.tanh(log_a)
    hs = jnp.sqrt(-2.0 * th / (1.0 - th)) * (gi * xc)
    row = lax.broadcasted_iota(jnp.int32, (t, width), 0) % SCAN_GROUP
    d = 1
    while d < SCAN_GROUP:
        keep = row >= d
        hs = hs + a * jnp.where(keep, pltpu.roll(hs, d, axis=0), 0.0)
        a = a * jnp.where(keep, pltpu.roll(a, d, axis=0), 1.0)
        d *= 2
        yield
    gate = jax.nn.gelu(proj[:, width:], approximate=True)
    carry = carry_ref[...]
    for g in range(t // SCAN_GROUP):
        rows = slice(g * SCAN_GROUP, (g + 1) * SCAN_GROUP)
        hg = hs[rows, :] + a[rows, :] * carry
        carry = hg[SCAN_GROUP - 1:SCAN_GROUP, :]
        y_ref[rows, :] = (gate[rows, :] * hg).astype(y_ref.dtype)
        if g % 8 == 7:
            yield
    carry_ref[...] = carry


N_RET_IN, N_GDN_IN, N_LRU_IN = 9, 8, 9
MIX_STRIDE = (1, 2, 2)


def _mixers_kernel(h_ref, *refs, chunk_decay):
    ret_in, refs = refs[:N_RET_IN], refs[N_RET_IN:]
    gdn_in, refs = refs[:N_GDN_IN], refs[N_GDN_IN:]
    lru_in, refs = refs[:N_LRU_IN], refs[N_LRU_IN:]
    yr_ref, yg_ref, yl_ref = refs[:3]
    ret_state, gdn_xs, gdn_state, u_ref, wf_ref, w_s_ref, qd_ref, kd_ref, qk_ref, lru_xs, lru_carry = refs[3:]

    @pl.when(pl.program_id(1) == 0)
    def _():
        ret_state[...] = jnp.zeros_like(ret_state)
        gdn_state[...] = jnp.zeros_like(gdn_state)
        lru_carry[...] = jnp.zeros_like(lru_carry)
        gdn_xs[0:CARRY_ROWS, :] = jnp.zeros((CARRY_ROWS, gdn_xs.shape[1]), F32)
        lru_xs[0:CARRY_ROWS, :] = jnp.zeros((CARRY_ROWS, lru_xs.shape[1]), F32)

    hb = h_ref[0].astype(BF16)
    branches = [
        (_gdn_body(hb, *gdn_in, yg_ref.at[0], gdn_xs, gdn_state, u_ref, wf_ref, w_s_ref, qd_ref, kd_ref, qk_ref),
         MIX_STRIDE[0]),
        (_retention_body(hb, *ret_in, yr_ref.at[0], ret_state, chunk_decay=chunk_decay), MIX_STRIDE[1]),
        (_lru_body(hb, *lru_in, yl_ref.at[0], lru_xs, lru_carry), MIX_STRIDE[2]),
    ]
    tick = 0
    while branches:
        for gen, stride in list(branches):
            if tick % stride == 0 and next(gen, StopIteration) is StopIteration:
                branches.remove((gen, stride))
        tick += 1


def _mixers(h, ret_args, gdn_args, lru_args):
    bsz, seq, d = h.shape
    t = min(SEQ_TILE, seq)
    width = N_HEADS * HEAD_DIM
    row = lambda vec: vec.reshape(1, -1)
    w_ret, b_ret, ret_g, ret_b = ret_args
    w_gdn, b_gdn, w_small, b_small, gdn_cw, a_log, dt_bias, gdn_g = gdn_args
    w_lru, b_lru, lru_cw, lru_cb, w_r, b_r, w_i, b_i, lam = lru_args
    lanes = w_small.shape[1]
    lru_w = lru_cw.shape[1]
    cos, sin, dmat, qd, kd, chunk_decay = _retention_tables(seq, t)
    pad_row = lambda vec: jnp.zeros((1, lanes), F32).at[0, N_HEADS:2 * N_HEADS].set(vec.astype(F32))
    seq_tile = lambda cols: pl.BlockSpec((1, t, cols), lambda i, j: (i, j, 0))
    pos_tile = pl.BlockSpec((t, HEAD_DIM), lambda i, j: (j, 0))
    ret_specs = [_const_spec((d, 4 * width)), _const_spec((1, 4 * width)), pos_tile, pos_tile,
                 _const_spec((N_HEADS, t, t)), _const_spec((N_HEADS, t, HEAD_DIM)), _const_spec((N_HEADS, t, HEAD_DIM)),
                 _const_spec((1, width)), _const_spec((1, width))]
    gdn_specs = [_const_spec((d, 4 * width)), _const_spec((1, 4 * width)), _const_spec((d, lanes)),
                 _const_spec((1, lanes)), _const_spec((CONV_WIDTH, 3 * width)), _const_spec((1, lanes)),
                 _const_spec((1, lanes)), _const_spec((1, HEAD_DIM))]
    lru_specs = [_const_spec((d, 2 * lru_w)), _const_spec((1, 2 * lru_w)), _const_spec((CONV_WIDTH, lru_w)),
                 _const_spec((1, lru_w)), _const_spec((lru_w, lru_w)), _const_spec((1, lru_w)),
                 _const_spec((lru_w, lru_w)), _const_spec((1, lru_w)), _const_spec((1, lru_w))]
    assert (len(ret_specs), len(gdn_specs), len(lru_specs)) == (N_RET_IN, N_GDN_IN, N_LRU_IN)
    out = jax.ShapeDtypeStruct((bsz, seq, width), BF16)
    return pl.pallas_call(
        functools.partial(_mixers_kernel, chunk_decay=chunk_decay),
        grid=(bsz, seq // t),
        in_specs=[seq_tile(d)] + ret_specs + gdn_specs + lru_specs,
        out_specs=[seq_tile(width), seq_tile(width), seq_tile(lru_w)],
        out_shape=[out, out, jax.ShapeDtypeStruct((bsz, seq, lru_w), BF16)],
        scratch_shapes=[pltpu.VMEM((N_HEADS, HEAD_DIM, HEAD_DIM), F32),
                        pltpu.VMEM((CARRY_ROWS + t, 3 * width), F32),
                        pltpu.VMEM((N_HEADS, HEAD_DIM, HEAD_DIM), F32),
                        pltpu.VMEM((t, width), F32),
                        pltpu.VMEM((t, width), F32),
                        pltpu.VMEM((t, width), BF16),
                        pltpu.VMEM((t, width), BF16),
                        pltpu.VMEM((t, width), BF16),
                        pltpu.VMEM((N_HEADS, t, GDN_CHUNK), BF16),
                        pltpu.VMEM((CARRY_ROWS + t, lru_w), F32),
                        pltpu.VMEM((1, lru_w), F32)],
        compiler_params=_params("parallel", "arbitrary"),
        name="token_mixers",
    )(h, w_ret, row(b_ret), cos, sin, dmat, qd, kd, row(ret_g), row(ret_b),
      w_gdn, row(b_gdn), w_small, row(b_small), gdn_cw, pad_row(a_log), pad_row(dt_bias), row(gdn_g),
      w_lru, row(b_lru), lru_cw, row(lru_cb), w_r, row(b_r), w_i, row(b_i), row(lam))


def _first_index_of_max(x, idx, size):
    m = jnp.max(x, axis=0, keepdims=True)
    first = jnp.min(jnp.where(x == m, idx, size), axis=0, keepdims=True)
    return m, idx == first


def _route(logits_t, bias_col, out):
    n_exp, t = logits_t.shape
    per_group = n_exp // N_GROUPS
    scores = _sigmoid(logits_t)
    sel = scores + bias_col
    idx_g = lax.broadcasted_iota(jnp.int32, (per_group, t), 0)
    group_scores = []
    for g in range(N_GROUPS):
        x = sel[g * per_group:(g + 1) * per_group, :]
        m1, hit = _first_index_of_max(x, idx_g, per_group)
        m2 = jnp.max(jnp.where(hit, NEG_INF, x), axis=0, keepdims=True)
        group_scores.append(m1 + m2)
        if g % 2 == 1:
            yield
    gsc = jnp.concatenate(group_scores, axis=0)
    idx_n = lax.broadcasted_iota(jnp.int32, (N_GROUPS, t), 0)
    gmask = jnp.zeros((N_GROUPS, t), F32)
    for _ in range(TOPK_GROUPS):
        _, hit = _first_index_of_max(gsc, idx_n, N_GROUPS)
        gmask = jnp.where(hit, 1.0, gmask)
        gsc = jnp.where(hit, NEG_INF, gsc)
    yield
    emask = jnp.concatenate([jnp.broadcast_to(gmask[g:g + 1, :], (per_group, t)) for g in range(N_GROUPS)], axis=0)
    cand = jnp.where(emask > 0.0, sel, NEG_INF)
    idx_e = lax.broadcasted_iota(jnp.int32, (n_exp, t), 0)
    picked = jnp.zeros((n_exp, t), F32)
    hits = []
    for _ in range(TOP_K):
        _, hit = _first_index_of_max(cand, idx_e, n_exp)
        hits.append(hit)
        picked = jnp.where(hit, 1.0, picked)
        cand = jnp.where(hit, NEG_INF, cand)
        yield
    gw = jnp.where(picked > 0.0, scores, 0.0)
    out.update(combine=gw / jnp.sum(gw, axis=0, keepdims=True) * ROUTED_SCALE, picked=picked, hits=hits, idx_e=idx_e)


def _pack_bf16_pairs(x):
    c = x.shape[1] // 2
    hi = pltpu.bitcast(x[:, :c].astype(BF16).astype(F32), jnp.int32)
    lo = pltpu.bitcast(x[:, c:].astype(BF16).astype(F32), jnp.int32)
    return hi | lax.shift_right_logical(lo, jnp.full(lo.shape, 16, jnp.int32))


def _unpack_bf16_pairs(w):
    hi = pltpu.bitcast(w & jnp.int32(-65536), F32)
    lo = pltpu.bitcast(lax.shift_left(w, jnp.full(w.shape, 16, jnp.int32)), F32)
    return hi, lo


def _merge_kernel(h_ref, yr_ref, yg_ref, yl_ref, wmg_ref, bmg_ref, wbr_ref, wout_ref, g_ref, b_ref,
                  rwt_ref, rb_ref, eye_ref, tri_ref, h1_ref, h1b_ref, xp_ref, eidx_ref, rank_ref, gwt_ref,
                  count_ref, *, alpha):
    @pl.when(pl.program_id(0) == 0)
    def _():
        count_ref[...] = jnp.zeros_like(count_ref)

    d = h_ref.shape[1]
    sub = eye_ref.shape[0]
    parts = [slice(s, s + sub) for s in range(0, h_ref.shape[0], sub)]
    pick = lambda hit, vals, zero: jnp.sum(jnp.where(hit, vals, zero), axis=0, keepdims=True)

    def dense(rows, logits):
        h = h_ref[rows, :]
        hb = h.astype(BF16)
        gates = _sigmoid(_dot(hb, wmg_ref[...]) + bmg_ref[...])
        yield
        mixed = None
        for n, y_ref in enumerate((yr_ref, yg_ref, yl_ref)):
            term = gates[:, n * d:(n + 1) * d] * _dot(y_ref[rows, :], wbr_ref[n])
            mixed = term if mixed is None else mixed + term
            yield
        mix = _dot(mixed.astype(BF16), wout_ref[...])
        yield
        h1 = _layer_norm(alpha * h + mix, g_ref[...], b_ref[...])
        h1b = h1.astype(BF16)
        h1_ref[rows, :] = h1
        h1b_ref[rows, :] = h1b
        xp_ref[rows, :] = _pack_bf16_pairs(h1)
        logits.append(_dot_nt(rwt_ref[...], h1b))
        yield

    def routing(rows, logit):
        res = {}
        yield from _route(logit, rb_ref[...], res)
        combine_t, picked, hits, idx_e = res["combine"], res["picked"], res["hits"], res["idx_e"]
        rank_full = count_ref[...] + _dot(picked.astype(BF16), tri_ref[...])
        count_ref[...] += jnp.sum(picked, axis=1, keepdims=True)
        eidx_ref[:, rows] = jnp.concatenate([pick(hit, idx_e, 0) for hit in hits], axis=0)
        yield
        rank_ref[:, rows] = jnp.concatenate([pick(hit, rank_full, 0.0) for hit in hits], axis=0).astype(jnp.int32)
        gw = jnp.concatenate([pick(hit, combine_t, 0.0) for hit in hits], axis=0)
        gwt_ref[rows, :] = lax.dot_general(eye_ref[...], gw, (((1,), (1,)), ((), ())),
                                           preferred_element_type=F32, precision=lax.Precision.HIGHEST)
        yield

    def interleave(gens):
        while gens:
            gens = [g for g in gens if next(g, StopIteration) is not StopIteration]

    logits = [[] for _ in parts]
    interleave([dense(rows, logits[s]) for s, rows in enumerate(parts)])
    interleave([routing(rows, logits[s][0]) for s, rows in enumerate(parts)])


def _merge(h2, y_ret, y_gdn, y_lru, w_mg, b_mg, w_branch, w_out, ln_g, ln_b, router_wt, router_b, alpha):
    n, d = h2.shape
    t = min(MERGE_TILE, n)
    width = y_ret.shape[1]
    n_exp = router_wt.shape[0]
    tile = lambda cols: pl.BlockSpec((t, cols), lambda i: (i, 0))
    lane_tile = pl.BlockSpec((TOP_K, t), lambda i: (0, i))
    sub = min(MERGE_SUB, t)
    tri = jnp.triu(jnp.ones((sub, sub), BF16), 1)
    return pl.pallas_call(
        functools.partial(_merge_kernel, alpha=alpha),
        grid=(n // t,),
        in_specs=[
            tile(d), tile(width), tile(width), tile(width),
            _const_spec((d, N_BRANCHES * d)),
            _const_spec((1, N_BRANCHES * d)),
            _const_spec((N_BRANCHES, width, d)),
            _const_spec((d, d)),
            _const_spec((1, d)),
            _const_spec((1, d)),
            _const_spec((n_exp, d)),
            _const_spec((n_exp, 1)),
            _const_spec((sub, sub)),
            _const_spec((sub, sub)),
        ],
        out_specs=[tile(d), tile(d), tile(d // 2), lane_tile, lane_tile, tile(TOP_K),
                   pl.BlockSpec((n_exp, 1), lambda i: (0, 0))],
        out_shape=[jax.ShapeDtypeStruct((n, d), F32), jax.ShapeDtypeStruct((n, d), BF16),
                   jax.ShapeDtypeStruct((n, d // 2), jnp.int32),
                   jax.ShapeDtypeStruct((TOP_K, n), jnp.int32), jax.ShapeDtypeStruct((TOP_K, n), jnp.int32),
                   jax.ShapeDtypeStruct((n, TOP_K), F32), jax.ShapeDtypeStruct((n_exp, 1), F32)],
        compiler_params=_params("arbitrary"),
        name="merge_route",
    )(h2, y_ret, y_gdn, y_lru, w_mg, b_mg.reshape(1, -1), w_branch, w_out, ln_g.reshape(1, -1),
      ln_b.reshape(1, -1), router_wt, router_b.reshape(-1, 1), jnp.eye(sub, dtype=F32), tri)


def _dest_kernel(start_ref, eidx_ref, rank_ref, o_ref, *, n_exp):
    eidx = eidx_ref[...]
    dest = rank_ref[...]
    for e in range(n_exp):
        dest = dest + jnp.where(eidx == e, start_ref[e], 0)
    o_ref[...] = dest


def _dest_rows(seg_start, eidx, rank):
    k, n = eidx.shape
    t = min(DEST_TILE, n)
    blk = pl.BlockSpec((k, t), lambda i, s: (0, i))
    return pl.pallas_call(
        functools.partial(_dest_kernel, n_exp=seg_start.shape[0]),
        grid_spec=pltpu.PrefetchScalarGridSpec(num_scalar_prefetch=1, grid=(n // t,), in_specs=[blk, blk],
                                               out_specs=blk),
        out_shape=jax.ShapeDtypeStruct((k, n), jnp.int32),
        compiler_params=_params("parallel"),
        name="dest_rows",
    )(seg_start, eidx, rank)


def _sc_workers():
    info = plsc.get_sparse_core_info()
    return info.num_cores, info.num_subcores


def _sc_mesh():
    return plsc.VectorSubcoreMesh(core_axis_name="c", subcore_axis_name="s")


def _sc_scatter_rows(x, dest, n_rows):
    n, c = x.shape
    k = dest.shape[0]
    n_cores, n_sub = _sc_workers()
    per_worker = n // (n_cores * n_sub)
    steps = per_worker // SC_WINDOW

    @functools.partial(pl.kernel, out_type=jax.ShapeDtypeStruct((n_rows, c), x.dtype), mesh=_sc_mesh(),
                       scratch_types=[pltpu.VMEM((k, SC_WINDOW), jnp.int32), pltpu.VMEM((SC_WINDOW, c), x.dtype),
                                      pltpu.SemaphoreType.DMA, pltpu.SemaphoreType.DMA],
                       name="dispatch_rows")
    def scatter(x_hbm, d_hbm, o_hbm, idx_v, rows_v, sem, row_sem):
        base = (lax.axis_index("s") * n_cores + lax.axis_index("c")) * per_worker

        @pl.loop(0, steps)
        def _(j):
            off = base + j * SC_WINDOW
            loads = [pltpu.async_copy(x_hbm.at[pl.ds(off, SC_WINDOW)], rows_v, row_sem)]
            loads += [pltpu.async_copy(d_hbm.at[pl.ds(kk * n + off, SC_WINDOW)], idx_v.at[kk], sem) for kk in range(k)]
            for cp in loads:
                cp.wait()
            copies = [pltpu.async_copy(rows_v, o_hbm.at[idx_v.at[kk]], sem) for kk in range(k)]
            for cp in copies:
                cp.wait()

    return scatter(x, dest.reshape(k * n))


def _sc_gather_rows(table, idx):
    b = idx.shape[0]
    c = table.shape[1]
    n_cores, n_sub = _sc_workers()
    per_worker = b // (n_cores * n_sub)
    steps = per_worker // SC_WINDOW

    half = SC_WINDOW // 2
    half_buf = lambda dtype, *shape: pltpu.VMEM((half,) + shape, dtype)

    @functools.partial(pl.kernel, out_type=jax.ShapeDtypeStruct((b, c), table.dtype), mesh=_sc_mesh(),
                       scratch_types=[half_buf(jnp.int32), half_buf(jnp.int32), half_buf(table.dtype, c),
                                      half_buf(table.dtype, c), pltpu.SemaphoreType.DMA, pltpu.SemaphoreType.DMA,
                                      pltpu.SemaphoreType.DMA, pltpu.SemaphoreType.DMA],
                       name="collect_rows")
    def gather(t_hbm, i_hbm, o_hbm, idx_a, idx_b, rows_a, rows_b, gat_a, gat_b, put_a, put_b):
        base = (lax.axis_index("s") * n_cores + lax.axis_index("c")) * per_worker

        @pl.loop(0, steps)
        def _(j):
            off_a = base + j * SC_WINDOW
            off_b = off_a + half
            pltpu.sync_copy(i_hbm.at[pl.ds(off_a, half)], idx_a)
            in_a = pltpu.async_copy(t_hbm.at[idx_a], rows_a, gat_a)
            pltpu.sync_copy(i_hbm.at[pl.ds(off_b, half)], idx_b)
            in_b = pltpu.async_copy(t_hbm.at[idx_b], rows_b, gat_b)
            in_a.wait()
            out_a = pltpu.async_copy(rows_a, o_hbm.at[pl.ds(off_a, half)], put_a)
            in_b.wait()
            out_b = pltpu.async_copy(rows_b, o_hbm.at[pl.ds(off_b, half)], put_b)
            out_a.wait()
            out_b.wait()

    return gather(table, idx)


def _expert_block_kernel(meta_ref, xs_ref, wgu_ref, wd_ref, ys_ref, wgu_b, wd_b, *, n_blocks):
    i = pl.program_id(0)

    @pl.when((i == 0) | (meta_ref[i] != meta_ref[jnp.maximum(i - 1, 0)]))
    def _():
        wgu_b[...] = wgu_ref[0, 0].astype(BF16)
        wd_b[...] = wd_ref[0, 0].astype(BF16)

    @pl.when(i < meta_ref[n_blocks])
    def _():
        half = xs_ref.shape[1]
        ff = wd_b.shape[0]

        def sub_block(rows):
            hi, lo = _unpack_bf16_pairs(xs_ref[rows, :])
            hib, lob = hi.astype(BF16), lo.astype(BF16)
            yield
            gu = _dot(hib, wgu_b[:half, :]) + _dot(lob, wgu_b[half:, :])
            yield
            mid = (_silu(gu[:, :ff]) * gu[:, ff:]).astype(BF16)
            yield
            out = _dot(mid, wd_b[...])
            yield
            ys_ref[rows, :] = _pack_bf16_pairs(out)
            yield

        waiting = [sub_block(slice(r, r + EXPERT_SUB)) for r in range(0, xs_ref.shape[0], EXPERT_SUB)]
        active = []
        while waiting or active:
            if waiting:
                active.append(waiting.pop(0))
            active = [g for g in active if next(g, StopIteration) is not StopIteration]


def _expert_blocks(meta, xs, w_gu, w_down, layer):
    rows, half = xs.shape
    _, _, d, ff2 = w_gu.shape
    n_blocks = rows // MOE_BLOCK
    row_blk = pl.BlockSpec((MOE_BLOCK, half), lambda i, meta: (jnp.minimum(i, meta[n_blocks] - 1), 0))
    return pl.pallas_call(
        functools.partial(_expert_block_kernel, n_blocks=n_blocks),
        grid_spec=pltpu.PrefetchScalarGridSpec(
            num_scalar_prefetch=1, grid=(n_blocks,),
            in_specs=[row_blk,
                      pl.BlockSpec((1, 1, d, ff2), lambda i, meta: (layer, meta[i], 0, 0)),
                      pl.BlockSpec((1, 1, ff2 // 2, d), lambda i, meta: (layer, meta[i], 0, 0))],
            out_specs=row_blk,
            scratch_shapes=[pltpu.VMEM((d, ff2), BF16), pltpu.VMEM((ff2 // 2, d), BF16)]),
        out_shape=jax.ShapeDtypeStruct((rows, half), jnp.int32),
        compiler_params=_params("arbitrary"),
        name="expert_blocks",
    )(meta, xs, w_gu, w_down)


def _routed_rows(xp, eidx, rank, counts, w_gu, w_down, layer):
    n = xp.shape[0]
    k = eidx.shape[0]
    n_exp = w_gu.shape[1]
    counts = counts.reshape(n_exp).astype(jnp.int32)
    padded = (counts + MOE_BLOCK - 1) // MOE_BLOCK * MOE_BLOCK
    seg_end = jnp.cumsum(padded)
    n_blocks = k * n // MOE_BLOCK + n_exp
    blk_start = jnp.arange(n_blocks, dtype=jnp.int32) * MOE_BLOCK
    blk_expert = jnp.minimum(jnp.sum(seg_end[None, :] <= blk_start[:, None], axis=1), n_exp - 1)
    meta = jnp.concatenate([blk_expert, seg_end[-1:] // MOE_BLOCK]).astype(jnp.int32)
    dest = _dest_rows(seg_end - padded, eidx, rank)
    xs = _sc_scatter_rows(xp, dest, n_blocks * MOE_BLOCK)
    return _expert_blocks(meta, xs, w_gu, w_down, layer), dest


def _shared_ple_kernel(h1_ref, h1b_ref, p_ref, wgu_ref, wd_ref, wpg_ref, bpg_ref, wpe_ref, o_ref, *, alpha):
    xb = h1b_ref[...]
    ff = wd_ref.shape[0]
    gu = _dot(xb, wgu_ref[...])
    shared = _dot((_silu(gu[:, :ff]) * gu[:, ff:]).astype(BF16), wd_ref[...])
    ple = _sigmoid(_dot(xb, wpg_ref[...]) + bpg_ref[...]) * _dot(p_ref[0].astype(BF16), wpe_ref[...])
    o_ref[...] = alpha * h1_ref[...] + shared + ple


def _shared_ple(h1, h1b, p3, layer, sh_w_gu, sh_w_down, ple_w_g, ple_b_g, ple_w_e, alpha):
    n, d = h1.shape
    t = min(ROW_TILE, n)
    pdim = p3.shape[2]
    ff2 = sh_w_gu.shape[1]
    tile = lambda cols: pl.BlockSpec((t, cols), lambda i: (i, 0))
    return pl.pallas_call(
        functools.partial(_shared_ple_kernel, alpha=alpha),
        grid=(n // t,),
        in_specs=[
            tile(d), tile(d), pl.BlockSpec((1, t, pdim), lambda i: (layer, i, 0)),
            _const_spec((d, ff2)),
            _const_spec((ff2 // 2, d)),
            _const_spec((d, d)),
            _const_spec((1, d)),
            _const_spec((pdim, d)),
        ],
        out_specs=tile(d),
        out_shape=jax.ShapeDtypeStruct((n, d), F32),
        compiler_params=_params("parallel"),
        name="shared_ple",
    )(h1, h1b, p3, sh_w_gu, sh_w_down, ple_w_g, ple_b_g.reshape(1, -1), ple_w_e)


def _combine_norm_kernel(pre_ref, yk_ref, gwt_ref, g_ref, b_ref, o_ref):
    gwt = gwt_ref[...]
    routed_hi = routed_lo = None
    for k in range(yk_ref.shape[0]):
        hi, lo = _unpack_bf16_pairs(yk_ref[k])
        wk = gwt[:, k:k + 1]
        routed_hi = hi * wk if routed_hi is None else routed_hi + hi * wk
        routed_lo = lo * wk if routed_lo is None else routed_lo + lo * wk
    routed = jnp.concatenate([routed_hi, routed_lo], axis=1)
    o_ref[...] = _layer_norm(pre_ref[...] + routed, g_ref[...], b_ref[...])


def _combine_norm(pre, ys, dest, gwt, ln_g, ln_b):
    n, d = pre.shape
    top_k = dest.shape[0]
    t = min(ROW_TILE, n)
    yk = _sc_gather_rows(ys, dest.reshape(top_k * n)).reshape(top_k, n, d // 2)
    tile = lambda cols: pl.BlockSpec((t, cols), lambda i: (i, 0))
    return pl.pallas_call(
        _combine_norm_kernel,
        grid=(n // t,),
        in_specs=[tile(d), pl.BlockSpec((top_k, t, d // 2), lambda i: (0, i, 0)), tile(top_k),
                  _const_spec((1, d)), _const_spec((1, d))],
        out_specs=tile(d),
        out_shape=jax.ShapeDtypeStruct((n, d), F32),
        compiler_params=_params("parallel"),
        name="combine_norm",
    )(pre, yk, gwt, ln_g.reshape(1, -1), ln_b.reshape(1, -1))


def _block_diag(w):
    g, i, j = w.shape
    eye = jnp.eye(g, dtype=w.dtype)
    return (eye[:, None, :, None] * w[:, :, None, :]).reshape(g * i, g * j)


def kernel(x, p, ln_in_g, ln_in_b, w_in, b_in, ret_norm_g, ret_norm_b, gdn_conv_w, gdn_a_log, gdn_dt_bias, gdn_norm_g, lru_conv_w, lru_conv_b, lru_w_r, lru_b_r, lru_w_i, lru_b_i, lru_lambda, w_branch, w_out, ln1_g, ln1_b, router_w, router_b, exp_w_gu, exp_w_down, sh_w_gu, sh_w_down, ple_w_e, ple_w_g, ple_b_g, ln2_g, ln2_b):
    bsz, seq, d = x.shape
    depth = w_in.shape[0]
    n = bsz * seq
    width = N_HEADS * HEAD_DIM
    alpha = (2 * depth) ** 0.25
    lanes = 128
    o_ret = 0
    o_gdn = o_ret + 4 * width
    o_small = o_gdn + 4 * width
    o_lru = o_small + 2 * N_HEADS
    o_mg = o_lru + 2 * width

    p3 = p.reshape(depth, n, -1)
    bf = lambda a: a.astype(BF16)
    h = _entry_norm(x.reshape(n, d), ln_in_g, ln_in_b)
    for l in range(depth):
        wl, bl = w_in[l], b_in[l]
        w_small = bf(jnp.zeros((d, lanes), F32).at[:, :2 * N_HEADS].set(wl[:, o_small:o_lru]))
        b_small = jnp.zeros((lanes,), F32).at[:2 * N_HEADS].set(bl[o_small:o_lru])
        w_ret, w_gdn, w_lru, w_mg = bf(wl[:, o_ret:o_gdn]), bf(wl[:, o_gdn:o_small]), bf(wl[:, o_lru:o_mg]), bf(wl[:, o_mg:])
        w_r, w_i = bf(_block_diag(lru_w_r[l])), bf(_block_diag(lru_w_i[l]))
        w_br, w_o, r_wt = bf(w_branch[l]), bf(w_out[l]), bf(router_w[l].T)
        w_sgu, w_sd, w_pg, w_pe = bf(sh_w_gu[l]), bf(sh_w_down[l]), bf(ple_w_g[l]), bf(ple_w_e[l])
        y_ret, y_gdn, y_lru = _mixers(
            h.reshape(bsz, seq, d), (w_ret, bl[o_ret:o_gdn], ret_norm_g[l], ret_norm_b[l]),
            (w_gdn, bl[o_gdn:o_small], w_small, b_small, gdn_conv_w[l], gdn_a_log[l], gdn_dt_bias[l], gdn_norm_g[l]),
            (w_lru, bl[o_lru:o_mg], lru_conv_w[l], lru_conv_b[l], w_r, lru_b_r[l], w_i, lru_b_i[l], lru_lambda[l]))
        h1, h1b, xp, eidx, rank, gwt, counts = _merge(
            h, y_ret.reshape(n, width), y_gdn.reshape(n, width), y_lru.reshape(n, width), w_mg, bl[o_mg:],
            w_br, w_o, ln1_g[l], ln1_b[l], r_wt, router_b[l], alpha)
        pre = _shared_ple(h1, h1b, p3, l, w_sgu, w_sd, w_pg, ple_b_g[l], w_pe, alpha)
        ys, dest = _routed_rows(xp, eidx, rank, counts, exp_w_gu, exp_w_down, l)
        h = _combine_norm(pre, ys, dest, gwt, ln2_g[l], ln2_b[l])
    return h.reshape(bsz, seq, d)
```

```python
import functools

import numpy as np
import jax
import jax.numpy as jnp
from jax import lax
from jax.experimental import pallas as pl
from jax.experimental.pallas import tpu as pltpu
from jax.experimental.pallas import tpu_sc as plsc

F32 = jnp.float32
BF16 = jnp.bfloat16

LANES = 128

HEAD_DIM = 128
N_HEADS = 4
LRU_C = 8.0
CONV_WIDTH = 4
N_BRANCHES = 3
ROPE_BASE = 10000.0
N_GROUPS = 8
TOPK_GROUPS = 4
TOP_K = 8
ROUTED_SCALE = 2.5
LN_EPS = 1e-5
GDN_CHUNK = 128
GDN_BASE = 16
SEQ_TILE = 512
MERGE_TILE = 512
MERGE_SUB = 128
ENTRY_TILE = 1024
ROW_TILE = 512
MOE_BLOCK = 1024
EXPERT_SUB = 256
DEST_TILE = 4096
SC_WINDOW = 128
CARRY_ROWS = 8
SCAN_GROUP = 16
VMEM_LIMIT_BYTES = 56 * 1024 * 1024
NEG_INF = float("-inf")


def _const_spec(shape):
    nd = len(shape)
    return pl.BlockSpec(shape, lambda *_: (0,) * nd, pipeline_mode=pl.Buffered(1))


def _params(*sem):
    return pltpu.CompilerParams(dimension_semantics=sem, vmem_limit_bytes=VMEM_LIMIT_BYTES)


def _layer_norm(x, g, b):
    mu = jnp.mean(x, axis=-1, keepdims=True)
    xc = x - mu
    var = jnp.mean(xc * xc, axis=-1, keepdims=True)
    return xc * lax.rsqrt(var + LN_EPS) * g + b


def _sigmoid(x):
    return 1.0 / (1.0 + jnp.exp(-x))


def _silu(x):
    return x * _sigmoid(x)


def _softplus(x):
    return jnp.maximum(x, 0.0) + jnp.log1p(jnp.exp(-jnp.abs(x)))


def _dot(a, b):
    return jnp.dot(a, b, preferred_element_type=F32)


def _dot_nt(a, b):
    return lax.dot_general(a, b, (((1,), (1,)), ((), ())), preferred_element_type=F32)


def _dot_tn(a, b):
    return lax.dot_general(a, b, (((0,), (0,)), ((), ())), preferred_element_type=F32)


def _ln_kernel(x_ref, g_ref, b_ref, o_ref):
    o_ref[...] = _layer_norm(x_ref[...], g_ref[...], b_ref[...])


def _entry_norm(x2, g, b):
    n, d = x2.shape
    t = min(ENTRY_TILE, n)
    return pl.pallas_call(
        _ln_kernel,
        grid=(n // t,),
        in_specs=[pl.BlockSpec((t, d), lambda i: (i, 0)), _const_spec((1, d)), _const_spec((1, d))],
        out_specs=pl.BlockSpec((t, d), lambda i: (i, 0)),
        out_shape=jax.ShapeDtypeStruct((n, d), F32),
        compiler_params=_params("parallel"),
        name="entry_norm",
    )(x2, g.reshape(1, d), b.reshape(1, d))


def _retention_body(hb, w_ref, b_ref, cos_ref, sin_ref, dmat_ref, qd_ref, kd_ref, ng_ref, nb_ref,
                    y_ref, state_ref, *, chunk_decay):
    width = N_HEADS * HEAD_DIM
    proj = _dot(hb, w_ref[...]) + b_ref[...]
    yield
    cos = cos_ref[...]
    sin = sin_ref[...]
    heads = range(N_HEADS)
    qs, ks, vbs = [], [], []
    for hh in heads:
        lo = hh * HEAD_DIM
        q = proj[:, lo:lo + HEAD_DIM]
        k = proj[:, width + lo:width + lo + HEAD_DIM]
        qs.append(q * cos + pltpu.roll(q, HEAD_DIM // 2, axis=1) * sin)
        ks.append((k * cos + pltpu.roll(k, HEAD_DIM // 2, axis=1) * sin) * (HEAD_DIM ** -0.5))
        vbs.append(proj[:, 2 * width + lo:2 * width + lo + HEAD_DIM].astype(BF16))
        yield
    states = [state_ref[hh] for hh in heads]
    scores, inter, outs = [], [], []
    for hh in heads:
        scores.append((_dot_nt(qs[hh].astype(BF16), ks[hh].astype(BF16)) * dmat_ref[hh]).astype(BF16))
        yield
    for hh in heads:
        inter.append(_dot((qs[hh] * qd_ref[hh]).astype(BF16), states[hh].astype(BF16)))
        yield
    for hh in heads:
        state_ref[hh] = states[hh] * chunk_decay[hh] + _dot_tn((ks[hh] * kd_ref[hh]).astype(BF16), vbs[hh])
        yield
    for hh in heads:
        outs.append(_dot(scores[hh], vbs[hh]) + inter[hh])
        yield
    for hh in heads:
        lo = hh * HEAD_DIM
        o = outs[hh]
        gate = proj[:, 3 * width + lo:3 * width + lo + HEAD_DIM]
        mu = jnp.mean(o, axis=-1, keepdims=True)
        oc = o - mu
        var = jnp.mean(oc * oc, axis=-1, keepdims=True)
        on = oc * lax.rsqrt(var + LN_EPS) * ng_ref[:, lo:lo + HEAD_DIM] + nb_ref[:, lo:lo + HEAD_DIM]
        y_ref[:, lo:lo + HEAD_DIM] = (_silu(gate) * on).astype(y_ref.dtype)
        yield


def _retention_tables(seq, tile):
    half = HEAD_DIM // 2
    inv_freq = ROPE_BASE ** (-np.linspace(0.0, 1.0, half))
    ang = np.arange(seq)[:, None] * inv_freq[None, :]
    cos = np.concatenate([np.cos(ang), np.cos(ang)], axis=1)
    sin = np.concatenate([-np.sin(ang), np.sin(ang)], axis=1)
    log_gamma = np.log1p(-np.exp2(-5.0 - np.arange(N_HEADS)))
    pos = np.arange(tile)
    diff = pos[:, None] - pos[None, :]
    dmat = np.where(diff >= 0, np.exp(log_gamma[:, None, None] * np.maximum(diff, 0)), 0.0)
    qd = np.exp(log_gamma[:, None] * (pos + 1.0))[:, :, None] * np.ones((1, 1, HEAD_DIM))
    kd = np.exp(log_gamma[:, None] * (tile - 1.0 - pos))[:, :, None] * np.ones((1, 1, HEAD_DIM))
    chunk_decay = tuple(float(c) for c in np.exp(log_gamma * tile))
    as32 = lambda a: jnp.asarray(a, F32)
    return as32(cos), as32(sin), as32(dmat), as32(qd), as32(kd), chunk_decay


def _causal_conv(x, xs_ref, cw_ref):
    t = x.shape[0]
    xs_ref[CARRY_ROWS:, :] = x
    ext = xs_ref[...]
    acc = ext * cw_ref[0:1, :]
    for j in range(1, CONV_WIDTH):
        acc = pltpu.roll(acc, 1, axis=0) + ext * cw_ref[j:j + 1, :]
    xs_ref[0:CARRY_ROWS, :] = xs_ref[t:t + CARRY_ROWS, :]
    return acc[CARRY_ROWS:, :]


def _cumsum_rows(x):
    n = x.shape[0]
    row = lax.broadcasted_iota(jnp.int32, x.shape, 0)
    d = 1
    while d < n:
        x = x + jnp.where(row >= d, pltpu.roll(x, d, axis=0), 0.0)
        d *= 2
    return x


def _gdn_body(hb, w_ref, b_ref, ws_ref, bs_ref, cw_ref, alog_ref, dtb_ref, ng_ref,
              y_ref, xs_ref, state_ref, u_ref, wf_ref, w_s_ref, qd_ref, kd_ref, qk_ref):
    width = N_HEADS * HEAD_DIM
    c = GDN_CHUNK
    t = hb.shape[0]
    proj = _dot(hb, w_ref[...]) + b_ref[...]
    small = _dot(hb, ws_ref[...]) + bs_ref[...]
    yield
    qkv = _silu(_causal_conv(proj[:, :3 * width], xs_ref, cw_ref))
    yield
    beta_all = _sigmoid(small)
    la_all = -jnp.exp(alog_ref[...]) * _softplus(small + dtb_ref[...])

    ri = lax.broadcasted_iota(jnp.int32, (c, c), 0)
    ci = lax.broadcasted_iota(jnp.int32, (c, c), 1)
    lower = ri >= ci
    strict = ri > ci
    base_blocks = (ri // GDN_BASE) == (ci // GDN_BASE)

    items = [(n, hh) for n in range(t // c) for hh in range(N_HEADS)]
    gcs = {}
    for n in range(t // c):
        la_c = la_all[n * c:(n + 1) * c, :]
        gc_c = _cumsum_rows(la_c)
        gcs[n] = (la_c, gc_c, jnp.exp(gc_c))
    g_last, negs, pws, rems = {}, {}, {}, {}
    for n, hh in items:
        r0, lo = n * c, hh * HEAD_DIM
        rows, cols = slice(r0, r0 + c), slice(lo, lo + HEAD_DIM)
        la_c, gc_c, egc_c = gcs[n]
        q = qkv[rows, lo:lo + HEAD_DIM]
        k = qkv[rows, width + lo:width + lo + HEAD_DIM]
        v = qkv[rows, 2 * width + lo:2 * width + lo + HEAD_DIM]
        q = q * lax.rsqrt(jnp.sum(q * q, axis=-1, keepdims=True) + 1e-6) * (HEAD_DIM ** -0.5)
        k = k * lax.rsqrt(jnp.sum(k * k, axis=-1, keepdims=True) + 1e-6)
        beta = beta_all[rows, hh:hh + 1]
        la = la_c[:, N_HEADS + hh:N_HEADS + hh + 1]
        gc = gc_c[:, N_HEADS + hh:N_HEADS + hh + 1]
        egc = egc_c[:, N_HEADS + hh:N_HEADS + hh + 1]
        gc_row = jnp.sum(jnp.where(ri <= ci, jnp.broadcast_to(la, (c, c)), 0.0), axis=0, keepdims=True)
        gc_last = gc_row[:, c - 1:c]
        decay = jnp.where(lower, jnp.exp(jnp.where(lower, gc - gc_row, 0.0)), 0.0)
        kb = k * beta
        kbf = k.astype(BF16)
        a_neg = jnp.where(strict, -(_dot_nt(kb.astype(BF16), kbf) * decay), 0.0)
        negs[n, hh] = a_neg
        pws[n, hh] = rems[n, hh] = jnp.where(base_blocks, a_neg, 0.0)
        u_ref[rows, cols] = v * beta
        wf_ref[rows, cols] = kb * egc
        qk_ref[hh, rows, :] = (_dot_nt(q.astype(BF16), kbf) * decay).astype(BF16)
        qd_ref[rows, cols] = (q * egc).astype(BF16)
        kd_ref[rows, cols] = (k * jnp.exp(gc_last - gc)).astype(BF16)
        g_last[n, hh] = jnp.exp(gc_last)
        yield
    m = 2
    while m < GDN_BASE:
        for it in items:
            pwb = pws[it].astype(BF16)
            pws[it] = _dot(pwb, pwb)
        yield
        for it in items:
            rems[it] = rems[it] + pws[it] + _dot(rems[it].astype(BF16), pws[it].astype(BF16))
        yield
        m *= 2
    size = GDN_BASE
    while size < c:
        pair = ((ri // (2 * size)) == (ci // (2 * size))) & ((ri // size) != (ci // size))
        mids = {}
        for it in items:
            link = jnp.where(pair, negs[it], 0.0)
            mids[it] = link + _dot(rems[it].astype(BF16), link.astype(BF16))
        yield
        for it in items:
            rems[it] = rems[it] + mids[it] + _dot(mids[it].astype(BF16), rems[it].astype(BF16))
        yield
        size *= 2
    for n, hh in items:
        rows, cols = slice(n * c, (n + 1) * c), slice(hh * HEAD_DIM, (hh + 1) * HEAD_DIM)
        remb = rems[n, hh].astype(BF16)
        u_ref[rows, cols] = u_ref[rows, cols] + _dot(remb, u_ref[rows, cols].astype(BF16))
        w_s_ref[rows, cols] = (wf_ref[rows, cols] + _dot(remb, wf_ref[rows, cols].astype(BF16))).astype(BF16)
    yield

    heads = range(N_HEADS)
    for n in range(t // c):
        rows = slice(n * c, (n + 1) * c)
        cols = [slice(hh * HEAD_DIM, (hh + 1) * HEAD_DIM) for hh in heads]
        states = [state_ref[hh] for hh in heads]
        sbs = [s.astype(BF16) for s in states]
        vnbs = [(u_ref[rows, cols[hh]] - _dot(w_s_ref[rows, cols[hh]], sbs[hh])).astype(BF16) for hh in heads]
        yield
        outs = [_dot(qd_ref[rows, cols[hh]], sbs[hh]) + _dot(qk_ref[hh, rows, :], vnbs[hh]) for hh in heads]
        for hh in heads:
            state_ref[hh] = states[hh] * g_last[n, hh] + _dot_tn(kd_ref[rows, cols[hh]], vnbs[hh])
        yield
        for hh in heads:
            o = outs[hh]
            o = o * lax.rsqrt(jnp.mean(o * o, axis=-1, keepdims=True) + 1e-6) * ng_ref[...]
            og = proj[rows, 3 * width + hh * HEAD_DIM:3 * width + (hh + 1) * HEAD_DIM]
            y_ref[rows, cols[hh]] = (o * _silu(og)).astype(y_ref.dtype)
        yield


def _lru_body(hb, w_ref, b_ref, cw_ref, cb_ref, wr_ref, br_ref, wi_ref, bi_ref, lam_ref,
              y_ref, xs_ref, carry_ref):
    width = cw_ref.shape[1]
    t = hb.shape[0]
    proj = _dot(hb, w_ref[...]) + b_ref[...]
    yield
    xc = _causal_conv(proj[:, :width], xs_ref, cw_ref) + cb_ref[...]
    xcb = xc.astype(BF16)
    yield
    r = _sigmoid(_dot(xcb, wr_ref[...]) + br_ref[...])
    gi = _sigmoid(_dot(xcb, wi_ref[...]) + bi_ref[...])
    yield
    log_a = -LRU_C * r * _softplus(-lam_ref[...])
    a = jnp.exp(log_a)
    th = jnp.tanh(log_a)
    hs = jnp.sqrt(-2.0 * th / (1.0 - th)) * (gi * xc)
    row = lax.broadcasted_iota(jnp.int32, (t, width), 0) % SCAN_GROUP
    d = 1
    while d < SCAN_GROUP:
        keep = row >= d
        hs = hs + a * jnp.where(keep, pltpu.roll(hs, d, axis=0), 0.0)
        a = a * jnp.where(keep, pltpu.roll(a, d, axis=0), 1.0)
        d *= 2
        yield
    gate = jax.nn.gelu(proj[:, width:], approximate=True)
    carry = carry_ref[...]
    for g in range(t // SCAN_GROUP):
        rows = slice(g * SCAN_GROUP, (g + 1) * SCAN_GROUP)
        hg = hs[rows, :] + a[rows, :] * carry
        carry = hg[SCAN_GROUP - 1:SCAN_GROUP, :]
        y_ref[rows, :] = (gate[rows, :] * hg).astype(y_ref.dtype)
        if g % 8 == 7:
            yield
    carry_ref[...] = carry


N_RET_IN, N_GDN_IN, N_LRU_IN = 9, 8, 9
MIX_STRIDE = (1, 2, 2)


def _mixers_kernel(h_ref, *refs, chunk_decay):
    ret_in, refs = refs[:N_RET_IN], refs[N_RET_IN:]
    gdn_in, refs = refs[:N_GDN_IN], refs[N_GDN_IN:]
    lru_in, refs = refs[:N_LRU_IN], refs[N_LRU_IN:]
    yr_ref, yg_ref, yl_ref = refs[:3]
    ret_state, gdn_xs, gdn_state, u_ref, wf_ref, w_s_ref, qd_ref, kd_ref, qk_ref, lru_xs, lru_carry = refs[3:]

    @pl.when(pl.program_id(1) == 0)
    def _():
        ret_state[...] = jnp.zeros_like(ret_state)
        gdn_state[...] = jnp.zeros_like(gdn_state)
        lru_carry[...] = jnp.zeros_like(lru_carry)
        gdn_xs[0:CARRY_ROWS, :] = jnp.zeros((CARRY_ROWS, gdn_xs.shape[1]), F32)
        lru_xs[0:CARRY_ROWS, :] = jnp.zeros((CARRY_ROWS, lru_xs.shape[1]), F32)

    hb = h_ref[0].astype(BF16)
    branches = [
        (_gdn_body(hb, *gdn_in, yg_ref.at[0], gdn_xs, gdn_state, u_ref, wf_ref, w_s_ref, qd_ref, kd_ref, qk_ref),
         MIX_STRIDE[0]),
        (_retention_body(hb, *ret_in, yr_ref.at[0], ret_state, chunk_decay=chunk_decay), MIX_STRIDE[1]),
        (_lru_body(hb, *lru_in, yl_ref.at[0], lru_xs, lru_carry), MIX_STRIDE[2]),
    ]
    tick = 0
    while branches:
        for gen, stride in list(branches):
            if tick % stride == 0 and next(gen, StopIteration) is StopIteration:
                branches.remove((gen, stride))
        tick += 1


def _mixers(h, ret_args, gdn_args, lru_args):
    bsz, seq, d = h.shape
    t = min(SEQ_TILE, seq)
    width = N_HEADS * HEAD_DIM
    row = lambda vec: vec.reshape(1, -1)
    w_ret, b_ret, ret_g, ret_b = ret_args
    w_gdn, b_gdn, w_small, b_small, gdn_cw, a_log, dt_bias, gdn_g = gdn_args
    w_lru, b_lru, lru_cw, lru_cb, w_r, b_r, w_i, b_i, lam = lru_args
    lanes = w_small.shape[1]
    lru_w = lru_cw.shape[1]
    cos, sin, dmat, qd, kd, chunk_decay = _retention_tables(seq, t)
    pad_row = lambda vec: jnp.zeros((1, lanes), F32).at[0, N_HEADS:2 * N_HEADS].set(vec.astype(F32))
    seq_tile = lambda cols: pl.BlockSpec((1, t, cols), lambda i, j: (i, j, 0))
    pos_tile = pl.BlockSpec((t, HEAD_DIM), lambda i, j: (j, 0))
    ret_specs = [_const_spec((d, 4 * width)), _const_spec((1, 4 * width)), pos_tile, pos_tile,
                 _const_spec((N_HEADS, t, t)), _const_spec((N_HEADS, t, HEAD_DIM)), _const_spec((N_HEADS, t, HEAD_DIM)),
                 _const_spec((1, width)), _const_spec((1, width))]
    gdn_specs = [_const_spec((d, 4 * width)), _const_spec((1, 4 * width)), _const_spec((d, lanes)),
                 _const_spec((1, lanes)), _const_spec((CONV_WIDTH, 3 * width)), _const_spec((1, lanes)),
                 _const_spec((1, lanes)), _const_spec((1, HEAD_DIM))]
    lru_specs = [_const_spec((d, 2 * lru_w)), _const_spec((1, 2 * lru_w)), _const_spec((CONV_WIDTH, lru_w)),
                 _const_spec((1, lru_w)), _const_spec((lru_w, lru_w)), _const_spec((1, lru_w)),
                 _const_spec((lru_w, lru_w)), _const_spec((1, lru_w)), _const_spec((1, lru_w))]
    assert (len(ret_specs), len(gdn_specs), len(lru_specs)) == (N_RET_IN, N_GDN_IN, N_LRU_IN)
    out = jax.ShapeDtypeStruct((bsz, seq, width), BF16)
    return pl.pallas_call(
        functools.partial(_mixers_kernel, chunk_decay=chunk_decay),
        grid=(bsz, seq // t),
        in_specs=[seq_tile(d)] + ret_specs + gdn_specs + lru_specs,
        out_specs=[seq_tile(width), seq_tile(width), seq_tile(lru_w)],
        out_shape=[out, out, jax.ShapeDtypeStruct((bsz, seq, lru_w), BF16)],
        scratch_shapes=[pltpu.VMEM((N_HEADS, HEAD_DIM, HEAD_DIM), F32),
                        pltpu.VMEM((CARRY_ROWS + t, 3 * width), F32),
                        pltpu.VMEM((N_HEADS, HEAD_DIM, HEAD_DIM), F32),
                        pltpu.VMEM((t, width), F32),
                        pltpu.VMEM((t, width), F32),
                        pltpu.VMEM((t, width), BF16),
                        pltpu.VMEM((t, width), BF16),
                        pltpu.VMEM((t, width), BF16),
                        pltpu.VMEM((N_HEADS, t, GDN_CHUNK), BF16),
                        pltpu.VMEM((CARRY_ROWS + t, lru_w), F32),
                        pltpu.VMEM((1, lru_w), F32)],
        compiler_params=_params("parallel", "arbitrary"),
        name="token_mixers",
    )(h, w_ret, row(b_ret), cos, sin, dmat, qd, kd, row(ret_g), row(ret_b),
      w_gdn, row(b_gdn), w_small, row(b_small), gdn_cw, pad_row(a_log), pad_row(dt_bias), row(gdn_g),
      w_lru, row(b_lru), lru_cw, row(lru_cb), w_r, row(b_r), w_i, row(b_i), row(lam))


def _first_index_of_max(x, idx, size):
    m = jnp.max(x, axis=0, keepdims=True)
    first = jnp.min(jnp.where(x == m, idx, size), axis=0, keepdims=True)
    return m, idx == first


def _route(logits_t, bias_col, out):
    n_exp, t = logits_t.shape
    per_group = n_exp // N_GROUPS
    scores = _sigmoid(logits_t)
    sel = scores + bias_col
    idx_g = lax.broadcasted_iota(jnp.int32, (per_group, t), 0)
    group_scores = []
    for g in range(N_GROUPS):
        x = sel[g * per_group:(g + 1) * per_group, :]
        m1, hit = _first_index_of_max(x, idx_g, per_group)
        m2 = jnp.max(jnp.where(hit, NEG_INF, x), axis=0, keepdims=True)
        group_scores.append(m1 + m2)
        if g % 2 == 1:
            yield
    gsc = jnp.concatenate(group_scores, axis=0)
    idx_n = lax.broadcasted_iota(jnp.int32, (N_GROUPS, t), 0)
    gmask = jnp.zeros((N_GROUPS, t), F32)
    for _ in range(TOPK_GROUPS):
        _, hit = _first_index_of_max(gsc, idx_n, N_GROUPS)
        gmask = jnp.where(hit, 1.0, gmask)
        gsc = jnp.where(hit, NEG_INF, gsc)
    yield
    emask = jnp.concatenate([jnp.broadcast_to(gmask[g:g + 1, :], (per_group, t)) for g in range(N_GROUPS)], axis=0)
    cand = jnp.where(emask > 0.0, sel, NEG_INF)
    idx_e = lax.broadcasted_iota(jnp.int32, (n_exp, t), 0)
    picked = jnp.zeros((n_exp, t), F32)
    hits = []
    for _ in range(TOP_K):
        _, hit = _first_index_of_max(cand, idx_e, n_exp)
        hits.append(hit)
        picked = jnp.where(hit, 1.0, picked)
        cand = jnp.where(hit, NEG_INF, cand)
        yield
    gw = jnp.where(picked > 0.0, scores, 0.0)
    out.update(combine=gw / jnp.sum(gw, axis=0, keepdims=True) * ROUTED_SCALE, picked=picked, hits=hits, idx_e=idx_e)


def _pack_bf16_pairs(x):
    c = x.shape[1] // 2
    hi = pltpu.bitcast(x[:, :c].astype(BF16).astype(F32), jnp.int32)
    lo = pltpu.bitcast(x[:, c:].astype(BF16).astype(F32), jnp.int32)
    return hi | lax.shift_right_logical(lo, jnp.full(lo.shape, 16, jnp.int32))


def _unpack_bf16_pairs(w):
    hi = pltpu.bitcast(w & jnp.int32(-65536), F32)
    lo = pltpu.bitcast(lax.shift_left(w, jnp.full(w.shape, 16, jnp.int32)), F32)
    return hi, lo


def _merge_kernel(h_ref, yr_ref, yg_ref, yl_ref, wmg_ref, bmg_ref, wbr_ref, wout_ref, g_ref, b_ref,
                  rwt_ref, rb_ref, eye_ref, tri_ref, h1_ref, h1b_ref, xp_ref, eidx_ref, rank_ref, gwt_ref,
                  count_ref, *, alpha):
    @pl.when(pl.program_id(0) == 0)
    def _():
        count_ref[...] = jnp.zeros_like(count_ref)

    d = h_ref.shape[1]
    sub = eye_ref.shape[0]
    parts = [slice(s, s + sub) for s in range(0, h_ref.shape[0], sub)]
    pick = lambda hit, vals, zero: jnp.sum(jnp.where(hit, vals, zero), axis=0, keepdims=True)

    def dense(rows, logits):
        h = h_ref[rows, :]
        hb = h.astype(BF16)
        gates = _sigmoid(_dot(hb, wmg_ref[...]) + bmg_ref[...])
        yield
        mixed = None
        for n, y_ref in enumerate((yr_ref, yg_ref, yl_ref)):
            term = gates[:, n * d:(n + 1) * d] * _dot(y_ref[rows, :], wbr_ref[n])
            mixed = term if mixed is None else mixed + term
            yield
        mix = _dot(mixed.astype(BF16), wout_ref[...])
        yield
        h1 = _layer_norm(alpha * h + mix, g_ref[...], b_ref[...])
        h1b = h1.astype(BF16)
        h1_ref[rows, :] = h1
        h1b_ref[rows, :] = h1b
        xp_ref[rows, :] = _pack_bf16_pairs(h1)
        logits.append(_dot_nt(rwt_ref[...], h1b))
        yield

    def routing(rows, logit):
        res = {}
        yield from _route(logit, rb_ref[...], res)
        combine_t, picked, hits, idx_e = res["combine"], res["picked"], res["hits"], res["idx_e"]
        rank_full = count_ref[...] + _dot(picked.astype(BF16), tri_ref[...])
        count_ref[...] += jnp.sum(picked, axis=1, keepdims=True)
        eidx_ref[:, rows] = jnp.concatenate([pick(hit, idx_e, 0) for hit in hits], axis=0)
        yield
        rank_ref[:, rows] = jnp.concatenate([pick(hit, rank_full, 0.0) for hit in hits], axis=0).astype(jnp.int32)
        gw = jnp.concatenate([pick(hit, combine_t, 0.0) for hit in hits], axis=0)
        gwt_ref[rows, :] = lax.dot_general(eye_ref[...], gw, (((1,), (1,)), ((), ())),
                                           preferred_element_type=F32, precision=lax.Precision.HIGHEST)
        yield

    def interleave(gens):
        while gens:
            gens = [g for g in gens if next(g, StopIteration) is not StopIteration]

    logits = [[] for _ in parts]
    interleave([dense(rows, logits[s]) for s, rows in enumerate(parts)])
    interleave([routing(rows, logits[s][0]) for s, rows in enumerate(parts)])


def _merge(h2, y_ret, y_gdn, y_lru, w_mg, b_mg, w_branch, w_out, ln_g, ln_b, router_wt, router_b, alpha):
    n, d = h2.shape
    t = min(MERGE_TILE, n)
    width = y_ret.shape[1]
    n_exp = router_wt.shape[0]
    tile = lambda cols: pl.BlockSpec((t, cols), lambda i: (i, 0))
    lane_tile = pl.BlockSpec((TOP_K, t), lambda i: (0, i))
    sub = min(MERGE_SUB, t)
    tri = jnp.triu(jnp.ones((sub, sub), BF16), 1)
    return pl.pallas_call(
        functools.partial(_merge_kernel, alpha=alpha),
        grid=(n // t,),
        in_specs=[
            tile(d), tile(width), tile(width), tile(width),
            _const_spec((d, N_BRANCHES * d)),
            _const_spec((1, N_BRANCHES * d)),
            _const_spec((N_BRANCHES, width, d)),
            _const_spec((d, d)),
            _const_spec((1, d)),
            _const_spec((1, d)),
            _const_spec((n_exp, d)),
            _const_spec((n_exp, 1)),
            _const_spec((sub, sub)),
            _const_spec((sub, sub)),
        ],
        out_specs=[tile(d), tile(d), tile(d // 2), lane_tile, lane_tile, tile(TOP_K),
                   pl.BlockSpec((n_exp, 1), lambda i: (0, 0))],
        out_shape=[jax.ShapeDtypeStruct((n, d), F32), jax.ShapeDtypeStruct((n, d), BF16),
                   jax.ShapeDtypeStruct((n, d // 2), jnp.int32),
                   jax.ShapeDtypeStruct((TOP_K, n), jnp.int32), jax.ShapeDtypeStruct((TOP_K, n), jnp.int32),
                   jax.ShapeDtypeStruct((n, TOP_K), F32), jax.ShapeDtypeStruct((n_exp, 1), F32)],
        compiler_params=_params("arbitrary"),
        name="merge_route",
    )(h2, y_ret, y_gdn, y_lru, w_mg, b_mg.reshape(1, -1), w_branch, w_out, ln_g.reshape(1, -1),
      ln_b.reshape(1, -1), router_wt, router_b.reshape(-1, 1), jnp.eye(sub, dtype=F32), tri)


def _dest_kernel(start_ref, eidx_ref, rank_ref, o_ref, *, n_exp):
    eidx = eidx_ref[...]
    dest = rank_ref[...]
    for e in range(n_exp):
        dest = dest + jnp.where(eidx == e, start_ref[e], 0)
    o_ref[...] = dest


def _dest_rows(seg_start, eidx, rank):
    k, n = eidx.shape
    t = min(DEST_TILE, n)
    blk = pl.BlockSpec((k, t), lambda i, s: (0, i))
    return pl.pallas_call(
        functools.partial(_dest_kernel, n_exp=seg_start.shape[0]),
        grid_spec=pltpu.PrefetchScalarGridSpec(num_scalar_prefetch=1, grid=(n // t,), in_specs=[blk, blk],
                                               out_specs=blk),
        out_shape=jax.ShapeDtypeStruct((k, n), jnp.int32),
        compiler_params=_params("parallel"),
        name="dest_rows",
    )(seg_start, eidx, rank)


def _sc_workers():
    info = plsc.get_sparse_core_info()
    return info.num_cores, info.num_subcores


def _sc_mesh():
    return plsc.VectorSubcoreMesh(core_axis_name="c", subcore_axis_name="s")


def _sc_scatter_rows(x, dest, n_rows):
    n, c = x.shape
    k = dest.shape[0]
    n_cores, n_sub = _sc_workers()
    per_worker = n // (n_cores * n_sub)
    steps = per_worker // SC_WINDOW

    @functools.partial(pl.kernel, out_type=jax.ShapeDtypeStruct((n_rows, c), x.dtype), mesh=_sc_mesh(),
                       scratch_types=[pltpu.VMEM((k, SC_WINDOW), jnp.int32), pltpu.VMEM((SC_WINDOW, c), x.dtype),
                                      pltpu.SemaphoreType.DMA, pltpu.SemaphoreType.DMA],
                       name="dispatch_rows")
    def scatter(x_hbm, d_hbm, o_hbm, idx_v, rows_v, sem, row_sem):
        base = (lax.axis_index("s") * n_cores + lax.axis_index("c")) * per_worker

        @pl.loop(0, steps)
        def _(j):
            off = base + j * SC_WINDOW
            loads = [pltpu.async_copy(x_hbm.at[pl.ds(off, SC_WINDOW)], rows_v, row_sem)]
            loads += [pltpu.async_copy(d_hbm.at[pl.ds(kk * n + off, SC_WINDOW)], idx_v.at[kk], sem) for kk in range(k)]
            for cp in loads:
                cp.wait()
            copies = [pltpu.async_copy(rows_v, o_hbm.at[idx_v.at[kk]], sem) for kk in range(k)]
            for cp in copies:
                cp.wait()

    return scatter(x, dest.reshape(k * n))


def _sc_gather_rows(table, idx):
    b = idx.shape[0]
    c = table.shape[1]
    n_cores, n_sub = _sc_workers()
    per_worker = b // (n_cores * n_sub)
    steps = per_worker // SC_WINDOW

    half = SC_WINDOW // 2
    half_buf = lambda dtype, *shape: pltpu.VMEM((half,) + shape, dtype)

    @functools.partial(pl.kernel, out_type=jax.ShapeDtypeStruct((b, c), table.dtype), mesh=_sc_mesh(),
                       scratch_types=[half_buf(jnp.int32), half_buf(jnp.int32), half_buf(table.dtype, c),
                                      half_buf(table.dtype, c), pltpu.SemaphoreType.DMA, pltpu.SemaphoreType.DMA,
                                      pltpu.SemaphoreType.DMA, pltpu.SemaphoreType.DMA],
                       name="collect_rows")
    def gather(t_hbm, i_hbm, o_hbm, idx_a, idx_b, rows_a, rows_b, gat_a, gat_b, put_a, put_b):
        base = (lax.axis_index("s") * n_cores + lax.axis_index("c")) * per_worker

        @pl.loop(0, steps)
        def _(j):
            off_a = base + j * SC_WINDOW
            off_b = off_a + half
            pltpu.sync_copy(i_hbm.at[pl.ds(off_a, half)], idx_a)
            in_a = pltpu.async_copy(t_hbm.at[idx_a], rows_a, gat_a)
            pltpu.sync_copy(i_hbm.at[pl.ds(off_b, half)], idx_b)
            in_b = pltpu.async_copy(t_hbm.at[idx_b], rows_b, gat_b)
            in_a.wait()
            out_a = pltpu.async_copy(rows_a, o_hbm.at[pl.ds(off_a, half)], put_a)
            in_b.wait()
            out_b = pltpu.async_copy(rows_b, o_hbm.at[pl.ds(off_b, half)], put_b)
            out_a.wait()
            out_b.wait()

    return gather(table, idx)


def _expert_block_kernel(meta_ref, xs_ref, wgu_ref, wd_ref, ys_ref, wgu_b, wd_b, *, n_blocks):
    i = pl.program_id(0)

    @pl.when((i == 0) | (meta_ref[i] != meta_ref[jnp.maximum(i - 1, 0)]))
    def _():
        wgu_b[...] = wgu_ref[0, 0].astype(BF16)
        wd_b[...] = wd_ref[0, 0].astype(BF16)

    @pl.when(i < meta_ref[n_blocks])
    def _():
        half = xs_ref.shape[1]
        ff = wd_b.shape[0]

        def sub_block(rows):
            hi, lo = _unpack_bf16_pairs(xs_ref[rows, :])
            hib, lob = hi.astype(BF16), lo.astype(BF16)
            yield
            gu = _dot(hib, wgu_b[:half, :]) + _dot(lob, wgu_b[half:, :])
            yield
            mid = (_silu(gu[:, :ff]) * gu[:, ff:]).astype(BF16)
            yield
            out = _dot(mid, wd_b[...])
            yield
            ys_ref[rows, :] = _pack_bf16_pairs(out)
            yield

        waiting = [sub_block(slice(r, r + EXPERT_SUB)) for r in range(0, xs_ref.shape[0], EXPERT_SUB)]
        active = []
        while waiting or active:
            if waiting:
                active.append(waiting.pop(0))
            active = [g for g in active if next(g, StopIteration) is not StopIteration]


def _expert_blocks(meta, xs, w_gu, w_down, layer):
    rows, half = xs.shape
    _, _, d, ff2 = w_gu.shape
    n_blocks = rows // MOE_BLOCK
    row_blk = pl.BlockSpec((MOE_BLOCK, half), lambda i, meta: (jnp.minimum(i, meta[n_blocks] - 1), 0))
    return pl.pallas_call(
        functools.partial(_expert_block_kernel, n_blocks=n_blocks),
        grid_spec=pltpu.PrefetchScalarGridSpec(
            num_scalar_prefetch=1, grid=(n_blocks,),
            in_specs=[row_blk,
                      pl.BlockSpec((1, 1, d, ff2), lambda i, meta: (layer, meta[i], 0, 0)),
                      pl.BlockSpec((1, 1, ff2 // 2, d), lambda i, meta: (layer, meta[i], 0, 0))],
            out_specs=row_blk,
            scratch_shapes=[pltpu.VMEM((d, ff2), BF16), pltpu.VMEM((ff2 // 2, d), BF16)]),
        out_shape=jax.ShapeDtypeStruct((rows, half), jnp.int32),
        compiler_params=_params("arbitrary"),
        name="expert_blocks",
    )(meta, xs, w_gu, w_down)


def _routed_rows(xp, eidx, rank, counts, w_gu, w_down, layer):
    n = xp.shape[0]
    k = eidx.shape[0]
    n_exp = w_gu.shape[1]
    counts = counts.reshape(n_exp).astype(jnp.int32)
    padded = (counts + MOE_BLOCK - 1) // MOE_BLOCK * MOE_BLOCK
    seg_end = jnp.cumsum(padded)
    n_blocks = k * n // MOE_BLOCK + n_exp
    blk_start = jnp.arange(n_blocks, dtype=jnp.int32) * MOE_BLOCK
    blk_expert = jnp.minimum(jnp.sum(seg_end[None, :] <= blk_start[:, None], axis=1), n_exp - 1)
    meta = jnp.concatenate([blk_expert, seg_end[-1:] // MOE_BLOCK]).astype(jnp.int32)
    dest = _dest_rows(seg_end - padded, eidx, rank)
    xs = _sc_scatter_rows(xp, dest, n_blocks * MOE_BLOCK)
    return _expert_blocks(meta, xs, w_gu, w_down, layer), dest


def _shared_ple_kernel(h1_ref, h1b_ref, p_ref, wgu_ref, wd_ref, wpg_ref, bpg_ref, wpe_ref, o_ref, *, alpha):
    xb = h1b_ref[...]
    ff = wd_ref.shape[0]
    gu = _dot(xb, wgu_ref[...])
    shared = _dot((_silu(gu[:, :ff]) * gu[:, ff:]).astype(BF16), wd_ref[...])
    ple = _sigmoid(_dot(xb, wpg_ref[...]) + bpg_ref[...]) * _dot(p_ref[0].astype(BF16), wpe_ref[...])
    o_ref[...] = alpha * h1_ref[...] + shared + ple


def _shared_ple(h1, h1b, p3, layer, sh_w_gu, sh_w_down, ple_w_g, ple_b_g, ple_w_e, alpha):
    n, d = h1.shape
    t = min(ROW_TILE, n)
    pdim = p3.shape[2]
    ff2 = sh_w_gu.shape[1]
    tile = lambda cols: pl.BlockSpec((t, cols), lambda i: (i, 0))
    return pl.pallas_call(
        functools.partial(_shared_ple_kernel, alpha=alpha),
        grid=(n // t,),
        in_specs=[
            tile(d), tile(d), pl.BlockSpec((1, t, pdim), lambda i: (layer, i, 0)),
            _const_spec((d, ff2)),
            _const_spec((ff2 // 2, d)),
            _const_spec((d, d)),
            _const_spec((1, d)),
            _const_spec((pdim, d)),
        ],
        out_specs=tile(d),
        out_shape=jax.ShapeDtypeStruct((n, d), F32),
        compiler_params=_params("parallel"),
        name="shared_ple",
    )(h1, h1b, p3, sh_w_gu, sh_w_down, ple_w_g, ple_b_g.reshape(1, -1), ple_w_e)


def _combine_norm_kernel(pre_ref, yk_ref, gwt_ref, g_ref, b_ref, o_ref):
    gwt = gwt_ref[...]
    routed_hi = routed_lo = None
    for k in range(yk_ref.shape[0]):
        hi, lo = _unpack_bf16_pairs(yk_ref[k])
        wk = gwt[:, k:k + 1]
        routed_hi = hi * wk if routed_hi is None else routed_hi + hi * wk
        routed_lo = lo * wk if routed_lo is None else routed_lo + lo * wk
    routed = jnp.concatenate([routed_hi, routed_lo], axis=1)
    o_ref[...] = _layer_norm(pre_ref[...] + routed, g_ref[...], b_ref[...])


def _combine_norm(pre, ys, dest, gwt, ln_g, ln_b):
    n, d = pre.shape
    top_k = dest.shape[0]
    t = min(ROW_TILE, n)
    yk = _sc_gather_rows(ys, dest.reshape(top_k * n)).reshape(top_k, n, d // 2)
    tile = lambda cols: pl.BlockSpec((t, cols), lambda i: (i, 0))
    return pl.pallas_call(
        _combine_norm_kernel,
        grid=(n // t,),
        in_specs=[tile(d), pl.BlockSpec((top_k, t, d // 2), lambda i: (0, i, 0)), tile(top_k),
                  _const_spec((1, d)), _const_spec((1, d))],
        out_specs=tile(d),
        out_shape=jax.ShapeDtypeStruct((n, d), F32),
        compiler_params=_params("parallel"),
        name="combine_norm",
    )(pre, yk, gwt, ln_g.reshape(1, -1), ln_b.reshape(1, -1))


def _block_diag(w):
    g, i, j = w.shape
    eye = jnp.eye(g, dtype=w.dtype)
    return (eye[:, None, :, None] * w[:, :, None, :]).reshape(g * i, g * j)


def kernel(x, p, ln_in_g, ln_in_b, w_in, b_in, ret_norm_g, ret_norm_b, gdn_conv_w, gdn_a_log, gdn_dt_bias, gdn_norm_g, lru_conv_w, lru_conv_b, lru_w_r, lru_b_r, lru_w_i, lru_b_i, lru_lambda, w_branch, w_out, ln1_g, ln1_b, router_w, router_b, exp_w_gu, exp_w_down, sh_w_gu, sh_w_down, ple_w_e, ple_w_g, ple_b_g, ln2_g, ln2_b):
    bsz, seq, d = x.shape
    depth = w_in.shape[0]
    n = bsz * seq
    width = N_HEADS * HEAD_DIM
    alpha = (2 * depth) ** 0.25
    lanes = LANES
    o_ret = 0
    o_gdn = o_ret + 4 * width
    o_small = o_gdn + 4 * width
    o_lru = o_small + 2 * N_HEADS
    o_mg = o_lru + 2 * width

    p3 = p.reshape(depth, n, -1)
    bf = lambda a: a.astype(BF16)
    h = _entry_norm(x.reshape(n, d), ln_in_g, ln_in_b)
    for l in range(depth):
        wl, bl = w_in[l], b_in[l]
        w_small = bf(jnp.zeros((d, lanes), F32).at[:, :2 * N_HEADS].set(wl[:, o_small:o_lru]))
        b_small = jnp.zeros((lanes,), F32).at[:2 * N_HEADS].set(bl[o_small:o_lru])
        w_ret, w_gdn, w_lru, w_mg = bf(wl[:, o_ret:o_gdn]), bf(wl[:, o_gdn:o_small]), bf(wl[:, o_lru:o_mg]), bf(wl[:, o_mg:])
        w_r, w_i = bf(_block_diag(lru_w_r[l])), bf(_block_diag(lru_w_i[l]))
        w_br, w_o, r_wt = bf(w_branch[l]), bf(w_out[l]), bf(router_w[l].T)
        w_sgu, w_sd, w_pg, w_pe = bf(sh_w_gu[l]), bf(sh_w_down[l]), bf(ple_w_g[l]), bf(ple_w_e[l])
        y_ret, y_gdn, y_lru = _mixers(
            h.reshape(bsz, seq, d), (w_ret, bl[o_ret:o_gdn], ret_norm_g[l], ret_norm_b[l]),
            (w_gdn, bl[o_gdn:o_small], w_small, b_small, gdn_conv_w[l], gdn_a_log[l], gdn_dt_bias[l], gdn_norm_g[l]),
            (w_lru, bl[o_lru:o_mg], lru_conv_w[l], lru_conv_b[l], w_r, lru_b_r[l], w_i, lru_b_i[l], lru_lambda[l]))
        h1, h1b, xp, eidx, rank, gwt, counts = _merge(
            h, y_ret.reshape(n, width), y_gdn.reshape(n, width), y_lru.reshape(n, width), w_mg, bl[o_mg:],
            w_br, w_o, ln1_g[l], ln1_b[l], r_wt, router_b[l], alpha)
        pre = _shared_ple(h1, h1b, p3, l, w_sgu, w_sd, w_pg, ple_b_g[l], w_pe, alpha)
        ys, dest = _routed_rows(xp, eidx, rank, counts, exp_w_gu, exp_w_down, l)
        h = _combine_norm(pre, ys, dest, gwt, ln2_g[l], ln2_b[l])
    return h.reshape(bsz, seq, d)
```

```python
import functools

import numpy as np
import jax
import jax.numpy as jnp
from jax import lax
from jax.experimental import pallas as pl
from jax.experimental.pallas import tpu as pltpu
from jax.experimental.pallas import tpu_sc as plsc

F32 = jnp.float32
BF16 = jnp.bfloat16

LANES = 128

HEAD_DIM = 128
N_HEADS = 4
LRU_C = 8.0
CONV_WIDTH = 4
N_BRANCHES = 3
ROPE_BASE = 10000.0
N_GROUPS = 8
TOPK_GROUPS = 4
TOP_K = 8
ROUTED_SCALE = 2.5
LN_EPS = 1e-5
GDN_CHUNK = 128
GDN_BASE = 16
SEQ_TILE = 512
MERGE_TILE = 512
MERGE_SUB = 128
ENTRY_TILE = 1024
ROW_TILE = 512
MOE_BLOCK = 1024
EXPERT_SUB = 256
DEST_TILE = 4096
SC_WINDOW = 128
CARRY_ROWS = 8
SCAN_GROUP = 16
VMEM_LIMIT_BYTES = 56 * 1024 * 1024
NEG_INF = float("-inf")


def _const_spec(shape):
    nd = len(shape)
    return pl.BlockSpec(shape, lambda *_: (0,) * nd, pipeline_mode=pl.Buffered(1))


def _params(*sem):
    return pltpu.CompilerParams(dimension_semantics=sem, vmem_limit_bytes=VMEM_LIMIT_BYTES)


def _layer_norm(x, g, b):
    mu = jnp.mean(x, axis=-1, keepdims=True)
    xc = x - mu
    var = jnp.mean(xc * xc, axis=-1, keepdims=True)
    return xc * lax.rsqrt(var + LN_EPS) * g + b


def _sigmoid(x):
    return 1.0 / (1.0 + jnp.exp(-x))


def _silu(x):
    return x * _sigmoid(x)


def _softplus(x):
    return jnp.maximum(x, 0.0) + jnp.log1p(jnp.exp(-jnp.abs(x)))


def _dot(a, b):
    return jnp.dot(a, b, preferred_element_type=F32)


def _dot_nt(a, b):
    return lax.dot_general(a, b, (((1,), (1,)), ((), ())), preferred_element_type=F32)


def _dot_tn(a, b):
    return lax.dot_general(a, b, (((0,), (0,)), ((), ())), preferred_element_type=F32)


def _ln_kernel(x_ref, g_ref, b_ref, o_ref):
    o_ref[...] = _layer_norm(x_ref[...], g_ref[...], b_ref[...])


def _entry_norm(x2, g, b):
    n, d = x2.shape
    t = min(ENTRY_TILE, n)
    return pl.pallas_call(
        _ln_kernel,
        grid=(n // t,),
        in_specs=[pl.BlockSpec((t, d), lambda i: (i, 0)), _const_spec((1, d)), _const_spec((1, d))],
        out_specs=pl.BlockSpec((t, d), lambda i: (i, 0)),
        out_shape=jax.ShapeDtypeStruct((n, d), F32),
        compiler_params=_params("parallel"),
        name="entry_norm",
    )(x2, g.reshape(1, d), b.reshape(1, d))


def _retention_body(hb, w_ref, b_ref, cos_ref, sin_ref, dmat_ref, qd_ref, kd_ref, ng_ref, nb_ref,
                    y_ref, state_ref, *, chunk_decay):
    width = N_HEADS * HEAD_DIM
    proj = _dot(hb, w_ref[...]) + b_ref[...]
    yield
    cos = cos_ref[...]
    sin = sin_ref[...]
    heads = range(N_HEADS)
    qs, ks, vbs = [], [], []
    for hh in heads:
        lo = hh * HEAD_DIM
        q = proj[:, lo:lo + HEAD_DIM]
        k = proj[:, width + lo:width + lo + HEAD_DIM]
        qs.append(q * cos + pltpu.roll(q, HEAD_DIM // 2, axis=1) * sin)
        ks.append((k * cos + pltpu.roll(k, HEAD_DIM // 2, axis=1) * sin) * (HEAD_DIM ** -0.5))
        vbs.append(proj[:, 2 * width + lo:2 * width + lo + HEAD_DIM].astype(BF16))
        yield
    states = [state_ref[hh] for hh in heads]
    scores, inter, outs = [], [], []
    for hh in heads:
        scores.append((_dot_nt(qs[hh].astype(BF16), ks[hh].astype(BF16)) * dmat_ref[hh]).astype(BF16))
        yield
    for hh in heads:
        inter.append(_dot((qs[hh] * qd_ref[hh]).astype(BF16), states[hh].astype(BF16)))
        yield
    for hh in heads:
        state_ref[hh] = states[hh] * chunk_decay[hh] + _dot_tn((ks[hh] * kd_ref[hh]).astype(BF16), vbs[hh])
        yield
    for hh in heads:
        outs.append(_dot(scores[hh], vbs[hh]) + inter[hh])
        yield
    for hh in heads:
        lo = hh * HEAD_DIM
        o = outs[hh]
        gate = proj[:, 3 * width + lo:3 * width + lo + HEAD_DIM]
        mu = jnp.mean(o, axis=-1, keepdims=True)
        oc = o - mu
        var = jnp.mean(oc * oc, axis=-1, keepdims=True)
        on = oc * lax.rsqrt(var + LN_EPS) * ng_ref[:, lo:lo + HEAD_DIM] + nb_ref[:, lo:lo + HEAD_DIM]
        y_ref[:, lo:lo + HEAD_DIM] = (_silu(gate) * on).astype(y_ref.dtype)
        yield


def _retention_tables(seq, tile):
    half = HEAD_DIM // 2
    inv_freq = ROPE_BASE ** (-np.linspace(0.0, 1.0, half))
    ang = np.arange(seq)[:, None] * inv_freq[None, :]
    cos = np.concatenate([np.cos(ang), np.cos(ang)], axis=1)
    sin = np.concatenate([-np.sin(ang), np.sin(ang)], axis=1)
    log_gamma = np.log1p(-np.exp2(-5.0 - np.arange(N_HEADS)))
    pos = np.arange(tile)
    diff = pos[:, None] - pos[None, :]
    dmat = np.where(diff >= 0, np.exp(log_gamma[:, None, None] * np.maximum(diff, 0)), 0.0)
    qd = np.exp(log_gamma[:, None] * (pos + 1.0))[:, :, None] * np.ones((1, 1, HEAD_DIM))
    kd = np.exp(log_gamma[:, None] * (tile - 1.0 - pos))[:, :, None] * np.ones((1, 1, HEAD_DIM))
    chunk_decay = tuple(float(c) for c in np.exp(log_gamma * tile))
    as32 = lambda a: jnp.asarray(a, F32)
    return as32(cos), as32(sin), as32(dmat), as32(qd), as32(kd), chunk_decay


def _causal_conv(x, xs_ref, cw_ref):
    t = x.shape[0]
    xs_ref[CARRY_ROWS:, :] = x
    ext = xs_ref[...]
    acc = ext * cw_ref[0:1, :]
    for j in range(1, CONV_WIDTH):
        acc = pltpu.roll(acc, 1, axis=0) + ext * cw_ref[j:j + 1, :]
    xs_ref[0:CARRY_ROWS, :] = xs_ref[t:t + CARRY_ROWS, :]
    return acc[CARRY_ROWS:, :]


def _cumsum_rows(x):
    n = x.shape[0]
    row = lax.broadcasted_iota(jnp.int32, x.shape, 0)
    d = 1
    while d < n:
        x = x + jnp.where(row >= d, pltpu.roll(x, d, axis=0), 0.0)
        d *= 2
    return x


def _gdn_body(hb, w_ref, b_ref, ws_ref, bs_ref, cw_ref, alog_ref, dtb_ref, ng_ref,
              y_ref, xs_ref, state_ref, u_ref, wf_ref, w_s_ref, qd_ref, kd_ref, qk_ref):
    width = N_HEADS * HEAD_DIM
    c = GDN_CHUNK
    t = hb.shape[0]
    proj = _dot(hb, w_ref[...]) + b_ref[...]
    small = _dot(hb, ws_ref[...]) + bs_ref[...]
    yield
    qkv = _silu(_causal_conv(proj[:, :3 * width], xs_ref, cw_ref))
    yield
    beta_all = _sigmoid(small)
    la_all = -jnp.exp(alog_ref[...]) * _softplus(small + dtb_ref[...])

    ri = lax.broadcasted_iota(jnp.int32, (c, c), 0)
    ci = lax.broadcasted_iota(jnp.int32, (c, c), 1)
    lower = ri >= ci
    strict = ri > ci
    base_blocks = (ri // GDN_BASE) == (ci // GDN_BASE)

    items = [(n, hh) for n in range(t // c) for hh in range(N_HEADS)]
    gcs = {}
    for n in range(t // c):
        la_c = la_all[n * c:(n + 1) * c, :]
        gc_c = _cumsum_rows(la_c)
        gcs[n] = (la_c, gc_c, jnp.exp(gc_c))
    g_last, negs, pws, rems = {}, {}, {}, {}
    for n, hh in items:
        r0, lo = n * c, hh * HEAD_DIM
        rows, cols = slice(r0, r0 + c), slice(lo, lo + HEAD_DIM)
        la_c, gc_c, egc_c = gcs[n]
        q = qkv[rows, lo:lo + HEAD_DIM]
        k = qkv[rows, width + lo:width + lo + HEAD_DIM]
        v = qkv[rows, 2 * width + lo:2 * width + lo + HEAD_DIM]
        q = q * lax.rsqrt(jnp.sum(q * q, axis=-1, keepdims=True) + 1e-6) * (HEAD_DIM ** -0.5)
        k = k * lax.rsqrt(jnp.sum(k * k, axis=-1, keepdims=True) + 1e-6)
        beta = beta_all[rows, hh:hh + 1]
        la = la_c[:, N_HEADS + hh:N_HEADS + hh + 1]
        gc = gc_c[:, N_HEADS + hh:N_HEADS + hh + 1]
        egc = egc_c[:, N_HEADS + hh:N_HEADS + hh + 1]
        gc_row = jnp.sum(jnp.where(ri <= ci, jnp.broadcast_to(la, (c, c)), 0.0), axis=0, keepdims=True)
        gc_last = gc_row[:, c - 1:c]
        decay = jnp.where(lower, jnp.exp(jnp.where(lower, gc - gc_row, 0.0)), 0.0)
        kb = k * beta
        kbf = k.astype(BF16)
        a_neg = jnp.where(strict, -(_dot_nt(kb.astype(BF16), kbf) * decay), 0.0)
        negs[n, hh] = a_neg
        pws[n, hh] = rems[n, hh] = jnp.where(base_blocks, a_neg, 0.0)
        u_ref[rows, cols] = v * beta
        wf_ref[rows, cols] = kb * egc
        qk_ref[hh, rows, :] = (_dot_nt(q.astype(BF16), kbf) * decay).astype(BF16)
        qd_ref[rows, cols] = (q * egc).astype(BF16)
        kd_ref[rows, cols] = (k * jnp.exp(gc_last - gc)).astype(BF16)
        g_last[n, hh] = jnp.exp(gc_last)
        yield
    m = 2
    while m < GDN_BASE:
        for it in items:
            pwb = pws[it].astype(BF16)
            pws[it] = _dot(pwb, pwb)
        yield
        for it in items:
            rems[it] = rems[it] + pws[it] + _dot(rems[it].astype(BF16), pws[it].astype(BF16))
        yield
        m *= 2
    size = GDN_BASE
    while size < c:
        pair = ((ri // (2 * size)) == (ci // (2 * size))) & ((ri // size) != (ci // size))
        mids = {}
        for it in items:
            link = jnp.where(pair, negs[it], 0.0)
            mids[it] = link + _dot(rems[it].astype(BF16), link.astype(BF16))
        yield
        for it in items:
            rems[it] = rems[it] + mids[it] + _dot(mids[it].astype(BF16), rems[it].astype(BF16))
        yield
        size *= 2
    for n, hh in items:
        rows, cols = slice(n * c, (n + 1) * c), slice(hh * HEAD_DIM, (hh + 1) * HEAD_DIM)
        remb = rems[n, hh].astype(BF16)
        u_ref[rows, cols] = u_ref[rows, cols] + _dot(remb, u_ref[rows, cols].astype(BF16))
        w_s_ref[rows, cols] = (wf_ref[rows, cols] + _dot(remb, wf_ref[rows, cols].astype(BF16))).astype(BF16)
    yield

    heads = range(N_HEADS)
    for n in range(t // c):
        rows = slice(n * c, (n + 1) * c)
        cols = [slice(hh * HEAD_DIM, (hh + 1) * HEAD_DIM) for hh in heads]
        states = [state_ref[hh] for hh in heads]
        sbs = [s.astype(BF16) for s in states]
        vnbs = [(u_ref[rows, cols[hh]] - _dot(w_s_ref[rows, cols[hh]], sbs[hh])).astype(BF16) for hh in heads]
        yield
        outs = [_dot(qd_ref[rows, cols[hh]], sbs[hh]) + _dot(qk_ref[hh, rows, :], vnbs[hh]) for hh in heads]
        for hh in heads:
            state_ref[hh] = states[hh] * g_last[n, hh] + _dot_tn(kd_ref[rows, cols[hh]], vnbs[hh])
        yield
        for hh in heads:
            o = outs[hh]
            o = o * lax.rsqrt(jnp.mean(o * o, axis=-1, keepdims=True) + 1e-6) * ng_ref[...]
            og = proj[rows, 3 * width + hh * HEAD_DIM:3 * width + (hh + 1) * HEAD_DIM]
            y_ref[rows, cols[hh]] = (o * _silu(og)).astype(y_ref.dtype)
        yield


def _lru_body(hb, w_ref, b_ref, cw_ref, cb_ref, wr_ref, br_ref, wi_ref, bi_ref, lam_ref,
              y_ref, xs_ref, carry_ref):
    width = cw_ref.shape[1]
    t = hb.shape[0]
    proj = _dot(hb, w_ref[...]) + b_ref[...]
    yield
    xc = _causal_conv(proj[:, :width], xs_ref, cw_ref) + cb_ref[...]
    xcb = xc.astype(BF16)
    yield
    r = _sigmoid(_dot(xcb, wr_ref[...]) + br_ref[...])
    gi = _sigmoid(_dot(xcb, wi_ref[...]) + bi_ref[...])
    yield
    log_a = -LRU_C * r * _softplus(-lam_ref[...])
    a = jnp.exp(log_a)
    th = jnp.tanh(log_a)
    hs = jnp.sqrt(-2.0 * th / (1.0 - th)) * (gi * xc)
    row = lax.broadcasted_iota(jnp.int32, (t, width), 0) % SCAN_GROUP
    d = 1
    while d < SCAN_GROUP:
        keep = row >= d
        hs = hs + a * jnp.where(keep, pltpu.roll(hs, d, axis=0), 0.0)
        a = a * jnp.where(keep, pltpu.roll(a, d, axis=0), 1.0)
        d *= 2
        yield
    gate = jax.nn.gelu(proj[:, width:], approximate=True)
    carry = carry_ref[...]
    for g in range(t // SCAN_GROUP):
        rows = slice(g * SCAN_GROUP, (g + 1) * SCAN_GROUP)
        hg = hs[rows, :] + a[rows, :] * carry
        carry = hg[SCAN_GROUP - 1:SCAN_GROUP, :]
        y_ref[rows, :] = (gate[rows, :] * hg).astype(y_ref.dtype)
        if g % 8 == 7:
            yield
    carry_ref[...] = carry


N_RET_IN, N_GDN_IN, N_LRU_IN = 9, 8, 9
MIX_STRIDE = (1, 1, 2)


def _mixers_kernel(h_ref, *refs, chunk_decay):
    ret_in, refs = refs[:N_RET_IN], refs[N_RET_IN:]
    gdn_in, refs = refs[:N_GDN_IN], refs[N_GDN_IN:]
    lru_in, refs = refs[:N_LRU_IN], refs[N_LRU_IN:]
    yr_ref, yg_ref, yl_ref = refs[:3]
    ret_state, gdn_xs, gdn_state, u_ref, wf_ref, w_s_ref, qd_ref, kd_ref, qk_ref, lru_xs, lru_carry = refs[3:]

    @pl.when(pl.program_id(1) == 0)
    def _():
        ret_state[...] = jnp.zeros_like(ret_state)
        gdn_state[...] = jnp.zeros_like(gdn_state)
        lru_carry[...] = jnp.zeros_like(lru_carry)
        gdn_xs[0:CARRY_ROWS, :] = jnp.zeros((CARRY_ROWS, gdn_xs.shape[1]), F32)
        lru_xs[0:CARRY_ROWS, :] = jnp.zeros((CARRY_ROWS, lru_xs.shape[1]), F32)

    hb = h_ref[0].astype(BF16)
    branches = [
        (_gdn_body(hb, *gdn_in, yg_ref.at[0], gdn_xs, gdn_state, u_ref, wf_ref, w_s_ref, qd_ref, kd_ref, qk_ref),
         MIX_STRIDE[0]),
        (_retention_body(hb, *ret_in, yr_ref.at[0], ret_state, chunk_decay=chunk_decay), MIX_STRIDE[1]),
        (_lru_body(hb, *lru_in, yl_ref.at[0], lru_xs, lru_carry), MIX_STRIDE[2]),
    ]
    tick = 0
    while branches:
        for gen, stride in list(branches):
            if tick % stride == 0 and next(gen, StopIteration) is StopIteration:
                branches.remove((gen, stride))
        tick += 1


def _mixers(h, ret_args, gdn_args, lru_args):
    bsz, seq, d = h.shape
    t = min(SEQ_TILE, seq)
    width = N_HEADS * HEAD_DIM
    row = lambda vec: vec.reshape(1, -1)
    w_ret, b_ret, ret_g, ret_b = ret_args
    w_gdn, b_gdn, w_small, b_small, gdn_cw, a_log, dt_bias, gdn_g = gdn_args
    w_lru, b_lru, lru_cw, lru_cb, w_r, b_r, w_i, b_i, lam = lru_args
    lanes = w_small.shape[1]
    lru_w = lru_cw.shape[1]
    cos, sin, dmat, qd, kd, chunk_decay = _retention_tables(seq, t)
    pad_row = lambda vec: jnp.zeros((1, lanes), F32).at[0, N_HEADS:2 * N_HEADS].set(vec.astype(F32))
    seq_tile = lambda cols: pl.BlockSpec((1, t, cols), lambda i, j: (i, j, 0))
    pos_tile = pl.BlockSpec((t, HEAD_DIM), lambda i, j: (j, 0))
    ret_specs = [_const_spec((d, 4 * width)), _const_spec((1, 4 * width)), pos_tile, pos_tile,
                 _const_spec((N_HEADS, t, t)), _const_spec((N_HEADS, t, HEAD_DIM)), _const_spec((N_HEADS, t, HEAD_DIM)),
                 _const_spec((1, width)), _const_spec((1, width))]
    gdn_specs = [_const_spec((d, 4 * width)), _const_spec((1, 4 * width)), _const_spec((d, lanes)),
                 _const_spec((1, lanes)), _const_spec((CONV_WIDTH, 3 * width)), _const_spec((1, lanes)),
                 _const_spec((1, lanes)), _const_spec((1, HEAD_DIM))]
    lru_specs = [_const_spec((d, 2 * lru_w)), _const_spec((1, 2 * lru_w)), _const_spec((CONV_WIDTH, lru_w)),
                 _const_spec((1, lru_w)), _const_spec((lru_w, lru_w)), _const_spec((1, lru_w)),
                 _const_spec((lru_w, lru_w)), _const_spec((1, lru_w)), _const_spec((1, lru_w))]
    assert (len(ret_specs), len(gdn_specs), len(lru_specs)) == (N_RET_IN, N_GDN_IN, N_LRU_IN)
    out = jax.ShapeDtypeStruct((bsz, seq, width), BF16)
    return pl.pallas_call(
        functools.partial(_mixers_kernel, chunk_decay=chunk_decay),
        grid=(bsz, seq // t),
        in_specs=[seq_tile(d)] + ret_specs + gdn_specs + lru_specs,
        out_specs=[seq_tile(width), seq_tile(width), seq_tile(lru_w)],
        out_shape=[out, out, jax.ShapeDtypeStruct((bsz, seq, lru_w), BF16)],
        scratch_shapes=[pltpu.VMEM((N_HEADS, HEAD_DIM, HEAD_DIM), F32),
                        pltpu.VMEM((CARRY_ROWS + t, 3 * width), F32),
                        pltpu.VMEM((N_HEADS, HEAD_DIM, HEAD_DIM), F32),
                        pltpu.VMEM((t, width), F32),
                        pltpu.VMEM((t, width), F32),
                        pltpu.VMEM((t, width), BF16),
                        pltpu.VMEM((t, width), BF16),
                        pltpu.VMEM((t, width), BF16),
                        pltpu.VMEM((N_HEADS, t, GDN_CHUNK), BF16),
                        pltpu.VMEM((CARRY_ROWS + t, lru_w), F32),
                        pltpu.VMEM((1, lru_w), F32)],
        compiler_params=_params("parallel", "arbitrary"),
        name="token_mixers",
    )(h, w_ret, row(b_ret), cos, sin, dmat, qd, kd, row(ret_g), row(ret_b),
      w_gdn, row(b_gdn), w_small, row(b_small), gdn_cw, pad_row(a_log), pad_row(dt_bias), row(gdn_g),
      w_lru, row(b_lru), lru_cw, row(lru_cb), w_r, row(b_r), w_i, row(b_i), row(lam))


def _first_index_of_max(x, idx, size):
    m = jnp.max(x, axis=0, keepdims=True)
    first = jnp.min(jnp.where(x == m, idx, size), axis=0, keepdims=True)
    return m, idx == first


def _route(logits_t, bias_col, out):
    n_exp, t = logits_t.shape
    per_group = n_exp // N_GROUPS
    scores = _sigmoid(logits_t)
    sel = scores + bias_col
    idx_g = lax.broadcasted_iota(jnp.int32, (per_group, t), 0)
    group_scores = []
    for g in range(N_GROUPS):
        x = sel[g * per_group:(g + 1) * per_group, :]
        m1, hit = _first_index_of_max(x, idx_g, per_group)
        m2 = jnp.max(jnp.where(hit, NEG_INF, x), axis=0, keepdims=True)
        group_scores.append(m1 + m2)
        if g % 2 == 1:
            yield
    gsc = jnp.concatenate(group_scores, axis=0)
    idx_n = lax.broadcasted_iota(jnp.int32, (N_GROUPS, t), 0)
    gmask = jnp.zeros((N_GROUPS, t), F32)
    for _ in range(TOPK_GROUPS):
        _, hit = _first_index_of_max(gsc, idx_n, N_GROUPS)
        gmask = jnp.where(hit, 1.0, gmask)
        gsc = jnp.where(hit, NEG_INF, gsc)
    yield
    emask = jnp.concatenate([jnp.broadcast_to(gmask[g:g + 1, :], (per_group, t)) for g in range(N_GROUPS)], axis=0)
    cand = jnp.where(emask > 0.0, sel, NEG_INF)
    idx_e = lax.broadcasted_iota(jnp.int32, (n_exp, t), 0)
    picked = jnp.zeros((n_exp, t), F32)
    hits = []
    for _ in range(TOP_K):
        _, hit = _first_index_of_max(cand, idx_e, n_exp)
        hits.append(hit)
        picked = jnp.where(hit, 1.0, picked)
        cand = jnp.where(hit, NEG_INF, cand)
        yield
    gw = jnp.where(picked > 0.0, scores, 0.0)
    out.update(combine=gw / jnp.sum(gw, axis=0, keepdims=True) * ROUTED_SCALE, picked=picked, hits=hits, idx_e=idx_e)


def _pack_bf16_pairs(x):
    c = x.shape[1] // 2
    hi = pltpu.bitcast(x[:, :c].astype(BF16).astype(F32), jnp.int32)
    lo = pltpu.bitcast(x[:, c:].astype(BF16).astype(F32), jnp.int32)
    return hi | lax.shift_right_logical(lo, jnp.full(lo.shape, 16, jnp.int32))


def _unpack_bf16_pairs(w):
    hi = pltpu.bitcast(w & jnp.int32(-65536), F32)
    lo = pltpu.bitcast(lax.shift_left(w, jnp.full(w.shape, 16, jnp.int32)), F32)
    return hi, lo


def _merge_kernel(h_ref, yr_ref, yg_ref, yl_ref, wmg_ref, bmg_ref, wbr_ref, wout_ref, g_ref, b_ref,
                  rwt_ref, rb_ref, eye_ref, tri_ref, h1_ref, h1b_ref, xp_ref, eidx_ref, rank_ref, gwt_ref,
                  count_ref, *, alpha):
    @pl.when(pl.program_id(0) == 0)
    def _():
        count_ref[...] = jnp.zeros_like(count_ref)

    d = h_ref.shape[1]
    sub = eye_ref.shape[0]
    parts = [slice(s, s + sub) for s in range(0, h_ref.shape[0], sub)]
    pick = lambda hit, vals, zero: jnp.sum(jnp.where(hit, vals, zero), axis=0, keepdims=True)

    def dense(rows, logits):
        h = h_ref[rows, :]
        hb = h.astype(BF16)
        gates = _sigmoid(_dot(hb, wmg_ref[...]) + bmg_ref[...])
        yield
        mixed = None
        for n, y_ref in enumerate((yr_ref, yg_ref, yl_ref)):
            term = gates[:, n * d:(n + 1) * d] * _dot(y_ref[rows, :], wbr_ref[n])
            mixed = term if mixed is None else mixed + term
            yield
        mix = _dot(mixed.astype(BF16), wout_ref[...])
        yield
        h1 = _layer_norm(alpha * h + mix, g_ref[...], b_ref[...])
        h1b = h1.astype(BF16)
        h1_ref[rows, :] = h1
        h1b_ref[rows, :] = h1b
        xp_ref[rows, :] = _pack_bf16_pairs(h1)
        logits.append(_dot_nt(rwt_ref[...], h1b))
        yield

    def routing(rows, logit):
        res = {}
        yield from _route(logit, rb_ref[...], res)
        combine_t, picked, hits, idx_e = res["combine"], res["picked"], res["hits"], res["idx_e"]
        rank_full = count_ref[...] + _dot(picked.astype(BF16), tri_ref[...])
        count_ref[...] += jnp.sum(picked, axis=1, keepdims=True)
        eidx_ref[:, rows] = jnp.concatenate([pick(hit, idx_e, 0) for hit in hits], axis=0)
        yield
        rank_ref[:, rows] = jnp.concatenate([pick(hit, rank_full, 0.0) for hit in hits], axis=0).astype(jnp.int32)
        gw = jnp.concatenate([pick(hit, combine_t, 0.0) for hit in hits], axis=0)
        gwt_ref[rows, :] = lax.dot_general(eye_ref[...], gw, (((1,), (1,)), ((), ())),
                                           preferred_element_type=F32, precision=lax.Precision.HIGHEST)
        yield

    def interleave(gens):
        while gens:
            gens = [g for g in gens if next(g, StopIteration) is not StopIteration]

    logits = [[] for _ in parts]
    interleave([dense(rows, logits[s]) for s, rows in enumerate(parts)])
    interleave([routing(rows, logits[s][0]) for s, rows in enumerate(parts)])


def _merge(h2, y_ret, y_gdn, y_lru, w_mg, b_mg, w_branch, w_out, ln_g, ln_b, router_wt, router_b, alpha):
    n, d = h2.shape
    t = min(MERGE_TILE, n)
    width = y_ret.shape[1]
    n_exp = router_wt.shape[0]
    tile = lambda cols: pl.BlockSpec((t, cols), lambda i: (i, 0))
    lane_tile = pl.BlockSpec((TOP_K, t), lambda i: (0, i))
    sub = min(MERGE_SUB, t)
    tri = jnp.triu(jnp.ones((sub, sub), BF16), 1)
    return pl.pallas_call(
        functools.partial(_merge_kernel, alpha=alpha),
        grid=(n // t,),
        in_specs=[
            tile(d), tile(width), tile(width), tile(width),
            _const_spec((d, N_BRANCHES * d)),
            _const_spec((1, N_BRANCHES * d)),
            _const_spec((N_BRANCHES, width, d)),
            _const_spec((d, d)),
            _const_spec((1, d)),
            _const_spec((1, d)),
            _const_spec((n_exp, d)),
            _const_spec((n_exp, 1)),
            _const_spec((sub, sub)),
            _const_spec((sub, sub)),
        ],
        out_specs=[tile(d), tile(d), tile(d // 2), lane_tile, lane_tile, tile(TOP_K),
                   pl.BlockSpec((n_exp, 1), lambda i: (0, 0))],
        out_shape=[jax.ShapeDtypeStruct((n, d), F32), jax.ShapeDtypeStruct((n, d), BF16),
                   jax.ShapeDtypeStruct((n, d // 2), jnp.int32),
                   jax.ShapeDtypeStruct((TOP_K, n), jnp.int32), jax.ShapeDtypeStruct((TOP_K, n), jnp.int32),
                   jax.ShapeDtypeStruct((n, TOP_K), F32), jax.ShapeDtypeStruct((n_exp, 1), F32)],
        compiler_params=_params("arbitrary"),
        name="merge_route",
    )(h2, y_ret, y_gdn, y_lru, w_mg, b_mg.reshape(1, -1), w_branch, w_out, ln_g.reshape(1, -1),
      ln_b.reshape(1, -1), router_wt, router_b.reshape(-1, 1), jnp.eye(sub, dtype=F32), tri)


def _dest_kernel(start_ref, eidx_ref, rank_ref, o_ref, *, n_exp):
    eidx = eidx_ref[...]
    dest = rank_ref[...]
    for e in range(n_exp):
        dest = dest + jnp.where(eidx == e, start_ref[e], 0)
    o_ref[...] = dest


def _dest_rows(seg_start, eidx, rank):
    k, n = eidx.shape
    t = min(DEST_TILE, n)
    blk = pl.BlockSpec((k, t), lambda i, s: (0, i))
    return pl.pallas_call(
        functools.partial(_dest_kernel, n_exp=seg_start.shape[0]),
        grid_spec=pltpu.PrefetchScalarGridSpec(num_scalar_prefetch=1, grid=(n // t,), in_specs=[blk, blk],
                                               out_specs=blk),
        out_shape=jax.ShapeDtypeStruct((k, n), jnp.int32),
        compiler_params=_params("parallel"),
        name="dest_rows",
    )(seg_start, eidx, rank)


def _sc_workers():
    info = plsc.get_sparse_core_info()
    return info.num_cores, info.num_subcores


def _sc_mesh():
    return plsc.VectorSubcoreMesh(core_axis_name="c", subcore_axis_name="s")


def _sc_scatter_rows(x, dest, n_rows):
    n, c = x.shape
    k = dest.shape[0]
    n_cores, n_sub = _sc_workers()
    per_worker = n // (n_cores * n_sub)
    steps = per_worker // SC_WINDOW

    @functools.partial(pl.kernel, out_type=jax.ShapeDtypeStruct((n_rows, c), x.dtype), mesh=_sc_mesh(),
                       scratch_types=[pltpu.VMEM((k, SC_WINDOW), jnp.int32), pltpu.VMEM((SC_WINDOW, c), x.dtype),
                                      pltpu.SemaphoreType.DMA, pltpu.SemaphoreType.DMA],
                       name="dispatch_rows")
    def scatter(x_hbm, d_hbm, o_hbm, idx_v, rows_v, sem, row_sem):
        base = (lax.axis_index("s") * n_cores + lax.axis_index("c")) * per_worker

        @pl.loop(0, steps)
        def _(j):
            off = base + j * SC_WINDOW
            loads = [pltpu.async_copy(x_hbm.at[pl.ds(off, SC_WINDOW)], rows_v, row_sem)]
            loads += [pltpu.async_copy(d_hbm.at[pl.ds(kk * n + off, SC_WINDOW)], idx_v.at[kk], sem) for kk in range(k)]
            for cp in loads:
                cp.wait()
            copies = [pltpu.async_copy(rows_v, o_hbm.at[idx_v.at[kk]], sem) for kk in range(k)]
            for cp in copies:
                cp.wait()

    return scatter(x, dest.reshape(k * n))


def _sc_gather_rows(table, idx):
    b = idx.shape[0]
    c = table.shape[1]
    n_cores, n_sub = _sc_workers()
    per_worker = b // (n_cores * n_sub)
    steps = per_worker // SC_WINDOW

    half = SC_WINDOW // 2
    half_buf = lambda dtype, *shape: pltpu.VMEM((half,) + shape, dtype)

    @functools.partial(pl.kernel, out_type=jax.ShapeDtypeStruct((b, c), table.dtype), mesh=_sc_mesh(),
                       scratch_types=[half_buf(jnp.int32), half_buf(jnp.int32), half_buf(table.dtype, c),
                                      half_buf(table.dtype, c), pltpu.SemaphoreType.DMA, pltpu.SemaphoreType.DMA,
                                      pltpu.SemaphoreType.DMA, pltpu.SemaphoreType.DMA],
                       name="collect_rows")
    def gather(t_hbm, i_hbm, o_hbm, idx_a, idx_b, rows_a, rows_b, gat_a, gat_b, put_a, put_b):
        base = (lax.axis_index("s") * n_cores + lax.axis_index("c")) * per_worker

        @pl.loop(0, steps)
        def _(j):
            off_a = base + j * SC_WINDOW
            off_b = off_a + half
            pltpu.sync_copy(i_hbm.at[pl.ds(off_a, half)], idx_a)
            in_a = pltpu.async_copy(t_hbm.at[idx_a], rows_a, gat_a)
            pltpu.sync_copy(i_hbm.at[pl.ds(off_b, half)], idx_b)
            in_b = pltpu.async_copy(t_hbm.at[idx_b], rows_b, gat_b)
            in_a.wait()
            out_a = pltpu.async_copy(rows_a, o_hbm.at[pl.ds(off_a, half)], put_a)
            in_b.wait()
            out_b = pltpu.async_copy(rows_b, o_hbm.at[pl.ds(off_b, half)], put_b)
            out_a.wait()
            out_b.wait()

    return gather(table, idx)


def _expert_block_kernel(meta_ref, xs_ref, wgu_ref, wd_ref, ys_ref, wgu_b, wd_b, *, n_blocks):
    i = pl.program_id(0)

    @pl.when((i == 0) | (meta_ref[i] != meta_ref[jnp.maximum(i - 1, 0)]))
    def _():
        wgu_b[...] = wgu_ref[0, 0].astype(BF16)
        wd_b[...] = wd_ref[0, 0].astype(BF16)

    @pl.when(i < meta_ref[n_blocks])
    def _():
        half = xs_ref.shape[1]
        ff = wd_b.shape[0]

        def sub_block(rows):
            hi, lo = _unpack_bf16_pairs(xs_ref[rows, :])
            hib, lob = hi.astype(BF16), lo.astype(BF16)
            yield
            gu = _dot(hib, wgu_b[:half, :]) + _dot(lob, wgu_b[half:, :])
            yield
            mid = (_silu(gu[:, :ff]) * gu[:, ff:]).astype(BF16)
            yield
            out = _dot(mid, wd_b[...])
            yield
            ys_ref[rows, :] = _pack_bf16_pairs(out)
            yield

        waiting = [sub_block(slice(r, r + EXPERT_SUB)) for r in range(0, xs_ref.shape[0], EXPERT_SUB)]
        active = []
        while waiting or active:
            if waiting:
                active.append(waiting.pop(0))
            active = [g for g in active if next(g, StopIteration) is not StopIteration]


def _expert_blocks(meta, xs, w_gu, w_down, layer):
    rows, half = xs.shape
    _, _, d, ff2 = w_gu.shape
    n_blocks = rows // MOE_BLOCK
    row_blk = pl.BlockSpec((MOE_BLOCK, half), lambda i, meta: (jnp.minimum(i, meta[n_blocks] - 1), 0))
    return pl.pallas_call(
        functools.partial(_expert_block_kernel, n_blocks=n_blocks),
        grid_spec=pltpu.PrefetchScalarGridSpec(
            num_scalar_prefetch=1, grid=(n_blocks,),
            in_specs=[row_blk,
                      pl.BlockSpec((1, 1, d, ff2), lambda i, meta: (layer, meta[i], 0, 0)),
                      pl.BlockSpec((1, 1, ff2 // 2, d), lambda i, meta: (layer, meta[i], 0, 0))],
            out_specs=row_blk,
            scratch_shapes=[pltpu.VMEM((d, ff2), BF16), pltpu.VMEM((ff2 // 2, d), BF16)]),
        out_shape=jax.ShapeDtypeStruct((rows, half), jnp.int32),
        compiler_params=_params("arbitrary"),
        name="expert_blocks",
    )(meta, xs, w_gu, w_down)


def _routed_rows(xp, eidx, rank, counts, w_gu, w_down, layer):
    n = xp.shape[0]
    k = eidx.shape[0]
    n_exp = w_gu.shape[1]
    counts = counts.reshape(n_exp).astype(jnp.int32)
    padded = (counts + MOE_BLOCK - 1) // MOE_BLOCK * MOE_BLOCK
    seg_end = jnp.cumsum(padded)
    n_blocks = k * n // MOE_BLOCK + n_exp
    blk_start = jnp.arange(n_blocks, dtype=jnp.int32) * MOE_BLOCK
    blk_expert = jnp.minimum(jnp.sum(seg_end[None, :] <= blk_start[:, None], axis=1), n_exp - 1)
    meta = jnp.concatenate([blk_expert, seg_end[-1:] // MOE_BLOCK]).astype(jnp.int32)
    dest = _dest_rows(seg_end - padded, eidx, rank)
    xs = _sc_scatter_rows(xp, dest, n_blocks * MOE_BLOCK)
    return _expert_blocks(meta, xs, w_gu, w_down, layer), dest


def _shared_ple_kernel(h1_ref, h1b_ref, p_ref, wgu_ref, wd_ref, wpg_ref, bpg_ref, wpe_ref, o_ref, *, alpha):
    xb = h1b_ref[...]
    ff = wd_ref.shape[0]
    gu = _dot(xb, wgu_ref[...])
    shared = _dot((_silu(gu[:, :ff]) * gu[:, ff:]).astype(BF16), wd_ref[...])
    ple = _sigmoid(_dot(xb, wpg_ref[...]) + bpg_ref[...]) * _dot(p_ref[0].astype(BF16), wpe_ref[...])
    o_ref[...] = alpha * h1_ref[...] + shared + ple


def _shared_ple(h1, h1b, p3, layer, sh_w_gu, sh_w_down, ple_w_g, ple_b_g, ple_w_e, alpha):
    n, d = h1.shape
    t = min(ROW_TILE, n)
    pdim = p3.shape[2]
    ff2 = sh_w_gu.shape[1]
    tile = lambda cols: pl.BlockSpec((t, cols), lambda i: (i, 0))
    return pl.pallas_call(
        functools.partial(_shared_ple_kernel, alpha=alpha),
        grid=(n // t,),
        in_specs=[
            tile(d), tile(d), pl.BlockSpec((1, t, pdim), lambda i: (layer, i, 0)),
            _const_spec((d, ff2)),
            _const_spec((ff2 // 2, d)),
            _const_spec((d, d)),
            _const_spec((1, d)),
            _const_spec((pdim, d)),
        ],
        out_specs=tile(d),
        out_shape=jax.ShapeDtypeStruct((n, d), F32),
        compiler_params=_params("parallel"),
        name="shared_ple",
    )(h1, h1b, p3, sh_w_gu, sh_w_down, ple_w_g, ple_b_g.reshape(1, -1), ple_w_e)


def _combine_norm_kernel(pre_ref, yk_ref, gwt_ref, g_ref, b_ref, o_ref):
    gwt = gwt_ref[...]
    routed_hi = routed_lo = None
    for k in range(yk_ref.shape[0]):
        hi, lo = _unpack_bf16_pairs(yk_ref[k])
        wk = gwt[:, k:k + 1]
        routed_hi = hi * wk if routed_hi is None else routed_hi + hi * wk
        routed_lo = lo * wk if routed_lo is None else routed_lo + lo * wk
    routed = jnp.concatenate([routed_hi, routed_lo], axis=1)
    o_ref[...] = _layer_norm(pre_ref[...] + routed, g_ref[...], b_ref[...])


def _combine_norm(pre, ys, dest, gwt, ln_g, ln_b):
    n, d = pre.shape
    top_k = dest.shape[0]
    t = min(ROW_TILE, n)
    yk = _sc_gather_rows(ys, dest.reshape(top_k * n)).reshape(top_k, n, d // 2)
    tile = lambda cols: pl.BlockSpec((t, cols), lambda i: (i, 0))
    return pl.pallas_call(
        _combine_norm_kernel,
        grid=(n // t,),
        in_specs=[tile(d), pl.BlockSpec((top_k, t, d // 2), lambda i: (0, i, 0)), tile(top_k),
                  _const_spec((1, d)), _const_spec((1, d))],
        out_specs=tile(d),
        out_shape=jax.ShapeDtypeStruct((n, d), F32),
        compiler_params=_params("parallel"),
        name="combine_norm",
    )(pre, yk, gwt, ln_g.reshape(1, -1), ln_b.reshape(1, -1))


def _block_diag(w):
    g, i, j = w.shape
    eye = jnp.eye(g, dtype=w.dtype)
    return (eye[:, None, :, None] * w[:, :, None, :]).reshape(g * i, g * j)


def kernel(x, p, ln_in_g, ln_in_b, w_in, b_in, ret_norm_g, ret_norm_b, gdn_conv_w, gdn_a_log, gdn_dt_bias, gdn_norm_g, lru_conv_w, lru_conv_b, lru_w_r, lru_b_r, lru_w_i, lru_b_i, lru_lambda, w_branch, w_out, ln1_g, ln1_b, router_w, router_b, exp_w_gu, exp_w_down, sh_w_gu, sh_w_down, ple_w_e, ple_w_g, ple_b_g, ln2_g, ln2_b):
    bsz, seq, d = x.shape
    depth = w_in.shape[0]
    n = bsz * seq
    width = N_HEADS * HEAD_DIM
    alpha = (2 * depth) ** 0.25
    lanes = LANES
    o_ret = 0
    o_gdn = o_ret + 4 * width
    o_small = o_gdn + 4 * width
    o_lru = o_small + 2 * N_HEADS
    o_mg = o_lru + 2 * width

    p3 = p.reshape(depth, n, -1)
    bf = lambda a: a.astype(BF16)
    h = _entry_norm(x.reshape(n, d), ln_in_g, ln_in_b)
    for l in range(depth):
        wl, bl = w_in[l], b_in[l]
        w_small = bf(jnp.zeros((d, lanes), F32).at[:, :2 * N_HEADS].set(wl[:, o_small:o_lru]))
        b_small = jnp.zeros((lanes,), F32).at[:2 * N_HEADS].set(bl[o_small:o_lru])
        w_ret, w_gdn, w_lru, w_mg = bf(wl[:, o_ret:o_gdn]), bf(wl[:, o_gdn:o_small]), bf(wl[:, o_lru:o_mg]), bf(wl[:, o_mg:])
        w_r, w_i = bf(_block_diag(lru_w_r[l])), bf(_block_diag(lru_w_i[l]))
        w_br, w_o, r_wt = bf(w_branch[l]), bf(w_out[l]), bf(router_w[l].T)
        w_sgu, w_sd, w_pg, w_pe = bf(sh_w_gu[l]), bf(sh_w_down[l]), bf(ple_w_g[l]), bf(ple_w_e[l])
        y_ret, y_gdn, y_lru = _mixers(
            h.reshape(bsz, seq, d), (w_ret, bl[o_ret:o_gdn], ret_norm_g[l], ret_norm_b[l]),
            (w_gdn, bl[o_gdn:o_small], w_small, b_small, gdn_conv_w[l], gdn_a_log[l], gdn_dt_bias[l], gdn_norm_g[l]),
            (w_lru, bl[o_lru:o_mg], lru_conv_w[l], lru_conv_b[l], w_r, lru_b_r[l], w_i, lru_b_i[l], lru_lambda[l]))
        h1, h1b, xp, eidx, rank, gwt, counts = _merge(
            h, y_ret.reshape(n, width), y_gdn.reshape(n, width), y_lru.reshape(n, width), w_mg, bl[o_mg:],
            w_br, w_o, ln1_g[l], ln1_b[l], r_wt, router_b[l], alpha)
        pre = _shared_ple(h1, h1b, p3, l, w_sgu, w_sd, w_pg, ple_b_g[l], w_pe, alpha)
        ys, dest = _routed_rows(xp, eidx, rank, counts, exp_w_gu, exp_w_down, l)
        h = _combine_norm(pre, ys, dest, gwt, ln2_g[l], ln2_b[l])
    return h.reshape(bsz, seq, d)
```

```python
import functools

import numpy as np
import jax
import jax.numpy as jnp
from jax import lax
from jax.experimental import pallas as pl
from jax.experimental.pallas import tpu as pltpu
from jax.experimental.pallas import tpu_sc as plsc

F32 = jnp.float32
BF16 = jnp.bfloat16

LANES = 128

HEAD_DIM = 128
N_HEADS = 4
LRU_C = 8.0
CONV_WIDTH = 4
N_BRANCHES = 3
ROPE_BASE = 10000.0
N_GROUPS = 8
TOPK_GROUPS = 4
TOP_K = 8
ROUTED_SCALE = 2.5
LN_EPS = 1e-5
GDN_CHUNK = 128
GDN_BASE = 16
SEQ_TILE = 512
MERGE_TILE = 512
MERGE_SUB = 128
ENTRY_TILE = 1024
ROW_TILE = 512
MOE_BLOCK = 2048
EXPERT_SUB = 256
DEST_TILE = 4096
SC_WINDOW = 128
CARRY_ROWS = 8
SCAN_GROUP = 16
VMEM_LIMIT_BYTES = 56 * 1024 * 1024
NEG_INF = float("-inf")


def _const_spec(shape):
    nd = len(shape)
    return pl.BlockSpec(shape, lambda *_: (0,) * nd, pipeline_mode=pl.Buffered(1))


def _params(*sem):
    return pltpu.CompilerParams(dimension_semantics=sem, vmem_limit_bytes=VMEM_LIMIT_BYTES)


def _layer_norm(x, g, b):
    mu = jnp.mean(x, axis=-1, keepdims=True)
    xc = x - mu
    var = jnp.mean(xc * xc, axis=-1, keepdims=True)
    return xc * lax.rsqrt(var + LN_EPS) * g + b


def _sigmoid(x):
    return 1.0 / (1.0 + jnp.exp(-x))


def _silu(x):
    return x * _sigmoid(x)


def _softplus(x):
    return jnp.maximum(x, 0.0) + jnp.log1p(jnp.exp(-jnp.abs(x)))


def _dot(a, b):
    return jnp.dot(a, b, preferred_element_type=F32)


def _dot_nt(a, b):
    return lax.dot_general(a, b, (((1,), (1,)), ((), ())), preferred_element_type=F32)


def _dot_tn(a, b):
    return lax.dot_general(a, b, (((0,), (0,)), ((), ())), preferred_element_type=F32)


def _ln_kernel(x_ref, g_ref, b_ref, o_ref):
    o_ref[...] = _layer_norm(x_ref[...], g_ref[...], b_ref[...])


def _entry_norm(x2, g, b):
    n, d = x2.shape
    t = min(ENTRY_TILE, n)
    return pl.pallas_call(
        _ln_kernel,
        grid=(n // t,),
        in_specs=[pl.BlockSpec((t, d), lambda i: (i, 0)), _const_spec((1, d)), _const_spec((1, d))],
        out_specs=pl.BlockSpec((t, d), lambda i: (i, 0)),
        out_shape=jax.ShapeDtypeStruct((n, d), F32),
        compiler_params=_params("parallel"),
        name="entry_norm",
    )(x2, g.reshape(1, d), b.reshape(1, d))


def _retention_body(hb, w_ref, b_ref, cos_ref, sin_ref, dmat_ref, qd_ref, kd_ref, ng_ref, nb_ref,
                    y_ref, state_ref, *, chunk_decay):
    width = N_HEADS * HEAD_DIM
    proj = _dot(hb, w_ref[...]) + b_ref[...]
    yield
    cos = cos_ref[...]
    sin = sin_ref[...]
    heads = range(N_HEADS)
    qs, ks, vbs = [], [], []
    for hh in heads:
        lo = hh * HEAD_DIM
        q = proj[:, lo:lo + HEAD_DIM]
        k = proj[:, width + lo:width + lo + HEAD_DIM]
        qs.append(q * cos + pltpu.roll(q, HEAD_DIM // 2, axis=1) * sin)
        ks.append((k * cos + pltpu.roll(k, HEAD_DIM // 2, axis=1) * sin) * (HEAD_DIM ** -0.5))
        vbs.append(proj[:, 2 * width + lo:2 * width + lo + HEAD_DIM].astype(BF16))
        yield
    states = [state_ref[hh] for hh in heads]
    scores, inter, outs = [], [], []
    for hh in heads:
        scores.append((_dot_nt(qs[hh].astype(BF16), ks[hh].astype(BF16)) * dmat_ref[hh]).astype(BF16))
        yield
    for hh in heads:
        inter.append(_dot((qs[hh] * qd_ref[hh]).astype(BF16), states[hh].astype(BF16)))
        yield
    for hh in heads:
        state_ref[hh] = states[hh] * chunk_decay[hh] + _dot_tn((ks[hh] * kd_ref[hh]).astype(BF16), vbs[hh])
        yield
    for hh in heads:
        outs.append(_dot(scores[hh], vbs[hh]) + inter[hh])
        yield
    for hh in heads:
        lo = hh * HEAD_DIM
        o = outs[hh]
        gate = proj[:, 3 * width + lo:3 * width + lo + HEAD_DIM]
        mu = jnp.mean(o, axis=-1, keepdims=True)
        oc = o - mu
        var = jnp.mean(oc * oc, axis=-1, keepdims=True)
        on = oc * lax.rsqrt(var + LN_EPS) * ng_ref[:, lo:lo + HEAD_DIM] + nb_ref[:, lo:lo + HEAD_DIM]
        y_ref[:, lo:lo + HEAD_DIM] = (_silu(gate) * on).astype(y_ref.dtype)
        yield


def _retention_tables(seq, tile):
    half = HEAD_DIM // 2
    inv_freq = ROPE_BASE ** (-np.linspace(0.0, 1.0, half))
    ang = np.arange(seq)[:, None] * inv_freq[None, :]
    cos = np.concatenate([np.cos(ang), np.cos(ang)], axis=1)
    sin = np.concatenate([-np.sin(ang), np.sin(ang)], axis=1)
    log_gamma = np.log1p(-np.exp2(-5.0 - np.arange(N_HEADS)))
    pos = np.arange(tile)
    diff = pos[:, None] - pos[None, :]
    dmat = np.where(diff >= 0, np.exp(log_gamma[:, None, None] * np.maximum(diff, 0)), 0.0)
    qd = np.exp(log_gamma[:, None] * (pos + 1.0))[:, :, None] * np.ones((1, 1, HEAD_DIM))
    kd = np.exp(log_gamma[:, None] * (tile - 1.0 - pos))[:, :, None] * np.ones((1, 1, HEAD_DIM))
    chunk_decay = tuple(float(c) for c in np.exp(log_gamma * tile))
    as32 = lambda a: jnp.asarray(a, F32)
    return as32(cos), as32(sin), as32(dmat), as32(qd), as32(kd), chunk_decay


def _causal_conv(x, xs_ref, cw_ref):
    t = x.shape[0]
    xs_ref[CARRY_ROWS:, :] = x
    ext = xs_ref[...]
    acc = ext * cw_ref[0:1, :]
    for j in range(1, CONV_WIDTH):
        acc = pltpu.roll(acc, 1, axis=0) + ext * cw_ref[j:j + 1, :]
    xs_ref[0:CARRY_ROWS, :] = xs_ref[t:t + CARRY_ROWS, :]
    return acc[CARRY_ROWS:, :]


def _cumsum_rows(x):
    n = x.shape[0]
    row = lax.broadcasted_iota(jnp.int32, x.shape, 0)
    d = 1
    while d < n:
        x = x + jnp.where(row >= d, pltpu.roll(x, d, axis=0), 0.0)
        d *= 2
    return x


def _gdn_body(hb, w_ref, b_ref, ws_ref, bs_ref, cw_ref, alog_ref, dtb_ref, ng_ref,
              y_ref, xs_ref, state_ref, u_ref, wf_ref, w_s_ref, qd_ref, kd_ref, qk_ref):
    width = N_HEADS * HEAD_DIM
    c = GDN_CHUNK
    t = hb.shape[0]
    proj = _dot(hb, w_ref[...]) + b_ref[...]
    small = _dot(hb, ws_ref[...]) + bs_ref[...]
    yield
    qkv = _silu(_causal_conv(proj[:, :3 * width], xs_ref, cw_ref))
    yield
    beta_all = _sigmoid(small)
    la_all = -jnp.exp(alog_ref[...]) * _softplus(small + dtb_ref[...])

    ri = lax.broadcasted_iota(jnp.int32, (c, c), 0)
    ci = lax.broadcasted_iota(jnp.int32, (c, c), 1)
    lower = ri >= ci
    strict = ri > ci
    base_blocks = (ri // GDN_BASE) == (ci // GDN_BASE)

    items = [(n, hh) for n in range(t // c) for hh in range(N_HEADS)]
    gcs = {}
    for n in range(t // c):
        la_c = la_all[n * c:(n + 1) * c, :]
        gc_c = _cumsum_rows(la_c)
        gcs[n] = (la_c, gc_c, jnp.exp(gc_c))
    g_last, negs, pws, rems = {}, {}, {}, {}
    for n, hh in items:
        r0, lo = n * c, hh * HEAD_DIM
        rows, cols = slice(r0, r0 + c), slice(lo, lo + HEAD_DIM)
        la_c, gc_c, egc_c = gcs[n]
        q = qkv[rows, lo:lo + HEAD_DIM]
        k = qkv[rows, width + lo:width + lo + HEAD_DIM]
        v = qkv[rows, 2 * width + lo:2 * width + lo + HEAD_DIM]
        q = q * lax.rsqrt(jnp.sum(q * q, axis=-1, keepdims=True) + 1e-6) * (HEAD_DIM ** -0.5)
        k = k * lax.rsqrt(jnp.sum(k * k, axis=-1, keepdims=True) + 1e-6)
        beta = beta_all[rows, hh:hh + 1]
        la = la_c[:, N_HEADS + hh:N_HEADS + hh + 1]
        gc = gc_c[:, N_HEADS + hh:N_HEADS + hh + 1]
        egc = egc_c[:, N_HEADS + hh:N_HEADS + hh + 1]
        gc_row = jnp.sum(jnp.where(ri <= ci, jnp.broadcast_to(la, (c, c)), 0.0), axis=0, keepdims=True)
        gc_last = gc_row[:, c - 1:c]
        decay = jnp.where(lower, jnp.exp(jnp.where(lower, gc - gc_row, 0.0)), 0.0)
        kb = k * beta
        kbf = k.astype(BF16)
        a_neg = jnp.where(strict, -(_dot_nt(kb.astype(BF16), kbf) * decay), 0.0)
        negs[n, hh] = a_neg
        pws[n, hh] = rems[n, hh] = jnp.where(base_blocks, a_neg, 0.0)
        u_ref[rows, cols] = v * beta
        wf_ref[rows, cols] = kb * egc
        qk_ref[hh, rows, :] = (_dot_nt(q.astype(BF16), kbf) * decay).astype(BF16)
        qd_ref[rows, cols] = (q * egc).astype(BF16)
        kd_ref[rows, cols] = (k * jnp.exp(gc_last - gc)).astype(BF16)
        g_last[n, hh] = jnp.exp(gc_last)
        yield
    m = 2
    while m < GDN_BASE:
        for it in items:
            pwb = pws[it].astype(BF16)
            pws[it] = _dot(pwb, pwb)
        yield
        for it in items:
            rems[it] = rems[it] + pws[it] + _dot(rems[it].astype(BF16), pws[it].astype(BF16))
        yield
        m *= 2
    size = GDN_BASE
    while size < c:
        pair = ((ri // (2 * size)) == (ci // (2 * size))) & ((ri // size) != (ci // size))
        mids = {}
        for it in items:
            link = jnp.where(pair, negs[it], 0.0)
            mids[it] = link + _dot(rems[it].astype(BF16), link.astype(BF16))
        yield
        for it in items:
            rems[it] = rems[it] + mids[it] + _dot(mids[it].astype(BF16), rems[it].astype(BF16))
        yield
        size *= 2
    for n, hh in items:
        rows, cols = slice(n * c, (n + 1) * c), slice(hh * HEAD_DIM, (hh + 1) * HEAD_DIM)
        remb = rems[n, hh].astype(BF16)
        u_ref[rows, cols] = u_ref[rows, cols] + _dot(remb, u_ref[rows, cols].astype(BF16))
        w_s_ref[rows, cols] = (wf_ref[rows, cols] + _dot(remb, wf_ref[rows, cols].astype(BF16))).astype(BF16)
    yield

    heads = range(N_HEADS)
    for n in range(t // c):
        rows = slice(n * c, (n + 1) * c)
        cols = [slice(hh * HEAD_DIM, (hh + 1) * HEAD_DIM) for hh in heads]
        states = [state_ref[hh] for hh in heads]
        sbs = [s.astype(BF16) for s in states]
        vnbs = [(u_ref[rows, cols[hh]] - _dot(w_s_ref[rows, cols[hh]], sbs[hh])).astype(BF16) for hh in heads]
        yield
        outs = [_dot(qd_ref[rows, cols[hh]], sbs[hh]) + _dot(qk_ref[hh, rows, :], vnbs[hh]) for hh in heads]
        for hh in heads:
            state_ref[hh] = states[hh] * g_last[n, hh] + _dot_tn(kd_ref[rows, cols[hh]], vnbs[hh])
        yield
        for hh in heads:
            o = outs[hh]
            o = o * lax.rsqrt(jnp.mean(o * o, axis=-1, keepdims=True) + 1e-6) * ng_ref[...]
            og = proj[rows, 3 * width + hh * HEAD_DIM:3 * width + (hh + 1) * HEAD_DIM]
            y_ref[rows, cols[hh]] = (o * _silu(og)).astype(y_ref.dtype)
        yield


def _lru_body(hb, w_ref, b_ref, cw_ref, cb_ref, wr_ref, br_ref, wi_ref, bi_ref, lam_ref,
              y_ref, xs_ref, carry_ref):
    width = cw_ref.shape[1]
    t = hb.shape[0]
    proj = _dot(hb, w_ref[...]) + b_ref[...]
    yield
    xc = _causal_conv(proj[:, :width], xs_ref, cw_ref) + cb_ref[...]
    xcb = xc.astype(BF16)
    yield
    r = _sigmoid(_dot(xcb, wr_ref[...]) + br_ref[...])
    gi = _sigmoid(_dot(xcb, wi_ref[...]) + bi_ref[...])
    yield
    log_a = -LRU_C * r * _softplus(-lam_ref[...])
    a = jnp.exp(log_a)
    th = jnp.tanh(log_a)
    hs = jnp.sqrt(-2.0 * th / (1.0 - th)) * (gi * xc)
    row = lax.broadcasted_iota(jnp.int32, (t, width), 0) % SCAN_GROUP
    d = 1
    while d < SCAN_GROUP:
        keep = row >= d
        hs = hs + a * jnp.where(keep, pltpu.roll(hs, d, axis=0), 0.0)
        a = a * jnp.where(keep, pltpu.roll(a, d, axis=0), 1.0)
        d *= 2
        yield
    gate = jax.nn.gelu(proj[:, width:], approximate=True)
    carry = carry_ref[...]
    for g in range(t // SCAN_GROUP):
        rows = slice(g * SCAN_GROUP, (g + 1) * SCAN_GROUP)
        hg = hs[rows, :] + a[rows, :] * carry
        carry = hg[SCAN_GROUP - 1:SCAN_GROUP, :]
        y_ref[rows, :] = (gate[rows, :] * hg).astype(y_ref.dtype)
        if g % 8 == 7:
            yield
    carry_ref[...] = carry


N_RET_IN, N_GDN_IN, N_LRU_IN = 9, 8, 9
MIX_STRIDE = (1, 1, 2)


def _mixers_kernel(h_ref, *refs, chunk_decay):
    ret_in, refs = refs[:N_RET_IN], refs[N_RET_IN:]
    gdn_in, refs = refs[:N_GDN_IN], refs[N_GDN_IN:]
    lru_in, refs = refs[:N_LRU_IN], refs[N_LRU_IN:]
    yr_ref, yg_ref, yl_ref = refs[:3]
    ret_state, gdn_xs, gdn_state, u_ref, wf_ref, w_s_ref, qd_ref, kd_ref, qk_ref, lru_xs, lru_carry = refs[3:]

    @pl.when(pl.program_id(1) == 0)
    def _():
        ret_state[...] = jnp.zeros_like(ret_state)
        gdn_state[...] = jnp.zeros_like(gdn_state)
        lru_carry[...] = jnp.zeros_like(lru_carry)
        gdn_xs[0:CARRY_ROWS, :] = jnp.zeros((CARRY_ROWS, gdn_xs.shape[1]), F32)
        lru_xs[0:CARRY_ROWS, :] = jnp.zeros((CARRY_ROWS, lru_xs.shape[1]), F32)

    hb = h_ref[0].astype(BF16)
    branches = [
        (_gdn_body(hb, *gdn_in, yg_ref.at[0], gdn_xs, gdn_state, u_ref, wf_ref, w_s_ref, qd_ref, kd_ref, qk_ref),
         MIX_STRIDE[0]),
        (_retention_body(hb, *ret_in, yr_ref.at[0], ret_state, chunk_decay=chunk_decay), MIX_STRIDE[1]),
        (_lru_body(hb, *lru_in, yl_ref.at[0], lru_xs, lru_carry), MIX_STRIDE[2]),
    ]
    tick = 0
    while branches:
        for gen, stride in list(branches):
            if tick % stride == 0 and next(gen, StopIteration) is StopIteration:
                branches.remove((gen, stride))
        tick += 1


def _mixers(h, ret_args, gdn_args, lru_args):
    bsz, seq, d = h.shape
    t = min(SEQ_TILE, seq)
    width = N_HEADS * HEAD_DIM
    row = lambda vec: vec.reshape(1, -1)
    w_ret, b_ret, ret_g, ret_b = ret_args
    w_gdn, b_gdn, w_small, b_small, gdn_cw, a_log, dt_bias, gdn_g = gdn_args
    w_lru, b_lru, lru_cw, lru_cb, w_r, b_r, w_i, b_i, lam = lru_args
    lanes = w_small.shape[1]
    lru_w = lru_cw.shape[1]
    cos, sin, dmat, qd, kd, chunk_decay = _retention_tables(seq, t)
    pad_row = lambda vec: jnp.zeros((1, lanes), F32).at[0, N_HEADS:2 * N_HEADS].set(vec.astype(F32))
    seq_tile = lambda cols: pl.BlockSpec((1, t, cols), lambda i, j: (i, j, 0))
    pos_tile = pl.BlockSpec((t, HEAD_DIM), lambda i, j: (j, 0))
    ret_specs = [_const_spec((d, 4 * width)), _const_spec((1, 4 * width)), pos_tile, pos_tile,
                 _const_spec((N_HEADS, t, t)), _const_spec((N_HEADS, t, HEAD_DIM)), _const_spec((N_HEADS, t, HEAD_DIM)),
                 _const_spec((1, width)), _const_spec((1, width))]
    gdn_specs = [_const_spec((d, 4 * width)), _const_spec((1, 4 * width)), _const_spec((d, lanes)),
                 _const_spec((1, lanes)), _const_spec((CONV_WIDTH, 3 * width)), _const_spec((1, lanes)),
                 _const_spec((1, lanes)), _const_spec((1, HEAD_DIM))]
    lru_specs = [_const_spec((d, 2 * lru_w)), _const_spec((1, 2 * lru_w)), _const_spec((CONV_WIDTH, lru_w)),
                 _const_spec((1, lru_w)), _const_spec((lru_w, lru_w)), _const_spec((1, lru_w)),
                 _const_spec((lru_w, lru_w)), _const_spec((1, lru_w)), _const_spec((1, lru_w))]
    assert (len(ret_specs), len(gdn_specs), len(lru_specs)) == (N_RET_IN, N_GDN_IN, N_LRU_IN)
    out = jax.ShapeDtypeStruct((bsz, seq, width), BF16)
    return pl.pallas_call(
        functools.partial(_mixers_kernel, chunk_decay=chunk_decay),
        grid=(bsz, seq // t),
        in_specs=[seq_tile(d)] + ret_specs + gdn_specs + lru_specs,
        out_specs=[seq_tile(width), seq_tile(width), seq_tile(lru_w)],
        out_shape=[out, out, jax.ShapeDtypeStruct((bsz, seq, lru_w), BF16)],
        scratch_shapes=[pltpu.VMEM((N_HEADS, HEAD_DIM, HEAD_DIM), F32),
                        pltpu.VMEM((CARRY_ROWS + t, 3 * width), F32),
                        pltpu.VMEM((N_HEADS, HEAD_DIM, HEAD_DIM), F32),
                        pltpu.VMEM((t, width), F32),
                        pltpu.VMEM((t, width), F32),
                        pltpu.VMEM((t, width), BF16),
                        pltpu.VMEM((t, width), BF16),
                        pltpu.VMEM((t, width), BF16),
                        pltpu.VMEM((N_HEADS, t, GDN_CHUNK), BF16),
                        pltpu.VMEM((CARRY_ROWS + t, lru_w), F32),
                        pltpu.VMEM((1, lru_w), F32)],
        compiler_params=_params("parallel", "arbitrary"),
        name="token_mixers",
    )(h, w_ret, row(b_ret), cos, sin, dmat, qd, kd, row(ret_g), row(ret_b),
      w_gdn, row(b_gdn), w_small, row(b_small), gdn_cw, pad_row(a_log), pad_row(dt_bias), row(gdn_g),
      w_lru, row(b_lru), lru_cw, row(lru_cb), w_r, row(b_r), w_i, row(b_i), row(lam))


def _first_index_of_max(x, idx, size):
    m = jnp.max(x, axis=0, keepdims=True)
    first = jnp.min(jnp.where(x == m, idx, size), axis=0, keepdims=True)
    return m, idx == first


def _route(logits_t, bias_col, out):
    n_exp, t = logits_t.shape
    per_group = n_exp // N_GROUPS
    scores = _sigmoid(logits_t)
    sel = scores + bias_col
    idx_g = lax.broadcasted_iota(jnp.int32, (per_group, t), 0)
    group_scores = []
    for g in range(N_GROUPS):
        x = sel[g * per_group:(g + 1) * per_group, :]
        m1, hit = _first_index_of_max(x, idx_g, per_group)
        m2 = jnp.max(jnp.where(hit, NEG_INF, x), axis=0, keepdims=True)
        group_scores.append(m1 + m2)
        if g % 2 == 1:
            yield
    gsc = jnp.concatenate(group_scores, axis=0)
    idx_n = lax.broadcasted_iota(jnp.int32, (N_GROUPS, t), 0)
    gmask = jnp.zeros((N_GROUPS, t), F32)
    for _ in range(TOPK_GROUPS):
        _, hit = _first_index_of_max(gsc, idx_n, N_GROUPS)
        gmask = jnp.where(hit, 1.0, gmask)
        gsc = jnp.where(hit, NEG_INF, gsc)
    yield
    emask = jnp.concatenate([jnp.broadcast_to(gmask[g:g + 1, :], (per_group, t)) for g in range(N_GROUPS)], axis=0)
    cand = jnp.where(emask > 0.0, sel, NEG_INF)
    idx_e = lax.broadcasted_iota(jnp.int32, (n_exp, t), 0)
    picked = jnp.zeros((n_exp, t), F32)
    hits = []
    for _ in range(TOP_K):
        _, hit = _first_index_of_max(cand, idx_e, n_exp)
        hits.append(hit)
        picked = jnp.where(hit, 1.0, picked)
        cand = jnp.where(hit, NEG_INF, cand)
        yield
    gw = jnp.where(picked > 0.0, scores, 0.0)
    out.update(combine=gw / jnp.sum(gw, axis=0, keepdims=True) * ROUTED_SCALE, picked=picked, hits=hits, idx_e=idx_e)


def _pack_bf16_pairs(x):
    c = x.shape[1] // 2
    hi = pltpu.bitcast(x[:, :c].astype(BF16).astype(F32), jnp.int32)
    lo = pltpu.bitcast(x[:, c:].astype(BF16).astype(F32), jnp.int32)
    return hi | lax.shift_right_logical(lo, jnp.full(lo.shape, 16, jnp.int32))


def _unpack_bf16_pairs(w):
    hi = pltpu.bitcast(w & jnp.int32(-65536), F32)
    lo = pltpu.bitcast(lax.shift_left(w, jnp.full(w.shape, 16, jnp.int32)), F32)
    return hi, lo


def _merge_kernel(h_ref, yr_ref, yg_ref, yl_ref, wmg_ref, bmg_ref, wbr_ref, wout_ref, g_ref, b_ref,
                  rwt_ref, rb_ref, eye_ref, tri_ref, h1_ref, h1b_ref, xp_ref, eidx_ref, rank_ref, gwt_ref,
                  count_ref, *, alpha):
    @pl.when(pl.program_id(0) == 0)
    def _():
        count_ref[...] = jnp.zeros_like(count_ref)

    d = h_ref.shape[1]
    sub = eye_ref.shape[0]
    parts = [slice(s, s + sub) for s in range(0, h_ref.shape[0], sub)]
    pick = lambda hit, vals, zero: jnp.sum(jnp.where(hit, vals, zero), axis=0, keepdims=True)

    def dense(rows, logits):
        h = h_ref[rows, :]
        hb = h.astype(BF16)
        gates = _sigmoid(_dot(hb, wmg_ref[...]) + bmg_ref[...])
        yield
        mixed = None
        for n, y_ref in enumerate((yr_ref, yg_ref, yl_ref)):
            term = gates[:, n * d:(n + 1) * d] * _dot(y_ref[rows, :], wbr_ref[n])
            mixed = term if mixed is None else mixed + term
            yield
        mix = _dot(mixed.astype(BF16), wout_ref[...])
        yield
        h1 = _layer_norm(alpha * h + mix, g_ref[...], b_ref[...])
        h1b = h1.astype(BF16)
        h1_ref[rows, :] = h1
        h1b_ref[rows, :] = h1b
        xp_ref[rows, :] = _pack_bf16_pairs(h1)
        logits.append(_dot_nt(rwt_ref[...], h1b))
        yield

    def routing(rows, logit):
        res = {}
        yield from _route(logit, rb_ref[...], res)
        combine_t, picked, hits, idx_e = res["combine"], res["picked"], res["hits"], res["idx_e"]
        rank_full = count_ref[...] + _dot(picked.astype(BF16), tri_ref[...])
        count_ref[...] += jnp.sum(picked, axis=1, keepdims=True)
        eidx_ref[:, rows] = jnp.concatenate([pick(hit, idx_e, 0) for hit in hits], axis=0)
        yield
        rank_ref[:, rows] = jnp.concatenate([pick(hit, rank_full, 0.0) for hit in hits], axis=0).astype(jnp.int32)
        gw = jnp.concatenate([pick(hit, combine_t, 0.0) for hit in hits], axis=0)
        gwt_ref[rows, :] = lax.dot_general(eye_ref[...], gw, (((1,), (1,)), ((), ())),
                                           preferred_element_type=F32, precision=lax.Precision.HIGHEST)
        yield

    def interleave(gens):
        while gens:
            gens = [g for g in gens if next(g, StopIteration) is not StopIteration]

    logits = [[] for _ in parts]
    interleave([dense(rows, logits[s]) for s, rows in enumerate(parts)])
    interleave([routing(rows, logits[s][0]) for s, rows in enumerate(parts)])


def _merge(h2, y_ret, y_gdn, y_lru, w_mg, b_mg, w_branch, w_out, ln_g, ln_b, router_wt, router_b, alpha):
    n, d = h2.shape
    t = min(MERGE_TILE, n)
    width = y_ret.shape[1]
    n_exp = router_wt.shape[0]
    tile = lambda cols: pl.BlockSpec((t, cols), lambda i: (i, 0))
    lane_tile = pl.BlockSpec((TOP_K, t), lambda i: (0, i))
    sub = min(MERGE_SUB, t)
    tri = jnp.triu(jnp.ones((sub, sub), BF16), 1)
    return pl.pallas_call(
        functools.partial(_merge_kernel, alpha=alpha),
        grid=(n // t,),
        in_specs=[
            tile(d), tile(width), tile(width), tile(width),
            _const_spec((d, N_BRANCHES * d)),
            _const_spec((1, N_BRANCHES * d)),
            _const_spec((N_BRANCHES, width, d)),
            _const_spec((d, d)),
            _const_spec((1, d)),
            _const_spec((1, d)),
            _const_spec((n_exp, d)),
            _const_spec((n_exp, 1)),
            _const_spec((sub, sub)),
            _const_spec((sub, sub)),
        ],
        out_specs=[tile(d), tile(d), tile(d // 2), lane_tile, lane_tile, tile(TOP_K),
                   pl.BlockSpec((n_exp, 1), lambda i: (0, 0))],
        out_shape=[jax.ShapeDtypeStruct((n, d), F32), jax.ShapeDtypeStruct((n, d), BF16),
                   jax.ShapeDtypeStruct((n, d // 2), jnp.int32),
                   jax.ShapeDtypeStruct((TOP_K, n), jnp.int32), jax.ShapeDtypeStruct((TOP_K, n), jnp.int32),
                   jax.ShapeDtypeStruct((n, TOP_K), F32), jax.ShapeDtypeStruct((n_exp, 1), F32)],
        compiler_params=_params("arbitrary"),
        name="merge_route",
    )(h2, y_ret, y_gdn, y_lru, w_mg, b_mg.reshape(1, -1), w_branch, w_out, ln_g.reshape(1, -1),
      ln_b.reshape(1, -1), router_wt, router_b.reshape(-1, 1), jnp.eye(sub, dtype=F32), tri)


def _dest_kernel(start_ref, eidx_ref, rank_ref, o_ref, *, n_exp):
    eidx = eidx_ref[...]
    dest = rank_ref[...]
    for e in range(n_exp):
        dest = dest + jnp.where(eidx == e, start_ref[e], 0)
    o_ref[...] = dest


def _dest_rows(seg_start, eidx, rank):
    k, n = eidx.shape
    t = min(DEST_TILE, n)
    blk = pl.BlockSpec((k, t), lambda i, s: (0, i))
    return pl.pallas_call(
        functools.partial(_dest_kernel, n_exp=seg_start.shape[0]),
        grid_spec=pltpu.PrefetchScalarGridSpec(num_scalar_prefetch=1, grid=(n // t,), in_specs=[blk, blk],
                                               out_specs=blk),
        out_shape=jax.ShapeDtypeStruct((k, n), jnp.int32),
        compiler_params=_params("parallel"),
        name="dest_rows",
    )(seg_start, eidx, rank)


def _sc_workers():
    info = plsc.get_sparse_core_info()
    return info.num_cores, info.num_subcores


def _sc_mesh():
    return plsc.VectorSubcoreMesh(core_axis_name="c", subcore_axis_name="s")


def _sc_scatter_rows(x, dest, n_rows):
    n, c = x.shape
    k = dest.shape[0]
    n_cores, n_sub = _sc_workers()
    per_worker = n // (n_cores * n_sub)
    steps = per_worker // SC_WINDOW

    @functools.partial(pl.kernel, out_type=jax.ShapeDtypeStruct((n_rows, c), x.dtype), mesh=_sc_mesh(),
                       scratch_types=[pltpu.VMEM((k, SC_WINDOW), jnp.int32), pltpu.VMEM((SC_WINDOW, c), x.dtype),
                                      pltpu.SemaphoreType.DMA, pltpu.SemaphoreType.DMA],
                       name="dispatch_rows")
    def scatter(x_hbm, d_hbm, o_hbm, idx_v, rows_v, sem, row_sem):
        base = (lax.axis_index("s") * n_cores + lax.axis_index("c")) * per_worker

        @pl.loop(0, steps)
        def _(j):
            off = base + j * SC_WINDOW
            loads = [pltpu.async_copy(x_hbm.at[pl.ds(off, SC_WINDOW)], rows_v, row_sem)]
            loads += [pltpu.async_copy(d_hbm.at[pl.ds(kk * n + off, SC_WINDOW)], idx_v.at[kk], sem) for kk in range(k)]
            for cp in loads:
                cp.wait()
            copies = [pltpu.async_copy(rows_v, o_hbm.at[idx_v.at[kk]], sem) for kk in range(k)]
            for cp in copies:
                cp.wait()

    return scatter(x, dest.reshape(k * n))


def _sc_gather_rows(table, idx):
    b = idx.shape[0]
    c = table.shape[1]
    n_cores, n_sub = _sc_workers()
    per_worker = b // (n_cores * n_sub)
    steps = per_worker // SC_WINDOW

    half = SC_WINDOW // 2
    half_buf = lambda dtype, *shape: pltpu.VMEM((half,) + shape, dtype)

    @functools.partial(pl.kernel, out_type=jax.ShapeDtypeStruct((b, c), table.dtype), mesh=_sc_mesh(),
                       scratch_types=[half_buf(jnp.int32), half_buf(jnp.int32), half_buf(table.dtype, c),
                                      half_buf(table.dtype, c), pltpu.SemaphoreType.DMA, pltpu.SemaphoreType.DMA,
                                      pltpu.SemaphoreType.DMA, pltpu.SemaphoreType.DMA],
                       name="collect_rows")
    def gather(t_hbm, i_hbm, o_hbm, idx_a, idx_b, rows_a, rows_b, gat_a, gat_b, put_a, put_b):
        base = (lax.axis_index("s") * n_cores + lax.axis_index("c")) * per_worker

        @pl.loop(0, steps)
        def _(j):
            off_a = base + j * SC_WINDOW
            off_b = off_a + half
            pltpu.sync_copy(i_hbm.at[pl.ds(off_a, half)], idx_a)
            in_a = pltpu.async_copy(t_hbm.at[idx_a], rows_a, gat_a)
            pltpu.sync_copy(i_hbm.at[pl.ds(off_b, half)], idx_b)
            in_b = pltpu.async_copy(t_hbm.at[idx_b], rows_b, gat_b)
            in_a.wait()
            out_a = pltpu.async_copy(rows_a, o_hbm.at[pl.ds(off_a, half)], put_a)
            in_b.wait()
            out_b = pltpu.async_copy(rows_b, o_hbm.at[pl.ds(off_b, half)], put_b)
            out_a.wait()
            out_b.wait()

    return gather(table, idx)


def _expert_block_kernel(meta_ref, xs_ref, wgu_ref, wd_ref, ys_ref, wgu_b, wd_b, *, n_blocks):
    i = pl.program_id(0)

    @pl.when((i == 0) | (meta_ref[i] != meta_ref[jnp.maximum(i - 1, 0)]))
    def _():
        wgu_b[...] = wgu_ref[0, 0].astype(BF16)
        wd_b[...] = wd_ref[0, 0].astype(BF16)

    @pl.when(i < meta_ref[n_blocks])
    def _():
        half = xs_ref.shape[1]
        ff = wd_b.shape[0]

        def sub_block(rows):
            hi, lo = _unpack_bf16_pairs(xs_ref[rows, :])
            hib, lob = hi.astype(BF16), lo.astype(BF16)
            yield
            gu = _dot(hib, wgu_b[:half, :]) + _dot(lob, wgu_b[half:, :])
            yield
            mid = (_silu(gu[:, :ff]) * gu[:, ff:]).astype(BF16)
            yield
            out = _dot(mid, wd_b[...])
            yield
            ys_ref[rows, :] = _pack_bf16_pairs(out)
            yield

        waiting = [sub_block(slice(r, r + EXPERT_SUB)) for r in range(0, xs_ref.shape[0], EXPERT_SUB)]
        active = []
        while waiting or active:
            if waiting:
                active.append(waiting.pop(0))
            active = [g for g in active if next(g, StopIteration) is not StopIteration]


def _expert_blocks(meta, xs, w_gu, w_down, layer):
    rows, half = xs.shape
    _, _, d, ff2 = w_gu.shape
    n_blocks = rows // MOE_BLOCK
    row_blk = pl.BlockSpec((MOE_BLOCK, half), lambda i, meta: (jnp.minimum(i, meta[n_blocks] - 1), 0))
    return pl.pallas_call(
        functools.partial(_expert_block_kernel, n_blocks=n_blocks),
        grid_spec=pltpu.PrefetchScalarGridSpec(
            num_scalar_prefetch=1, grid=(n_blocks,),
            in_specs=[row_blk,
                      pl.BlockSpec((1, 1, d, ff2), lambda i, meta: (layer, meta[i], 0, 0)),
                      pl.BlockSpec((1, 1, ff2 // 2, d), lambda i, meta: (layer, meta[i], 0, 0))],
            out_specs=row_blk,
            scratch_shapes=[pltpu.VMEM((d, ff2), BF16), pltpu.VMEM((ff2 // 2, d), BF16)]),
        out_shape=jax.ShapeDtypeStruct((rows, half), jnp.int32),
        compiler_params=_params("arbitrary"),
        name="expert_blocks",
    )(meta, xs, w_gu, w_down)


def _routed_rows(xp, eidx, rank, counts, w_gu, w_down, layer):
    n = xp.shape[0]
    k = eidx.shape[0]
    n_exp = w_gu.shape[1]
    counts = counts.reshape(n_exp).astype(jnp.int32)
    padded = (counts + MOE_BLOCK - 1) // MOE_BLOCK * MOE_BLOCK
    seg_end = jnp.cumsum(padded)
    n_blocks = k * n // MOE_BLOCK + n_exp
    blk_start = jnp.arange(n_blocks, dtype=jnp.int32) * MOE_BLOCK
    blk_expert = jnp.minimum(jnp.sum(seg_end[None, :] <= blk_start[:, None], axis=1), n_exp - 1)
    meta = jnp.concatenate([blk_expert, seg_end[-1:] // MOE_BLOCK]).astype(jnp.int32)
    dest = _dest_rows(seg_end - padded, eidx, rank)
    xs = _sc_scatter_rows(xp, dest, n_blocks * MOE_BLOCK)
    return _expert_blocks(meta, xs, w_gu, w_down, layer), dest


def _shared_ple_kernel(h1_ref, h1b_ref, p_ref, wgu_ref, wd_ref, wpg_ref, bpg_ref, wpe_ref, o_ref, *, alpha):
    xb = h1b_ref[...]
    ff = wd_ref.shape[0]
    gu = _dot(xb, wgu_ref[...])
    shared = _dot((_silu(gu[:, :ff]) * gu[:, ff:]).astype(BF16), wd_ref[...])
    ple = _sigmoid(_dot(xb, wpg_ref[...]) + bpg_ref[...]) * _dot(p_ref[0].astype(BF16), wpe_ref[...])
    o_ref[...] = alpha * h1_ref[...] + shared + ple


def _shared_ple(h1, h1b, p3, layer, sh_w_gu, sh_w_down, ple_w_g, ple_b_g, ple_w_e, alpha):
    n, d = h1.shape
    t = min(ROW_TILE, n)
    pdim = p3.shape[2]
    ff2 = sh_w_gu.shape[1]
    tile = lambda cols: pl.BlockSpec((t, cols), lambda i: (i, 0))
    return pl.pallas_call(
        functools.partial(_shared_ple_kernel, alpha=alpha),
        grid=(n // t,),
        in_specs=[
            tile(d), tile(d), pl.BlockSpec((1, t, pdim), lambda i: (layer, i, 0)),
            _const_spec((d, ff2)),
            _const_spec((ff2 // 2, d)),
            _const_spec((d, d)),
            _const_spec((1, d)),
            _const_spec((pdim, d)),
        ],
        out_specs=tile(d),
        out_shape=jax.ShapeDtypeStruct((n, d), F32),
        compiler_params=_params("parallel"),
        name="shared_ple",
    )(h1, h1b, p3, sh_w_gu, sh_w_down, ple_w_g, ple_b_g.reshape(1, -1), ple_w_e)


def _combine_norm_kernel(pre_ref, yk_ref, gwt_ref, g_ref, b_ref, o_ref):
    gwt = gwt_ref[...]
    routed_hi = routed_lo = None
    for k in range(yk_ref.shape[0]):
        hi, lo = _unpack_bf16_pairs(yk_ref[k])
        wk = gwt[:, k:k + 1]
        routed_hi = hi * wk if routed_hi is None else routed_hi + hi * wk
        routed_lo = lo * wk if routed_lo is None else routed_lo + lo * wk
    routed = jnp.concatenate([routed_hi, routed_lo], axis=1)
    o_ref[...] = _layer_norm(pre_ref[...] + routed, g_ref[...], b_ref[...])


def _combine_norm(pre, ys, dest, gwt, ln_g, ln_b):
    n, d = pre.shape
    top_k = dest.shape[0]
    t = min(ROW_TILE, n)
    yk = _sc_gather_rows(ys, dest.reshape(top_k * n)).reshape(top_k, n, d // 2)
    tile = lambda cols: pl.BlockSpec((t, cols), lambda i: (i, 0))
    return pl.pallas_call(
        _combine_norm_kernel,
        grid=(n // t,),
        in_specs=[tile(d), pl.BlockSpec((top_k, t, d // 2), lambda i: (0, i, 0)), tile(top_k),
                  _const_spec((1, d)), _const_spec((1, d))],
        out_specs=tile(d),
        out_shape=jax.ShapeDtypeStruct((n, d), F32),
        compiler_params=_params("parallel"),
        name="combine_norm",
    )(pre, yk, gwt, ln_g.reshape(1, -1), ln_b.reshape(1, -1))


def _block_diag(w):
    g, i, j = w.shape
    eye = jnp.eye(g, dtype=w.dtype)
    return (eye[:, None, :, None] * w[:, :, None, :]).reshape(g * i, g * j)


def kernel(x, p, ln_in_g, ln_in_b, w_in, b_in, ret_norm_g, ret_norm_b, gdn_conv_w, gdn_a_log, gdn_dt_bias, gdn_norm_g, lru_conv_w, lru_conv_b, lru_w_r, lru_b_r, lru_w_i, lru_b_i, lru_lambda, w_branch, w_out, ln1_g, ln1_b, router_w, router_b, exp_w_gu, exp_w_down, sh_w_gu, sh_w_down, ple_w_e, ple_w_g, ple_b_g, ln2_g, ln2_b):
    bsz, seq, d = x.shape
    depth = w_in.shape[0]
    n = bsz * seq
    width = N_HEADS * HEAD_DIM
    alpha = (2 * depth) ** 0.25
    lanes = LANES
    o_ret = 0
    o_gdn = o_ret + 4 * width
    o_small = o_gdn + 4 * width
    o_lru = o_small + 2 * N_HEADS
    o_mg = o_lru + 2 * width

    p3 = p.reshape(depth, n, -1)
    bf = lambda a: a.astype(BF16)
    h = _entry_norm(x.reshape(n, d), ln_in_g, ln_in_b)
    for l in range(depth):
        wl, bl = w_in[l], b_in[l]
        w_small = bf(jnp.zeros((d, lanes), F32).at[:, :2 * N_HEADS].set(wl[:, o_small:o_lru]))
        b_small = jnp.zeros((lanes,), F32).at[:2 * N_HEADS].set(bl[o_small:o_lru])
        w_ret, w_gdn, w_lru, w_mg = bf(wl[:, o_ret:o_gdn]), bf(wl[:, o_gdn:o_small]), bf(wl[:, o_lru:o_mg]), bf(wl[:, o_mg:])
        w_r, w_i = bf(_block_diag(lru_w_r[l])), bf(_block_diag(lru_w_i[l]))
        w_br, w_o, r_wt = bf(w_branch[l]), bf(w_out[l]), bf(router_w[l].T)
        w_sgu, w_sd, w_pg, w_pe = bf(sh_w_gu[l]), bf(sh_w_down[l]), bf(ple_w_g[l]), bf(ple_w_e[l])
        y_ret, y_gdn, y_lru = _mixers(
            h.reshape(bsz, seq, d), (w_ret, bl[o_ret:o_gdn], ret_norm_g[l], ret_norm_b[l]),
            (w_gdn, bl[o_gdn:o_small], w_small, b_small, gdn_conv_w[l], gdn_a_log[l], gdn_dt_bias[l], gdn_norm_g[l]),
            (w_lru, bl[o_lru:o_mg], lru_conv_w[l], lru_conv_b[l], w_r, lru_b_r[l], w_i, lru_b_i[l], lru_lambda[l]))
        h1, h1b, xp, eidx, rank, gwt, counts = _merge(
            h, y_ret.reshape(n, width), y_gdn.reshape(n, width), y_lru.reshape(n, width), w_mg, bl[o_mg:],
            w_br, w_o, ln1_g[l], ln1_b[l], r_wt, router_b[l], alpha)
        pre = _shared_ple(h1, h1b, p3, l, w_sgu, w_sd, w_pg, ple_b_g[l], w_pe, alpha)
        ys, dest = _routed_rows(xp, eidx, rank, counts, exp_w_gu, exp_w_down, l)
        h = _combine_norm(pre, ys, dest, gwt, ln2_g[l], ln2_b[l])
    return h.reshape(bsz, seq, d)
```
